```python
import jax, jax.numpy as jnp
from jax import lax
import numpy as np

D_MODEL = 1024
BATCH = 2
SEQ = 8192
DEPTH = 4

N_MEM = 256
HEAD_DIM = 64
FOX_HEADS = D_MODEL // HEAD_DIM
FOX_W = FOX_HEADS * HEAD_DIM
SB_HEADS = D_MODEL // HEAD_DIM
SB_W = SB_HEADS * HEAD_DIM
LRU_W = D_MODEL
LRU_BLOCKS = D_MODEL // HEAD_DIM
LRU_BW = LRU_W // LRU_BLOCKS
LRU_CONV = 4
LRU_C = 8.0
MEM_HEADS = 4
MEM_HD = D_MODEL // MEM_HEADS
MEM_W = MEM_HEADS * MEM_HD
N_BRANCH = 4
BRANCH_W = D_MODEL
D_FF = ((8 * D_MODEL // 3 + 255) // 256) * 256
FFN_CONV = 3
Q_BLOCK = 128
EPS = 1e-6
N_IN = 3 * FOX_W + FOX_HEADS + 2 * LRU_W + 3 * SB_W + MEM_W + N_BRANCH * D_MODEL

kernel_name = "hybrid_fox_rglru_stickbreak_memory_trunk"


def _column_slices():
    widths = (("fox_q", FOX_W), ("fox_k", FOX_W), ("fox_v", FOX_W), ("fox_f", FOX_HEADS),
              ("lru_x", LRU_W), ("lru_g", LRU_W),
              ("sb_q", SB_W), ("sb_k", SB_W), ("sb_v", SB_W),
              ("mem_q", MEM_W), ("gates", N_BRANCH * D_MODEL))
    out, start = {}, 0
    for name, w in widths:
        out[name] = (start, start + w)
        start += w
    return out


def _rms(x, g):
    x32 = x.astype(jnp.float32)
    y = x32 * lax.rsqrt(jnp.mean(x32 * x32, axis=-1, keepdims=True) + EPS)
    return (y * g.astype(jnp.float32)).astype(x.dtype)


def _causal_dwconv(x, w, b):
    k, s = w.shape[0], x.shape[1]
    xp = jnp.pad(x, ((0, 0), (k - 1, 0), (0, 0)))
    out = b
    for i in range(k):
        out = out + xp[:, i:i + s] * w[i]
    return out


def _heads(t, n):
    b, s, _ = t.shape
    return t.reshape(b, s, n, -1).transpose(0, 2, 1, 3)


def _merge_heads(t):
    b, h, s, d = t.shape
    return t.transpose(0, 2, 1, 3).reshape(b, s, h * d)


def _fox_attention(q, k, v, log_f):
    b, h, s, d = q.shape
    nb = s // Q_BLOCK
    scale = d ** -0.5
    cum_f = jnp.cumsum(log_f, axis=-1)
    q_blk = q.reshape(b, h, nb, Q_BLOCK, d).transpose(2, 0, 1, 3, 4)
    f_blk = cum_f.reshape(b, h, nb, Q_BLOCK).transpose(2, 0, 1, 3)
    k_pos = jnp.arange(s)

    def block(args):
        qi, fi, bi = args
        q_pos = bi * Q_BLOCK + jnp.arange(Q_BLOCK)
        logits = jnp.einsum('bhqd,bhkd->bhqk', qi, k) * scale + (fi[..., :, None] - cum_f[..., None, :])
        logits = jnp.where(k_pos[None, :] <= q_pos[:, None], logits, -jnp.inf)
        p = jax.nn.softmax(logits, axis=-1)
        return jnp.einsum('bhqk,bhkd->bhqd', p, v)

    out = lax.map(block, (q_blk, f_blk, jnp.arange(nb)))
    return out.transpose(1, 2, 0, 3, 4).reshape(b, h, s, d)


def _stick_breaking_attention(q, k, v):
    b, h, s, d = q.shape
    nb = s // Q_BLOCK
    scale = d ** -0.5
    q_blk = q.reshape(b, h, nb, Q_BLOCK, d).transpose(2, 0, 1, 3, 4)
    k_pos = jnp.arange(s)

    def block(args):
        qi, bi = args
        q_pos = bi * Q_BLOCK + jnp.arange(Q_BLOCK)
        causal = k_pos[None, :] < q_pos[:, None]
        z = jnp.einsum('bhqd,bhkd->bhqk', qi, k) * scale
        log_1m_beta = jnp.where(causal, jax.nn.log_sigmoid(-z), 0.0)
        excl = lax.cumsum(log_1m_beta, axis=3, reverse=True) - log_1m_beta
        weights = jnp.where(causal, jnp.exp(jax.nn.log_sigmoid(z) + excl), 0.0)
        return jnp.einsum('bhqk,bhkd->bhqd', weights, v)

    out = lax.map(block, (q_blk, jnp.arange(nb)))
    return out.transpose(1, 2, 0, 3, 4).reshape(b, h, s, d)


def _mem_attention(q, k, v):
    scale = q.shape[-1] ** -0.5
    p = jax.nn.softmax(jnp.einsum('bhsd,bhmd->bhsm', q, k) * scale, axis=-1)
    return jnp.einsum('bhsm,bhmd->bhsd', p, v)


def _rg_lru_branch(x, gate, conv_w, conv_b, w_a, b_a, w_x, b_x, lam):
    b, s, w = x.shape
    xc = _causal_dwconv(x, conv_w, conv_b).astype(jnp.float32)
    xg = xc.reshape(b, s, LRU_BLOCKS, LRU_BW)
    r = jax.nn.sigmoid(jnp.einsum('bsnd,nde->bsne', xg, w_a).reshape(b, s, w) + b_a)
    i = jax.nn.sigmoid(jnp.einsum('bsnd,nde->bsne', xg, w_x).reshape(b, s, w) + b_x)
    log_a = -LRU_C * r * jax.nn.softplus(-lam)
    a = jnp.exp(log_a)
    u = jnp.sqrt(-jnp.expm1(2.0 * log_a)) * (i * xc)

    def combine(c1, c2):
        a1, h1 = c1
        a2, h2 = c2
        return a1 * a2, a2 * h1 + h2

    _, hs = lax.associative_scan(combine, (a, u), axis=1)
    return hs * jax.nn.gelu(gate.astype(jnp.float32))


def setup_inputs(seed: int = 0) -> dict:
    key = jax.random.key(seed)
    ks = jax.random.split(key, 32)
    f32 = jnp.float32

    def nrm(k, shape, scale):
        return jax.random.normal(k, shape, f32) * scale

    def gain(k, shape):
        return 1.0 + 0.02 * jax.random.normal(k, shape, f32)

    res_scale = (2 * DEPTH) ** -0.5
    u = jax.random.uniform(ks[12], (DEPTH, LRU_W), f32, 0.9, 0.999)
    sig = u ** (1.0 / LRU_C)
    lru_lambda = jnp.log(sig) - jnp.log1p(-sig)
    return {
        "x": nrm(ks[0], (BATCH, SEQ, D_MODEL), 1.0),
        "mem": nrm(ks[1], (BATCH, N_MEM, D_MODEL), 1.0),
        "attn_norm_g": gain(ks[2], (DEPTH, D_MODEL)),
        "mem_norm_g": gain(ks[3], (DEPTH, D_MODEL)),
        "w_in": nrm(ks[4], (DEPTH, D_MODEL, N_IN), D_MODEL ** -0.5),
        "b_forget": jax.random.uniform(ks[5], (DEPTH, FOX_HEADS), f32, 2.0, 6.0),
        "fox_q_norm_g": gain(ks[6], (DEPTH, HEAD_DIM)),
        "fox_k_norm_g": gain(ks[7], (DEPTH, HEAD_DIM)),
        "lru_conv_w": nrm(ks[8], (DEPTH, LRU_CONV, LRU_W), LRU_CONV ** -0.5),
        "lru_conv_b": nrm(ks[9], (DEPTH, LRU_W), 0.01),
        "lru_w_a": nrm(ks[10], (DEPTH, LRU_BLOCKS, LRU_BW, LRU_BW), LRU_BW ** -0.5),
        "lru_b_a": nrm(ks[11], (DEPTH, LRU_W), 0.01),
        "lru_w_x": nrm(ks[13], (DEPTH, LRU_BLOCKS, LRU_BW, LRU_BW), LRU_BW ** -0.5),
        "lru_b_x": nrm(ks[14], (DEPTH, LRU_W), 0.01),
        "lru_lambda": lru_lambda,
        "w_mem_kv": nrm(ks[15], (DEPTH, D_MODEL, 2 * MEM_W), D_MODEL ** -0.5),
        "mem_q_norm_g": gain(ks[16], (DEPTH, MEM_HD)),
        "mem_k_norm_g": gain(ks[17], (DEPTH, MEM_HD)),
        "b_gate": nrm(ks[18], (DEPTH, N_BRANCH, D_MODEL), 0.01),
        "w_branch": nrm(ks[19], (DEPTH, N_BRANCH, BRANCH_W, D_MODEL), BRANCH_W ** -0.5),
        "w_out": nrm(ks[20], (DEPTH, D_MODEL, D_MODEL), D_MODEL ** -0.5 * res_scale),
        "ffn_norm_g": gain(ks[21], (DEPTH, D_MODEL)),
        "w_up": nrm(ks[22], (DEPTH, D_MODEL, 2 * D_FF), D_MODEL ** -0.5),
        "ffn_conv_w": nrm(ks[23], (DEPTH, FFN_CONV, D_FF), FFN_CONV ** -0.5),
        "ffn_conv_b": nrm(ks[24], (DEPTH, D_FF), 0.01),
        "w_down": nrm(ks[25], (DEPTH, D_FF, D_MODEL), D_FF ** -0.5 * res_scale),
    }


def reference(x, mem, attn_norm_g, mem_norm_g, w_in, b_forget, fox_q_norm_g, fox_k_norm_g,
              lru_conv_w, lru_conv_b, lru_w_a, lru_b_a, lru_w_x, lru_b_x, lru_lambda,
              w_mem_kv, mem_q_norm_g, mem_k_norm_g, b_gate, w_branch, w_out,
              ffn_norm_g, w_up, ffn_conv_w, ffn_conv_b, w_down):
    f32 = jnp.float32
    cols = _column_slices()
    b, s, _ = x.shape
    for l in range(DEPTH):
        h = _rms(x, attn_norm_g[l])
        w_l = w_in[l]

        def proj(name):
            lo, hi = cols[name]
            return h @ w_l[:, lo:hi]

        fq = _rms(_heads(proj("fox_q"), FOX_HEADS), fox_q_norm_g[l]).astype(f32)
        fk = _rms(_heads(proj("fox_k"), FOX_HEADS), fox_k_norm_g[l]).astype(f32)
        fv = _heads(proj("fox_v"), FOX_HEADS).astype(f32)
        log_f = jax.nn.log_sigmoid(proj("fox_f").astype(f32) + b_forget[l]).transpose(0, 2, 1)
        y_fox = _merge_heads(_fox_attention(fq, fk, fv, log_f))

        y_lru = _rg_lru_branch(proj("lru_x"), proj("lru_g"), lru_conv_w[l], lru_conv_b[l],
                               lru_w_a[l], lru_b_a[l], lru_w_x[l], lru_b_x[l], lru_lambda[l])

        sq = _heads(proj("sb_q"), SB_HEADS).astype(f32)
        sk = _heads(proj("sb_k"), SB_HEADS).astype(f32)
        sv = _heads(proj("sb_v"), SB_HEADS).astype(f32)
        y_sb = _merge_heads(_stick_breaking_attention(sq, sk, sv))

        mem_n = _rms(mem, mem_norm_g[l])
        w_kv = w_mem_kv[l]
        mk = _rms(_heads(mem_n @ w_kv[:, :MEM_W], MEM_HEADS), mem_k_norm_g[l]).astype(f32)
        mv = _heads(mem_n @ w_kv[:, MEM_W:], MEM_HEADS).astype(f32)
        mq = _rms(_heads(proj("mem_q"), MEM_HEADS), mem_q_norm_g[l]).astype(f32)
        y_mem = _merge_heads(_mem_attention(mq, mk, mv))

        branches = jnp.stack([y_fox, y_lru, y_sb, y_mem], axis=2).astype(x.dtype)
        gates = jax.nn.sigmoid(proj("gates").reshape(b, s, N_BRANCH, D_MODEL) + b_gate[l])
        projected = jnp.einsum('bsnc,ncd->bsnd', branches, w_branch[l])
        mixed = jnp.sum(gates * projected, axis=2)
        x = x + mixed @ w_out[l]

        h2 = _rms(x, ffn_norm_g[l])
        w_u = w_up[l]
        gate_pre = _causal_dwconv(h2 @ w_u[:, :D_FF], ffn_conv_w[l], ffn_conv_b[l])
        val = h2 @ w_u[:, D_FF:]
        x = x + (jax.nn.silu(gate_pre) * val) @ w_down[l]
    return x
```

```python
import functools

import jax
import jax.numpy as jnp
from jax import lax
from jax.experimental import pallas as pl
from jax.experimental.pallas import tpu as pltpu

F32 = jnp.float32
BF16 = jnp.bfloat16
HIGHEST = lax.Precision.HIGHEST

D_MODEL = 1024
HEAD_DIM = 64
N_HEADS = 16
N_PAIRS = N_HEADS // 2
LANES = 128
N_MEM = 256
MEM_HEADS = 4
MEM_HD = 256
D_FF = 2816
LRU_C = 8.0
EPS = 1e-6
NEG = -1e30

C_GT, C_FQ, C_FK, C_FV, C_LX, C_LG, C_SQ, C_SK, C_SV, C_MQ = (
    0, 4096, 5120, 6144, 7168, 8192, 9216, 10240, 11264, 12288)
N_Y = 13312
W_FOX_END, W_F_END, W_GATES = 3072, 3088, 9232

VMEM_LIMIT = 56 * 1024 * 1024


def _cparams(n_axes):
    return pltpu.CompilerParams(dimension_semantics=("arbitrary",) * n_axes,
                                vmem_limit_bytes=VMEM_LIMIT)


def _dot(a, b):
    return jnp.dot(a, b, preferred_element_type=F32)


def _dot_nt(a, b):
    return lax.dot_general(a, b, (((1,), (1,)), ((), ())), preferred_element_type=F32)


def _log1p_exp_neg_abs(z):
    return jnp.log(1.0 + jnp.exp(-jnp.abs(z)))


def _proj_kernel(x_ref, g_ref, w_ref, wf_ref, y_ref, f_ref, h_ref):
    @pl.when(pl.program_id(1) == 0)
    def _():
        x = x_ref[...]
        h = x * lax.rsqrt(jnp.mean(x * x, axis=-1, keepdims=True) + EPS) * g_ref[...]
        h_ref[...] = h.astype(BF16)
        f_ref[...] = jnp.dot(h, wf_ref[...], precision=HIGHEST, preferred_element_type=F32)

    y_ref[...] = _dot(h_ref[...], w_ref[...]).astype(BF16)


def _proj(x2d, g, wy, wf, tm=1024, tn=1024):
    m = x2d.shape[0]
    return pl.pallas_call(
        _proj_kernel,
        grid=(m // tm, N_Y // tn),
        in_specs=[pl.BlockSpec((tm, D_MODEL), lambda i, j: (i, 0)),
                  pl.BlockSpec((1, D_MODEL), lambda i, j: (0, 0)),
                  pl.BlockSpec((D_MODEL, tn), lambda i, j: (0, j)),
                  pl.BlockSpec((D_MODEL, LANES), lambda i, j: (0, 0))],
        out_specs=[pl.BlockSpec((tm, tn), lambda i, j: (i, j)),
                   pl.BlockSpec((tm, LANES), lambda i, j: (i, 0))],
        out_shape=[jax.ShapeDtypeStruct((m, N_Y), BF16),
                   jax.ShapeDtypeStruct((m, LANES), F32)],
        scratch_shapes=[pltpu.VMEM((tm, D_MODEL), BF16)],
        compiler_params=_cparams(2),
        name="proj",
    )(x2d, g, wy, wf)


def _fcum_kernel(f_ref, b_ref, o_ref, carry_ref):
    @pl.when(pl.program_id(1) == 0)
    def _():
        carry_ref[...] = jnp.zeros_like(carry_ref)

    z = f_ref[0] + b_ref[...]
    log_f = jnp.minimum(z, 0.0) - _log1p_exp_neg_abs(z)
    ts = z.shape[0]
    row = lax.broadcasted_iota(jnp.int32, (ts, ts), 0)
    col = lax.broadcasted_iota(jnp.int32, (ts, ts), 1)
    tri = (row >= col).astype(F32)
    c = jnp.dot(tri, log_f, precision=HIGHEST, preferred_element_type=F32) + carry_ref[...]
    o_ref[0] = c
    carry_ref[...] = c[ts - 1:ts, :]


def _fcum(f3d, b_pad, ts=512):
    b, s, _ = f3d.shape
    return pl.pallas_call(
        _fcum_kernel,
        grid=(b, s // ts),
        in_specs=[pl.BlockSpec((1, ts, LANES), lambda i, j: (i, j, 0)),
                  pl.BlockSpec((1, LANES), lambda i, j: (0, 0))],
        out_specs=pl.BlockSpec((1, ts, LANES), lambda i, j: (i, j, 0)),
        out_shape=jax.ShapeDtypeStruct((b, s, LANES), F32),
        scratch_shapes=[pltpu.VMEM((1, LANES), F32)],
        compiler_params=_cparams(2),
        name="fcum",
    )(f3d, b_pad)


def _foxprep_kernel(yq_ref, yk_ref, f_ref, gq_ref, gk_ref, qa_ref, ka_ref):
    hp = pl.program_id(1)
    ts = yq_ref.shape[1]
    lane = lax.broadcasted_iota(jnp.int32, (ts, LANES), 1)
    lo_half = lane < HEAD_DIM

    def headnorm(y, g):
        y2 = y * y
        ss_a = jnp.sum(jnp.where(lo_half, y2, 0.0), axis=-1, keepdims=True)
        ss_b = jnp.sum(jnp.where(lo_half, 0.0, y2), axis=-1, keepdims=True)
        ms = jnp.where(lo_half, ss_a, ss_b) * (1.0 / HEAD_DIM)
        return y * lax.rsqrt(ms + EPS) * g

    qn = headnorm(yq_ref[0].astype(F32), gq_ref[...]) * (HEAD_DIM ** -0.5)
    kn = headnorm(yk_ref[0].astype(F32), gk_ref[...])

    f = f_ref[0]
    f_hi = f.astype(BF16)
    r1 = f - f_hi.astype(F32)
    f_mid = r1.astype(BF16)
    f_lo = (r1 - f_mid.astype(F32)).astype(BF16)
    pieces = (f_hi, f_mid, f_lo)

    sel_r = lax.broadcasted_iota(jnp.int32, (LANES, LANES), 0)
    sel_c = lax.broadcasted_iota(jnp.int32, (LANES, LANES), 1)

    def place(head, first_col):
        out = jnp.zeros((ts, LANES), F32)
        for c, piece in enumerate(pieces):
            sel = ((sel_r == head) & (sel_c == first_col + c)).astype(BF16)
            out = out + _dot(piece, sel)
        return out

    ones_q = ((lane >= HEAD_DIM + 3) & (lane < HEAD_DIM + 6)).astype(F32)
    ones_k = ((lane >= HEAD_DIM) & (lane < HEAD_DIM + 3)).astype(F32)
    for e in range(2):
        head = 2 * hp + e
        q_part = qn if e == 0 else pltpu.roll(qn, HEAD_DIM, axis=1)
        k_part = kn if e == 0 else pltpu.roll(kn, HEAD_DIM, axis=1)
        q_aug = jnp.where(lo_half, q_part, place(head, HEAD_DIM) + ones_q)
        k_aug = jnp.where(lo_half, k_part, ones_k - place(head, HEAD_DIM + 3))
        qa_ref[0, e] = q_aug.astype(BF16)
        ka_ref[0, e] = k_aug.astype(BF16)


def _foxprep(y3d, fcum, gq2, gk2, ts=512):
    b, s, _ = y3d.shape
    out = jax.ShapeDtypeStruct((b, N_HEADS, s, LANES), BF16)
    return pl.pallas_call(
        _foxprep_kernel,
        grid=(b, N_PAIRS, s // ts),
        in_specs=[pl.BlockSpec((1, ts, LANES), lambda i, p, j: (i, j, C_FQ // LANES + p)),
                  pl.BlockSpec((1, ts, LANES), lambda i, p, j: (i, j, C_FK // LANES + p)),
                  pl.BlockSpec((1, ts, LANES), lambda i, p, j: (i, j, 0)),
                  pl.BlockSpec((1, LANES), lambda i, p, j: (0, 0)),
                  pl.BlockSpec((1, LANES), lambda i, p, j: (0, 0))],
        out_specs=[pl.BlockSpec((1, 2, ts, LANES), lambda i, p, j: (i, p, j, 0)),
                   pl.BlockSpec((1, 2, ts, LANES), lambda i, p, j: (i, p, j, 0))],
        out_shape=[out, out],
        compiler_params=_cparams(3),
        name="foxprep",
    )(y3d, y3d, fcum, gq2, gk2)


def _fox_kernel(qa_ref, ka_ref, v_ref, o_ref, *, tq):
    i = pl.program_id(2)
    row = lax.broadcasted_iota(jnp.int32, (tq, tq), 0)
    col = lax.broadcasted_iota(jnp.int32, (tq, tq), 1)
    causal = col <= row
    lane = lax.broadcasted_iota(jnp.int32, (tq, LANES), 1)

    def head_out(e):
        qa = qa_ref[0, e]

        def step(j, carry, masked):
            m, l, acc = carry
            start = pl.multiple_of(j * tq, tq)
            k = ka_ref[0, e, pl.ds(start, tq), :]
            v = v_ref[0, pl.ds(start, tq), :]
            s = _dot_nt(qa, k)
            if masked:
                s = jnp.where(causal, s, NEG)
            m_new = jnp.maximum(m, jnp.max(s, axis=-1, keepdims=True))
            alpha = jnp.exp(m - m_new)
            p = jnp.exp(s - m_new)
            l = alpha * l + jnp.sum(p, axis=-1, keepdims=True)
            acc = alpha * acc + _dot(p.astype(BF16), v)
            return m_new, l, acc

        init = (jnp.full((tq, 1), NEG, F32), jnp.zeros((tq, 1), F32),
                jnp.zeros((tq, LANES), F32))
        carry = lax.fori_loop(0, i, functools.partial(step, masked=False), init)
        _, l, acc = step(i, carry, True)
        return acc / l

    o_ref[0] = jnp.where(lane < HEAD_DIM, head_out(0), head_out(1)).astype(BF16)


def _fox(qa, ka, y3d, tq=256):
    b, _, s, _ = qa.shape
    return pl.pallas_call(
        functools.partial(_fox_kernel, tq=tq),
        grid=(b, N_PAIRS, s // tq),
        in_specs=[pl.BlockSpec((1, 2, tq, LANES), lambda i, p, j: (i, p, j, 0)),
                  pl.BlockSpec((1, 2, s, LANES), lambda i, p, j: (i, p, 0, 0)),
                  pl.BlockSpec((1, s, LANES), lambda i, p, j: (i, 0, C_FV // LANES + p))],
        out_specs=pl.BlockSpec((1, tq, LANES), lambda i, p, j: (i, j, p)),
        out_shape=jax.ShapeDtypeStruct((b, s, D_MODEL), BF16),
        compiler_params=_cparams(3),
        name="fox",
    )(qa, ka, y3d)


def _sb_kernel(q_ref, k_ref, v_ref, o_ref, *, tq):
    i = pl.program_id(2)
    row = lax.broadcasted_iota(jnp.int32, (tq, tq), 0)
    col = lax.broadcasted_iota(jnp.int32, (tq, tq), 1)
    causal = col < row
    later = (row > col).astype(BF16)
    lane = lax.broadcasted_iota(jnp.int32, (tq, LANES), 1)
    q2 = q_ref[0] * (HEAD_DIM ** -0.5)

    def head_out(e):
        mine = (lane < HEAD_DIM) if e == 0 else (lane >= HEAD_DIM)
        q = jnp.where(mine, q2, 0.0).astype(BF16)

        def step(j, carry, masked):
            r, acc = carry
            start = pl.multiple_of(j * tq, tq)
            k = k_ref[0, pl.ds(start, tq), :]
            v = v_ref[0, pl.ds(start, tq), :]
            z = _dot_nt(q, k)
            lp = _log1p_exp_neg_abs(z)
            log_1m_beta = -jnp.maximum(z, 0.0) - lp
            log_beta = jnp.minimum(z, 0.0) - lp
            if masked:
                log_1m_beta = jnp.where(causal, log_1m_beta, 0.0)
            hi = log_1m_beta.astype(BF16)
            lo = (log_1m_beta - hi.astype(F32)).astype(BF16)
            excl = _dot(hi, later) + _dot(lo, later)
            log_w = log_beta + excl + r
            if masked:
                log_w = jnp.where(causal, log_w, NEG)
            w = jnp.exp(log_w)
            acc = acc + _dot(w.astype(BF16), v)
            r = r + jnp.sum(log_1m_beta, axis=-1, keepdims=True)
            return r, acc

        carry = step(i, (jnp.zeros((tq, 1), F32), jnp.zeros((tq, LANES), F32)), True)
        _, acc = lax.fori_loop(
            0, i, lambda t, c: step(i - 1 - t, c, False), carry)
        return acc

    o_ref[0] = jnp.where(lane < HEAD_DIM, head_out(0), head_out(1)).astype(BF16)


def _sb(y3d, tq=256):
    b, s, _ = y3d.shape
    return pl.pallas_call(
        functools.partial(_sb_kernel, tq=tq),
        grid=(b, N_PAIRS, s // tq),
        in_specs=[pl.BlockSpec((1, tq, LANES), lambda i, p, j: (i, j, C_SQ // LANES + p)),
                  pl.BlockSpec((1, s, LANES), lambda i, p, j: (i, 0, C_SK // LANES + p)),
                  pl.BlockSpec((1, s, LANES), lambda i, p, j: (i, 0, C_SV // LANES + p))],
        out_specs=pl.BlockSpec((1, tq, LANES), lambda i, p, j: (i, j, p)),
        out_shape=jax.ShapeDtypeStruct((b, s, D_MODEL), BF16),
        compiler_params=_cparams(3),
        name="sb",
    )(y3d, y3d, y3d)


LRU_TC = 256
LRU_HALO = 8


def _lru_kernel(x_ref, g_ref, cw_ref, cb_ref, wa_ref, ba_ref, wx_ref, bx_ref, lam_ref,
                o_ref, xbuf_ref, h_ref, *, ts):
    @pl.when(pl.program_id(2) == 0)
    def _():
        xbuf_ref[0:LRU_HALO, :] = jnp.zeros((LRU_HALO, LRU_TC), F32)
        h_ref[...] = jnp.zeros_like(h_ref)

    xbuf_ref[LRU_HALO:LRU_HALO + ts, :] = x_ref[0].astype(F32)
    cw = cw_ref[...]
    xc = cb_ref[...] + xbuf_ref[LRU_HALO:LRU_HALO + ts, :] * cw[3:4, :]
    for d in range(1, 4):
        xc = xc + xbuf_ref[LRU_HALO - d:LRU_HALO - d + ts, :] * cw[3 - d:4 - d, :]
    xbuf_ref[0:LRU_HALO, :] = xbuf_ref[ts:ts + LRU_HALO, :]

    xcb = xc.astype(BF16)
    r = jax.nn.sigmoid(_dot(xcb, wa_ref[0]) + ba_ref[...])
    gi = jax.nn.sigmoid(_dot(xcb, wx_ref[0]) + bx_ref[...])
    lam = lam_ref[...]
    softplus_neg_lam = jnp.maximum(-lam, 0.0) + _log1p_exp_neg_abs(lam)
    log_a = -LRU_C * r * softplus_neg_lam
    a = jnp.exp(log_a)
    u = jnp.sqrt(-jnp.tanh(log_a) * (a * a + 1.0)) * (gi * xc)

    row = lax.broadcasted_iota(jnp.int32, (ts, LRU_TC), 0)
    d = 1
    while d < ts:
        keep = row >= d
        a_sh = jnp.where(keep, pltpu.roll(a, d, axis=0), 1.0)
        u_sh = jnp.where(keep, pltpu.roll(u, d, axis=0), 0.0)
        u = a * u_sh + u
        a = a * a_sh
        d *= 2
    h = u + a * h_ref[...]
    h_ref[...] = h[ts - 1:ts, :]
    o_ref[0] = (h * jax.nn.gelu(g_ref[0].astype(F32))).astype(BF16)


def _lru(y3d, cw, cb, wa_bd, ba, wx_bd, bx, lam, ts=256):
    b, s, _ = y3d.shape
    nct = D_MODEL // LRU_TC
    vec = lambda rows: pl.BlockSpec((rows, LRU_TC), lambda i, c, j: (0, c))
    mat = pl.BlockSpec((1, LRU_TC, LRU_TC), lambda i, c, j: (c, 0, 0))
    return pl.pallas_call(
        functools.partial(_lru_kernel, ts=ts),
        grid=(b, nct, s // ts),
        in_specs=[pl.BlockSpec((1, ts, LRU_TC), lambda i, c, j: (i, j, C_LX // LRU_TC + c)),
                  pl.BlockSpec((1, ts, LRU_TC), lambda i, c, j: (i, j, C_LG // LRU_TC + c)),
                  vec(4), vec(1), mat, vec(1), mat, vec(1), vec(1)],
        out_specs=pl.BlockSpec((1, ts, LRU_TC), lambda i, c, j: (i, j, c)),
        out_shape=jax.ShapeDtypeStruct((b, s, D_MODEL), BF16),
        scratch_shapes=[pltpu.VMEM((ts + LRU_HALO, LRU_TC), F32),
                        pltpu.VMEM((1, LRU_TC), F32)],
        compiler_params=_cparams(3),
        name="lru",
    )(y3d, y3d, cw, cb, wa_bd, ba, wx_bd, bx, lam)


def _memkv_kernel(mem_ref, g_ref, w_ref, gk_ref, k_ref, v_ref):
    x = mem_ref[0]
    h = x * lax.rsqrt(jnp.mean(x * x, axis=-1, keepdims=True) + EPS) * g_ref[...]
    kv = _dot(h.astype(BF16), w_ref[...])
    for hd in range(MEM_HEADS):
        kh = kv[:, hd * MEM_HD:(hd + 1) * MEM_HD]
        kh = kh * lax.rsqrt(jnp.mean(kh * kh, axis=-1, keepdims=True) + EPS) * gk_ref[...]
        k_ref[0, :, hd * MEM_HD:(hd + 1) * MEM_HD] = kh.astype(BF16)
    v_ref[0] = kv[:, D_MODEL:].astype(BF16)


def _memkv(mem, g, w_kv, gk):
    b = mem.shape[0]
    out = jax.ShapeDtypeStruct((b, N_MEM, D_MODEL), BF16)
    return pl.pallas_call(
        _memkv_kernel,
        grid=(b,),
        in_specs=[pl.BlockSpec((1, N_MEM, D_MODEL), lambda i: (i, 0, 0)),
                  pl.BlockSpec((1, D_MODEL), lambda i: (0, 0)),
                  pl.BlockSpec((D_MODEL, 2 * D_MODEL), lambda i: (0, 0)),
                  pl.BlockSpec((1, MEM_HD), lambda i: (0, 0))],
        out_specs=[pl.BlockSpec((1, N_MEM, D_MODEL), lambda i: (i, 0, 0)),
                   pl.BlockSpec((1, N_MEM, D_MODEL), lambda i: (i, 0, 0))],
        out_shape=[out, out],
        compiler_params=_cparams(1),
        name="memkv",
    )(mem, g, w_kv, gk)


def _mem_kernel(q_ref, k_ref, v_ref, gq_ref, o_ref):
    for hd in range(MEM_HEADS):
        sl = slice(hd * MEM_HD, (hd + 1) * MEM_HD)
        q = q_ref[0, :, sl].astype(F32)
        q = q * lax.rsqrt(jnp.mean(q * q, axis=-1, keepdims=True) + EPS) * gq_ref[...]
        q = (q * (MEM_HD ** -0.5)).astype(BF16)
        s = _dot_nt(q, k_ref[0, :, sl])
        p = jnp.exp(s - jnp.max(s, axis=-1, keepdims=True))
        o = _dot(p.astype(BF16), v_ref[0, :, sl]) / jnp.sum(p, axis=-1, keepdims=True)
        o_ref[0, :, sl] = o.astype(BF16)


def _mem(y3d, mk, mv, gq, ts=512):
    b, s, _ = y3d.shape
    return pl.pallas_call(
        _mem_kernel,
        grid=(b, s // ts),
        in_specs=[pl.BlockSpec((1, ts, D_MODEL), lambda i, j: (i, j, C_MQ // D_MODEL)),
                  pl.BlockSpec((1, N_MEM, D_MODEL), lambda i, j: (i, 0, 0)),
                  pl.BlockSpec((1, N_MEM, D_MODEL), lambda i, j: (i, 0, 0)),
                  pl.BlockSpec((1, MEM_HD), lambda i, j: (0, 0))],
        out_specs=pl.BlockSpec((1, ts, D_MODEL), lambda i, j: (i, j, 0)),
        out_shape=jax.ShapeDtypeStruct((b, s, D_MODEL), BF16),
        compiler_params=_cparams(2),
        name="mem",
    )(y3d, mk, mv, gq)


def _merge_kernel(x_ref, b0_ref, b1_ref, b2_ref, b3_ref, gt_ref, bg_ref, wb_ref, wo_ref, o_ref):
    mixed = None
    for n, br in enumerate((b0_ref, b1_ref, b2_ref, b3_ref)):
        gate = jax.nn.sigmoid(
            gt_ref[:, n * D_MODEL:(n + 1) * D_MODEL].astype(F32) + bg_ref[n:n + 1, :])
        term = gate * _dot(br[...], wb_ref[n])
        mixed = term if mixed is None else mixed + term
    o_ref[...] = x_ref[...] + _dot(mixed.astype(BF16), wo_ref[...])


def _merge(x2d, branches, y2d, bg, wb, wo, tm=256):
    m = x2d.shape[0]
    row = lambda w: pl.BlockSpec((tm, w), lambda i: (i, 0))
    return pl.pallas_call(
        _merge_kernel,
        grid=(m // tm,),
        in_specs=[row(D_MODEL), row(D_MODEL), row(D_MODEL), row(D_MODEL), row(D_MODEL),
                  pl.BlockSpec((tm, 4 * D_MODEL), lambda i: (i, C_GT // (4 * D_MODEL))),
                  pl.BlockSpec((4, D_MODEL), lambda i: (0, 0)),
                  pl.BlockSpec((4, D_MODEL, D_MODEL), lambda i: (0, 0, 0)),
                  pl.BlockSpec((D_MODEL, D_MODEL), lambda i: (0, 0))],
        out_specs=row(D_MODEL),
        out_shape=jax.ShapeDtypeStruct((m, D_MODEL), F32),
        compiler_params=_cparams(1),
        name="merge",
    )(x2d, *branches, y2d, bg, wb, wo)


FFN_TF = 256
FFN_HALO = 8


def _ffn_kernel(x_ref, g_ref, wg_ref, wv_ref, cw_ref, cb_ref, wd_ref, o_ref,
                h_ref, acc_ref, gbuf_ref, tail_ref, *, tm, tiles_per_seq):
    i = pl.program_id(0)
    f = pl.program_id(1)
    nf = pl.num_programs(1)

    @pl.when(f == 0)
    def _():
        x = x_ref[...]
        h = x * lax.rsqrt(jnp.mean(x * x, axis=-1, keepdims=True) + EPS) * g_ref[...]
        h_ref[...] = h.astype(BF16)
        acc_ref[...] = x

    h = h_ref[...]
    gbuf_ref[FFN_HALO:FFN_HALO + tm, :] = _dot(h, wg_ref[...])

    @pl.when((i % tiles_per_seq) == 0)
    def _():
        tail_ref[f] = jnp.zeros((FFN_HALO, FFN_TF), F32)

    gbuf_ref[0:FFN_HALO, :] = tail_ref[f]
    cw = cw_ref[...]
    gate = cb_ref[...] + gbuf_ref[FFN_HALO:FFN_HALO + tm, :] * cw[2:3, :]
    for d in range(1, 3):
        gate = gate + gbuf_ref[FFN_HALO - d:FFN_HALO - d + tm, :] * cw[2 - d:3 - d, :]
    tail_ref[f] = gbuf_ref[tm:tm + FFN_HALO, :]
    act = gate * jax.nn.sigmoid(gate) * _dot(h, wv_ref[...])
    acc_ref[...] += _dot(act.astype(BF16), wd_ref[...])

    @pl.when(f == nf - 1)
    def _():
        o_ref[...] = acc_ref[...]


def _ffn(x2d, g, w_up, cw, cb, w_down, seq, tm=1024):
    m = x2d.shape[0]
    nf = D_FF // FFN_TF
    return pl.pallas_call(
        functools.partial(_ffn_kernel, tm=tm, tiles_per_seq=seq // tm),
        grid=(m // tm, nf),
        in_specs=[pl.BlockSpec((tm, D_MODEL), lambda i, f: (i, 0)),
                  pl.BlockSpec((1, D_MODEL), lambda i, f: (0, 0)),
                  pl.BlockSpec((D_MODEL, FFN_TF), lambda i, f: (0, f)),
                  pl.BlockSpec((D_MODEL, FFN_TF), lambda i, f: (0, nf + f)),
                  pl.BlockSpec((3, FFN_TF), lambda i, f: (0, f)),
                  pl.BlockSpec((1, FFN_TF), lambda i, f: (0, f)),
                  pl.BlockSpec((FFN_TF, D_MODEL), lambda i, f: (f, 0))],
        out_specs=pl.BlockSpec((tm, D_MODEL), lambda i, f: (i, 0)),
        out_shape=jax.ShapeDtypeStruct((m, D_MODEL), F32),
        scratch_shapes=[pltpu.VMEM((tm, D_MODEL), BF16),
                        pltpu.VMEM((tm, D_MODEL), F32),
                        pltpu.VMEM((tm + FFN_HALO, FFN_TF), F32),
                        pltpu.VMEM((nf, FFN_HALO, FFN_TF), F32)],
        compiler_params=_cparams(2),
        name="ffn",
    )(x2d, g, w_up, w_up, cw, cb, w_down)


def _block_diag(w):
    per = LRU_TC // HEAD_DIM
    w4 = w.reshape(D_MODEL // LRU_TC, per, HEAD_DIM, HEAD_DIM)
    eye = jnp.eye(per, dtype=w.dtype)
    bd = jnp.einsum('cpde,pq->cpdqe', w4, eye)
    return bd.reshape(D_MODEL // LRU_TC, LRU_TC, LRU_TC)


def kernel(x, mem, attn_norm_g, mem_norm_g, w_in, b_forget, fox_q_norm_g, fox_k_norm_g,
           lru_conv_w, lru_conv_b, lru_w_a, lru_b_a, lru_w_x, lru_b_x, lru_lambda,
           w_mem_kv, mem_q_norm_g, mem_k_norm_g, b_gate, w_branch, w_out,
           ffn_norm_g, w_up, ffn_conv_w, ffn_conv_b, w_down):
    b, s, d = x.shape
    depth = w_in.shape[0]
    m = b * s
    x2d = x.reshape(m, d)
    for l in range(depth):
        wy = jnp.concatenate([w_in[l, :, W_GATES:], w_in[l, :, :W_FOX_END],
                              w_in[l, :, W_F_END:W_GATES]], axis=1).astype(BF16)
        wf = jnp.pad(w_in[l, :, W_FOX_END:W_F_END], ((0, 0), (0, LANES - N_HEADS)))
        bf = jnp.pad(b_forget[l], (0, LANES - N_HEADS)).reshape(1, LANES)

        y2d, f2d = _proj(x2d, attn_norm_g[l].reshape(1, d), wy, wf)
        y3d = y2d.reshape(b, s, N_Y)

        fcum = _fcum(f2d.reshape(b, s, LANES), bf)
        qa, ka = _foxprep(y3d, fcum,
                          jnp.tile(fox_q_norm_g[l], 2).reshape(1, LANES),
                          jnp.tile(fox_k_norm_g[l], 2).reshape(1, LANES))
        y_fox = _fox(qa, ka, y3d)
        y_sb = _sb(y3d)
        y_lru = _lru(y3d, lru_conv_w[l], lru_conv_b[l].reshape(1, d),
                     _block_diag(lru_w_a[l]).astype(BF16), lru_b_a[l].reshape(1, d),
                     _block_diag(lru_w_x[l]).astype(BF16), lru_b_x[l].reshape(1, d),
                     lru_lambda[l].reshape(1, d))
        mk, mv = _memkv(mem, mem_norm_g[l].reshape(1, d), w_mem_kv[l].astype(BF16),
                        mem_k_norm_g[l].reshape(1, MEM_HD))
        y_mem = _mem(y3d, mk, mv, mem_q_norm_g[l].reshape(1, MEM_HD))

        branches = [t.reshape(m, d) for t in (y_fox, y_lru, y_sb, y_mem)]
        x2d = _merge(x2d, branches, y2d, b_gate[l], w_branch[l].astype(BF16),
                     w_out[l].astype(BF16))
        x2d = _ffn(x2d, ffn_norm_g[l].reshape(1, d), w_up[l].astype(BF16), ffn_conv_w[l],
                   ffn_conv_b[l].reshape(1, D_FF), w_down[l].astype(BF16), s)
    return x2d.reshape(b, s, d)
```

```python
import functools

import jax
import jax.numpy as jnp
from jax import lax
from jax.experimental import pallas as pl
from jax.experimental.pallas import tpu as pltpu

F32 = jnp.float32
BF16 = jnp.bfloat16
HIGHEST = lax.Precision.HIGHEST

D_MODEL = 1024
HEAD_DIM = 64
N_HEADS = 16
N_PAIRS = N_HEADS // 2
LANES = 128
N_MEM = 256
MEM_HEADS = 4
MEM_HD = 256
D_FF = 2816
LRU_C = 8.0
EPS = 1e-6
NEG = -1e30
LOG2E = 1.4426950408889634

C_GT, C_FQ, C_FK, C_FV, C_LX, C_LG, C_SQ, C_SK, C_SV, C_MQ = (
    0, 4096, 5120, 6144, 7168, 8192, 9216, 10240, 11264, 12288)
N_Y = 13312
W_FOX_END, W_F_END, W_GATES = 3072, 3088, 9232

VMEM_LIMIT = 56 * 1024 * 1024


def _cparams(n_axes):
    return pltpu.CompilerParams(dimension_semantics=("arbitrary",) * n_axes,
                                vmem_limit_bytes=VMEM_LIMIT)


def _dot(a, b):
    return jnp.dot(a, b, preferred_element_type=F32)


def _dot_nt(a, b):
    return lax.dot_general(a, b, (((1,), (1,)), ((), ())), preferred_element_type=F32)


def _log1p_exp_neg_abs(z):
    return jnp.log(1.0 + jnp.exp(-jnp.abs(z)))


def _proj_kernel(x_ref, g_ref, w_ref, wf_ref, y_ref, f_ref, h_ref):
    @pl.when(pl.program_id(1) == 0)
    def _():
        x = x_ref[...]
        h = x * lax.rsqrt(jnp.mean(x * x, axis=-1, keepdims=True) + EPS) * g_ref[...]
        h_ref[...] = h.astype(BF16)
        f_ref[...] = jnp.dot(h, wf_ref[...], precision=HIGHEST, preferred_element_type=F32)

    y_ref[...] = _dot(h_ref[...], w_ref[...]).astype(BF16)


def _proj(x2d, g, wy, wf, tm=1024, tn=1024):
    m = x2d.shape[0]
    return pl.pallas_call(
        _proj_kernel,
        grid=(m // tm, N_Y // tn),
        in_specs=[pl.BlockSpec((tm, D_MODEL), lambda i, j: (i, 0)),
                  pl.BlockSpec((1, D_MODEL), lambda i, j: (0, 0)),
                  pl.BlockSpec((D_MODEL, tn), lambda i, j: (0, j)),
                  pl.BlockSpec((D_MODEL, LANES), lambda i, j: (0, 0))],
        out_specs=[pl.BlockSpec((tm, tn), lambda i, j: (i, j)),
                   pl.BlockSpec((tm, LANES), lambda i, j: (i, 0))],
        out_shape=[jax.ShapeDtypeStruct((m, N_Y), BF16),
                   jax.ShapeDtypeStruct((m, LANES), F32)],
        scratch_shapes=[pltpu.VMEM((tm, D_MODEL), BF16)],
        compiler_params=_cparams(2),
        name="proj",
    )(x2d, g, wy, wf)


def _fcum_kernel(f_ref, b_ref, o_ref, carry_ref):
    @pl.when(pl.program_id(1) == 0)
    def _():
        carry_ref[...] = jnp.zeros_like(carry_ref)

    z = f_ref[0] + b_ref[...]
    log_f = jnp.minimum(z, 0.0) - _log1p_exp_neg_abs(z)
    ts = z.shape[0]
    row = lax.broadcasted_iota(jnp.int32, (ts, ts), 0)
    col = lax.broadcasted_iota(jnp.int32, (ts, ts), 1)
    tri = (row >= col).astype(F32)
    c = jnp.dot(tri, log_f, precision=HIGHEST, preferred_element_type=F32) + carry_ref[...]
    o_ref[0] = c
    carry_ref[...] = c[ts - 1:ts, :]


def _fcum(f3d, b_pad, ts=512):
    b, s, _ = f3d.shape
    return pl.pallas_call(
        _fcum_kernel,
        grid=(b, s // ts),
        in_specs=[pl.BlockSpec((1, ts, LANES), lambda i, j: (i, j, 0)),
                  pl.BlockSpec((1, LANES), lambda i, j: (0, 0))],
        out_specs=pl.BlockSpec((1, ts, LANES), lambda i, j: (i, j, 0)),
        out_shape=jax.ShapeDtypeStruct((b, s, LANES), F32),
        scratch_shapes=[pltpu.VMEM((1, LANES), F32)],
        compiler_params=_cparams(2),
        name="fcum",
    )(f3d, b_pad)


def _foxprep_kernel(yq_ref, yk_ref, f_ref, gq_ref, gk_ref, qa_ref, ka_ref):
    hp = pl.program_id(1)
    ts = yq_ref.shape[1]
    lane = lax.broadcasted_iota(jnp.int32, (ts, LANES), 1)
    lo_half = lane < HEAD_DIM

    def headnorm(y, g):
        y2 = y * y
        ss_a = jnp.sum(jnp.where(lo_half, y2, 0.0), axis=-1, keepdims=True)
        ss_b = jnp.sum(jnp.where(lo_half, 0.0, y2), axis=-1, keepdims=True)
        ms = jnp.where(lo_half, ss_a, ss_b) * (1.0 / HEAD_DIM)
        return y * lax.rsqrt(ms + EPS) * g

    qn = headnorm(yq_ref[0].astype(F32), gq_ref[...]) * (HEAD_DIM ** -0.5 * LOG2E)
    kn = headnorm(yk_ref[0].astype(F32), gk_ref[...])

    f = f_ref[0] * LOG2E
    f_hi = f.astype(BF16)
    r1 = f - f_hi.astype(F32)
    f_mid = r1.astype(BF16)
    f_lo = (r1 - f_mid.astype(F32)).astype(BF16)
    pieces = (f_hi, f_mid, f_lo)

    sel_r = lax.broadcasted_iota(jnp.int32, (LANES, LANES), 0)
    sel_c = lax.broadcasted_iota(jnp.int32, (LANES, LANES), 1)

    def place(head, first_col):
        out = jnp.zeros((ts, LANES), F32)
        for c, piece in enumerate(pieces):
            sel = ((sel_r == head) & (sel_c == first_col + c)).astype(BF16)
            out = out + _dot(piece, sel)
        return out

    ones_q = ((lane >= HEAD_DIM + 3) & (lane < HEAD_DIM + 6)).astype(F32)
    ones_k = ((lane >= HEAD_DIM) & (lane < HEAD_DIM + 3)).astype(F32)
    for e in range(2):
        head = 2 * hp + e
        q_part = qn if e == 0 else pltpu.roll(qn, HEAD_DIM, axis=1)
        k_part = kn if e == 0 else pltpu.roll(kn, HEAD_DIM, axis=1)
        q_aug = jnp.where(lo_half, q_part, place(head, HEAD_DIM) + ones_q)
        k_aug = jnp.where(lo_half, k_part, ones_k - place(head, HEAD_DIM + 3))
        qa_ref[0, e] = q_aug.astype(BF16)
        ka_ref[0, e] = k_aug.astype(BF16)


def _foxprep(y3d, fcum, gq2, gk2, ts=512):
    b, s, _ = y3d.shape
    out = jax.ShapeDtypeStruct((b, N_HEADS, s, LANES), BF16)
    return pl.pallas_call(
        _foxprep_kernel,
        grid=(b, N_PAIRS, s // ts),
        in_specs=[pl.BlockSpec((1, ts, LANES), lambda i, p, j: (i, j, C_FQ // LANES + p)),
                  pl.BlockSpec((1, ts, LANES), lambda i, p, j: (i, j, C_FK // LANES + p)),
                  pl.BlockSpec((1, ts, LANES), lambda i, p, j: (i, j, 0)),
                  pl.BlockSpec((1, LANES), lambda i, p, j: (0, 0)),
                  pl.BlockSpec((1, LANES), lambda i, p, j: (0, 0))],
        out_specs=[pl.BlockSpec((1, 2, ts, LANES), lambda i, p, j: (i, p, j, 0)),
                   pl.BlockSpec((1, 2, ts, LANES), lambda i, p, j: (i, p, j, 0))],
        out_shape=[out, out],
        compiler_params=_cparams(3),
        name="foxprep",
    )(y3d, y3d, fcum, gq2, gk2)


def _fox_kernel(qa_ref, ka_ref, v_ref, o_ref, *, tq):
    i = pl.program_id(2)
    row = lax.broadcasted_iota(jnp.int32, (tq, tq), 0)
    col = lax.broadcasted_iota(jnp.int32, (tq, tq), 1)
    causal = col <= row
    lane = lax.broadcasted_iota(jnp.int32, (tq, LANES), 1)
    qa = (qa_ref[0, 0], qa_ref[0, 1])

    def step(j, carry, masked):
        start = pl.multiple_of(j * tq, tq)
        v = v_ref[0, pl.ds(start, tq), :]
        out = []
        for e in range(2):
            m, l, acc = carry[e]
            k = ka_ref[0, e, pl.ds(start, tq), :]
            s = _dot_nt(qa[e], k)
            if masked:
                s = jnp.where(causal, s, NEG)
            m_new = jnp.maximum(m, jnp.max(s, axis=-1, keepdims=True))
            alpha = jnp.exp2(m - m_new)
            p = jnp.exp2(s - m_new)
            l = alpha * l + jnp.sum(p, axis=-1, keepdims=True)
            acc = alpha * acc + _dot(p.astype(BF16), v)
            out.append((m_new, l, acc))
        return tuple(out)

    init1 = (jnp.full((tq, 1), NEG, F32), jnp.zeros((tq, 1), F32),
             jnp.zeros((tq, LANES), F32))
    carry = lax.fori_loop(0, i, functools.partial(step, masked=False), (init1, init1))
    (_, l0, acc0), (_, l1, acc1) = step(i, carry, True)
    o_ref[0] = jnp.where(lane < HEAD_DIM, acc0 / l0, acc1 / l1).astype(BF16)


def _fox(qa, ka, y3d, tq=512):
    b, _, s, _ = qa.shape
    return pl.pallas_call(
        functools.partial(_fox_kernel, tq=tq),
        grid=(b, N_PAIRS, s // tq),
        in_specs=[pl.BlockSpec((1, 2, tq, LANES), lambda i, p, j: (i, p, j, 0)),
                  pl.BlockSpec((1, 2, s, LANES), lambda i, p, j: (i, p, 0, 0)),
                  pl.BlockSpec((1, s, LANES), lambda i, p, j: (i, 0, C_FV // LANES + p))],
        out_specs=pl.BlockSpec((1, tq, LANES), lambda i, p, j: (i, j, p)),
        out_shape=jax.ShapeDtypeStruct((b, s, D_MODEL), BF16),
        compiler_params=_cparams(3),
        name="fox",
    )(qa, ka, y3d)


SB_KC = 256


def _sb_kernel(q_ref, k_ref, v_ref, o_ref, *, tq):
    i = pl.program_id(2)
    kc = SB_KC
    nsub = tq // kc
    row = lax.broadcasted_iota(jnp.int32, (tq, kc), 0)
    col = lax.broadcasted_iota(jnp.int32, (tq, kc), 1)
    kr = lax.broadcasted_iota(jnp.int32, (kc, kc), 0)
    kcol = lax.broadcasted_iota(jnp.int32, (kc, kc), 1)
    at_or_after = (kr >= kcol).astype(BF16)
    lane = lax.broadcasted_iota(jnp.int32, (tq, LANES), 1)
    q2 = q_ref[0]
    zero = jnp.zeros_like(q2)
    q = (jnp.where(lane < HEAD_DIM, q2, zero), jnp.where(lane >= HEAD_DIM, q2, zero))

    def step(start, carry, mask):
        k = k_ref[0, pl.ds(start, kc), :]
        v = v_ref[0, pl.ds(start, kc), :]
        out = []
        for e in range(2):
            r, acc = carry[e]
            z = _dot_nt(q[e], k)
            nz = -z
            log_1m_beta = jnp.minimum(nz, 0.0) - jnp.log2(1.0 + jnp.exp2(jnp.minimum(z, nz)))
            if mask is not None:
                log_1m_beta = jnp.where(mask, log_1m_beta, 0.0)
            hi = log_1m_beta.astype(BF16)
            lo = (log_1m_beta - hi.astype(F32)).astype(BF16)
            incl = _dot(hi, at_or_after) + _dot(lo, at_or_after)
            log_w = z + incl + r
            if mask is not None:
                log_w = jnp.where(mask, log_w, NEG)
            acc = acc + _dot(jnp.exp2(log_w).astype(BF16), v)
            r = r + jnp.sum(log_1m_beta, axis=-1, keepdims=True)
            out.append((r, acc))
        return tuple(out)

    init1 = (jnp.zeros((tq, 1), F32), jnp.zeros((tq, LANES), F32))
    carry = (init1, init1)
    for c in reversed(range(nsub)):
        carry = step(pl.multiple_of(i * tq + c * kc, kc), carry, (col + c * kc) < row)
    nfull = i * nsub
    carry = lax.fori_loop(
        0, nfull, lambda t, cr: step(pl.multiple_of((nfull - 1 - t) * kc, kc), cr, None), carry)
    o_ref[0] = jnp.where(lane < HEAD_DIM, carry[0][1], carry[1][1]).astype(BF16)


def _sb(y3d, tq=512):
    b, s, _ = y3d.shape
    return pl.pallas_call(
        functools.partial(_sb_kernel, tq=tq),
        grid=(b, N_PAIRS, s // tq),
        in_specs=[pl.BlockSpec((1, tq, LANES), lambda i, p, j: (i, j, C_SQ // LANES + p)),
                  pl.BlockSpec((1, s, LANES), lambda i, p, j: (i, 0, C_SK // LANES + p)),
                  pl.BlockSpec((1, s, LANES), lambda i, p, j: (i, 0, C_SV // LANES + p))],
        out_specs=pl.BlockSpec((1, tq, LANES), lambda i, p, j: (i, j, p)),
        out_shape=jax.ShapeDtypeStruct((b, s, D_MODEL), BF16),
        compiler_params=_cparams(3),
        name="sb",
    )(y3d, y3d, y3d)


LRU_TC = 256
LRU_HALO = 8


def _lru_kernel(x_ref, g_ref, cw_ref, cb_ref, wa_ref, ba_ref, wx_ref, bx_ref, lam_ref,
                o_ref, xbuf_ref, h_ref, *, ts):
    @pl.when(pl.program_id(2) == 0)
    def _():
        xbuf_ref[0:LRU_HALO, :] = jnp.zeros((LRU_HALO, LRU_TC), F32)
        h_ref[...] = jnp.zeros_like(h_ref)

    xbuf_ref[LRU_HALO:LRU_HALO + ts, :] = x_ref[0].astype(F32)
    cw = cw_ref[...]
    xc = cb_ref[...] + xbuf_ref[LRU_HALO:LRU_HALO + ts, :] * cw[3:4, :]
    for d in range(1, 4):
        xc = xc + xbuf_ref[LRU_HALO - d:LRU_HALO - d + ts, :] * cw[3 - d:4 - d, :]
    xbuf_ref[0:LRU_HALO, :] = xbuf_ref[ts:ts + LRU_HALO, :]

    xcb = xc.astype(BF16)
    r = jax.nn.sigmoid(_dot(xcb, wa_ref[0]) + ba_ref[...])
    gi = jax.nn.sigmoid(_dot(xcb, wx_ref[0]) + bx_ref[...])
    lam = lam_ref[...]
    softplus_neg_lam = jnp.maximum(-lam, 0.0) + _log1p_exp_neg_abs(lam)
    log_a = -LRU_C * r * softplus_neg_lam
    a = jnp.exp(log_a)
    u = jnp.sqrt(-jnp.tanh(log_a) * (a * a + 1.0)) * (gi * xc)

    row = lax.broadcasted_iota(jnp.int32, (ts, LRU_TC), 0)
    d = 1
    while d < ts:
        keep = row >= d
        a_sh = jnp.where(keep, pltpu.roll(a, d, axis=0), 1.0)
        u_sh = jnp.where(keep, pltpu.roll(u, d, axis=0), 0.0)
        u = a * u_sh + u
        a = a * a_sh
        d *= 2
    h = u + a * h_ref[...]
    h_ref[...] = h[ts - 1:ts, :]
    o_ref[0] = (h * jax.nn.gelu(g_ref[0].astype(F32))).astype(BF16)


def _lru(y3d, cw, cb, wa_bd, ba, wx_bd, bx, lam, ts=256):
    b, s, _ = y3d.shape
    nct = D_MODEL // LRU_TC
    vec = lambda rows: pl.BlockSpec((rows, LRU_TC), lambda i, c, j: (0, c))
    mat = pl.BlockSpec((1, LRU_TC, LRU_TC), lambda i, c, j: (c, 0, 0))
    return pl.pallas_call(
        functools.partial(_lru_kernel, ts=ts),
        grid=(b, nct, s // ts),
        in_specs=[pl.BlockSpec((1, ts, LRU_TC), lambda i, c, j: (i, j, C_LX // LRU_TC + c)),
                  pl.BlockSpec((1, ts, LRU_TC), lambda i, c, j: (i, j, C_LG // LRU_TC + c)),
                  vec(4), vec(1), mat, vec(1), mat, vec(1), vec(1)],
        out_specs=pl.BlockSpec((1, ts, LRU_TC), lambda i, c, j: (i, j, c)),
        out_shape=jax.ShapeDtypeStruct((b, s, D_MODEL), BF16),
        scratch_shapes=[pltpu.VMEM((ts + LRU_HALO, LRU_TC), F32),
                        pltpu.VMEM((1, LRU_TC), F32)],
        compiler_params=_cparams(3),
        name="lru",
    )(y3d, y3d, cw, cb, wa_bd, ba, wx_bd, bx, lam)


def _memkv_kernel(mem_ref, g_ref, w_ref, gk_ref, k_ref, v_ref):
    x = mem_ref[0]
    h = x * lax.rsqrt(jnp.mean(x * x, axis=-1, keepdims=True) + EPS) * g_ref[...]
    kv = _dot(h.astype(BF16), w_ref[...])
    for hd in range(MEM_HEADS):
        kh = kv[:, hd * MEM_HD:(hd + 1) * MEM_HD]
        kh = kh * lax.rsqrt(jnp.mean(kh * kh, axis=-1, keepdims=True) + EPS) * gk_ref[...]
        k_ref[0, :, hd * MEM_HD:(hd + 1) * MEM_HD] = kh.astype(BF16)
    v_ref[0] = kv[:, D_MODEL:].astype(BF16)


def _memkv(mem, g, w_kv, gk):
    b = mem.shape[0]
    out = jax.ShapeDtypeStruct((b, N_MEM, D_MODEL), BF16)
    return pl.pallas_call(
        _memkv_kernel,
        grid=(b,),
        in_specs=[pl.BlockSpec((1, N_MEM, D_MODEL), lambda i: (i, 0, 0)),
                  pl.BlockSpec((1, D_MODEL), lambda i: (0, 0)),
                  pl.BlockSpec((D_MODEL, 2 * D_MODEL), lambda i: (0, 0)),
                  pl.BlockSpec((1, MEM_HD), lambda i: (0, 0))],
        out_specs=[pl.BlockSpec((1, N_MEM, D_MODEL), lambda i: (i, 0, 0)),
                   pl.BlockSpec((1, N_MEM, D_MODEL), lambda i: (i, 0, 0))],
        out_shape=[out, out],
        compiler_params=_cparams(1),
        name="memkv",
    )(mem, g, w_kv, gk)


def _mem_kernel(q_ref, k_ref, v_ref, gq_ref, o_ref):
    for hd in range(MEM_HEADS):
        sl = slice(hd * MEM_HD, (hd + 1) * MEM_HD)
        q = q_ref[0, :, sl].astype(F32)
        q = q * lax.rsqrt(jnp.mean(q * q, axis=-1, keepdims=True) + EPS) * gq_ref[...]
        q = (q * (MEM_HD ** -0.5)).astype(BF16)
        s = _dot_nt(q, k_ref[0, :, sl])
        p = jnp.exp(s - jnp.max(s, axis=-1, keepdims=True))
        o = _dot(p.astype(BF16), v_ref[0, :, sl]) / jnp.sum(p, axis=-1, keepdims=True)
        o_ref[0, :, sl] = o.astype(BF16)


def _mem(y3d, mk, mv, gq, ts=512):
    b, s, _ = y3d.shape
    return pl.pallas_call(
        _mem_kernel,
        grid=(b, s // ts),
        in_specs=[pl.BlockSpec((1, ts, D_MODEL), lambda i, j: (i, j, C_MQ // D_MODEL)),
                  pl.BlockSpec((1, N_MEM, D_MODEL), lambda i, j: (i, 0, 0)),
                  pl.BlockSpec((1, N_MEM, D_MODEL), lambda i, j: (i, 0, 0)),
                  pl.BlockSpec((1, MEM_HD), lambda i, j: (0, 0))],
        out_specs=pl.BlockSpec((1, ts, D_MODEL), lambda i, j: (i, j, 0)),
        out_shape=jax.ShapeDtypeStruct((b, s, D_MODEL), BF16),
        compiler_params=_cparams(2),
        name="mem",
    )(y3d, mk, mv, gq)


def _merge_kernel(x_ref, b0_ref, b1_ref, b2_ref, b3_ref, gt_ref, bg_ref, wb_ref, wo_ref, o_ref):
    mixed = None
    for n, br in enumerate((b0_ref, b1_ref, b2_ref, b3_ref)):
        gate = jax.nn.sigmoid(
            gt_ref[:, n * D_MODEL:(n + 1) * D_MODEL].astype(F32) + bg_ref[n:n + 1, :])
        term = gate * _dot(br[...], wb_ref[n])
        mixed = term if mixed is None else mixed + term
    o_ref[...] = x_ref[...] + _dot(mixed.astype(BF16), wo_ref[...])


def _merge(x2d, branches, y2d, bg, wb, wo, tm=256):
    m = x2d.shape[0]
    row = lambda w: pl.BlockSpec((tm, w), lambda i: (i, 0))
    return pl.pallas_call(
        _merge_kernel,
        grid=(m // tm,),
        in_specs=[row(D_MODEL), row(D_MODEL), row(D_MODEL), row(D_MODEL), row(D_MODEL),
                  pl.BlockSpec((tm, 4 * D_MODEL), lambda i: (i, C_GT // (4 * D_MODEL))),
                  pl.BlockSpec((4, D_MODEL), lambda i: (0, 0)),
                  pl.BlockSpec((4, D_MODEL, D_MODEL), lambda i: (0, 0, 0)),
                  pl.BlockSpec((D_MODEL, D_MODEL), lambda i: (0, 0))],
        out_specs=row(D_MODEL),
        out_shape=jax.ShapeDtypeStruct((m, D_MODEL), F32),
        compiler_params=_cparams(1),
        name="merge",
    )(x2d, *branches, y2d, bg, wb, wo)


FFN_TF = 256
FFN_HALO = 8


def _ffn_kernel(x_ref, g_ref, wg_ref, wv_ref, cw_ref, cb_ref, wd_ref, o_ref,
                h_ref, acc_ref, gbuf_ref, tail_ref, *, tm, tiles_per_seq):
    i = pl.program_id(0)
    f = pl.program_id(1)
    nf = pl.num_programs(1)

    @pl.when(f == 0)
    def _():
        x = x_ref[...]
        h = x * lax.rsqrt(jnp.mean(x * x, axis=-1, keepdims=True) + EPS) * g_ref[...]
        h_ref[...] = h.astype(BF16)
        acc_ref[...] = x

    h = h_ref[...]
    gbuf_ref[FFN_HALO:FFN_HALO + tm, :] = _dot(h, wg_ref[...])

    @pl.when((i % tiles_per_seq) == 0)
    def _():
        tail_ref[f] = jnp.zeros((FFN_HALO, FFN_TF), F32)

    gbuf_ref[0:FFN_HALO, :] = tail_ref[f]
    cw = cw_ref[...]
    gate = cb_ref[...] + gbuf_ref[FFN_HALO:FFN_HALO + tm, :] * cw[2:3, :]
    for d in range(1, 3):
        gate = gate + gbuf_ref[FFN_HALO - d:FFN_HALO - d + tm, :] * cw[2 - d:3 - d, :]
    tail_ref[f] = gbuf_ref[tm:tm + FFN_HALO, :]
    act = gate * jax.nn.sigmoid(gate) * _dot(h, wv_ref[...])
    acc_ref[...] += _dot(act.astype(BF16), wd_ref[...])

    @pl.when(f == nf - 1)
    def _():
        o_ref[...] = acc_ref[...]


def _ffn(x2d, g, w_up, cw, cb, w_down, seq, tm=1024):
    m = x2d.shape[0]
    nf = D_FF // FFN_TF
    return pl.pallas_call(
        functools.partial(_ffn_kernel, tm=tm, tiles_per_seq=seq // tm),
        grid=(m // tm, nf),
        in_specs=[pl.BlockSpec((tm, D_MODEL), lambda i, f: (i, 0)),
                  pl.BlockSpec((1, D_MODEL), lambda i, f: (0, 0)),
                  pl.BlockSpec((D_MODEL, FFN_TF), lambda i, f: (0, f)),
                  pl.BlockSpec((D_MODEL, FFN_TF), lambda i, f: (0, nf + f)),
                  pl.BlockSpec((3, FFN_TF), lambda i, f: (0, f)),
                  pl.BlockSpec((1, FFN_TF), lambda i, f: (0, f)),
                  pl.BlockSpec((FFN_TF, D_MODEL), lambda i, f: (f, 0))],
        out_specs=pl.BlockSpec((tm, D_MODEL), lambda i, f: (i, 0)),
        out_shape=jax.ShapeDtypeStruct((m, D_MODEL), F32),
        scratch_shapes=[pltpu.VMEM((tm, D_MODEL), BF16),
                        pltpu.VMEM((tm, D_MODEL), F32),
                        pltpu.VMEM((tm + FFN_HALO, FFN_TF), F32),
                        pltpu.VMEM((nf, FFN_HALO, FFN_TF), F32)],
        compiler_params=_cparams(2),
        name="ffn",
    )(x2d, g, w_up, w_up, cw, cb, w_down)


def _block_diag(w):
    per = LRU_TC // HEAD_DIM
    w4 = w.reshape(D_MODEL // LRU_TC, per, HEAD_DIM, HEAD_DIM)
    eye = jnp.eye(per, dtype=w.dtype)
    bd = jnp.einsum('cpde,pq->cpdqe', w4, eye)
    return bd.reshape(D_MODEL // LRU_TC, LRU_TC, LRU_TC)


def kernel(x, mem, attn_norm_g, mem_norm_g, w_in, b_forget, fox_q_norm_g, fox_k_norm_g,
           lru_conv_w, lru_conv_b, lru_w_a, lru_b_a, lru_w_x, lru_b_x, lru_lambda,
           w_mem_kv, mem_q_norm_g, mem_k_norm_g, b_gate, w_branch, w_out,
           ffn_norm_g, w_up, ffn_conv_w, ffn_conv_b, w_down):
    b, s, d = x.shape
    depth = w_in.shape[0]
    m = b * s
    x2d = x.reshape(m, d)
    for l in range(depth):
        wy = jnp.concatenate([w_in[l, :, W_GATES:], w_in[l, :, :W_FOX_END],
                              w_in[l, :, W_F_END:W_GATES]], axis=1)
        sb_q_scale = jnp.where((jnp.arange(N_Y) >= C_SQ) & (jnp.arange(N_Y) < C_SK),
                               HEAD_DIM ** -0.5 * LOG2E, 1.0).astype(F32)
        wy = (wy * sb_q_scale).astype(BF16)
        wf = jnp.pad(w_in[l, :, W_FOX_END:W_F_END], ((0, 0), (0, LANES - N_HEADS)))
        bf = jnp.pad(b_forget[l], (0, LANES - N_HEADS)).reshape(1, LANES)

        y2d, f2d = _proj(x2d, attn_norm_g[l].reshape(1, d), wy, wf)
        y3d = y2d.reshape(b, s, N_Y)

        fcum = _fcum(f2d.reshape(b, s, LANES), bf)
        qa, ka = _foxprep(y3d, fcum,
                          jnp.tile(fox_q_norm_g[l], 2).reshape(1, LANES),
                          jnp.tile(fox_k_norm_g[l], 2).reshape(1, LANES))
        y_fox = _fox(qa, ka, y3d)
        y_sb = _sb(y3d)
        y_lru = _lru(y3d, lru_conv_w[l], lru_conv_b[l].reshape(1, d),
                     _block_diag(lru_w_a[l]).astype(BF16), lru_b_a[l].reshape(1, d),
                     _block_diag(lru_w_x[l]).astype(BF16), lru_b_x[l].reshape(1, d),
                     lru_lambda[l].reshape(1, d))
        mk, mv = _memkv(mem, mem_norm_g[l].reshape(1, d), w_mem_kv[l].astype(BF16),
                        mem_k_norm_g[l].reshape(1, MEM_HD))
        y_mem = _mem(y3d, mk, mv, mem_q_norm_g[l].reshape(1, MEM_HD))

        branches = [t.reshape(m, d) for t in (y_fox, y_lru, y_sb, y_mem)]
        x2d = _merge(x2d, branches, y2d, b_gate[l], w_branch[l].astype(BF16),
                     w_out[l].astype(BF16))
        x2d = _ffn(x2d, ffn_norm_g[l].reshape(1, d), w_up[l].astype(BF16), ffn_conv_w[l],
                   ffn_conv_b[l].reshape(1, D_FF), w_down[l].astype(BF16), s)
    return x2d.reshape(b, s, d)
```

```python
import functools

import jax
import jax.numpy as jnp
from jax import lax
from jax.experimental import pallas as pl
from jax.experimental.pallas import tpu as pltpu

F32 = jnp.float32
BF16 = jnp.bfloat16
HIGHEST = lax.Precision.HIGHEST

D_MODEL = 1024
HEAD_DIM = 64
N_HEADS = 16
N_PAIRS = N_HEADS // 2
LANES = 128
N_MEM = 256
MEM_HEADS = 4
MEM_HD = 256
D_FF = 2816
LRU_C = 8.0
EPS = 1e-6
NEG = -1e30
LOG2E = 1.4426950408889634

C_GT, C_FQ, C_FK, C_FV, C_LX, C_LG, C_SQ, C_SK, C_SV, C_MQ = (
    0, 4096, 5120, 6144, 7168, 8192, 9216, 10240, 11264, 12288)
N_Y = 13312
W_FOX_END, W_F_END, W_GATES = 3072, 3088, 9232

VMEM_LIMIT = 56 * 1024 * 1024


def _cparams(n_axes):
    return pltpu.CompilerParams(dimension_semantics=("arbitrary",) * n_axes,
                                vmem_limit_bytes=VMEM_LIMIT)


def _dot(a, b):
    return jnp.dot(a, b, preferred_element_type=F32)


def _dot_nt(a, b):
    return lax.dot_general(a, b, (((1,), (1,)), ((), ())), preferred_element_type=F32)


def _log1p_exp_neg_abs(z):
    return jnp.log(1.0 + jnp.exp(-jnp.abs(z)))


def _proj_kernel(x_ref, g_ref, w_ref, wf_ref, y_ref, f_ref, h_ref):
    @pl.when(pl.program_id(1) == 0)
    def _():
        x = x_ref[...]
        h = x * lax.rsqrt(jnp.mean(x * x, axis=-1, keepdims=True) + EPS) * g_ref[...]
        h_ref[...] = h.astype(BF16)
        f_ref[...] = jnp.dot(h, wf_ref[...], precision=HIGHEST, preferred_element_type=F32)

    y_ref[...] = _dot(h_ref[...], w_ref[...]).astype(BF16)


def _proj(x2d, g, wy, wf, tm=1024, tn=1024):
    m = x2d.shape[0]
    return pl.pallas_call(
        _proj_kernel,
        grid=(m // tm, N_Y // tn),
        in_specs=[pl.BlockSpec((tm, D_MODEL), lambda i, j: (i, 0)),
                  pl.BlockSpec((1, D_MODEL), lambda i, j: (0, 0)),
                  pl.BlockSpec((D_MODEL, tn), lambda i, j: (0, j)),
                  pl.BlockSpec((D_MODEL, LANES), lambda i, j: (0, 0))],
        out_specs=[pl.BlockSpec((tm, tn), lambda i, j: (i, j)),
                   pl.BlockSpec((tm, LANES), lambda i, j: (i, 0))],
        out_shape=[jax.ShapeDtypeStruct((m, N_Y), BF16),
                   jax.ShapeDtypeStruct((m, LANES), F32)],
        scratch_shapes=[pltpu.VMEM((tm, D_MODEL), BF16)],
        compiler_params=_cparams(2),
        name="proj",
    )(x2d, g, wy, wf)


def _fcum_kernel(f_ref, b_ref, o_ref, carry_ref):
    @pl.when(pl.program_id(1) == 0)
    def _():
        carry_ref[...] = jnp.zeros_like(carry_ref)

    z = f_ref[0] + b_ref[...]
    log_f = jnp.minimum(z, 0.0) - _log1p_exp_neg_abs(z)
    ts = z.shape[0]
    row = lax.broadcasted_iota(jnp.int32, (ts, ts), 0)
    col = lax.broadcasted_iota(jnp.int32, (ts, ts), 1)
    tri = (row >= col).astype(F32)
    c = jnp.dot(tri, log_f, precision=HIGHEST, preferred_element_type=F32) + carry_ref[...]
    o_ref[0] = c
    carry_ref[...] = c[ts - 1:ts, :]


def _fcum(f3d, b_pad, ts=512):
    b, s, _ = f3d.shape
    return pl.pallas_call(
        _fcum_kernel,
        grid=(b, s // ts),
        in_specs=[pl.BlockSpec((1, ts, LANES), lambda i, j: (i, j, 0)),
                  pl.BlockSpec((1, LANES), lambda i, j: (0, 0))],
        out_specs=pl.BlockSpec((1, ts, LANES), lambda i, j: (i, j, 0)),
        out_shape=jax.ShapeDtypeStruct((b, s, LANES), F32),
        scratch_shapes=[pltpu.VMEM((1, LANES), F32)],
        compiler_params=_cparams(2),
        name="fcum",
    )(f3d, b_pad)


def _foxprep_kernel(yq_ref, yk_ref, f_ref, gq_ref, gk_ref, qa_ref, ka_ref):
    hp = pl.program_id(1)
    ts = yq_ref.shape[1]
    lane = lax.broadcasted_iota(jnp.int32, (ts, LANES), 1)
    lo_half = lane < HEAD_DIM

    def headnorm(y, g):
        y2 = y * y
        ss_a = jnp.sum(jnp.where(lo_half, y2, 0.0), axis=-1, keepdims=True)
        ss_b = jnp.sum(jnp.where(lo_half, 0.0, y2), axis=-1, keepdims=True)
        ms = jnp.where(lo_half, ss_a, ss_b) * (1.0 / HEAD_DIM)
        return y * lax.rsqrt(ms + EPS) * g

    qn = headnorm(yq_ref[0].astype(F32), gq_ref[...]) * (HEAD_DIM ** -0.5 * LOG2E)
    kn = headnorm(yk_ref[0].astype(F32), gk_ref[...])

    f = f_ref[0] * LOG2E
    f_hi = f.astype(BF16)
    r1 = f - f_hi.astype(F32)
    f_mid = r1.astype(BF16)
    f_lo = (r1 - f_mid.astype(F32)).astype(BF16)
    pieces = (f_hi, f_mid, f_lo)

    sel_r = lax.broadcasted_iota(jnp.int32, (LANES, LANES), 0)
    sel_c = lax.broadcasted_iota(jnp.int32, (LANES, LANES), 1)

    def place(head, first_col):
        out = jnp.zeros((ts, LANES), F32)
        for c, piece in enumerate(pieces):
            sel = ((sel_r == head) & (sel_c == first_col + c)).astype(BF16)
            out = out + _dot(piece, sel)
        return out

    ones_q = ((lane >= HEAD_DIM + 3) & (lane < HEAD_DIM + 6)).astype(F32)
    ones_k = ((lane >= HEAD_DIM) & (lane < HEAD_DIM + 3)).astype(F32)
    for e in range(2):
        head = 2 * hp + e
        q_part = qn if e == 0 else pltpu.roll(qn, HEAD_DIM, axis=1)
        k_part = kn if e == 0 else pltpu.roll(kn, HEAD_DIM, axis=1)
        q_aug = jnp.where(lo_half, q_part, place(head, HEAD_DIM) + ones_q)
        k_aug = jnp.where(lo_half, k_part, ones_k - place(head, HEAD_DIM + 3))
        qa_ref[0, e] = q_aug.astype(BF16)
        ka_ref[0, e] = k_aug.astype(BF16)


def _foxprep(y3d, fcum, gq2, gk2, ts=512):
    b, s, _ = y3d.shape
    out = jax.ShapeDtypeStruct((b, N_HEADS, s, LANES), BF16)
    return pl.pallas_call(
        _foxprep_kernel,
        grid=(b, N_PAIRS, s // ts),
        in_specs=[pl.BlockSpec((1, ts, LANES), lambda i, p, j: (i, j, C_FQ // LANES + p)),
                  pl.BlockSpec((1, ts, LANES), lambda i, p, j: (i, j, C_FK // LANES + p)),
                  pl.BlockSpec((1, ts, LANES), lambda i, p, j: (i, j, 0)),
                  pl.BlockSpec((1, LANES), lambda i, p, j: (0, 0)),
                  pl.BlockSpec((1, LANES), lambda i, p, j: (0, 0))],
        out_specs=[pl.BlockSpec((1, 2, ts, LANES), lambda i, p, j: (i, p, j, 0)),
                   pl.BlockSpec((1, 2, ts, LANES), lambda i, p, j: (i, p, j, 0))],
        out_shape=[out, out],
        compiler_params=_cparams(3),
        name="foxprep",
    )(y3d, y3d, fcum, gq2, gk2)


FOX_TS = 512


def _fox_kernel(qa_ref, ka_ref, v_ref, o_ref, *, tq):
    i = pl.program_id(2)
    ts = FOX_TS
    nslab = tq // ts
    row = lax.broadcasted_iota(jnp.int32, (ts, ts), 0)
    col = lax.broadcasted_iota(jnp.int32, (ts, ts), 1)
    causal = col <= row
    lane = lax.broadcasted_iota(jnp.int32, (ts, LANES), 1)

    chains = [(h, e, qa_ref[0, e, h * ts:(h + 1) * ts, :])
              for h in range(nslab) for e in range(2)]

    def step(start, carry, masks):
        v = v_ref[0, pl.ds(start, ts), :]
        ks = [ka_ref[0, e, pl.ds(start, ts), :] for e in range(2)]
        live = [n for n, (h, _, _) in enumerate(chains) if masks[h] is not False]
        ss = {n: _dot_nt(chains[n][2], ks[chains[n][1]]) for n in live}
        ps, stats = {}, {}
        for n in live:
            m, l, _ = carry[n]
            s = ss[n]
            if masks[chains[n][0]] is not None:
                s = jnp.where(causal, s, NEG)
            m_new = jnp.maximum(m, jnp.max(s, axis=-1, keepdims=True))
            alpha = jnp.exp2(m - m_new)
            p = jnp.exp2(s - m_new)
            ps[n] = p.astype(BF16)
            stats[n] = (m_new, alpha, alpha * l + jnp.sum(p, axis=-1, keepdims=True))
        out = list(carry)
        for n in live:
            m_new, alpha, l = stats[n]
            out[n] = (m_new, l, alpha * carry[n][2] + _dot(ps[n], v))
        return tuple(out)

    carry = tuple((jnp.full((ts, 1), NEG, F32), jnp.zeros((ts, 1), F32),
                   jnp.zeros((ts, LANES), F32)) for _ in chains)
    no_mask = [None] * nslab
    carry = lax.fori_loop(
        0, i * nslab, lambda j, cr: step(pl.multiple_of(j * ts, ts), cr, no_mask), carry)
    for c in range(nslab):
        masks = [False if c > h else (True if c == h else None) for h in range(nslab)]
        carry = step(pl.multiple_of(i * tq + c * ts, ts), carry, masks)
    for h in range(nslab):
        (_, l0, acc0), (_, l1, acc1) = carry[2 * h], carry[2 * h + 1]
        o_ref[0, h * ts:(h + 1) * ts, :] = jnp.where(
            lane < HEAD_DIM, acc0 / l0, acc1 / l1).astype(BF16)


def _fox(qa, ka, y3d, tq=1024):
    b, _, s, _ = qa.shape
    return pl.pallas_call(
        functools.partial(_fox_kernel, tq=tq),
        grid=(b, N_PAIRS, s // tq),
        in_specs=[pl.BlockSpec((1, 2, tq, LANES), lambda i, p, j: (i, p, j, 0)),
                  pl.BlockSpec((1, 2, s, LANES), lambda i, p, j: (i, p, 0, 0)),
                  pl.BlockSpec((1, s, LANES), lambda i, p, j: (i, 0, C_FV // LANES + p))],
        out_specs=pl.BlockSpec((1, tq, LANES), lambda i, p, j: (i, j, p)),
        out_shape=jax.ShapeDtypeStruct((b, s, D_MODEL), BF16),
        compiler_params=_cparams(3),
        name="fox",
    )(qa, ka, y3d)


SB_KC = 256
SB_TS = 512
SB_MAX_LOG2 = 126.0


def _sb_kernel(q_ref, k_ref, v_ref, o_ref, *, tq):
    i = pl.program_id(2)
    kc = SB_KC
    ts = SB_TS
    nsub = tq // kc
    row = lax.broadcasted_iota(jnp.int32, (ts, kc), 0)
    col = lax.broadcasted_iota(jnp.int32, (ts, kc), 1)
    kr = lax.broadcasted_iota(jnp.int32, (kc, kc), 0)
    kcol = lax.broadcasted_iota(jnp.int32, (kc, kc), 1)
    at_or_after = (kr >= kcol).astype(BF16)
    lane = lax.broadcasted_iota(jnp.int32, (ts, LANES), 1)

    chains = []
    for h in range(tq // ts):
        q2 = q_ref[0, h * ts:(h + 1) * ts, :]
        zero = jnp.zeros_like(q2)
        chains.append((h, jnp.where(lane < HEAD_DIM, q2, zero)))
        chains.append((h, jnp.where(lane >= HEAD_DIM, q2, zero)))

    def step(start, carry, masks):
        k = k_ref[0, pl.ds(start, kc), :]
        v = v_ref[0, pl.ds(start, kc), :]
        live = [n for n, (h, _) in enumerate(chains) if masks[h] is not False]
        zs = {n: jnp.minimum(_dot_nt(chains[n][1], k), SB_MAX_LOG2) for n in live}
        incls, lsums = {}, {}
        for n in live:
            mask = masks[chains[n][0]]
            log_1m_beta = jnp.log(1.0 + jnp.exp2(zs[n])) * (-LOG2E)
            if mask is not None:
                log_1m_beta = jnp.where(mask, log_1m_beta, 0.0)
            incls[n] = _dot(log_1m_beta.astype(BF16), at_or_after)
            lsums[n] = jnp.sum(log_1m_beta, axis=-1, keepdims=True)
        out = list(carry)
        for n in live:
            mask = masks[chains[n][0]]
            r, acc = carry[n]
            log_w = zs[n] + incls[n] + r
            if mask is not None:
                log_w = jnp.where(mask, log_w, NEG)
            acc = acc + _dot(jnp.exp2(log_w).astype(BF16), v)
            out[n] = (r + lsums[n], acc)
        return tuple(out)

    carry = tuple((jnp.zeros((ts, 1), F32), jnp.zeros((ts, LANES), F32)) for _ in chains)
    for c in reversed(range(nsub)):
        masks = []
        for h in range(tq // ts):
            if c * kc >= (h + 1) * ts:
                masks.append(False)
            elif (c + 1) * kc <= h * ts:
                masks.append(None)
            else:
                masks.append((col + (c * kc - h * ts)) < row)
        carry = step(pl.multiple_of(i * tq + c * kc, kc), carry, masks)
    nfull = i * nsub
    no_mask = [None] * (tq // ts)
    carry = lax.fori_loop(
        0, nfull, lambda t, cr: step(pl.multiple_of((nfull - 1 - t) * kc, kc), cr, no_mask),
        carry)
    lane_o = lane < HEAD_DIM
    for h in range(tq // ts):
        o_ref[0, h * ts:(h + 1) * ts, :] = jnp.where(
            lane_o, carry[2 * h][1], carry[2 * h + 1][1]).astype(BF16)


def _sb(y3d, tq=1024):
    b, s, _ = y3d.shape
    return pl.pallas_call(
        functools.partial(_sb_kernel, tq=tq),
        grid=(b, N_PAIRS, s // tq),
        in_specs=[pl.BlockSpec((1, tq, LANES), lambda i, p, j: (i, j, C_SQ // LANES + p)),
                  pl.BlockSpec((1, s, LANES), lambda i, p, j: (i, 0, C_SK // LANES + p)),
                  pl.BlockSpec((1, s, LANES), lambda i, p, j: (i, 0, C_SV // LANES + p))],
        out_specs=pl.BlockSpec((1, tq, LANES), lambda i, p, j: (i, j, p)),
        out_shape=jax.ShapeDtypeStruct((b, s, D_MODEL), BF16),
        compiler_params=_cparams(3),
        name="sb",
    )(y3d, y3d, y3d)


LRU_TC = 256
LRU_HALO = 8


def _lru_kernel(x_ref, g_ref, cw_ref, cb_ref, wa_ref, ba_ref, wx_ref, bx_ref, lam_ref,
                o_ref, xbuf_ref, h_ref, *, ts):
    @pl.when(pl.program_id(2) == 0)
    def _():
        xbuf_ref[0:LRU_HALO, :] = jnp.zeros((LRU_HALO, LRU_TC), F32)
        h_ref[...] = jnp.zeros_like(h_ref)

    xbuf_ref[LRU_HALO:LRU_HALO + ts, :] = x_ref[0].astype(F32)
    cw = cw_ref[...]
    xc = cb_ref[...] + xbuf_ref[LRU_HALO:LRU_HALO + ts, :] * cw[3:4, :]
    for d in range(1, 4):
        xc = xc + xbuf_ref[LRU_HALO - d:LRU_HALO - d + ts, :] * cw[3 - d:4 - d, :]
    xbuf_ref[0:LRU_HALO, :] = xbuf_ref[ts:ts + LRU_HALO, :]

    xcb = xc.astype(BF16)
    r = jax.nn.sigmoid(_dot(xcb, wa_ref[0]) + ba_ref[...])
    gi = jax.nn.sigmoid(_dot(xcb, wx_ref[0]) + bx_ref[...])
    lam = lam_ref[...]
    softplus_neg_lam = jnp.maximum(-lam, 0.0) + _log1p_exp_neg_abs(lam)
    log_a = -LRU_C * r * softplus_neg_lam
    a = jnp.exp(log_a)
    u = jnp.sqrt(-jnp.tanh(log_a) * (a * a + 1.0)) * (gi * xc)

    row = lax.broadcasted_iota(jnp.int32, (ts, LRU_TC), 0)
    d = 1
    while d < ts:
        keep = row >= d
        a_sh = jnp.where(keep, pltpu.roll(a, d, axis=0), 1.0)
        u_sh = jnp.where(keep, pltpu.roll(u, d, axis=0), 0.0)
        u = a * u_sh + u
        a = a * a_sh
        d *= 2
    h = u + a * h_ref[...]
    h_ref[...] = h[ts - 1:ts, :]
    o_ref[0] = (h * jax.nn.gelu(g_ref[0].astype(F32))).astype(BF16)


def _lru(y3d, cw, cb, wa_bd, ba, wx_bd, bx, lam, ts=256):
    b, s, _ = y3d.shape
    nct = D_MODEL // LRU_TC
    vec = lambda rows: pl.BlockSpec((rows, LRU_TC), lambda i, c, j: (0, c))
    mat = pl.BlockSpec((1, LRU_TC, LRU_TC), lambda i, c, j: (c, 0, 0))
    return pl.pallas_call(
        functools.partial(_lru_kernel, ts=ts),
        grid=(b, nct, s // ts),
        in_specs=[pl.BlockSpec((1, ts, LRU_TC), lambda i, c, j: (i, j, C_LX // LRU_TC + c)),
                  pl.BlockSpec((1, ts, LRU_TC), lambda i, c, j: (i, j, C_LG // LRU_TC + c)),
                  vec(4), vec(1), mat, vec(1), mat, vec(1), vec(1)],
        out_specs=pl.BlockSpec((1, ts, LRU_TC), lambda i, c, j: (i, j, c)),
        out_shape=jax.ShapeDtypeStruct((b, s, D_MODEL), BF16),
        scratch_shapes=[pltpu.VMEM((ts + LRU_HALO, LRU_TC), F32),
                        pltpu.VMEM((1, LRU_TC), F32)],
        compiler_params=_cparams(3),
        name="lru",
    )(y3d, y3d, cw, cb, wa_bd, ba, wx_bd, bx, lam)


def _memkv_kernel(mem_ref, g_ref, w_ref, gk_ref, k_ref, v_ref):
    x = mem_ref[0]
    h = x * lax.rsqrt(jnp.mean(x * x, axis=-1, keepdims=True) + EPS) * g_ref[...]
    kv = _dot(h.astype(BF16), w_ref[...])
    for hd in range(MEM_HEADS):
        kh = kv[:, hd * MEM_HD:(hd + 1) * MEM_HD]
        kh = kh * lax.rsqrt(jnp.mean(kh * kh, axis=-1, keepdims=True) + EPS) * gk_ref[...]
        k_ref[0, :, hd * MEM_HD:(hd + 1) * MEM_HD] = kh.astype(BF16)
    v_ref[0] = kv[:, D_MODEL:].astype(BF16)


def _memkv(mem, g, w_kv, gk):
    b = mem.shape[0]
    out = jax.ShapeDtypeStruct((b, N_MEM, D_MODEL), BF16)
    return pl.pallas_call(
        _memkv_kernel,
        grid=(b,),
        in_specs=[pl.BlockSpec((1, N_MEM, D_MODEL), lambda i: (i, 0, 0)),
                  pl.BlockSpec((1, D_MODEL), lambda i: (0, 0)),
                  pl.BlockSpec((D_MODEL, 2 * D_MODEL), lambda i: (0, 0)),
                  pl.BlockSpec((1, MEM_HD), lambda i: (0, 0))],
        out_specs=[pl.BlockSpec((1, N_MEM, D_MODEL), lambda i: (i, 0, 0)),
                   pl.BlockSpec((1, N_MEM, D_MODEL), lambda i: (i, 0, 0))],
        out_shape=[out, out],
        compiler_params=_cparams(1),
        name="memkv",
    )(mem, g, w_kv, gk)


def _mem_kernel(q_ref, k_ref, v_ref, gq_ref, o_ref):
    for hd in range(MEM_HEADS):
        sl = slice(hd * MEM_HD, (hd + 1) * MEM_HD)
        q = q_ref[0, :, sl].astype(F32)
        q = q * lax.rsqrt(jnp.mean(q * q, axis=-1, keepdims=True) + EPS) * gq_ref[...]
        q = (q * (MEM_HD ** -0.5)).astype(BF16)
        s = _dot_nt(q, k_ref[0, :, sl])
        p = jnp.exp(s - jnp.max(s, axis=-1, keepdims=True))
        o = _dot(p.astype(BF16), v_ref[0, :, sl]) / jnp.sum(p, axis=-1, keepdims=True)
        o_ref[0, :, sl] = o.astype(BF16)


def _mem(y3d, mk, mv, gq, ts=512):
    b, s, _ = y3d.shape
    return pl.pallas_call(
        _mem_kernel,
        grid=(b, s // ts),
        in_specs=[pl.BlockSpec((1, ts, D_MODEL), lambda i, j: (i, j, C_MQ // D_MODEL)),
                  pl.BlockSpec((1, N_MEM, D_MODEL), lambda i, j: (i, 0, 0)),
                  pl.BlockSpec((1, N_MEM, D_MODEL), lambda i, j: (i, 0, 0)),
                  pl.BlockSpec((1, MEM_HD), lambda i, j: (0, 0))],
        out_specs=pl.BlockSpec((1, ts, D_MODEL), lambda i, j: (i, j, 0)),
        out_shape=jax.ShapeDtypeStruct((b, s, D_MODEL), BF16),
        compiler_params=_cparams(2),
        name="mem",
    )(y3d, mk, mv, gq)


def _merge_kernel(x_ref, b0_ref, b1_ref, b2_ref, b3_ref, gt_ref, bg_ref, wb_ref, wo_ref, o_ref):
    mixed = None
    for n, br in enumerate((b0_ref, b1_ref, b2_ref, b3_ref)):
        gate = jax.nn.sigmoid(
            gt_ref[:, n * D_MODEL:(n + 1) * D_MODEL].astype(F32) + bg_ref[n:n + 1, :])
        term = gate * _dot(br[...], wb_ref[n])
        mixed = term if mixed is None else mixed + term
    o_ref[...] = x_ref[...] + _dot(mixed.astype(BF16), wo_ref[...])


def _merge(x2d, branches, y2d, bg, wb, wo, tm=256):
    m = x2d.shape[0]
    row = lambda w: pl.BlockSpec((tm, w), lambda i: (i, 0))
    return pl.pallas_call(
        _merge_kernel,
        grid=(m // tm,),
        in_specs=[row(D_MODEL), row(D_MODEL), row(D_MODEL), row(D_MODEL), row(D_MODEL),
                  pl.BlockSpec((tm, 4 * D_MODEL), lambda i: (i, C_GT // (4 * D_MODEL))),
                  pl.BlockSpec((4, D_MODEL), lambda i: (0, 0)),
                  pl.BlockSpec((4, D_MODEL, D_MODEL), lambda i: (0, 0, 0)),
                  pl.BlockSpec((D_MODEL, D_MODEL), lambda i: (0, 0))],
        out_specs=row(D_MODEL),
        out_shape=jax.ShapeDtypeStruct((m, D_MODEL), F32),
        compiler_params=_cparams(1),
        name="merge",
    )(x2d, *branches, y2d, bg, wb, wo)


FFN_TF = 256
FFN_HALO = 8


def _ffn_kernel(x_ref, g_ref, wg_ref, wv_ref, cw_ref, cb_ref, wd_ref, o_ref,
                h_ref, acc_ref, gbuf_ref, tail_ref, *, tm, tiles_per_seq):
    i = pl.program_id(0)
    f = pl.program_id(1)
    nf = pl.num_programs(1)

    @pl.when(f == 0)
    def _():
        x = x_ref[...]
        h = x * lax.rsqrt(jnp.mean(x * x, axis=-1, keepdims=True) + EPS) * g_ref[...]
        h_ref[...] = h.astype(BF16)
        acc_ref[...] = x

    h = h_ref[...]
    gbuf_ref[FFN_HALO:FFN_HALO + tm, :] = _dot(h, wg_ref[...])

    @pl.when((i % tiles_per_seq) == 0)
    def _():
        tail_ref[f] = jnp.zeros((FFN_HALO, FFN_TF), F32)

    gbuf_ref[0:FFN_HALO, :] = tail_ref[f]
    cw = cw_ref[...]
    gate = cb_ref[...] + gbuf_ref[FFN_HALO:FFN_HALO + tm, :] * cw[2:3, :]
    for d in range(1, 3):
        gate = gate + gbuf_ref[FFN_HALO - d:FFN_HALO - d + tm, :] * cw[2 - d:3 - d, :]
    tail_ref[f] = gbuf_ref[tm:tm + FFN_HALO, :]
    act = gate * jax.nn.sigmoid(gate) * _dot(h, wv_ref[...])
    acc_ref[...] += _dot(act.astype(BF16), wd_ref[...])

    @pl.when(f == nf - 1)
    def _():
        o_ref[...] = acc_ref[...]


def _ffn(x2d, g, w_up, cw, cb, w_down, seq, tm=1024):
    m = x2d.shape[0]
    nf = D_FF // FFN_TF
    return pl.pallas_call(
        functools.partial(_ffn_kernel, tm=tm, tiles_per_seq=seq // tm),
        grid=(m // tm, nf),
        in_specs=[pl.BlockSpec((tm, D_MODEL), lambda i, f: (i, 0)),
                  pl.BlockSpec((1, D_MODEL), lambda i, f: (0, 0)),
                  pl.BlockSpec((D_MODEL, FFN_TF), lambda i, f: (0, f)),
                  pl.BlockSpec((D_MODEL, FFN_TF), lambda i, f: (0, nf + f)),
                  pl.BlockSpec((3, FFN_TF), lambda i, f: (0, f)),
                  pl.BlockSpec((1, FFN_TF), lambda i, f: (0, f)),
                  pl.BlockSpec((FFN_TF, D_MODEL), lambda i, f: (f, 0))],
        out_specs=pl.BlockSpec((tm, D_MODEL), lambda i, f: (i, 0)),
        out_shape=jax.ShapeDtypeStruct((m, D_MODEL), F32),
        scratch_shapes=[pltpu.VMEM((tm, D_MODEL), BF16),
                        pltpu.VMEM((tm, D_MODEL), F32),
                        pltpu.VMEM((tm + FFN_HALO, FFN_TF), F32),
                        pltpu.VMEM((nf, FFN_HALO, FFN_TF), F32)],
        compiler_params=_cparams(2),
        name="ffn",
    )(x2d, g, w_up, w_up, cw, cb, w_down)


def _block_diag(w):
    per = LRU_TC // HEAD_DIM
    w4 = w.reshape(D_MODEL // LRU_TC, per, HEAD_DIM, HEAD_DIM)
    eye = jnp.eye(per, dtype=w.dtype)
    bd = jnp.einsum('cpde,pq->cpdqe', w4, eye)
    return bd.reshape(D_MODEL // LRU_TC, LRU_TC, LRU_TC)


def kernel(x, mem, attn_norm_g, mem_norm_g, w_in, b_forget, fox_q_norm_g, fox_k_norm_g,
           lru_conv_w, lru_conv_b, lru_w_a, lru_b_a, lru_w_x, lru_b_x, lru_lambda,
           w_mem_kv, mem_q_norm_g, mem_k_norm_g, b_gate, w_branch, w_out,
           ffn_norm_g, w_up, ffn_conv_w, ffn_conv_b, w_down):
    b, s, d = x.shape
    depth = w_in.shape[0]
    m = b * s
    x2d = x.reshape(m, d)
    for l in range(depth):
        wy = jnp.concatenate([w_in[l, :, W_GATES:], w_in[l, :, :W_FOX_END],
                              w_in[l, :, W_F_END:W_GATES]], axis=1)
        sb_q_scale = jnp.where((jnp.arange(N_Y) >= C_SQ) & (jnp.arange(N_Y) < C_SK),
                               HEAD_DIM ** -0.5 * LOG2E, 1.0).astype(F32)
        wy = (wy * sb_q_scale).astype(BF16)
        wf = jnp.pad(w_in[l, :, W_FOX_END:W_F_END], ((0, 0), (0, LANES - N_HEADS)))
        bf = jnp.pad(b_forget[l], (0, LANES - N_HEADS)).reshape(1, LANES)

        y2d, f2d = _proj(x2d, attn_norm_g[l].reshape(1, d), wy, wf)
        y3d = y2d.reshape(b, s, N_Y)

        fcum = _fcum(f2d.reshape(b, s, LANES), bf)
        qa, ka = _foxprep(y3d, fcum,
                          jnp.tile(fox_q_norm_g[l], 2).reshape(1, LANES),
                          jnp.tile(fox_k_norm_g[l], 2).reshape(1, LANES))
        y_fox = _fox(qa, ka, y3d)
        y_sb = _sb(y3d)
        y_lru = _lru(y3d, lru_conv_w[l], lru_conv_b[l].reshape(1, d),
                     _block_diag(lru_w_a[l]).astype(BF16), lru_b_a[l].reshape(1, d),
                     _block_diag(lru_w_x[l]).astype(BF16), lru_b_x[l].reshape(1, d),
                     lru_lambda[l].reshape(1, d))
        mk, mv = _memkv(mem, mem_norm_g[l].reshape(1, d), w_mem_kv[l].astype(BF16),
                        mem_k_norm_g[l].reshape(1, MEM_HD))
        y_mem = _mem(y3d, mk, mv, mem_q_norm_g[l].reshape(1, MEM_HD))

        branches = [t.reshape(m, d) for t in (y_fox, y_lru, y_sb, y_mem)]
        x2d = _merge(x2d, branches, y2d, b_gate[l], w_branch[l].astype(BF16),
                     w_out[l].astype(BF16))
        x2d = _ffn(x2d, ffn_norm_g[l].reshape(1, d), w_up[l].astype(BF16), ffn_conv_w[l],
                   ffn_conv_b[l].reshape(1, D_FF), w_down[l].astype(BF16), s)
    return x2d.reshape(b, s, d)
```

```python
import functools

import jax
import jax.numpy as jnp
from jax import lax
from jax.experimental import pallas as pl
from jax.experimental.pallas import tpu as pltpu

F32 = jnp.float32
BF16 = jnp.bfloat16
HIGHEST = lax.Precision.HIGHEST

D_MODEL = 1024
HEAD_DIM = 64
N_HEADS = 16
N_PAIRS = N_HEADS // 2
LANES = 128
N_MEM = 256
MEM_HEADS = 4
MEM_HD = 256
D_FF = 2816
LRU_C = 8.0
EPS = 1e-6
NEG = -1e30
LOG2E = 1.4426950408889634

C_GT, C_FQ, C_FK, C_FV, C_LX, C_LG, C_SQ, C_SK, C_SV, C_MQ = (
    0, 4096, 5120, 6144, 7168, 8192, 9216, 10240, 11264, 12288)
N_Y = 13312
W_FOX_END, W_F_END, W_GATES = 3072, 3088, 9232

VMEM_LIMIT = 56 * 1024 * 1024


def _cparams(n_axes):
    return pltpu.CompilerParams(dimension_semantics=("arbitrary",) * n_axes,
                                vmem_limit_bytes=VMEM_LIMIT)


def _dot(a, b):
    return jnp.dot(a, b, preferred_element_type=F32)


def _dot_nt(a, b):
    return lax.dot_general(a, b, (((1,), (1,)), ((), ())), preferred_element_type=F32)


def _log1p_exp_neg_abs(z):
    return jnp.log(1.0 + jnp.exp(-jnp.abs(z)))


def _proj_kernel(x_ref, g_ref, w_ref, wf_ref, y_ref, f_ref, h_ref):
    @pl.when(pl.program_id(1) == 0)
    def _():
        x = x_ref[...]
        h = x * lax.rsqrt(jnp.mean(x * x, axis=-1, keepdims=True) + EPS) * g_ref[...]
        h_ref[...] = h.astype(BF16)
        f_ref[...] = jnp.dot(h, wf_ref[...], precision=HIGHEST, preferred_element_type=F32)

    y_ref[...] = _dot(h_ref[...], w_ref[...]).astype(BF16)


def _proj(x2d, g, wy, wf, tm=1024, tn=1024):
    m = x2d.shape[0]
    return pl.pallas_call(
        _proj_kernel,
        grid=(m // tm, N_Y // tn),
        in_specs=[pl.BlockSpec((tm, D_MODEL), lambda i, j: (i, 0)),
                  pl.BlockSpec((1, D_MODEL), lambda i, j: (0, 0)),
                  pl.BlockSpec((D_MODEL, tn), lambda i, j: (0, j)),
                  pl.BlockSpec((D_MODEL, LANES), lambda i, j: (0, 0))],
        out_specs=[pl.BlockSpec((tm, tn), lambda i, j: (i, j)),
                   pl.BlockSpec((tm, LANES), lambda i, j: (i, 0))],
        out_shape=[jax.ShapeDtypeStruct((m, N_Y), BF16),
                   jax.ShapeDtypeStruct((m, LANES), F32)],
        scratch_shapes=[pltpu.VMEM((tm, D_MODEL), BF16)],
        compiler_params=_cparams(2),
        name="proj",
    )(x2d, g, wy, wf)


def _fcum_kernel(f_ref, b_ref, o_ref, carry_ref):
    @pl.when(pl.program_id(1) == 0)
    def _():
        carry_ref[...] = jnp.zeros_like(carry_ref)

    z = f_ref[0] + b_ref[...]
    log_f = jnp.minimum(z, 0.0) - _log1p_exp_neg_abs(z)
    ts = z.shape[0]
    row = lax.broadcasted_iota(jnp.int32, (ts, ts), 0)
    col = lax.broadcasted_iota(jnp.int32, (ts, ts), 1)
    tri = (row >= col).astype(F32)
    c = jnp.dot(tri, log_f, precision=HIGHEST, preferred_element_type=F32) + carry_ref[...]
    o_ref[0] = c
    carry_ref[...] = c[ts - 1:ts, :]


def _fcum(f3d, b_pad, ts=512):
    b, s, _ = f3d.shape
    return pl.pallas_call(
        _fcum_kernel,
        grid=(b, s // ts),
        in_specs=[pl.BlockSpec((1, ts, LANES), lambda i, j: (i, j, 0)),
                  pl.BlockSpec((1, LANES), lambda i, j: (0, 0))],
        out_specs=pl.BlockSpec((1, ts, LANES), lambda i, j: (i, j, 0)),
        out_shape=jax.ShapeDtypeStruct((b, s, LANES), F32),
        scratch_shapes=[pltpu.VMEM((1, LANES), F32)],
        compiler_params=_cparams(2),
        name="fcum",
    )(f3d, b_pad)


def _foxprep_kernel(yq_ref, yk_ref, f_ref, gq_ref, gk_ref, qa_ref, ka_ref):
    hp = pl.program_id(1)
    ts = yq_ref.shape[1]
    lane = lax.broadcasted_iota(jnp.int32, (ts, LANES), 1)
    lo_half = lane < HEAD_DIM

    def headnorm(y, g):
        y2 = y * y
        ss_a = jnp.sum(jnp.where(lo_half, y2, 0.0), axis=-1, keepdims=True)
        ss_b = jnp.sum(jnp.where(lo_half, 0.0, y2), axis=-1, keepdims=True)
        ms = jnp.where(lo_half, ss_a, ss_b) * (1.0 / HEAD_DIM)
        return y * lax.rsqrt(ms + EPS) * g

    qn = headnorm(yq_ref[0].astype(F32), gq_ref[...]) * (HEAD_DIM ** -0.5 * LOG2E)
    kn = headnorm(yk_ref[0].astype(F32), gk_ref[...])

    f = f_ref[0] * LOG2E
    f_hi = f.astype(BF16)
    r1 = f - f_hi.astype(F32)
    f_mid = r1.astype(BF16)
    f_lo = (r1 - f_mid.astype(F32)).astype(BF16)
    pieces = (f_hi, f_mid, f_lo)

    sel_r = lax.broadcasted_iota(jnp.int32, (LANES, LANES), 0)
    sel_c = lax.broadcasted_iota(jnp.int32, (LANES, LANES), 1)

    def place(head, first_col):
        out = jnp.zeros((ts, LANES), F32)
        for c, piece in enumerate(pieces):
            sel = ((sel_r == head) & (sel_c == first_col + c)).astype(BF16)
            out = out + _dot(piece, sel)
        return out

    ones_q = ((lane >= HEAD_DIM + 3) & (lane < HEAD_DIM + 6)).astype(F32)
    ones_k = ((lane >= HEAD_DIM) & (lane < HEAD_DIM + 3)).astype(F32)
    for e in range(2):
        head = 2 * hp + e
        q_part = qn if e == 0 else pltpu.roll(qn, HEAD_DIM, axis=1)
        k_part = kn if e == 0 else pltpu.roll(kn, HEAD_DIM, axis=1)
        q_aug = jnp.where(lo_half, q_part, place(head, HEAD_DIM) + ones_q)
        k_aug = jnp.where(lo_half, k_part, ones_k - place(head, HEAD_DIM + 3))
        qa_ref[0, e] = q_aug.astype(BF16)
        ka_ref[0, e] = k_aug.astype(BF16)


def _foxprep(y3d, fcum, gq2, gk2, ts=512):
    b, s, _ = y3d.shape
    out = jax.ShapeDtypeStruct((b, N_HEADS, s, LANES), BF16)
    return pl.pallas_call(
        _foxprep_kernel,
        grid=(b, N_PAIRS, s // ts),
        in_specs=[pl.BlockSpec((1, ts, LANES), lambda i, p, j: (i, j, C_FQ // LANES + p)),
                  pl.BlockSpec((1, ts, LANES), lambda i, p, j: (i, j, C_FK // LANES + p)),
                  pl.BlockSpec((1, ts, LANES), lambda i, p, j: (i, j, 0)),
                  pl.BlockSpec((1, LANES), lambda i, p, j: (0, 0)),
                  pl.BlockSpec((1, LANES), lambda i, p, j: (0, 0))],
        out_specs=[pl.BlockSpec((1, 2, ts, LANES), lambda i, p, j: (i, p, j, 0)),
                   pl.BlockSpec((1, 2, ts, LANES), lambda i, p, j: (i, p, j, 0))],
        out_shape=[out, out],
        compiler_params=_cparams(3),
        name="foxprep",
    )(y3d, y3d, fcum, gq2, gk2)


FOX_TS = 512


FOX_RB = 32


def _fox_kernel(qa_ref, ka_ref, v_ref, o_ref, s_ref, p_ref, m_ref, l_ref, a_ref, acc_ref,
                *, tq):
    i = pl.program_id(2)
    ts, rb = FOX_TS, FOX_RB
    nslab = tq // ts
    chains = [(h, e) for h in range(nslab) for e in range(2)]
    row = lax.broadcasted_iota(jnp.int32, (rb, ts), 0)
    col = lax.broadcasted_iota(jnp.int32, (rb, ts), 1)

    m_ref[...] = jnp.full(m_ref.shape, NEG, F32)
    l_ref[...] = jnp.zeros(l_ref.shape, F32)
    acc_ref[...] = jnp.zeros(acc_ref.shape, F32)

    def logits(n, start):
        h, e = chains[n]
        s_ref[n] = _dot_nt(qa_ref[0, e, h * ts:(h + 1) * ts, :],
                           ka_ref[0, e, pl.ds(start, ts), :])

    def softmax(n, diagonal):
        for r0 in range(0, ts, rb):
            rows = slice(r0, r0 + rb)
            s = s_ref[n, rows, :]
            if diagonal:
                s = jnp.where(col <= row + r0, s, NEG)
            m_old = m_ref[n, rows, :]
            m_new = jnp.maximum(m_old, jnp.max(s, axis=-1, keepdims=True))
            alpha = jnp.exp2(m_old - m_new)
            p = jnp.exp2(s - jnp.concatenate([m_new] * (ts // LANES), axis=1))
            l_ref[n, rows, :] = alpha * l_ref[n, rows, :] + jnp.sum(p, axis=-1, keepdims=True)
            m_ref[n, rows, :] = m_new
            a_ref[n, rows, :] = alpha
            p_ref[n, rows, :] = p.astype(BF16)

    def accumulate(n, start):
        acc_ref[n] = a_ref[n] * acc_ref[n] + _dot(p_ref[n], v_ref[0, pl.ds(start, ts), :])

    def block(start, live, diagonal_slab, live_next):
        prev = None
        for n in live:
            softmax(n, chains[n][0] == diagonal_slab)
            if n in live_next:
                logits(n, start + ts)
            if prev is not None:
                accumulate(prev, start)
            prev = n
        accumulate(prev, start)

    everyone = list(range(len(chains)))
    for n in everyone:
        logits(n, 0)

    def body(j, _):
        block(pl.multiple_of(j * ts, ts), everyone, None, everyone)
        return 0

    lax.fori_loop(0, i * nslab, body, 0)
    for c in range(nslab):
        live = [n for n in everyone if chains[n][0] >= c]
        live_next = [n for n in everyone if chains[n][0] >= c + 1] if c + 1 < nslab else []
        block(pl.multiple_of(i * tq + c * ts, ts), live, c, live_next)

    lane = lax.broadcasted_iota(jnp.int32, (ts, LANES), 1)
    for h in range(nslab):
        o0 = acc_ref[2 * h] / l_ref[2 * h]
        o1 = acc_ref[2 * h + 1] / l_ref[2 * h + 1]
        o_ref[0, h * ts:(h + 1) * ts, :] = jnp.where(lane < HEAD_DIM, o0, o1).astype(BF16)


def _fox(qa, ka, y3d, tq=1024):
    b, _, s, _ = qa.shape
    nch = 2 * (tq // FOX_TS)
    return pl.pallas_call(
        functools.partial(_fox_kernel, tq=tq),
        grid=(b, N_PAIRS, s // tq),
        in_specs=[pl.BlockSpec((1, 2, tq, LANES), lambda i, p, j: (i, p, j, 0)),
                  pl.BlockSpec((1, 2, s, LANES), lambda i, p, j: (i, p, 0, 0)),
                  pl.BlockSpec((1, s, LANES), lambda i, p, j: (i, 0, C_FV // LANES + p))],
        out_specs=pl.BlockSpec((1, tq, LANES), lambda i, p, j: (i, j, p)),
        out_shape=jax.ShapeDtypeStruct((b, s, D_MODEL), BF16),
        scratch_shapes=[pltpu.VMEM((nch, FOX_TS, FOX_TS), F32),
                        pltpu.VMEM((nch, FOX_TS, FOX_TS), BF16),
                        pltpu.VMEM((nch, FOX_TS, LANES), F32),
                        pltpu.VMEM((nch, FOX_TS, LANES), F32),
                        pltpu.VMEM((nch, FOX_TS, LANES), F32),
                        pltpu.VMEM((nch, FOX_TS, LANES), F32)],
        compiler_params=_cparams(3),
        name="fox",
    )(qa, ka, y3d)


SB_KC = 256
SB_TS = 512
SB_MAX_LOG2 = 126.0


SB_RB = 64


def _sb_kernel(q_ref, k_ref, v_ref, o_ref, qm_ref, z_ref, lb_ref, ic_ref, w_ref,
               rs_ref, r_ref, acc_ref, *, tq):
    i = pl.program_id(2)
    kc, ts, rb = SB_KC, SB_TS, SB_RB
    nsub = tq // kc
    nslab = tq // ts
    chains = [(h, e) for h in range(nslab) for e in range(2)]
    row = lax.broadcasted_iota(jnp.int32, (rb, kc), 0)
    col = lax.broadcasted_iota(jnp.int32, (rb, kc), 1)
    kr = lax.broadcasted_iota(jnp.int32, (kc, kc), 0)
    kcol = lax.broadcasted_iota(jnp.int32, (kc, kc), 1)
    at_or_after = (kr >= kcol).astype(BF16)

    lane = lax.broadcasted_iota(jnp.int32, (ts, LANES), 1)
    for n, (h, e) in enumerate(chains):
        q2 = q_ref[0, h * ts:(h + 1) * ts, :]
        mine = (lane < HEAD_DIM) if e == 0 else (lane >= HEAD_DIM)
        qm_ref[n] = jnp.where(mine, q2, jnp.zeros_like(q2))
    r_ref[...] = jnp.zeros(r_ref.shape, F32)
    acc_ref[...] = jnp.zeros(acc_ref.shape, F32)

    def logits(n, start):
        z_ref[n] = jnp.minimum(_dot_nt(qm_ref[n], k_ref[0, pl.ds(start, kc), :]), SB_MAX_LOG2)

    def log_terms(n, offset):
        for r0 in range(0, ts, rb):
            rows = slice(r0, r0 + rb)
            log_1m_beta = jnp.log(1.0 + jnp.exp2(z_ref[n, rows, :])) * (-LOG2E)
            if offset is not None:
                log_1m_beta = jnp.where(col + (offset - r0) < row, log_1m_beta, 0.0)
            lb_ref[n, rows, :] = log_1m_beta.astype(BF16)
            rs_ref[n, rows, :] = jnp.broadcast_to(
                jnp.sum(log_1m_beta, axis=-1, keepdims=True), (rb, LANES))

    def cumulate(n):
        ic_ref[n] = _dot(lb_ref[n], at_or_after)

    def weights(n, offset):
        for r0 in range(0, ts, rb):
            rows = slice(r0, r0 + rb)
            r = r_ref[n, rows, :]
            log_w = (z_ref[n, rows, :] + ic_ref[n, rows, :]
                     + jnp.concatenate([r] * (kc // LANES), axis=1))
            if offset is not None:
                log_w = jnp.where(col + (offset - r0) < row, log_w, NEG)
            w_ref[n, rows, :] = jnp.exp2(log_w).astype(BF16)
            r_ref[n, rows, :] = r + rs_ref[n, rows, :]

    def accumulate(n, start):
        acc_ref[n] += _dot(w_ref[n], v_ref[0, pl.ds(start, kc), :])

    def chunk(start, next_start, live, offsets, live_next):
        prev = None
        for n in live:
            log_terms(n, offsets[chains[n][0]])
            cumulate(n)
            if prev is not None:
                weights(prev, offsets[chains[prev][0]])
                if prev in live_next:
                    logits(prev, next_start)
                accumulate(prev, start)
            prev = n
        weights(prev, offsets[chains[prev][0]])
        if prev in live_next:
            logits(prev, next_start)
        accumulate(prev, start)
        for n in live_next:
            if n not in live:
                logits(n, next_start)

    def visibility(c):
        live, offsets = [], {}
        for n, (h, _) in enumerate(chains):
            if c * kc >= (h + 1) * ts:
                continue
            live.append(n)
            offsets[h] = None if (c + 1) * kc <= h * ts else c * kc - h * ts
        return live, offsets

    everyone = list(range(len(chains)))
    no_offsets = {h: None for h in range(nslab)}
    live, offsets = visibility(nsub - 1)
    for n in live:
        logits(n, pl.multiple_of(i * tq + (nsub - 1) * kc, kc))
    for c in reversed(range(nsub)):
        start = pl.multiple_of(i * tq + c * kc, kc)
        live_next = visibility(c - 1)[0] if c > 0 else everyone
        chunk(start, pl.multiple_of(jnp.maximum(start - kc, 0), kc), live, offsets, live_next)
        if c > 0:
            live, offsets = visibility(c - 1)

    def body(t, _):
        start = pl.multiple_of(i * tq - (t + 1) * kc, kc)
        chunk(start, pl.multiple_of(jnp.maximum(start - kc, 0), kc), everyone, no_offsets,
              everyone)
        return 0

    lax.fori_loop(0, i * nsub, body, 0)
    for h in range(nslab):
        o_ref[0, h * ts:(h + 1) * ts, :] = jnp.where(
            lane < HEAD_DIM, acc_ref[2 * h], acc_ref[2 * h + 1]).astype(BF16)


def _sb(y3d, tq=1024):
    b, s, _ = y3d.shape
    nch = 2 * (tq // SB_TS)
    return pl.pallas_call(
        functools.partial(_sb_kernel, tq=tq),
        grid=(b, N_PAIRS, s // tq),
        in_specs=[pl.BlockSpec((1, tq, LANES), lambda i, p, j: (i, j, C_SQ // LANES + p)),
                  pl.BlockSpec((1, s, LANES), lambda i, p, j: (i, 0, C_SK // LANES + p)),
                  pl.BlockSpec((1, s, LANES), lambda i, p, j: (i, 0, C_SV // LANES + p))],
        out_specs=pl.BlockSpec((1, tq, LANES), lambda i, p, j: (i, j, p)),
        out_shape=jax.ShapeDtypeStruct((b, s, D_MODEL), BF16),
        scratch_shapes=[pltpu.VMEM((nch, SB_TS, LANES), BF16),
                        pltpu.VMEM((nch, SB_TS, SB_KC), F32),
                        pltpu.VMEM((nch, SB_TS, SB_KC), BF16),
                        pltpu.VMEM((nch, SB_TS, SB_KC), F32),
                        pltpu.VMEM((nch, SB_TS, SB_KC), BF16),
                        pltpu.VMEM((nch, SB_TS, LANES), F32),
                        pltpu.VMEM((nch, SB_TS, LANES), F32),
                        pltpu.VMEM((nch, SB_TS, LANES), F32)],
        compiler_params=_cparams(3),
        name="sb",
    )(y3d, y3d, y3d)


LRU_TC = 256
LRU_HALO = 8


def _lru_kernel(x_ref, g_ref, cw_ref, cb_ref, wa_ref, ba_ref, wx_ref, bx_ref, lam_ref,
                o_ref, xbuf_ref, h_ref, *, ts):
    @pl.when(pl.program_id(2) == 0)
    def _():
        xbuf_ref[0:LRU_HALO, :] = jnp.zeros((LRU_HALO, LRU_TC), F32)
        h_ref[...] = jnp.zeros_like(h_ref)

    xbuf_ref[LRU_HALO:LRU_HALO + ts, :] = x_ref[0].astype(F32)
    cw = cw_ref[...]
    xc = cb_ref[...] + xbuf_ref[LRU_HALO:LRU_HALO + ts, :] * cw[3:4, :]
    for d in range(1, 4):
        xc = xc + xbuf_ref[LRU_HALO - d:LRU_HALO - d + ts, :] * cw[3 - d:4 - d, :]
    xbuf_ref[0:LRU_HALO, :] = xbuf_ref[ts:ts + LRU_HALO, :]

    xcb = xc.astype(BF16)
    r = jax.nn.sigmoid(_dot(xcb, wa_ref[0]) + ba_ref[...])
    gi = jax.nn.sigmoid(_dot(xcb, wx_ref[0]) + bx_ref[...])
    lam = lam_ref[...]
    softplus_neg_lam = jnp.maximum(-lam, 0.0) + _log1p_exp_neg_abs(lam)
    log_a = -LRU_C * r * softplus_neg_lam
    a = jnp.exp(log_a)
    u = jnp.sqrt(-jnp.tanh(log_a) * (a * a + 1.0)) * (gi * xc)

    row = lax.broadcasted_iota(jnp.int32, (ts, LRU_TC), 0)
    d = 1
    while d < ts:
        keep = row >= d
        a_sh = jnp.where(keep, pltpu.roll(a, d, axis=0), 1.0)
        u_sh = jnp.where(keep, pltpu.roll(u, d, axis=0), 0.0)
        u = a * u_sh + u
        a = a * a_sh
        d *= 2
    h = u + a * h_ref[...]
    h_ref[...] = h[ts - 1:ts, :]
    o_ref[0] = (h * jax.nn.gelu(g_ref[0].astype(F32))).astype(BF16)


def _lru(y3d, cw, cb, wa_bd, ba, wx_bd, bx, lam, ts=256):
    b, s, _ = y3d.shape
    nct = D_MODEL // LRU_TC
    vec = lambda rows: pl.BlockSpec((rows, LRU_TC), lambda i, c, j: (0, c))
    mat = pl.BlockSpec((1, LRU_TC, LRU_TC), lambda i, c, j: (c, 0, 0))
    return pl.pallas_call(
        functools.partial(_lru_kernel, ts=ts),
        grid=(b, nct, s // ts),
        in_specs=[pl.BlockSpec((1, ts, LRU_TC), lambda i, c, j: (i, j, C_LX // LRU_TC + c)),
                  pl.BlockSpec((1, ts, LRU_TC), lambda i, c, j: (i, j, C_LG // LRU_TC + c)),
                  vec(4), vec(1), mat, vec(1), mat, vec(1), vec(1)],
        out_specs=pl.BlockSpec((1, ts, LRU_TC), lambda i, c, j: (i, j, c)),
        out_shape=jax.ShapeDtypeStruct((b, s, D_MODEL), BF16),
        scratch_shapes=[pltpu.VMEM((ts + LRU_HALO, LRU_TC), F32),
                        pltpu.VMEM((1, LRU_TC), F32)],
        compiler_params=_cparams(3),
        name="lru",
    )(y3d, y3d, cw, cb, wa_bd, ba, wx_bd, bx, lam)


def _memkv_kernel(mem_ref, g_ref, w_ref, gk_ref, k_ref, v_ref):
    x = mem_ref[0]
    h = x * lax.rsqrt(jnp.mean(x * x, axis=-1, keepdims=True) + EPS) * g_ref[...]
    kv = _dot(h.astype(BF16), w_ref[...])
    for hd in range(MEM_HEADS):
        kh = kv[:, hd * MEM_HD:(hd + 1) * MEM_HD]
        kh = kh * lax.rsqrt(jnp.mean(kh * kh, axis=-1, keepdims=True) + EPS) * gk_ref[...]
        k_ref[0, :, hd * MEM_HD:(hd + 1) * MEM_HD] = kh.astype(BF16)
    v_ref[0] = kv[:, D_MODEL:].astype(BF16)


def _memkv(mem, g, w_kv, gk):
    b = mem.shape[0]
    out = jax.ShapeDtypeStruct((b, N_MEM, D_MODEL), BF16)
    return pl.pallas_call(
        _memkv_kernel,
        grid=(b,),
        in_specs=[pl.BlockSpec((1, N_MEM, D_MODEL), lambda i: (i, 0, 0)),
                  pl.BlockSpec((1, D_MODEL), lambda i: (0, 0)),
                  pl.BlockSpec((D_MODEL, 2 * D_MODEL), lambda i: (0, 0)),
                  pl.BlockSpec((1, MEM_HD), lambda i: (0, 0))],
        out_specs=[pl.BlockSpec((1, N_MEM, D_MODEL), lambda i: (i, 0, 0)),
                   pl.BlockSpec((1, N_MEM, D_MODEL), lambda i: (i, 0, 0))],
        out_shape=[out, out],
        compiler_params=_cparams(1),
        name="memkv",
    )(mem, g, w_kv, gk)


def _mem_kernel(q_ref, k_ref, v_ref, gq_ref, o_ref):
    for hd in range(MEM_HEADS):
        sl = slice(hd * MEM_HD, (hd + 1) * MEM_HD)
        q = q_ref[0, :, sl].astype(F32)
        q = q * lax.rsqrt(jnp.mean(q * q, axis=-1, keepdims=True) + EPS) * gq_ref[...]
        q = (q * (MEM_HD ** -0.5)).astype(BF16)
        s = _dot_nt(q, k_ref[0, :, sl])
        p = jnp.exp(s - jnp.max(s, axis=-1, keepdims=True))
        o = _dot(p.astype(BF16), v_ref[0, :, sl]) / jnp.sum(p, axis=-1, keepdims=True)
        o_ref[0, :, sl] = o.astype(BF16)


def _mem(y3d, mk, mv, gq, ts=512):
    b, s, _ = y3d.shape
    return pl.pallas_call(
        _mem_kernel,
        grid=(b, s // ts),
        in_specs=[pl.BlockSpec((1, ts, D_MODEL), lambda i, j: (i, j, C_MQ // D_MODEL)),
                  pl.BlockSpec((1, N_MEM, D_MODEL), lambda i, j: (i, 0, 0)),
                  pl.BlockSpec((1, N_MEM, D_MODEL), lambda i, j: (i, 0, 0)),
                  pl.BlockSpec((1, MEM_HD), lambda i, j: (0, 0))],
        out_specs=pl.BlockSpec((1, ts, D_MODEL), lambda i, j: (i, j, 0)),
        out_shape=jax.ShapeDtypeStruct((b, s, D_MODEL), BF16),
        compiler_params=_cparams(2),
        name="mem",
    )(y3d, mk, mv, gq)


def _merge_kernel(x_ref, b0_ref, b1_ref, b2_ref, b3_ref, gt_ref, bg_ref, wb_ref, wo_ref, o_ref):
    mixed = None
    for n, br in enumerate((b0_ref, b1_ref, b2_ref, b3_ref)):
        gate = jax.nn.sigmoid(
            gt_ref[:, n * D_MODEL:(n + 1) * D_MODEL].astype(F32) + bg_ref[n:n + 1, :])
        term = gate * _dot(br[...], wb_ref[n])
        mixed = term if mixed is None else mixed + term
    o_ref[...] = x_ref[...] + _dot(mixed.astype(BF16), wo_ref[...])


def _merge(x2d, branches, y2d, bg, wb, wo, tm=256):
    m = x2d.shape[0]
    row = lambda w: pl.BlockSpec((tm, w), lambda i: (i, 0))
    return pl.pallas_call(
        _merge_kernel,
        grid=(m // tm,),
        in_specs=[row(D_MODEL), row(D_MODEL), row(D_MODEL), row(D_MODEL), row(D_MODEL),
                  pl.BlockSpec((tm, 4 * D_MODEL), lambda i: (i, C_GT // (4 * D_MODEL))),
                  pl.BlockSpec((4, D_MODEL), lambda i: (0, 0)),
                  pl.BlockSpec((4, D_MODEL, D_MODEL), lambda i: (0, 0, 0)),
                  pl.BlockSpec((D_MODEL, D_MODEL), lambda i: (0, 0))],
        out_specs=row(D_MODEL),
        out_shape=jax.ShapeDtypeStruct((m, D_MODEL), F32),
        compiler_params=_cparams(1),
        name="merge",
    )(x2d, *branches, y2d, bg, wb, wo)


FFN_TF = 256
FFN_HALO = 8


def _ffn_kernel(x_ref, g_ref, wg_ref, wv_ref, cw_ref, cb_ref, wd_ref, o_ref,
                h_ref, acc_ref, gbuf_ref, tail_ref, *, tm, tiles_per_seq):
    i = pl.program_id(0)
    f = pl.program_id(1)
    nf = pl.num_programs(1)

    @pl.when(f == 0)
    def _():
        x = x_ref[...]
        h = x * lax.rsqrt(jnp.mean(x * x, axis=-1, keepdims=True) + EPS) * g_ref[...]
        h_ref[...] = h.astype(BF16)
        acc_ref[...] = x

    h = h_ref[...]
    gbuf_ref[FFN_HALO:FFN_HALO + tm, :] = _dot(h, wg_ref[...])

    @pl.when((i % tiles_per_seq) == 0)
    def _():
        tail_ref[f] = jnp.zeros((FFN_HALO, FFN_TF), F32)

    gbuf_ref[0:FFN_HALO, :] = tail_ref[f]
    cw = cw_ref[...]
    gate = cb_ref[...] + gbuf_ref[FFN_HALO:FFN_HALO + tm, :] * cw[2:3, :]
    for d in range(1, 3):
        gate = gate + gbuf_ref[FFN_HALO - d:FFN_HALO - d + tm, :] * cw[2 - d:3 - d, :]
    tail_ref[f] = gbuf_ref[tm:tm + FFN_HALO, :]
    act = gate * jax.nn.sigmoid(gate) * _dot(h, wv_ref[...])
    acc_ref[...] += _dot(act.astype(BF16), wd_ref[...])

    @pl.when(f == nf - 1)
    def _():
        o_ref[...] = acc_ref[...]


def _ffn(x2d, g, w_up, cw, cb, w_down, seq, tm=1024):
    m = x2d.shape[0]
    nf = D_FF // FFN_TF
    return pl.pallas_call(
        functools.partial(_ffn_kernel, tm=tm, tiles_per_seq=seq // tm),
        grid=(m // tm, nf),
        in_specs=[pl.BlockSpec((tm, D_MODEL), lambda i, f: (i, 0)),
                  pl.BlockSpec((1, D_MODEL), lambda i, f: (0, 0)),
                  pl.BlockSpec((D_MODEL, FFN_TF), lambda i, f: (0, f)),
                  pl.BlockSpec((D_MODEL, FFN_TF), lambda i, f: (0, nf + f)),
                  pl.BlockSpec((3, FFN_TF), lambda i, f: (0, f)),
                  pl.BlockSpec((1, FFN_TF), lambda i, f: (0, f)),
                  pl.BlockSpec((FFN_TF, D_MODEL), lambda i, f: (f, 0))],
        out_specs=pl.BlockSpec((tm, D_MODEL), lambda i, f: (i, 0)),
        out_shape=jax.ShapeDtypeStruct((m, D_MODEL), F32),
        scratch_shapes=[pltpu.VMEM((tm, D_MODEL), BF16),
                        pltpu.VMEM((tm, D_MODEL), F32),
                        pltpu.VMEM((tm + FFN_HALO, FFN_TF), F32),
                        pltpu.VMEM((nf, FFN_HALO, FFN_TF), F32)],
        compiler_params=_cparams(2),
        name="ffn",
    )(x2d, g, w_up, w_up, cw, cb, w_down)


def _block_diag(w):
    per = LRU_TC // HEAD_DIM
    w4 = w.reshape(D_MODEL // LRU_TC, per, HEAD_DIM, HEAD_DIM)
    eye = jnp.eye(per, dtype=w.dtype)
    bd = jnp.einsum('cpde,pq->cpdqe', w4, eye)
    return bd.reshape(D_MODEL // LRU_TC, LRU_TC, LRU_TC)


def kernel(x, mem, attn_norm_g, mem_norm_g, w_in, b_forget, fox_q_norm_g, fox_k_norm_g,
           lru_conv_w, lru_conv_b, lru_w_a, lru_b_a, lru_w_x, lru_b_x, lru_lambda,
           w_mem_kv, mem_q_norm_g, mem_k_norm_g, b_gate, w_branch, w_out,
           ffn_norm_g, w_up, ffn_conv_w, ffn_conv_b, w_down):
    b, s, d = x.shape
    depth = w_in.shape[0]
    m = b * s
    x2d = x.reshape(m, d)
    for l in range(depth):
        wy = jnp.concatenate([w_in[l, :, W_GATES:], w_in[l, :, :W_FOX_END],
                              w_in[l, :, W_F_END:W_GATES]], axis=1)
        sb_q_scale = jnp.where((jnp.arange(N_Y) >= C_SQ) & (jnp.arange(N_Y) < C_SK),
                               HEAD_DIM ** -0.5 * LOG2E, 1.0).astype(F32)
        wy = (wy * sb_q_scale).astype(BF16)
        wf = jnp.pad(w_in[l, :, W_FOX_END:W_F_END], ((0, 0), (0, LANES - N_HEADS)))
        bf = jnp.pad(b_forget[l], (0, LANES - N_HEADS)).reshape(1, LANES)

        y2d, f2d = _proj(x2d, attn_norm_g[l].reshape(1, d), wy, wf)
        y3d = y2d.reshape(b, s, N_Y)

        fcum = _fcum(f2d.reshape(b, s, LANES), bf)
        qa, ka = _foxprep(y3d, fcum,
                          jnp.tile(fox_q_norm_g[l], 2).reshape(1, LANES),
                          jnp.tile(fox_k_norm_g[l], 2).reshape(1, LANES))
        y_fox = _fox(qa, ka, y3d)
        y_sb = _sb(y3d)
        y_lru = _lru(y3d, lru_conv_w[l], lru_conv_b[l].reshape(1, d),
                     _block_diag(lru_w_a[l]).astype(BF16), lru_b_a[l].reshape(1, d),
                     _block_diag(lru_w_x[l]).astype(BF16), lru_b_x[l].reshape(1, d),
                     lru_lambda[l].reshape(1, d))
        mk, mv = _memkv(mem, mem_norm_g[l].reshape(1, d), w_mem_kv[l].astype(BF16),
                        mem_k_norm_g[l].reshape(1, MEM_HD))
        y_mem = _mem(y3d, mk, mv, mem_q_norm_g[l].reshape(1, MEM_HD))

        branches = [t.reshape(m, d) for t in (y_fox, y_lru, y_sb, y_mem)]
        x2d = _merge(x2d, branches, y2d, b_gate[l], w_branch[l].astype(BF16),
                     w_out[l].astype(BF16))
        x2d = _ffn(x2d, ffn_norm_g[l].reshape(1, d), w_up[l].astype(BF16), ffn_conv_w[l],
                   ffn_conv_b[l].reshape(1, D_FF), w_down[l].astype(BF16), s)
    return x2d.reshape(b, s, d)
```

```python
import functools

import jax
import jax.numpy as jnp
from jax import lax
from jax.experimental import pallas as pl
from jax.experimental.pallas import tpu as pltpu

F32 = jnp.float32
BF16 = jnp.bfloat16
HIGHEST = lax.Precision.HIGHEST

D_MODEL = 1024
HEAD_DIM = 64
N_HEADS = 16
N_PAIRS = N_HEADS // 2
LANES = 128
N_MEM = 256
MEM_HEADS = 4
MEM_HD = 256
D_FF = 2816
LRU_C = 8.0
EPS = 1e-6
NEG = -1e30
LOG2E = 1.4426950408889634

C_GT, C_FQ, C_FK, C_FV, C_LX, C_LG, C_SQ, C_SK, C_SV, C_MQ = (
    0, 4096, 5120, 6144, 7168, 8192, 9216, 10240, 11264, 12288)
N_Y = 13312
W_FOX_END, W_F_END, W_GATES = 3072, 3088, 9232

VMEM_LIMIT = 56 * 1024 * 1024


def _cparams(n_axes):
    return pltpu.CompilerParams(dimension_semantics=("arbitrary",) * n_axes,
                                vmem_limit_bytes=VMEM_LIMIT)


def _dot(a, b):
    return jnp.dot(a, b, preferred_element_type=F32)


def _dot_nt(a, b):
    return lax.dot_general(a, b, (((1,), (1,)), ((), ())), preferred_element_type=F32)


def _log1p_exp_neg_abs(z):
    return jnp.log(1.0 + jnp.exp(-jnp.abs(z)))


def _proj_kernel(x_ref, g_ref, w_ref, wf_ref, y_ref, f_ref, h_ref):
    @pl.when(pl.program_id(1) == 0)
    def _():
        x = x_ref[...]
        h = x * lax.rsqrt(jnp.mean(x * x, axis=-1, keepdims=True) + EPS) * g_ref[...]
        h_ref[...] = h.astype(BF16)
        f_ref[...] = jnp.dot(h, wf_ref[...], precision=HIGHEST, preferred_element_type=F32)

    y_ref[...] = _dot(h_ref[...], w_ref[...]).astype(BF16)


def _proj(x2d, g, wy, wf, tm=1024, tn=1024):
    m = x2d.shape[0]
    return pl.pallas_call(
        _proj_kernel,
        grid=(m // tm, N_Y // tn),
        in_specs=[pl.BlockSpec((tm, D_MODEL), lambda i, j: (i, 0)),
                  pl.BlockSpec((1, D_MODEL), lambda i, j: (0, 0)),
                  pl.BlockSpec((D_MODEL, tn), lambda i, j: (0, j)),
                  pl.BlockSpec((D_MODEL, LANES), lambda i, j: (0, 0))],
        out_specs=[pl.BlockSpec((tm, tn), lambda i, j: (i, j)),
                   pl.BlockSpec((tm, LANES), lambda i, j: (i, 0))],
        out_shape=[jax.ShapeDtypeStruct((m, N_Y), BF16),
                   jax.ShapeDtypeStruct((m, LANES), F32)],
        scratch_shapes=[pltpu.VMEM((tm, D_MODEL), BF16)],
        compiler_params=_cparams(2),
        name="proj",
    )(x2d, g, wy, wf)


def _fcum_kernel(f_ref, b_ref, o_ref, carry_ref):
    @pl.when(pl.program_id(1) == 0)
    def _():
        carry_ref[...] = jnp.zeros_like(carry_ref)

    z = f_ref[0] + b_ref[...]
    log_f = jnp.minimum(z, 0.0) - _log1p_exp_neg_abs(z)
    ts = z.shape[0]
    row = lax.broadcasted_iota(jnp.int32, (ts, ts), 0)
    col = lax.broadcasted_iota(jnp.int32, (ts, ts), 1)
    tri = (row >= col).astype(F32)
    c = jnp.dot(tri, log_f, precision=HIGHEST, preferred_element_type=F32) + carry_ref[...]
    o_ref[0] = c
    carry_ref[...] = c[ts - 1:ts, :]


def _fcum(f3d, b_pad, ts=512):
    b, s, _ = f3d.shape
    return pl.pallas_call(
        _fcum_kernel,
        grid=(b, s // ts),
        in_specs=[pl.BlockSpec((1, ts, LANES), lambda i, j: (i, j, 0)),
                  pl.BlockSpec((1, LANES), lambda i, j: (0, 0))],
        out_specs=pl.BlockSpec((1, ts, LANES), lambda i, j: (i, j, 0)),
        out_shape=jax.ShapeDtypeStruct((b, s, LANES), F32),
        scratch_shapes=[pltpu.VMEM((1, LANES), F32)],
        compiler_params=_cparams(2),
        name="fcum",
    )(f3d, b_pad)


def _foxprep_kernel(yq_ref, yk_ref, f_ref, gq_ref, gk_ref, qa_ref, ka_ref):
    hp = pl.program_id(1)
    ts = yq_ref.shape[1]
    lane = lax.broadcasted_iota(jnp.int32, (ts, LANES), 1)
    lo_half = lane < HEAD_DIM

    def headnorm(y, g):
        y2 = y * y
        ss_a = jnp.sum(jnp.where(lo_half, y2, 0.0), axis=-1, keepdims=True)
        ss_b = jnp.sum(jnp.where(lo_half, 0.0, y2), axis=-1, keepdims=True)
        ms = jnp.where(lo_half, ss_a, ss_b) * (1.0 / HEAD_DIM)
        return y * lax.rsqrt(ms + EPS) * g

    qn = headnorm(yq_ref[0].astype(F32), gq_ref[...]) * (HEAD_DIM ** -0.5 * LOG2E)
    kn = headnorm(yk_ref[0].astype(F32), gk_ref[...])

    f = f_ref[0] * LOG2E
    f_hi = f.astype(BF16)
    r1 = f - f_hi.astype(F32)
    f_mid = r1.astype(BF16)
    f_lo = (r1 - f_mid.astype(F32)).astype(BF16)
    pieces = (f_hi, f_mid, f_lo)

    sel_r = lax.broadcasted_iota(jnp.int32, (LANES, LANES), 0)
    sel_c = lax.broadcasted_iota(jnp.int32, (LANES, LANES), 1)

    def place(head, first_col):
        out = jnp.zeros((ts, LANES), F32)
        for c, piece in enumerate(pieces):
            sel = ((sel_r == head) & (sel_c == first_col + c)).astype(BF16)
            out = out + _dot(piece, sel)
        return out

    ones_q = ((lane >= HEAD_DIM + 3) & (lane < HEAD_DIM + 6)).astype(F32)
    ones_k = ((lane >= HEAD_DIM) & (lane < HEAD_DIM + 3)).astype(F32)
    for e in range(2):
        head = 2 * hp + e
        q_part = qn if e == 0 else pltpu.roll(qn, HEAD_DIM, axis=1)
        k_part = kn if e == 0 else pltpu.roll(kn, HEAD_DIM, axis=1)
        q_aug = jnp.where(lo_half, q_part, place(head, HEAD_DIM) + ones_q)
        k_aug = jnp.where(lo_half, k_part, ones_k - place(head, HEAD_DIM + 3))
        qa_ref[0, e] = q_aug.astype(BF16)
        ka_ref[0, e] = k_aug.astype(BF16)


def _foxprep(y3d, fcum, gq2, gk2, ts=512):
    b, s, _ = y3d.shape
    out = jax.ShapeDtypeStruct((b, N_HEADS, s, LANES), BF16)
    return pl.pallas_call(
        _foxprep_kernel,
        grid=(b, N_PAIRS, s // ts),
        in_specs=[pl.BlockSpec((1, ts, LANES), lambda i, p, j: (i, j, C_FQ // LANES + p)),
                  pl.BlockSpec((1, ts, LANES), lambda i, p, j: (i, j, C_FK // LANES + p)),
                  pl.BlockSpec((1, ts, LANES), lambda i, p, j: (i, j, 0)),
                  pl.BlockSpec((1, LANES), lambda i, p, j: (0, 0)),
                  pl.BlockSpec((1, LANES), lambda i, p, j: (0, 0))],
        out_specs=[pl.BlockSpec((1, 2, ts, LANES), lambda i, p, j: (i, p, j, 0)),
                   pl.BlockSpec((1, 2, ts, LANES), lambda i, p, j: (i, p, j, 0))],
        out_shape=[out, out],
        compiler_params=_cparams(3),
        name="foxprep",
    )(y3d, y3d, fcum, gq2, gk2)


FOX_TS = 512


FOX_RB = 32


def _fox_kernel(qa_ref, ka_ref, v_ref, o_ref, s_ref, p_ref, m_ref, l_ref, a_ref, acc_ref,
                *, tq):
    i = pl.program_id(2)
    ts, rb = FOX_TS, FOX_RB
    nslab = tq // ts
    chains = [(h, e) for h in range(nslab) for e in range(2)]
    row = lax.broadcasted_iota(jnp.int32, (rb, ts), 0)
    col = lax.broadcasted_iota(jnp.int32, (rb, ts), 1)

    m_ref[...] = jnp.full(m_ref.shape, NEG, F32)
    l_ref[...] = jnp.zeros(l_ref.shape, F32)
    acc_ref[...] = jnp.zeros(acc_ref.shape, F32)

    def logits(n, start):
        h, e = chains[n]
        s_ref[n] = _dot_nt(qa_ref[0, e, h * ts:(h + 1) * ts, :],
                           ka_ref[0, e, pl.ds(start, ts), :])

    def softmax(n, diagonal):
        for r0 in range(0, ts, rb):
            rows = slice(r0, r0 + rb)
            s = s_ref[n, rows, :]
            if diagonal:
                s = jnp.where(col <= row + r0, s, NEG)
            m_old = m_ref[n, rows, :]
            m_new = jnp.maximum(m_old, jnp.max(s, axis=-1, keepdims=True))
            alpha = jnp.exp2(m_old - m_new)
            p = jnp.exp2(s - jnp.concatenate([m_new] * (ts // LANES), axis=1))
            l_ref[n, rows, :] = alpha * l_ref[n, rows, :] + jnp.sum(p, axis=-1, keepdims=True)
            m_ref[n, rows, :] = m_new
            a_ref[n, rows, :] = alpha
            p_ref[n, rows, :] = p.astype(BF16)

    def accumulate(n, start):
        acc_ref[n] = a_ref[n] * acc_ref[n] + _dot(p_ref[n], v_ref[0, pl.ds(start, ts), :])

    def block(start, live, diagonal_slab, live_next):
        prev = None
        for n in live:
            softmax(n, chains[n][0] == diagonal_slab)
            if n in live_next:
                logits(n, start + ts)
            if prev is not None:
                accumulate(prev, start)
            prev = n
        accumulate(prev, start)

    everyone = list(range(len(chains)))
    for n in everyone:
        logits(n, 0)

    def body(j, _):
        block(pl.multiple_of(j * ts, ts), everyone, None, everyone)
        return 0

    lax.fori_loop(0, i * nslab, body, 0)
    for c in range(nslab):
        live = [n for n in everyone if chains[n][0] >= c]
        live_next = [n for n in everyone if chains[n][0] >= c + 1] if c + 1 < nslab else []
        block(pl.multiple_of(i * tq + c * ts, ts), live, c, live_next)

    lane = lax.broadcasted_iota(jnp.int32, (ts, LANES), 1)
    for h in range(nslab):
        o0 = acc_ref[2 * h] / l_ref[2 * h]
        o1 = acc_ref[2 * h + 1] / l_ref[2 * h + 1]
        o_ref[0, h * ts:(h + 1) * ts, :] = jnp.where(lane < HEAD_DIM, o0, o1).astype(BF16)


def _fox(qa, ka, y3d, tq=1024):
    b, _, s, _ = qa.shape
    nch = 2 * (tq // FOX_TS)
    return pl.pallas_call(
        functools.partial(_fox_kernel, tq=tq),
        grid=(b, N_PAIRS, s // tq),
        in_specs=[pl.BlockSpec((1, 2, tq, LANES), lambda i, p, j: (i, p, j, 0)),
                  pl.BlockSpec((1, 2, s, LANES), lambda i, p, j: (i, p, 0, 0)),
                  pl.BlockSpec((1, s, LANES), lambda i, p, j: (i, 0, C_FV // LANES + p))],
        out_specs=pl.BlockSpec((1, tq, LANES), lambda i, p, j: (i, j, p)),
        out_shape=jax.ShapeDtypeStruct((b, s, D_MODEL), BF16),
        scratch_shapes=[pltpu.VMEM((nch, FOX_TS, FOX_TS), F32),
                        pltpu.VMEM((nch, FOX_TS, FOX_TS), BF16),
                        pltpu.VMEM((nch, FOX_TS, LANES), F32),
                        pltpu.VMEM((nch, FOX_TS, LANES), F32),
                        pltpu.VMEM((nch, FOX_TS, LANES), F32),
                        pltpu.VMEM((nch, FOX_TS, LANES), F32)],
        compiler_params=_cparams(3),
        name="fox",
    )(qa, ka, y3d)


SB_KC = 256
SB_TS = 512
SB_MAX_LOG2 = 126.0


SB_RB = 64
SB_UNROLL = 2


def _sb_kernel(q_ref, k_ref, v_ref, o_ref, qm_ref, z_ref, lb_ref, ic_ref, w_ref,
               rs_ref, r_ref, acc_ref, *, tq):
    i = pl.program_id(2)
    kc, ts, rb = SB_KC, SB_TS, SB_RB
    nsub = tq // kc
    nslab = tq // ts
    chains = [(h, e) for h in range(nslab) for e in range(2)]
    row = lax.broadcasted_iota(jnp.int32, (rb, kc), 0)
    col = lax.broadcasted_iota(jnp.int32, (rb, kc), 1)
    kr = lax.broadcasted_iota(jnp.int32, (kc, kc), 0)
    kcol = lax.broadcasted_iota(jnp.int32, (kc, kc), 1)
    at_or_after = (kr >= kcol).astype(BF16)

    lane = lax.broadcasted_iota(jnp.int32, (ts, LANES), 1)
    for n, (h, e) in enumerate(chains):
        q2 = q_ref[0, h * ts:(h + 1) * ts, :]
        mine = (lane < HEAD_DIM) if e == 0 else (lane >= HEAD_DIM)
        qm_ref[n] = jnp.where(mine, q2, jnp.zeros_like(q2))
    r_ref[...] = jnp.zeros(r_ref.shape, F32)
    acc_ref[...] = jnp.zeros(acc_ref.shape, F32)

    def logits(n, start):
        z_ref[n] = jnp.minimum(_dot_nt(qm_ref[n], k_ref[0, pl.ds(start, kc), :]), SB_MAX_LOG2)

    def log_terms(n, offset):
        for r0 in range(0, ts, rb):
            rows = slice(r0, r0 + rb)
            log_1m_beta = jnp.log(1.0 + jnp.exp2(z_ref[n, rows, :])) * (-LOG2E)
            if offset is not None:
                log_1m_beta = jnp.where(col + (offset - r0) < row, log_1m_beta, 0.0)
            lb_ref[n, rows, :] = log_1m_beta.astype(BF16)
            rs_ref[n, rows, :] = jnp.broadcast_to(
                jnp.sum(log_1m_beta, axis=-1, keepdims=True), (rb, LANES))

    def cumulate(n):
        ic_ref[n] = _dot(lb_ref[n], at_or_after)

    def weights(n, offset):
        for r0 in range(0, ts, rb):
            rows = slice(r0, r0 + rb)
            r = r_ref[n, rows, :]
            log_w = (z_ref[n, rows, :] + ic_ref[n, rows, :]
                     + jnp.concatenate([r] * (kc // LANES), axis=1))
            if offset is not None:
                log_w = jnp.where(col + (offset - r0) < row, log_w, NEG)
            w_ref[n, rows, :] = jnp.exp2(log_w.astype(BF16))
            r_ref[n, rows, :] = r + rs_ref[n, rows, :]

    def accumulate(n, start):
        acc_ref[n] += _dot(w_ref[n], v_ref[0, pl.ds(start, kc), :])

    def chunk(start, next_start, live, offsets, live_next):
        prev = None
        for n in live:
            log_terms(n, offsets[chains[n][0]])
            cumulate(n)
            if prev is not None:
                weights(prev, offsets[chains[prev][0]])
                if prev in live_next:
                    logits(prev, next_start)
                accumulate(prev, start)
            prev = n
        weights(prev, offsets[chains[prev][0]])
        if prev in live_next:
            logits(prev, next_start)
        accumulate(prev, start)
        for n in live_next:
            if n not in live:
                logits(n, next_start)

    def visibility(c):
        live, offsets = [], {}
        for n, (h, _) in enumerate(chains):
            if c * kc >= (h + 1) * ts:
                continue
            live.append(n)
            offsets[h] = None if (c + 1) * kc <= h * ts else c * kc - h * ts
        return live, offsets

    everyone = list(range(len(chains)))
    no_offsets = {h: None for h in range(nslab)}
    live, offsets = visibility(nsub - 1)
    for n in live:
        logits(n, pl.multiple_of(i * tq + (nsub - 1) * kc, kc))
    for c in reversed(range(nsub)):
        start = pl.multiple_of(i * tq + c * kc, kc)
        live_next = visibility(c - 1)[0] if c > 0 else everyone
        chunk(start, pl.multiple_of(jnp.maximum(start - kc, 0), kc), live, offsets, live_next)
        if c > 0:
            live, offsets = visibility(c - 1)

    def body(t, _):
        for u in range(SB_UNROLL):
            start = pl.multiple_of(i * tq - (t * SB_UNROLL + u + 1) * kc, kc)
            chunk(start, pl.multiple_of(jnp.maximum(start - kc, 0), kc), everyone, no_offsets,
                  everyone)
        return 0

    lax.fori_loop(0, i * (nsub // SB_UNROLL), body, 0)
    for h in range(nslab):
        o_ref[0, h * ts:(h + 1) * ts, :] = jnp.where(
            lane < HEAD_DIM, acc_ref[2 * h], acc_ref[2 * h + 1]).astype(BF16)


def _sb(y3d, tq=1024):
    b, s, _ = y3d.shape
    nch = 2 * (tq // SB_TS)
    return pl.pallas_call(
        functools.partial(_sb_kernel, tq=tq),
        grid=(b, N_PAIRS, s // tq),
        in_specs=[pl.BlockSpec((1, tq, LANES), lambda i, p, j: (i, j, C_SQ // LANES + p)),
                  pl.BlockSpec((1, s, LANES), lambda i, p, j: (i, 0, C_SK // LANES + p)),
                  pl.BlockSpec((1, s, LANES), lambda i, p, j: (i, 0, C_SV // LANES + p))],
        out_specs=pl.BlockSpec((1, tq, LANES), lambda i, p, j: (i, j, p)),
        out_shape=jax.ShapeDtypeStruct((b, s, D_MODEL), BF16),
        scratch_shapes=[pltpu.VMEM((nch, SB_TS, LANES), BF16),
                        pltpu.VMEM((nch, SB_TS, SB_KC), F32),
                        pltpu.VMEM((nch, SB_TS, SB_KC), BF16),
                        pltpu.VMEM((nch, SB_TS, SB_KC), F32),
                        pltpu.VMEM((nch, SB_TS, SB_KC), BF16),
                        pltpu.VMEM((nch, SB_TS, LANES), F32),
                        pltpu.VMEM((nch, SB_TS, LANES), F32),
                        pltpu.VMEM((nch, SB_TS, LANES), F32)],
        compiler_params=_cparams(3),
        name="sb",
    )(y3d, y3d, y3d)


LRU_TC = 256
LRU_HALO = 8


def _lru_kernel(x_ref, g_ref, cw_ref, cb_ref, wa_ref, ba_ref, wx_ref, bx_ref, lam_ref,
                o_ref, xbuf_ref, h_ref, *, ts):
    @pl.when(pl.program_id(2) == 0)
    def _():
        xbuf_ref[0:LRU_HALO, :] = jnp.zeros((LRU_HALO, LRU_TC), F32)
        h_ref[...] = jnp.zeros_like(h_ref)

    xbuf_ref[LRU_HALO:LRU_HALO + ts, :] = x_ref[0].astype(F32)
    cw = cw_ref[...]
    xc = cb_ref[...] + xbuf_ref[LRU_HALO:LRU_HALO + ts, :] * cw[3:4, :]
    for d in range(1, 4):
        xc = xc + xbuf_ref[LRU_HALO - d:LRU_HALO - d + ts, :] * cw[3 - d:4 - d, :]
    xbuf_ref[0:LRU_HALO, :] = xbuf_ref[ts:ts + LRU_HALO, :]

    xcb = xc.astype(BF16)
    r = jax.nn.sigmoid(_dot(xcb, wa_ref[0]) + ba_ref[...])
    gi = jax.nn.sigmoid(_dot(xcb, wx_ref[0]) + bx_ref[...])
    lam = lam_ref[...]
    softplus_neg_lam = jnp.maximum(-lam, 0.0) + _log1p_exp_neg_abs(lam)
    log_a = -LRU_C * r * softplus_neg_lam
    a = jnp.exp(log_a)
    u = jnp.sqrt(-jnp.tanh(log_a) * (a * a + 1.0)) * (gi * xc)

    row = lax.broadcasted_iota(jnp.int32, (ts, LRU_TC), 0)
    d = 1
    while d < ts:
        keep = row >= d
        a_sh = jnp.where(keep, pltpu.roll(a, d, axis=0), 1.0)
        u_sh = jnp.where(keep, pltpu.roll(u, d, axis=0), 0.0)
        u = a * u_sh + u
        a = a * a_sh
        d *= 2
    h = u + a * h_ref[...]
    h_ref[...] = h[ts - 1:ts, :]
    o_ref[0] = (h * jax.nn.gelu(g_ref[0].astype(F32))).astype(BF16)


def _lru(y3d, cw, cb, wa_bd, ba, wx_bd, bx, lam, ts=256):
    b, s, _ = y3d.shape
    nct = D_MODEL // LRU_TC
    vec = lambda rows: pl.BlockSpec((rows, LRU_TC), lambda i, c, j: (0, c))
    mat = pl.BlockSpec((1, LRU_TC, LRU_TC), lambda i, c, j: (c, 0, 0))
    return pl.pallas_call(
        functools.partial(_lru_kernel, ts=ts),
        grid=(b, nct, s // ts),
        in_specs=[pl.BlockSpec((1, ts, LRU_TC), lambda i, c, j: (i, j, C_LX // LRU_TC + c)),
                  pl.BlockSpec((1, ts, LRU_TC), lambda i, c, j: (i, j, C_LG // LRU_TC + c)),
                  vec(4), vec(1), mat, vec(1), mat, vec(1), vec(1)],
        out_specs=pl.BlockSpec((1, ts, LRU_TC), lambda i, c, j: (i, j, c)),
        out_shape=jax.ShapeDtypeStruct((b, s, D_MODEL), BF16),
        scratch_shapes=[pltpu.VMEM((ts + LRU_HALO, LRU_TC), F32),
                        pltpu.VMEM((1, LRU_TC), F32)],
        compiler_params=_cparams(3),
        name="lru",
    )(y3d, y3d, cw, cb, wa_bd, ba, wx_bd, bx, lam)


def _memkv_kernel(mem_ref, g_ref, w_ref, gk_ref, k_ref, v_ref):
    x = mem_ref[0]
    h = x * lax.rsqrt(jnp.mean(x * x, axis=-1, keepdims=True) + EPS) * g_ref[...]
    kv = _dot(h.astype(BF16), w_ref[...])
    for hd in range(MEM_HEADS):
        kh = kv[:, hd * MEM_HD:(hd + 1) * MEM_HD]
        kh = kh * lax.rsqrt(jnp.mean(kh * kh, axis=-1, keepdims=True) + EPS) * gk_ref[...]
        k_ref[0, :, hd * MEM_HD:(hd + 1) * MEM_HD] = kh.astype(BF16)
    v_ref[0] = kv[:, D_MODEL:].astype(BF16)


def _memkv(mem, g, w_kv, gk):
    b = mem.shape[0]
    out = jax.ShapeDtypeStruct((b, N_MEM, D_MODEL), BF16)
    return pl.pallas_call(
        _memkv_kernel,
        grid=(b,),
        in_specs=[pl.BlockSpec((1, N_MEM, D_MODEL), lambda i: (i, 0, 0)),
                  pl.BlockSpec((1, D_MODEL), lambda i: (0, 0)),
                  pl.BlockSpec((D_MODEL, 2 * D_MODEL), lambda i: (0, 0)),
                  pl.BlockSpec((1, MEM_HD), lambda i: (0, 0))],
        out_specs=[pl.BlockSpec((1, N_MEM, D_MODEL), lambda i: (i, 0, 0)),
                   pl.BlockSpec((1, N_MEM, D_MODEL), lambda i: (i, 0, 0))],
        out_shape=[out, out],
        compiler_params=_cparams(1),
        name="memkv",
    )(mem, g, w_kv, gk)


def _mem_kernel(q_ref, k_ref, v_ref, gq_ref, o_ref):
    for hd in range(MEM_HEADS):
        sl = slice(hd * MEM_HD, (hd + 1) * MEM_HD)
        q = q_ref[0, :, sl].astype(F32)
        q = q * lax.rsqrt(jnp.mean(q * q, axis=-1, keepdims=True) + EPS) * gq_ref[...]
        q = (q * (MEM_HD ** -0.5)).astype(BF16)
        s = _dot_nt(q, k_ref[0, :, sl])
        p = jnp.exp(s - jnp.max(s, axis=-1, keepdims=True))
        o = _dot(p.astype(BF16), v_ref[0, :, sl]) / jnp.sum(p, axis=-1, keepdims=True)
        o_ref[0, :, sl] = o.astype(BF16)


def _mem(y3d, mk, mv, gq, ts=512):
    b, s, _ = y3d.shape
    return pl.pallas_call(
        _mem_kernel,
        grid=(b, s // ts),
        in_specs=[pl.BlockSpec((1, ts, D_MODEL), lambda i, j: (i, j, C_MQ // D_MODEL)),
                  pl.BlockSpec((1, N_MEM, D_MODEL), lambda i, j: (i, 0, 0)),
                  pl.BlockSpec((1, N_MEM, D_MODEL), lambda i, j: (i, 0, 0)),
                  pl.BlockSpec((1, MEM_HD), lambda i, j: (0, 0))],
        out_specs=pl.BlockSpec((1, ts, D_MODEL), lambda i, j: (i, j, 0)),
        out_shape=jax.ShapeDtypeStruct((b, s, D_MODEL), BF16),
        compiler_params=_cparams(2),
        name="mem",
    )(y3d, mk, mv, gq)


def _merge_kernel(x_ref, b0_ref, b1_ref, b2_ref, b3_ref, gt_ref, bg_ref, wb_ref, wo_ref, o_ref):
    mixed = None
    for n, br in enumerate((b0_ref, b1_ref, b2_ref, b3_ref)):
        gate = jax.nn.sigmoid(
            gt_ref[:, n * D_MODEL:(n + 1) * D_MODEL].astype(F32) + bg_ref[n:n + 1, :])
        term = gate * _dot(br[...], wb_ref[n])
        mixed = term if mixed is None else mixed + term
    o_ref[...] = x_ref[...] + _dot(mixed.astype(BF16), wo_ref[...])


def _merge(x2d, branches, y2d, bg, wb, wo, tm=256):
    m = x2d.shape[0]
    row = lambda w: pl.BlockSpec((tm, w), lambda i: (i, 0))
    return pl.pallas_call(
        _merge_kernel,
        grid=(m // tm,),
        in_specs=[row(D_MODEL), row(D_MODEL), row(D_MODEL), row(D_MODEL), row(D_MODEL),
                  pl.BlockSpec((tm, 4 * D_MODEL), lambda i: (i, C_GT // (4 * D_MODEL))),
                  pl.BlockSpec((4, D_MODEL), lambda i: (0, 0)),
                  pl.BlockSpec((4, D_MODEL, D_MODEL), lambda i: (0, 0, 0)),
                  pl.BlockSpec((D_MODEL, D_MODEL), lambda i: (0, 0))],
        out_specs=row(D_MODEL),
        out_shape=jax.ShapeDtypeStruct((m, D_MODEL), F32),
        compiler_params=_cparams(1),
        name="merge",
    )(x2d, *branches, y2d, bg, wb, wo)


FFN_TC = 256
FFN_HALO = 8
FFN_RB = 64


def _ffn_kernel(x_ref, g_ref, wg_ref, wv_ref, cw_ref, cb_ref, wd_ref, o_ref,
                h_ref, gbuf_ref, vbuf_ref, act_ref, *, tm, tiles_per_seq):
    i = pl.program_id(0)
    x = x_ref[...]
    h_ref[...] = (x * lax.rsqrt(jnp.mean(x * x, axis=-1, keepdims=True) + EPS)
                  * g_ref[...]).astype(BF16)

    @pl.when((i % tiles_per_seq) == 0)
    def _():
        gbuf_ref[0:FFN_HALO, :] = jnp.zeros((FFN_HALO, D_FF), F32)

    def up(c):
        cols = slice(c * FFN_TC, (c + 1) * FFN_TC)
        gbuf_ref[FFN_HALO:FFN_HALO + tm, cols] = _dot(h_ref[...], wg_ref[:, cols])
        vbuf_ref[:, cols] = _dot(h_ref[...], wv_ref[:, cols])

    def activate(c):
        cols = slice(c * FFN_TC, (c + 1) * FFN_TC)
        cw = cw_ref[:, cols]
        cb = cb_ref[:, cols]
        for r0 in range(0, tm, FFN_RB):
            gate = cb
            for d in range(3):
                lo = FFN_HALO + r0 - d
                gate = gate + gbuf_ref[lo:lo + FFN_RB, cols] * cw[2 - d:3 - d, :]
            act = gate * jax.nn.sigmoid(gate) * vbuf_ref[r0:r0 + FFN_RB, cols]
            act_ref[r0:r0 + FFN_RB, cols] = act.astype(BF16)

    nc = D_FF // FFN_TC
    up(0)
    for c in range(1, nc):
        up(c)
        activate(c - 1)
    activate(nc - 1)
    o_ref[...] = x_ref[...] + _dot(act_ref[...], wd_ref[...])
    gbuf_ref[0:FFN_HALO, :] = gbuf_ref[tm:tm + FFN_HALO, :]


def _ffn(x2d, g, w_up, cw, cb, w_down, seq, tm=512):
    m = x2d.shape[0]
    resident = pl.Buffered(1)
    return pl.pallas_call(
        functools.partial(_ffn_kernel, tm=tm, tiles_per_seq=seq // tm),
        grid=(m // tm,),
        in_specs=[pl.BlockSpec((tm, D_MODEL), lambda i: (i, 0)),
                  pl.BlockSpec((1, D_MODEL), lambda i: (0, 0)),
                  pl.BlockSpec((D_MODEL, D_FF), lambda i: (0, 0), pipeline_mode=resident),
                  pl.BlockSpec((D_MODEL, D_FF), lambda i: (0, 1), pipeline_mode=resident),
                  pl.BlockSpec((3, D_FF), lambda i: (0, 0)),
                  pl.BlockSpec((1, D_FF), lambda i: (0, 0)),
                  pl.BlockSpec((D_FF, D_MODEL), lambda i: (0, 0), pipeline_mode=resident)],
        out_specs=pl.BlockSpec((tm, D_MODEL), lambda i: (i, 0)),
        out_shape=jax.ShapeDtypeStruct((m, D_MODEL), F32),
        scratch_shapes=[pltpu.VMEM((tm, D_MODEL), BF16),
                        pltpu.VMEM((tm + FFN_HALO, D_FF), F32),
                        pltpu.VMEM((tm, D_FF), F32),
                        pltpu.VMEM((tm, D_FF), BF16)],
        compiler_params=_cparams(1),
        name="ffn",
    )(x2d, g, w_up, w_up, cw, cb, w_down)


def _block_diag(w):
    per = LRU_TC // HEAD_DIM
    w4 = w.reshape(D_MODEL // LRU_TC, per, HEAD_DIM, HEAD_DIM)
    eye = jnp.eye(per, dtype=w.dtype)
    bd = jnp.einsum('cpde,pq->cpdqe', w4, eye)
    return bd.reshape(D_MODEL // LRU_TC, LRU_TC, LRU_TC)


def kernel(x, mem, attn_norm_g, mem_norm_g, w_in, b_forget, fox_q_norm_g, fox_k_norm_g,
           lru_conv_w, lru_conv_b, lru_w_a, lru_b_a, lru_w_x, lru_b_x, lru_lambda,
           w_mem_kv, mem_q_norm_g, mem_k_norm_g, b_gate, w_branch, w_out,
           ffn_norm_g, w_up, ffn_conv_w, ffn_conv_b, w_down):
    b, s, d = x.shape
    depth = w_in.shape[0]
    m = b * s
    x2d = x.reshape(m, d)
    for l in range(depth):
        wy = jnp.concatenate([w_in[l, :, W_GATES:], w_in[l, :, :W_FOX_END],
                              w_in[l, :, W_F_END:W_GATES]], axis=1)
        sb_q_scale = jnp.where((jnp.arange(N_Y) >= C_SQ) & (jnp.arange(N_Y) < C_SK),
                               HEAD_DIM ** -0.5 * LOG2E, 1.0).astype(F32)
        wy = (wy * sb_q_scale).astype(BF16)
        wf = jnp.pad(w_in[l, :, W_FOX_END:W_F_END], ((0, 0), (0, LANES - N_HEADS)))
        bf = jnp.pad(b_forget[l], (0, LANES - N_HEADS)).reshape(1, LANES)

        y2d, f2d = _proj(x2d, attn_norm_g[l].reshape(1, d), wy, wf)
        y3d = y2d.reshape(b, s, N_Y)

        fcum = _fcum(f2d.reshape(b, s, LANES), bf)
        qa, ka = _foxprep(y3d, fcum,
                          jnp.tile(fox_q_norm_g[l], 2).reshape(1, LANES),
                          jnp.tile(fox_k_norm_g[l], 2).reshape(1, LANES))
        y_fox = _fox(qa, ka, y3d)
        y_sb = _sb(y3d)
        y_lru = _lru(y3d, lru_conv_w[l], lru_conv_b[l].reshape(1, d),
                     _block_diag(lru_w_a[l]).astype(BF16), lru_b_a[l].reshape(1, d),
                     _block_diag(lru_w_x[l]).astype(BF16), lru_b_x[l].reshape(1, d),
                     lru_lambda[l].reshape(1, d))
        mk, mv = _memkv(mem, mem_norm_g[l].reshape(1, d), w_mem_kv[l].astype(BF16),
                        mem_k_norm_g[l].reshape(1, MEM_HD))
        y_mem = _mem(y3d, mk, mv, mem_q_norm_g[l].reshape(1, MEM_HD))

        branches = [t.reshape(m, d) for t in (y_fox, y_lru, y_sb, y_mem)]
        x2d = _merge(x2d, branches, y2d, b_gate[l], w_branch[l].astype(BF16),
                     w_out[l].astype(BF16))
        x2d = _ffn(x2d, ffn_norm_g[l].reshape(1, d), w_up[l].astype(BF16), ffn_conv_w[l],
                   ffn_conv_b[l].reshape(1, D_FF), w_down[l].astype(BF16), s)
    return x2d.reshape(b, s, d)
```

```python
import functools

import jax
import jax.numpy as jnp
from jax import lax
from jax.experimental import pallas as pl
from jax.experimental.pallas import tpu as pltpu

F32 = jnp.float32
BF16 = jnp.bfloat16
HIGHEST = lax.Precision.HIGHEST

D_MODEL = 1024
HEAD_DIM = 64
N_HEADS = 16
N_PAIRS = N_HEADS // 2
LANES = 128
SUBLANES = 8
N_MEM = 256
MEM_HEADS = 4
MEM_HD = 256
D_FF = 2816
LRU_C = 8.0
EPS = 1e-6
NEG = -1e30
LOG2E = 1.4426950408889634
F32_TINY = 1.1754944e-38

C_GT, C_FQ, C_FK, C_FV, C_LX, C_LG, C_SQ, C_SK, C_SV, C_MQ = (
    0, 4096, 5120, 6144, 7168, 8192, 9216, 10240, 11264, 12288)
N_Y = 13312
W_FOX_END, W_F_END, W_GATES = 3072, 3088, 9232

VMEM_LIMIT = 56 * 1024 * 1024


def _cparams(n_axes):
    return pltpu.CompilerParams(dimension_semantics=("arbitrary",) * n_axes,
                                vmem_limit_bytes=VMEM_LIMIT)


def _dot(a, b):
    return jnp.dot(a, b, preferred_element_type=F32)


def _dot_nt(a, b):
    return lax.dot_general(a, b, (((1,), (1,)), ((), ())), preferred_element_type=F32)


def _log1p_exp_neg_abs(z):
    return jnp.log(1.0 + jnp.exp(-jnp.abs(z)))


def _proj_kernel(x_ref, g_ref, w_ref, wf_ref, y_ref, f_ref, h_ref):
    @pl.when(pl.program_id(1) == 0)
    def _():
        x = x_ref[...]
        h = x * lax.rsqrt(jnp.mean(x * x, axis=-1, keepdims=True) + EPS) * g_ref[...]
        h_ref[...] = h.astype(BF16)
        f_ref[...] = jnp.dot(h, wf_ref[...], precision=HIGHEST, preferred_element_type=F32)

    y_ref[...] = _dot(h_ref[...], w_ref[...]).astype(BF16)


def _proj(x2d, g, wy, wf, tm=1024, tn=1024):
    m = x2d.shape[0]
    return pl.pallas_call(
        _proj_kernel,
        grid=(m // tm, N_Y // tn),
        in_specs=[pl.BlockSpec((tm, D_MODEL), lambda i, j: (i, 0)),
                  pl.BlockSpec((1, D_MODEL), lambda i, j: (0, 0)),
                  pl.BlockSpec((D_MODEL, tn), lambda i, j: (0, j)),
                  pl.BlockSpec((D_MODEL, LANES), lambda i, j: (0, 0))],
        out_specs=[pl.BlockSpec((tm, tn), lambda i, j: (i, j)),
                   pl.BlockSpec((tm, LANES), lambda i, j: (i, 0))],
        out_shape=[jax.ShapeDtypeStruct((m, N_Y), BF16),
                   jax.ShapeDtypeStruct((m, LANES), F32)],
        scratch_shapes=[pltpu.VMEM((tm, D_MODEL), BF16)],
        compiler_params=_cparams(2),
        name="proj",
    )(x2d, g, wy, wf)


def _fcum_kernel(f_ref, b_ref, o_ref, carry_ref):
    @pl.when(pl.program_id(1) == 0)
    def _():
        carry_ref[...] = jnp.zeros_like(carry_ref)

    z = f_ref[0] + b_ref[...]
    log_f = jnp.minimum(z, 0.0) - _log1p_exp_neg_abs(z)
    ts = z.shape[0]
    row = lax.broadcasted_iota(jnp.int32, (ts, ts), 0)
    col = lax.broadcasted_iota(jnp.int32, (ts, ts), 1)
    tri = (row >= col).astype(F32)
    c = jnp.dot(tri, log_f, precision=HIGHEST, preferred_element_type=F32) + carry_ref[...]
    carry_ref[...] = c[ts - 1:ts, :]

    lane = lax.broadcasted_iota(jnp.int32, (ts, LANES), 1)
    f = jnp.where(lane < N_HEADS, c * LOG2E, 0.0)
    hi = f.astype(BF16).astype(F32)
    r1 = f - hi
    mid = r1.astype(BF16).astype(F32)
    lo = (r1 - mid).astype(BF16).astype(F32)
    packed = hi + pltpu.roll(mid, N_HEADS, axis=1) + pltpu.roll(lo, 2 * N_HEADS, axis=1)
    o_ref[0] = packed.astype(BF16)


def _fcum(f3d, b_pad, ts=512):
    b, s, _ = f3d.shape
    return pl.pallas_call(
        _fcum_kernel,
        grid=(b, s // ts),
        in_specs=[pl.BlockSpec((1, ts, LANES), lambda i, j: (i, j, 0)),
                  pl.BlockSpec((1, LANES), lambda i, j: (0, 0))],
        out_specs=pl.BlockSpec((1, ts, LANES), lambda i, j: (i, j, 0)),
        out_shape=jax.ShapeDtypeStruct((b, s, LANES), BF16),
        scratch_shapes=[pltpu.VMEM((1, LANES), F32)],
        compiler_params=_cparams(2),
        name="fcum",
    )(f3d, b_pad)


def _foxprep_kernel(yq_ref, yk_ref, f_ref, gq_ref, gk_ref, qa_ref, ka_ref):
    hp = pl.program_id(1)
    ts = yq_ref.shape[1]
    lane = lax.broadcasted_iota(jnp.int32, (ts, LANES), 1)
    lo_half = lane < HEAD_DIM

    hr = lax.broadcasted_iota(jnp.int32, (LANES, LANES), 0)
    hc = lax.broadcasted_iota(jnp.int32, (LANES, LANES), 1)
    same_head = ((hr < HEAD_DIM) == (hc < HEAD_DIM)).astype(BF16)

    def headnorm(y, g):
        ms = _dot((y * y).astype(BF16), same_head) * (1.0 / HEAD_DIM)
        return y * lax.rsqrt(ms + EPS) * g

    qn = headnorm(yq_ref[0].astype(F32), gq_ref[...]) * (HEAD_DIM ** -0.5 * LOG2E)
    kn = headnorm(yk_ref[0].astype(F32), gk_ref[...])

    pieces = f_ref[0]
    sel_r = lax.broadcasted_iota(jnp.int32, (LANES, 2 * LANES), 0)
    sel_c = lax.broadcasted_iota(jnp.int32, (LANES, 2 * LANES), 1)
    ones_q = ((lane >= HEAD_DIM + 3) & (lane < HEAD_DIM + 6)).astype(F32)
    ones_k = ((lane >= HEAD_DIM) & (lane < HEAD_DIM + 3)).astype(F32)
    for e in range(2):
        head = 2 * hp + e
        piece_of_row = sel_r - head
        to_q = (piece_of_row == (sel_c - HEAD_DIM) * N_HEADS) & (sel_c < HEAD_DIM + 3)
        to_k = (piece_of_row == (sel_c - (LANES + HEAD_DIM + 3)) * N_HEADS) & (
            sel_c >= LANES + HEAD_DIM + 3)
        valid = (piece_of_row >= 0) & (piece_of_row <= 2 * N_HEADS)
        sel = (jnp.where(valid & to_q, 1.0, 0.0) - jnp.where(valid & to_k, 1.0, 0.0)).astype(BF16)
        placed = _dot(pieces, sel)
        q_part = qn if e == 0 else pltpu.roll(qn, HEAD_DIM, axis=1)
        k_part = kn if e == 0 else pltpu.roll(kn, HEAD_DIM, axis=1)
        q_aug = jnp.where(lo_half, q_part, placed[:, :LANES] + ones_q)
        k_aug = jnp.where(lo_half, k_part, placed[:, LANES:] + ones_k)
        qa_ref[0, e] = q_aug.astype(BF16)
        ka_ref[0, e] = k_aug.astype(BF16)


def _foxprep(y3d, fcum, gq2, gk2, ts=512):
    b, s, _ = y3d.shape
    out = jax.ShapeDtypeStruct((b, N_HEADS, s, LANES), BF16)
    return pl.pallas_call(
        _foxprep_kernel,
        grid=(b, N_PAIRS, s // ts),
        in_specs=[pl.BlockSpec((1, ts, LANES), lambda i, p, j: (i, j, C_FQ // LANES + p)),
                  pl.BlockSpec((1, ts, LANES), lambda i, p, j: (i, j, C_FK // LANES + p)),
                  pl.BlockSpec((1, ts, LANES), lambda i, p, j: (i, j, 0)),
                  pl.BlockSpec((1, LANES), lambda i, p, j: (0, 0)),
                  pl.BlockSpec((1, LANES), lambda i, p, j: (0, 0))],
        out_specs=[pl.BlockSpec((1, 2, ts, LANES), lambda i, p, j: (i, p, j, 0)),
                   pl.BlockSpec((1, 2, ts, LANES), lambda i, p, j: (i, p, j, 0))],
        out_shape=[out, out],
        compiler_params=_cparams(3),
        name="foxprep",
    )(y3d, y3d, fcum, gq2, gk2)


FOX_TS = 512


FOX_RB = 32


def _fox_kernel(qa_ref, ka_ref, v_ref, o_ref, s_ref, p_ref, m_ref, l_ref, a_ref, acc_ref,
                *, tq):
    i = pl.program_id(2)
    ts, rb = FOX_TS, FOX_RB
    nslab = tq // ts
    chains = [(h, e) for h in range(nslab) for e in range(2)]
    row = lax.broadcasted_iota(jnp.int32, (rb, ts), 0)
    col = lax.broadcasted_iota(jnp.int32, (rb, ts), 1)

    m_ref[...] = jnp.full(m_ref.shape, NEG, F32)
    l_ref[...] = jnp.zeros(l_ref.shape, F32)
    acc_ref[...] = jnp.zeros(acc_ref.shape, F32)

    def logits(n, start):
        h, e = chains[n]
        s_ref[n] = _dot_nt(qa_ref[0, e, h * ts:(h + 1) * ts, :],
                           ka_ref[0, e, pl.ds(start, ts), :])

    def softmax(n, diagonal):
        for r0 in range(0, ts, rb):
            rows = slice(r0, r0 + rb)
            s = s_ref[n, rows, :]
            if diagonal:
                s = jnp.where(col <= row + r0, s, NEG)
            m_old = m_ref[n, rows, :]
            m_new = jnp.maximum(m_old, jnp.max(s, axis=-1, keepdims=True))
            alpha = jnp.exp2(m_old - m_new)
            p = jnp.exp2(s - jnp.concatenate([m_new] * (ts // LANES), axis=1))
            l_ref[n, rows, :] = alpha * l_ref[n, rows, :] + jnp.sum(p, axis=-1, keepdims=True)
            m_ref[n, rows, :] = m_new
            a_ref[n, rows, :] = alpha
            p_ref[n, rows, :] = p.astype(BF16)

    def accumulate(n, start):
        acc_ref[n] = a_ref[n] * acc_ref[n] + _dot(p_ref[n], v_ref[0, pl.ds(start, ts), :])

    def block(start, live, diagonal_slab, live_next):
        prev = None
        for n in live:
            softmax(n, chains[n][0] == diagonal_slab)
            if n in live_next:
                logits(n, start + ts)
            if prev is not None:
                accumulate(prev, start)
            prev = n
        accumulate(prev, start)

    everyone = list(range(len(chains)))
    for n in everyone:
        logits(n, 0)

    def body(j, _):
        block(pl.multiple_of(j * ts, ts), everyone, None, everyone)
        return 0

    lax.fori_loop(0, i * nslab, body, 0)
    for c in range(nslab):
        live = [n for n in everyone if chains[n][0] >= c]
        live_next = [n for n in everyone if chains[n][0] >= c + 1] if c + 1 < nslab else []
        block(pl.multiple_of(i * tq + c * ts, ts), live, c, live_next)

    lane = lax.broadcasted_iota(jnp.int32, (ts, LANES), 1)
    for h in range(nslab):
        o0 = acc_ref[2 * h] / l_ref[2 * h]
        o1 = acc_ref[2 * h + 1] / l_ref[2 * h + 1]
        o_ref[0, h * ts:(h + 1) * ts, :] = jnp.where(lane < HEAD_DIM, o0, o1).astype(BF16)


def _fox(qa, ka, y3d, tq=1024):
    b, _, s, _ = qa.shape
    nch = 2 * (tq // FOX_TS)
    return pl.pallas_call(
        functools.partial(_fox_kernel, tq=tq),
        grid=(b, N_PAIRS, s // tq),
        in_specs=[pl.BlockSpec((1, 2, tq, LANES), lambda i, p, j: (i, p, j, 0)),
                  pl.BlockSpec((1, 2, s, LANES), lambda i, p, j: (i, p, 0, 0)),
                  pl.BlockSpec((1, s, LANES), lambda i, p, j: (i, 0, C_FV // LANES + p))],
        out_specs=pl.BlockSpec((1, tq, LANES), lambda i, p, j: (i, j, p)),
        out_shape=jax.ShapeDtypeStruct((b, s, D_MODEL), BF16),
        scratch_shapes=[pltpu.VMEM((nch, FOX_TS, FOX_TS), F32),
                        pltpu.VMEM((nch, FOX_TS, FOX_TS), BF16),
                        pltpu.VMEM((nch, FOX_TS, LANES), F32),
                        pltpu.VMEM((nch, FOX_TS, LANES), F32),
                        pltpu.VMEM((nch, FOX_TS, LANES), F32),
                        pltpu.VMEM((nch, FOX_TS, LANES), F32)],
        compiler_params=_cparams(3),
        name="fox",
    )(qa, ka, y3d)


SB_KC = 256
SB_TS = 512
SB_MAX_LOG2 = 126.0
SB_DEAD_LOG2 = -(SB_MAX_LOG2 + 150.0)


SB_RB = 64
SB_UNROLL = 2


def _sb_kernel(q_ref, k_ref, v_ref, o_ref, qm_ref, z_ref, lb_ref, ic_ref, w_ref,
               rs_ref, r_ref, acc_ref, *, tq):
    i = pl.program_id(2)
    kc, ts, rb = SB_KC, SB_TS, SB_RB
    nsub = tq // kc
    nslab = tq // ts
    chains = [(h, e) for h in range(nslab) for e in range(2)]
    row = lax.broadcasted_iota(jnp.int32, (rb, kc), 0)
    col = lax.broadcasted_iota(jnp.int32, (rb, kc), 1)
    kr = lax.broadcasted_iota(jnp.int32, (kc, kc), 0)
    kcol = lax.broadcasted_iota(jnp.int32, (kc, kc), 1)
    at_or_after = (kr >= kcol).astype(BF16)

    lane = lax.broadcasted_iota(jnp.int32, (ts, LANES), 1)
    for n, (h, e) in enumerate(chains):
        q2 = q_ref[0, h * ts:(h + 1) * ts, :]
        mine = (lane < HEAD_DIM) if e == 0 else (lane >= HEAD_DIM)
        qm_ref[n] = jnp.where(mine, q2, jnp.zeros_like(q2))
    r_ref[...] = jnp.zeros(r_ref.shape, F32)
    acc_ref[...] = jnp.zeros(acc_ref.shape, F32)

    def logits(n, start):
        z_ref[n] = jnp.minimum(_dot_nt(qm_ref[n], k_ref[0, pl.ds(start, kc), :]), SB_MAX_LOG2)

    def log_terms(n, offset):
        for r0 in range(0, ts, rb):
            rows = slice(r0, r0 + rb)
            log_1m_beta = jnp.log(1.0 + jnp.exp2(z_ref[n, rows, :])) * (-LOG2E)
            if offset is not None:
                log_1m_beta = jnp.where(col + (offset - r0) < row, log_1m_beta, 0.0)
            lb_ref[n, rows, :] = log_1m_beta.astype(BF16)
            rs_ref[n, rows, :] = jnp.broadcast_to(
                jnp.sum(log_1m_beta, axis=-1, keepdims=True), (rb, LANES))

    def cumulate(n):
        ic_ref[n] = _dot(lb_ref[n], at_or_after)

    def weights(n, offset):
        for r0 in range(0, ts, rb):
            rows = slice(r0, r0 + rb)
            r = r_ref[n, rows, :]
            log_w = (z_ref[n, rows, :] + ic_ref[n, rows, :]
                     + jnp.concatenate([r] * (kc // LANES), axis=1))
            if offset is not None:
                log_w = jnp.where(col + (offset - r0) < row, log_w, NEG)
            w_ref[n, rows, :] = jnp.exp2(log_w.astype(BF16))
            r_ref[n, rows, :] = r + rs_ref[n, rows, :]

    def accumulate(n, start):
        acc_ref[n] += _dot(w_ref[n], v_ref[0, pl.ds(start, kc), :])

    def chunk(start, next_start, live, offsets, live_next):
        prev = None
        for n in live:
            log_terms(n, offsets[chains[n][0]])
            cumulate(n)
            if prev is not None:
                weights(prev, offsets[chains[prev][0]])
                if prev in live_next:
                    logits(prev, next_start)
                accumulate(prev, start)
            prev = n
        weights(prev, offsets[chains[prev][0]])
        if prev in live_next:
            logits(prev, next_start)
        accumulate(prev, start)
        for n in live_next:
            if n not in live:
                logits(n, next_start)

    def visibility(c):
        live, offsets = [], {}
        for n, (h, _) in enumerate(chains):
            if c * kc >= (h + 1) * ts:
                continue
            live.append(n)
            offsets[h] = None if (c + 1) * kc <= h * ts else c * kc - h * ts
        return live, offsets

    everyone = list(range(len(chains)))
    no_offsets = {h: None for h in range(nslab)}
    live, offsets = visibility(nsub - 1)
    for n in live:
        logits(n, pl.multiple_of(i * tq + (nsub - 1) * kc, kc))
    for c in reversed(range(nsub)):
        start = pl.multiple_of(i * tq + c * kc, kc)
        live_next = visibility(c - 1)[0] if c > 0 else everyone
        chunk(start, pl.multiple_of(jnp.maximum(start - kc, 0), kc), live, offsets, live_next)
        if c > 0:
            live, offsets = visibility(c - 1)

    def alive(first_chain):
        return (jnp.max(r_ref[first_chain:]) > SB_DEAD_LOG2).astype(jnp.int32)

    trips = i * (nsub // SB_UNROLL)

    def walk(t0, live, first_chain):
        def body(state):
            t, _ = state
            for u in range(SB_UNROLL):
                start = pl.multiple_of(i * tq - (t * SB_UNROLL + u + 1) * kc, kc)
                chunk(start, pl.multiple_of(jnp.maximum(start - kc, 0), kc), live, no_offsets,
                      live)
            return t + 1, alive(first_chain)

        t_end, _ = lax.while_loop(lambda state: (state[0] < trips) & (state[1] > 0), body,
                                  (t0, alive(first_chain)))
        return t_end

    t_mid = walk(jnp.int32(0), everyone, len(chains) - 2)
    walk(t_mid, everyone[:2], 0)
    for h in range(nslab):
        o_ref[0, h * ts:(h + 1) * ts, :] = jnp.where(
            lane < HEAD_DIM, acc_ref[2 * h], acc_ref[2 * h + 1]).astype(BF16)


def _sb(y3d, tq=1024):
    b, s, _ = y3d.shape
    nch = 2 * (tq // SB_TS)
    return pl.pallas_call(
        functools.partial(_sb_kernel, tq=tq),
        grid=(b, N_PAIRS, s // tq),
        in_specs=[pl.BlockSpec((1, tq, LANES), lambda i, p, j: (i, j, C_SQ // LANES + p)),
                  pl.BlockSpec((1, s, LANES), lambda i, p, j: (i, 0, C_SK // LANES + p)),
                  pl.BlockSpec((1, s, LANES), lambda i, p, j: (i, 0, C_SV // LANES + p))],
        out_specs=pl.BlockSpec((1, tq, LANES), lambda i, p, j: (i, j, p)),
        out_shape=jax.ShapeDtypeStruct((b, s, D_MODEL), BF16),
        scratch_shapes=[pltpu.VMEM((nch, SB_TS, LANES), BF16),
                        pltpu.VMEM((nch, SB_TS, SB_KC), F32),
                        pltpu.VMEM((nch, SB_TS, SB_KC), BF16),
                        pltpu.VMEM((nch, SB_TS, SB_KC), F32),
                        pltpu.VMEM((nch, SB_TS, SB_KC), BF16),
                        pltpu.VMEM((nch, SB_TS, LANES), F32),
                        pltpu.VMEM((nch, SB_TS, LANES), F32),
                        pltpu.VMEM((nch, SB_TS, LANES), F32)],
        compiler_params=_cparams(3),
        name="sb",
    )(y3d, y3d, y3d)


LRU_TC = 256
LRU_HALO = 8


def _lru_kernel(x_ref, g_ref, cw_ref, cb_ref, wa_ref, ba_ref, wx_ref, bx_ref, lam_ref,
                o_ref, xbuf_ref, h_ref, *, ts):
    @pl.when(pl.program_id(2) == 0)
    def _():
        xbuf_ref[0:LRU_HALO, :] = jnp.zeros((LRU_HALO, LRU_TC), F32)
        h_ref[...] = jnp.zeros_like(h_ref)

    xbuf_ref[LRU_HALO:LRU_HALO + ts, :] = x_ref[0].astype(F32)
    cw = cw_ref[...]
    xc = cb_ref[...] + xbuf_ref[LRU_HALO:LRU_HALO + ts, :] * cw[3:4, :]
    for d in range(1, 4):
        xc = xc + xbuf_ref[LRU_HALO - d:LRU_HALO - d + ts, :] * cw[3 - d:4 - d, :]
    xbuf_ref[0:LRU_HALO, :] = xbuf_ref[ts:ts + LRU_HALO, :]

    xcb = xc.astype(BF16)
    r = jax.nn.sigmoid(_dot(xcb, wa_ref[0]) + ba_ref[...])
    gi = jax.nn.sigmoid(_dot(xcb, wx_ref[0]) + bx_ref[...])
    lam = lam_ref[...]
    softplus_neg_lam = jnp.maximum(-lam, 0.0) + _log1p_exp_neg_abs(lam)
    log_a = -LRU_C * r * softplus_neg_lam
    a = jnp.exp(log_a)
    one_m_a2 = -jnp.tanh(log_a) * (a * a + 1.0)
    u = one_m_a2 * lax.rsqrt(jnp.maximum(one_m_a2, F32_TINY)) * (gi * xc)

    row = lax.broadcasted_iota(jnp.int32, (ts, LRU_TC), 0) % SUBLANES
    d = 1
    while d < SUBLANES:
        keep = row >= d
        a_sh = jnp.where(keep, pltpu.roll(a, d, axis=0), 1.0)
        u_sh = jnp.where(keep, pltpu.roll(u, d, axis=0), 0.0)
        u = a * u_sh + u
        a = a * a_sh
        d *= 2
    state = h_ref[...]
    groups = []
    for r0 in range(0, ts, SUBLANES):
        h = u[r0:r0 + SUBLANES, :] + a[r0:r0 + SUBLANES, :] * state
        groups.append(h)
        state = h[SUBLANES - 1:SUBLANES, :]
    h_ref[...] = state
    h = jnp.concatenate(groups, axis=0)
    o_ref[0] = (h * jax.nn.gelu(g_ref[0].astype(F32))).astype(BF16)


def _lru(y3d, cw, cb, wa_bd, ba, wx_bd, bx, lam, ts=256):
    b, s, _ = y3d.shape
    nct = D_MODEL // LRU_TC
    vec = lambda rows: pl.BlockSpec((rows, LRU_TC), lambda i, c, j: (0, c))
    mat = pl.BlockSpec((1, LRU_TC, LRU_TC), lambda i, c, j: (c, 0, 0))
    return pl.pallas_call(
        functools.partial(_lru_kernel, ts=ts),
        grid=(b, nct, s // ts),
        in_specs=[pl.BlockSpec((1, ts, LRU_TC), lambda i, c, j: (i, j, C_LX // LRU_TC + c)),
                  pl.BlockSpec((1, ts, LRU_TC), lambda i, c, j: (i, j, C_LG // LRU_TC + c)),
                  vec(4), vec(1), mat, vec(1), mat, vec(1), vec(1)],
        out_specs=pl.BlockSpec((1, ts, LRU_TC), lambda i, c, j: (i, j, c)),
        out_shape=jax.ShapeDtypeStruct((b, s, D_MODEL), BF16),
        scratch_shapes=[pltpu.VMEM((ts + LRU_HALO, LRU_TC), F32),
                        pltpu.VMEM((1, LRU_TC), F32)],
        compiler_params=_cparams(3),
        name="lru",
    )(y3d, y3d, cw, cb, wa_bd, ba, wx_bd, bx, lam)


def _memkv_kernel(mem_ref, g_ref, w_ref, gk_ref, k_ref, v_ref):
    x = mem_ref[0]
    h = x * lax.rsqrt(jnp.mean(x * x, axis=-1, keepdims=True) + EPS) * g_ref[...]
    kv = _dot(h.astype(BF16), w_ref[...])
    for hd in range(MEM_HEADS):
        kh = kv[:, hd * MEM_HD:(hd + 1) * MEM_HD]
        kh = kh * lax.rsqrt(jnp.mean(kh * kh, axis=-1, keepdims=True) + EPS) * gk_ref[...]
        k_ref[0, :, hd * MEM_HD:(hd + 1) * MEM_HD] = kh.astype(BF16)
    v_ref[0] = kv[:, D_MODEL:].astype(BF16)


def _memkv(mem, g, w_kv, gk):
    b = mem.shape[0]
    out = jax.ShapeDtypeStruct((b, N_MEM, D_MODEL), BF16)
    return pl.pallas_call(
        _memkv_kernel,
        grid=(b,),
        in_specs=[pl.BlockSpec((1, N_MEM, D_MODEL), lambda i: (i, 0, 0)),
                  pl.BlockSpec((1, D_MODEL), lambda i: (0, 0)),
                  pl.BlockSpec((D_MODEL, 2 * D_MODEL), lambda i: (0, 0)),
                  pl.BlockSpec((1, MEM_HD), lambda i: (0, 0))],
        out_specs=[pl.BlockSpec((1, N_MEM, D_MODEL), lambda i: (i, 0, 0)),
                   pl.BlockSpec((1, N_MEM, D_MODEL), lambda i: (i, 0, 0))],
        out_shape=[out, out],
        compiler_params=_cparams(1),
        name="memkv",
    )(mem, g, w_kv, gk)


def _mem_kernel(q_ref, k_ref, v_ref, gq_ref, o_ref):
    for hd in range(MEM_HEADS):
        sl = slice(hd * MEM_HD, (hd + 1) * MEM_HD)
        q = q_ref[0, :, sl].astype(F32)
        q = q * lax.rsqrt(jnp.mean(q * q, axis=-1, keepdims=True) + EPS) * gq_ref[...]
        q = (q * (MEM_HD ** -0.5)).astype(BF16)
        s = _dot_nt(q, k_ref[0, :, sl])
        p = jnp.exp(s - jnp.max(s, axis=-1, keepdims=True))
        o = _dot(p.astype(BF16), v_ref[0, :, sl]) / jnp.sum(p, axis=-1, keepdims=True)
        o_ref[0, :, sl] = o.astype(BF16)


def _mem(y3d, mk, mv, gq, ts=512):
    b, s, _ = y3d.shape
    return pl.pallas_call(
        _mem_kernel,
        grid=(b, s // ts),
        in_specs=[pl.BlockSpec((1, ts, D_MODEL), lambda i, j: (i, j, C_MQ // D_MODEL)),
                  pl.BlockSpec((1, N_MEM, D_MODEL), lambda i, j: (i, 0, 0)),
                  pl.BlockSpec((1, N_MEM, D_MODEL), lambda i, j: (i, 0, 0)),
                  pl.BlockSpec((1, MEM_HD), lambda i, j: (0, 0))],
        out_specs=pl.BlockSpec((1, ts, D_MODEL), lambda i, j: (i, j, 0)),
        out_shape=jax.ShapeDtypeStruct((b, s, D_MODEL), BF16),
        compiler_params=_cparams(2),
        name="mem",
    )(y3d, mk, mv, gq)


def _merge_kernel(x_ref, b0_ref, b1_ref, b2_ref, b3_ref, gt_ref, bg_ref, wb_ref, wo_ref, o_ref):
    mixed = None
    for n, br in enumerate((b0_ref, b1_ref, b2_ref, b3_ref)):
        gate = jax.nn.sigmoid(
            gt_ref[:, n * D_MODEL:(n + 1) * D_MODEL].astype(F32) + bg_ref[n:n + 1, :])
        term = gate * _dot(br[...], wb_ref[n])
        mixed = term if mixed is None else mixed + term
    o_ref[...] = x_ref[...] + _dot(mixed.astype(BF16), wo_ref[...])


def _merge(x2d, branches, y2d, bg, wb, wo, tm=256):
    m = x2d.shape[0]
    row = lambda w: pl.BlockSpec((tm, w), lambda i: (i, 0))
    return pl.pallas_call(
        _merge_kernel,
        grid=(m // tm,),
        in_specs=[row(D_MODEL), row(D_MODEL), row(D_MODEL), row(D_MODEL), row(D_MODEL),
                  pl.BlockSpec((tm, 4 * D_MODEL), lambda i: (i, C_GT // (4 * D_MODEL))),
                  pl.BlockSpec((4, D_MODEL), lambda i: (0, 0)),
                  pl.BlockSpec((4, D_MODEL, D_MODEL), lambda i: (0, 0, 0)),
                  pl.BlockSpec((D_MODEL, D_MODEL), lambda i: (0, 0))],
        out_specs=row(D_MODEL),
        out_shape=jax.ShapeDtypeStruct((m, D_MODEL), F32),
        compiler_params=_cparams(1),
        name="merge",
    )(x2d, *branches, y2d, bg, wb, wo)


FFN_TC = 256
FFN_HALO = 8
FFN_RB = 64


def _ffn_kernel(x_ref, g_ref, wg_ref, wv_ref, cw_ref, cb_ref, wd_ref, o_ref,
                h_ref, gbuf_ref, vbuf_ref, act_ref, *, tm, tiles_per_seq):
    i = pl.program_id(0)
    x = x_ref[...]
    h_ref[...] = (x * lax.rsqrt(jnp.mean(x * x, axis=-1, keepdims=True) + EPS)
                  * g_ref[...]).astype(BF16)

    @pl.when((i % tiles_per_seq) == 0)
    def _():
        gbuf_ref[0:FFN_HALO, :] = jnp.zeros((FFN_HALO, D_FF), F32)

    def up(c):
        cols = slice(c * FFN_TC, (c + 1) * FFN_TC)
        gbuf_ref[FFN_HALO:FFN_HALO + tm, cols] = _dot(h_ref[...], wg_ref[:, cols])
        vbuf_ref[:, cols] = _dot(h_ref[...], wv_ref[:, cols])

    def activate(c):
        cols = slice(c * FFN_TC, (c + 1) * FFN_TC)
        cw = cw_ref[:, cols]
        cb = cb_ref[:, cols]
        for r0 in range(0, tm, FFN_RB):
            gate = cb
            for d in range(3):
                lo = FFN_HALO + r0 - d
                gate = gate + gbuf_ref[lo:lo + FFN_RB, cols] * cw[2 - d:3 - d, :]
            act = gate * jax.nn.sigmoid(gate) * vbuf_ref[r0:r0 + FFN_RB, cols]
            act_ref[r0:r0 + FFN_RB, cols] = act.astype(BF16)

    nc = D_FF // FFN_TC
    up(0)
    for c in range(1, nc):
        up(c)
        activate(c - 1)
    activate(nc - 1)
    o_ref[...] = x_ref[...] + _dot(act_ref[...], wd_ref[...])
    gbuf_ref[0:FFN_HALO, :] = gbuf_ref[tm:tm + FFN_HALO, :]


def _ffn(x2d, g, w_up, cw, cb, w_down, seq, tm=512):
    m = x2d.shape[0]
    resident = pl.Buffered(1)
    return pl.pallas_call(
        functools.partial(_ffn_kernel, tm=tm, tiles_per_seq=seq // tm),
        grid=(m // tm,),
        in_specs=[pl.BlockSpec((tm, D_MODEL), lambda i: (i, 0)),
                  pl.BlockSpec((1, D_MODEL), lambda i: (0, 0)),
                  pl.BlockSpec((D_MODEL, D_FF), lambda i: (0, 0), pipeline_mode=resident),
                  pl.BlockSpec((D_MODEL, D_FF), lambda i: (0, 1), pipeline_mode=resident),
                  pl.BlockSpec((3, D_FF), lambda i: (0, 0)),
                  pl.BlockSpec((1, D_FF), lambda i: (0, 0)),
                  pl.BlockSpec((D_FF, D_MODEL), lambda i: (0, 0), pipeline_mode=resident)],
        out_specs=pl.BlockSpec((tm, D_MODEL), lambda i: (i, 0)),
        out_shape=jax.ShapeDtypeStruct((m, D_MODEL), F32),
        scratch_shapes=[pltpu.VMEM((tm, D_MODEL), BF16),
                        pltpu.VMEM((tm + FFN_HALO, D_FF), F32),
                        pltpu.VMEM((tm, D_FF), F32),
                        pltpu.VMEM((tm, D_FF), BF16)],
        compiler_params=_cparams(1),
        name="ffn",
    )(x2d, g, w_up, w_up, cw, cb, w_down)


def _block_diag(w):
    per = LRU_TC // HEAD_DIM
    w4 = w.reshape(D_MODEL // LRU_TC, per, HEAD_DIM, HEAD_DIM)
    eye = jnp.eye(per, dtype=w.dtype)
    bd = jnp.einsum('cpde,pq->cpdqe', w4, eye)
    return bd.reshape(D_MODEL // LRU_TC, LRU_TC, LRU_TC)


def kernel(x, mem, attn_norm_g, mem_norm_g, w_in, b_forget, fox_q_norm_g, fox_k_norm_g,
           lru_conv_w, lru_conv_b, lru_w_a, lru_b_a, lru_w_x, lru_b_x, lru_lambda,
           w_mem_kv, mem_q_norm_g, mem_k_norm_g, b_gate, w_branch, w_out,
           ffn_norm_g, w_up, ffn_conv_w, ffn_conv_b, w_down):
    b, s, d = x.shape
    depth = w_in.shape[0]
    m = b * s
    x2d = x.reshape(m, d)
    for l in range(depth):
        wy = jnp.concatenate([w_in[l, :, W_GATES:], w_in[l, :, :W_FOX_END],
                              w_in[l, :, W_F_END:W_GATES]], axis=1)
        sb_q_scale = jnp.where((jnp.arange(N_Y) >= C_SQ) & (jnp.arange(N_Y) < C_SK),
                               HEAD_DIM ** -0.5 * LOG2E, 1.0).astype(F32)
        wy = (wy * sb_q_scale).astype(BF16)
        wf = jnp.pad(w_in[l, :, W_FOX_END:W_F_END], ((0, 0), (0, LANES - N_HEADS)))
        bf = jnp.pad(b_forget[l], (0, LANES - N_HEADS)).reshape(1, LANES)

        y2d, f2d = _proj(x2d, attn_norm_g[l].reshape(1, d), wy, wf)
        y3d = y2d.reshape(b, s, N_Y)

        fcum = _fcum(f2d.reshape(b, s, LANES), bf)
        qa, ka = _foxprep(y3d, fcum,
                          jnp.tile(fox_q_norm_g[l], 2).reshape(1, LANES),
                          jnp.tile(fox_k_norm_g[l], 2).reshape(1, LANES))
        y_fox = _fox(qa, ka, y3d)
        y_sb = _sb(y3d)
        y_lru = _lru(y3d, lru_conv_w[l], lru_conv_b[l].reshape(1, d),
                     _block_diag(lru_w_a[l]).astype(BF16), lru_b_a[l].reshape(1, d),
                     _block_diag(lru_w_x[l]).astype(BF16), lru_b_x[l].reshape(1, d),
                     lru_lambda[l].reshape(1, d))
        mk, mv = _memkv(mem, mem_norm_g[l].reshape(1, d), w_mem_kv[l].astype(BF16),
                        mem_k_norm_g[l].reshape(1, MEM_HD))
        y_mem = _mem(y3d, mk, mv, mem_q_norm_g[l].reshape(1, MEM_HD))

        branches = [t.reshape(m, d) for t in (y_fox, y_lru, y_sb, y_mem)]
        x2d = _merge(x2d, branches, y2d, b_gate[l], w_branch[l].astype(BF16),
                     w_out[l].astype(BF16))
        x2d = _ffn(x2d, ffn_norm_g[l].reshape(1, d), w_up[l].astype(BF16), ffn_conv_w[l],
                   ffn_conv_b[l].reshape(1, D_FF), w_down[l].astype(BF16), s)
    return x2d.reshape(b, s, d)
```

```python
import functools

import jax
import jax.numpy as jnp
from jax import lax
from jax.experimental import pallas as pl
from jax.experimental.pallas import tpu as pltpu

F32 = jnp.float32
BF16 = jnp.bfloat16
HIGHEST = lax.Precision.HIGHEST

D_MODEL = 1024
HEAD_DIM = 64
N_HEADS = 16
N_PAIRS = N_HEADS // 2
LANES = 128
SUBLANES = 8
N_MEM = 256
MEM_HEADS = 4
MEM_HD = 256
D_FF = 2816
LRU_C = 8.0
EPS = 1e-6
NEG = -1e30
LOG2E = 1.4426950408889634
F32_TINY = 1.1754944e-38

C_GT, C_FQ, C_FK, C_FV, C_LX, C_LG, C_SQ, C_SK, C_SV, C_MQ = (
    0, 4096, 5120, 6144, 7168, 8192, 9216, 10240, 11264, 12288)
N_Y = 13312
W_FOX_END, W_F_END, W_GATES = 3072, 3088, 9232

VMEM_LIMIT = 56 * 1024 * 1024


def _cparams(n_axes):
    return pltpu.CompilerParams(dimension_semantics=("arbitrary",) * n_axes,
                                vmem_limit_bytes=VMEM_LIMIT)


def _dot(a, b):
    return jnp.dot(a, b, preferred_element_type=F32)


def _dot_nt(a, b):
    return lax.dot_general(a, b, (((1,), (1,)), ((), ())), preferred_element_type=F32)


def _log1p_exp_neg_abs(z):
    return jnp.log(1.0 + jnp.exp(-jnp.abs(z)))


def _proj_kernel(x_ref, g_ref, w_ref, wf_ref, y_ref, f_ref, h_ref):
    @pl.when(pl.program_id(1) == 0)
    def _():
        x = x_ref[...]
        h = x * lax.rsqrt(jnp.mean(x * x, axis=-1, keepdims=True) + EPS) * g_ref[...]
        h_ref[...] = h.astype(BF16)
        f_ref[...] = jnp.dot(h, wf_ref[...], precision=HIGHEST, preferred_element_type=F32)

    y_ref[...] = _dot(h_ref[...], w_ref[...]).astype(BF16)


def _proj(x2d, g, wy, wf, tm=1024, tn=1024):
    m = x2d.shape[0]
    return pl.pallas_call(
        _proj_kernel,
        grid=(m // tm, N_Y // tn),
        in_specs=[pl.BlockSpec((tm, D_MODEL), lambda i, j: (i, 0)),
                  pl.BlockSpec((1, D_MODEL), lambda i, j: (0, 0)),
                  pl.BlockSpec((D_MODEL, tn), lambda i, j: (0, j)),
                  pl.BlockSpec((D_MODEL, LANES), lambda i, j: (0, 0))],
        out_specs=[pl.BlockSpec((tm, tn), lambda i, j: (i, j)),
                   pl.BlockSpec((tm, LANES), lambda i, j: (i, 0))],
        out_shape=[jax.ShapeDtypeStruct((m, N_Y), BF16),
                   jax.ShapeDtypeStruct((m, LANES), F32)],
        scratch_shapes=[pltpu.VMEM((tm, D_MODEL), BF16)],
        compiler_params=_cparams(2),
        name="proj",
    )(x2d, g, wy, wf)


def _fcum_kernel(f_ref, b_ref, o_ref, carry_ref):
    @pl.when(pl.program_id(1) == 0)
    def _():
        carry_ref[...] = jnp.zeros_like(carry_ref)

    z = f_ref[0] + b_ref[...]
    log_f = jnp.minimum(z, 0.0) - _log1p_exp_neg_abs(z)
    ts = z.shape[0]
    row = lax.broadcasted_iota(jnp.int32, (ts, ts), 0)
    col = lax.broadcasted_iota(jnp.int32, (ts, ts), 1)
    tri = (row >= col).astype(F32)
    c = jnp.dot(tri, log_f, precision=HIGHEST, preferred_element_type=F32) + carry_ref[...]
    carry_ref[...] = c[ts - 1:ts, :]

    lane = lax.broadcasted_iota(jnp.int32, (ts, LANES), 1)
    f = jnp.where(lane < N_HEADS, c * LOG2E, 0.0)
    hi = f.astype(BF16).astype(F32)
    r1 = f - hi
    mid = r1.astype(BF16).astype(F32)
    lo = (r1 - mid).astype(BF16).astype(F32)
    packed = hi + pltpu.roll(mid, N_HEADS, axis=1) + pltpu.roll(lo, 2 * N_HEADS, axis=1)
    o_ref[0] = packed.astype(BF16)


def _fcum(f3d, b_pad, ts=512):
    b, s, _ = f3d.shape
    return pl.pallas_call(
        _fcum_kernel,
        grid=(b, s // ts),
        in_specs=[pl.BlockSpec((1, ts, LANES), lambda i, j: (i, j, 0)),
                  pl.BlockSpec((1, LANES), lambda i, j: (0, 0))],
        out_specs=pl.BlockSpec((1, ts, LANES), lambda i, j: (i, j, 0)),
        out_shape=jax.ShapeDtypeStruct((b, s, LANES), BF16),
        scratch_shapes=[pltpu.VMEM((1, LANES), F32)],
        compiler_params=_cparams(2),
        name="fcum",
    )(f3d, b_pad)


def _foxprep_kernel(yq_ref, yk_ref, yv_ref, f_ref, gq_ref, gk_ref, qa_ref, ka_ref, vt_ref):
    hp = pl.program_id(1)
    for c in range(vt_ref.shape[2]):
        v = yv_ref[0, c * FOX_TS:(c + 1) * FOX_TS, :]
        vt_ref[0, 0, c] = v.astype(F32).T.astype(BF16)
    ts = yq_ref.shape[1]
    lane = lax.broadcasted_iota(jnp.int32, (ts, LANES), 1)
    lo_half = lane < HEAD_DIM

    hr = lax.broadcasted_iota(jnp.int32, (LANES, LANES), 0)
    hc = lax.broadcasted_iota(jnp.int32, (LANES, LANES), 1)
    same_head = ((hr < HEAD_DIM) == (hc < HEAD_DIM)).astype(BF16)

    def headnorm(y, g):
        ms = _dot((y * y).astype(BF16), same_head) * (1.0 / HEAD_DIM)
        return y * lax.rsqrt(ms + EPS) * g

    qn = headnorm(yq_ref[0].astype(F32), gq_ref[...]) * (HEAD_DIM ** -0.5 * LOG2E)
    kn = headnorm(yk_ref[0].astype(F32), gk_ref[...])

    pieces = f_ref[0]
    sel_r = lax.broadcasted_iota(jnp.int32, (LANES, 2 * LANES), 0)
    sel_c = lax.broadcasted_iota(jnp.int32, (LANES, 2 * LANES), 1)
    ones_q = ((lane >= HEAD_DIM + 3) & (lane < HEAD_DIM + 6)).astype(F32)
    ones_k = ((lane >= HEAD_DIM) & (lane < HEAD_DIM + 3)).astype(F32)
    for e in range(2):
        head = 2 * hp + e
        piece_of_row = sel_r - head
        to_q = (piece_of_row == (sel_c - HEAD_DIM) * N_HEADS) & (sel_c < HEAD_DIM + 3)
        to_k = (piece_of_row == (sel_c - (LANES + HEAD_DIM + 3)) * N_HEADS) & (
            sel_c >= LANES + HEAD_DIM + 3)
        valid = (piece_of_row >= 0) & (piece_of_row <= 2 * N_HEADS)
        sel = (jnp.where(valid & to_q, 1.0, 0.0) - jnp.where(valid & to_k, 1.0, 0.0)).astype(BF16)
        placed = _dot(pieces, sel)
        q_part = qn if e == 0 else pltpu.roll(qn, HEAD_DIM, axis=1)
        k_part = kn if e == 0 else pltpu.roll(kn, HEAD_DIM, axis=1)
        q_aug = jnp.where(lo_half, q_part, placed[:, :LANES] + ones_q)
        k_aug = jnp.where(lo_half, k_part, placed[:, LANES:] + ones_k)
        qa_ref[0, e] = q_aug.astype(BF16)
        ka_ref[0, e] = k_aug.astype(BF16)


def _foxprep(y3d, fcum, gq2, gk2, ts=2048):
    b, s, _ = y3d.shape
    out = jax.ShapeDtypeStruct((b, N_HEADS, s, LANES), BF16)
    vt = jax.ShapeDtypeStruct((b, N_PAIRS, s // FOX_TS, LANES, FOX_TS), BF16)
    return pl.pallas_call(
        _foxprep_kernel,
        grid=(b, N_PAIRS, s // ts),
        in_specs=[pl.BlockSpec((1, ts, LANES), lambda i, p, j: (i, j, C_FQ // LANES + p)),
                  pl.BlockSpec((1, ts, LANES), lambda i, p, j: (i, j, C_FK // LANES + p)),
                  pl.BlockSpec((1, ts, LANES), lambda i, p, j: (i, j, C_FV // LANES + p)),
                  pl.BlockSpec((1, ts, LANES), lambda i, p, j: (i, j, 0)),
                  pl.BlockSpec((1, LANES), lambda i, p, j: (0, 0)),
                  pl.BlockSpec((1, LANES), lambda i, p, j: (0, 0))],
        out_specs=[pl.BlockSpec((1, 2, ts, LANES), lambda i, p, j: (i, p, j, 0)),
                   pl.BlockSpec((1, 2, ts, LANES), lambda i, p, j: (i, p, j, 0)),
                   pl.BlockSpec((1, 1, ts // FOX_TS, LANES, FOX_TS),
                                lambda i, p, j: (i, p, j, 0, 0))],
        out_shape=[out, out, vt],
        compiler_params=_cparams(3),
        name="foxprep",
    )(y3d, y3d, y3d, fcum, gq2, gk2)


FOX_TS = 512
FOX_KB = 64


def _fox_kernel(qa_ref, ka_ref, vt_ref, o_ref, s_ref, p_ref, m_ref, l_ref, a_ref, acc_ref,
                *, tq):
    i = pl.program_id(2)
    ts = FOX_TS
    nslab = tq // ts
    chains = [(h, e) for h in range(nslab) for e in range(2)]
    key = lax.broadcasted_iota(jnp.int32, (FOX_KB, LANES), 0)
    query = lax.broadcasted_iota(jnp.int32, (FOX_KB, LANES), 1)

    m_ref[...] = jnp.full(m_ref.shape, NEG, F32)
    l_ref[...] = jnp.zeros(l_ref.shape, F32)
    acc_ref[...] = jnp.zeros(acc_ref.shape, F32)

    def logits(n, block):
        h, e = chains[n]
        start = pl.multiple_of(block * ts, ts)
        s_ref[n] = _dot_nt(ka_ref[0, e, pl.ds(start, ts), :],
                           qa_ref[0, e, h * ts:(h + 1) * ts, :])

    def softmax(n, diagonal):
        kb = FOX_KB
        for c0 in range(0, ts, LANES):
            cols = slice(c0, c0 + LANES)

            def piece(k0):
                s = s_ref[n, k0:k0 + kb, cols]
                if diagonal:
                    s = jnp.where(key + k0 <= query + c0, s, NEG)
                return s

            top = piece(0)
            for k0 in range(kb, ts, kb):
                top = jnp.maximum(top, piece(k0))
            m_old = m_ref[n, :, cols]
            m_new = jnp.maximum(m_old, jnp.max(top, axis=0, keepdims=True))
            alpha = jnp.exp2(m_old - m_new)
            total = None
            for k0 in range(0, ts, kb):
                p = jnp.exp2(piece(k0) - m_new[0:1, :])
                p_ref[n, k0:k0 + kb, cols] = p.astype(BF16)
                total = p if total is None else total + p
            l_ref[n, :, cols] = alpha * l_ref[n, :, cols] + jnp.sum(total, axis=0, keepdims=True)
            m_ref[n, :, cols] = m_new
            a_ref[n, :, cols] = alpha

    def accumulate(n, block):
        e = chains[n][1]
        vt = vt_ref[0, 0, block, e * HEAD_DIM:(e + 1) * HEAD_DIM, :]
        acc_ref[n] = a_ref[n, 0:1, :] * acc_ref[n] + _dot(vt, p_ref[n])

    def step(block, live, diagonal_slab, live_next):
        prev = None
        for n in live:
            softmax(n, chains[n][0] == diagonal_slab)
            if n in live_next:
                logits(n, block + 1)
            if prev is not None:
                accumulate(prev, block)
            prev = n
        accumulate(prev, block)

    everyone = list(range(len(chains)))
    for n in everyone:
        logits(n, 0)

    def body(j, _):
        step(j, everyone, None, everyone)
        return 0

    lax.fori_loop(0, i * nslab, body, 0)
    for c in range(nslab):
        live = [n for n in everyone if chains[n][0] >= c]
        live_next = [n for n in everyone if chains[n][0] >= c + 1] if c + 1 < nslab else []
        step(i * nslab + c, live, c, live_next)

    for h in range(nslab):
        o0 = acc_ref[2 * h] / l_ref[2 * h, 0:1, :]
        o1 = acc_ref[2 * h + 1] / l_ref[2 * h + 1, 0:1, :]
        o_ref[0, h * ts:(h + 1) * ts, :] = jnp.concatenate([o0, o1], axis=0).T.astype(BF16)


def _fox(qa, ka, vt, tq=1024):
    b, _, s, _ = qa.shape
    nch = 2 * (tq // FOX_TS)
    return pl.pallas_call(
        functools.partial(_fox_kernel, tq=tq),
        grid=(b, N_PAIRS, s // tq),
        in_specs=[pl.BlockSpec((1, 2, tq, LANES), lambda i, p, j: (i, p, j, 0)),
                  pl.BlockSpec((1, 2, s, LANES), lambda i, p, j: (i, p, 0, 0)),
                  pl.BlockSpec((1, 1, s // FOX_TS, LANES, FOX_TS),
                               lambda i, p, j: (i, p, 0, 0, 0))],
        out_specs=pl.BlockSpec((1, tq, LANES), lambda i, p, j: (i, j, p)),
        out_shape=jax.ShapeDtypeStruct((b, s, D_MODEL), BF16),
        scratch_shapes=[pltpu.VMEM((nch, FOX_TS, FOX_TS), F32),
                        pltpu.VMEM((nch, FOX_TS, FOX_TS), BF16),
                        pltpu.VMEM((nch, SUBLANES, FOX_TS), F32),
                        pltpu.VMEM((nch, SUBLANES, FOX_TS), F32),
                        pltpu.VMEM((nch, SUBLANES, FOX_TS), F32),
                        pltpu.VMEM((nch, HEAD_DIM, FOX_TS), F32)],
        compiler_params=_cparams(3),
        name="fox",
    )(qa, ka, vt)


SB_KC = 256
SB_TS = 512
SB_MAX_LOG2 = 126.0
SB_DEAD_LOG2 = -(SB_MAX_LOG2 + 150.0)


SB_RB = 64
SB_UNROLL = 2


def _sb_kernel(q_ref, k_ref, v_ref, o_ref, qm_ref, z_ref, lb_ref, ic_ref, w_ref,
               rs_ref, r_ref, acc_ref, *, tq):
    i = pl.program_id(2)
    kc, ts, rb = SB_KC, SB_TS, SB_RB
    nsub = tq // kc
    nslab = tq // ts
    chains = [(h, e) for h in range(nslab) for e in range(2)]
    row = lax.broadcasted_iota(jnp.int32, (rb, kc), 0)
    col = lax.broadcasted_iota(jnp.int32, (rb, kc), 1)
    kr = lax.broadcasted_iota(jnp.int32, (kc, kc), 0)
    kcol = lax.broadcasted_iota(jnp.int32, (kc, kc), 1)
    at_or_after = (kr >= kcol).astype(BF16)

    lane = lax.broadcasted_iota(jnp.int32, (ts, LANES), 1)
    for n, (h, e) in enumerate(chains):
        q2 = q_ref[0, h * ts:(h + 1) * ts, :]
        mine = (lane < HEAD_DIM) if e == 0 else (lane >= HEAD_DIM)
        qm_ref[n] = jnp.where(mine, q2, jnp.zeros_like(q2))
    r_ref[...] = jnp.zeros(r_ref.shape, F32)
    acc_ref[...] = jnp.zeros(acc_ref.shape, F32)

    def logits(n, start):
        z_ref[n] = jnp.minimum(_dot_nt(qm_ref[n], k_ref[0, pl.ds(start, kc), :]), SB_MAX_LOG2)

    def log_terms(n, offset):
        for r0 in range(0, ts, rb):
            rows = slice(r0, r0 + rb)
            log_1m_beta = jnp.log(1.0 + jnp.exp2(z_ref[n, rows, :])) * (-LOG2E)
            if offset is not None:
                log_1m_beta = jnp.where(col + (offset - r0) < row, log_1m_beta, 0.0)
            lb_ref[n, rows, :] = log_1m_beta.astype(BF16)
            rs_ref[n, rows, :] = jnp.broadcast_to(
                jnp.sum(log_1m_beta, axis=-1, keepdims=True), (rb, LANES))

    def cumulate(n):
        ic_ref[n] = _dot(lb_ref[n], at_or_after)

    def weights(n, offset):
        for r0 in range(0, ts, rb):
            rows = slice(r0, r0 + rb)
            r = r_ref[n, rows, :]
            log_w = (z_ref[n, rows, :] + ic_ref[n, rows, :]
                     + jnp.concatenate([r] * (kc // LANES), axis=1))
            if offset is not None:
                log_w = jnp.where(col + (offset - r0) < row, log_w, NEG)
            w_ref[n, rows, :] = jnp.exp2(log_w.astype(BF16))
            r_ref[n, rows, :] = r + rs_ref[n, rows, :]

    def accumulate(n, start):
        acc_ref[n] += _dot(w_ref[n], v_ref[0, pl.ds(start, kc), :])

    def chunk(start, next_start, live, offsets, live_next):
        prev = None
        for n in live:
            log_terms(n, offsets[chains[n][0]])
            cumulate(n)
            if prev is not None:
                weights(prev, offsets[chains[prev][0]])
                if prev in live_next:
                    logits(prev, next_start)
                accumulate(prev, start)
            prev = n
        weights(prev, offsets[chains[prev][0]])
        if prev in live_next:
            logits(prev, next_start)
        accumulate(prev, start)
        for n in live_next:
            if n not in live:
                logits(n, next_start)

    def visibility(c):
        live, offsets = [], {}
        for n, (h, _) in enumerate(chains):
            if c * kc >= (h + 1) * ts:
                continue
            live.append(n)
            offsets[h] = None if (c + 1) * kc <= h * ts else c * kc - h * ts
        return live, offsets

    everyone = list(range(len(chains)))
    no_offsets = {h: None for h in range(nslab)}
    live, offsets = visibility(nsub - 1)
    for n in live:
        logits(n, pl.multiple_of(i * tq + (nsub - 1) * kc, kc))
    for c in reversed(range(nsub)):
        start = pl.multiple_of(i * tq + c * kc, kc)
        live_next = visibility(c - 1)[0] if c > 0 else everyone
        chunk(start, pl.multiple_of(jnp.maximum(start - kc, 0), kc), live, offsets, live_next)
        if c > 0:
            live, offsets = visibility(c - 1)

    def alive(first_chain):
        return (jnp.max(r_ref[first_chain:]) > SB_DEAD_LOG2).astype(jnp.int32)

    trips = i * (nsub // SB_UNROLL)

    def walk(t0, live, first_chain):
        def body(state):
            t, _ = state
            for u in range(SB_UNROLL):
                start = pl.multiple_of(i * tq - (t * SB_UNROLL + u + 1) * kc, kc)
                chunk(start, pl.multiple_of(jnp.maximum(start - kc, 0), kc), live, no_offsets,
                      live)
            return t + 1, alive(first_chain)

        t_end, _ = lax.while_loop(lambda state: (state[0] < trips) & (state[1] > 0), body,
                                  (t0, alive(first_chain)))
        return t_end

    t_mid = walk(jnp.int32(0), everyone, len(chains) - 2)
    walk(t_mid, everyone[:2], 0)
    for h in range(nslab):
        o_ref[0, h * ts:(h + 1) * ts, :] = jnp.where(
            lane < HEAD_DIM, acc_ref[2 * h], acc_ref[2 * h + 1]).astype(BF16)


def _sb(y3d, tq=1024):
    b, s, _ = y3d.shape
    nch = 2 * (tq // SB_TS)
    return pl.pallas_call(
        functools.partial(_sb_kernel, tq=tq),
        grid=(b, N_PAIRS, s // tq),
        in_specs=[pl.BlockSpec((1, tq, LANES), lambda i, p, j: (i, j, C_SQ // LANES + p)),
                  pl.BlockSpec((1, s, LANES), lambda i, p, j: (i, 0, C_SK // LANES + p)),
                  pl.BlockSpec((1, s, LANES), lambda i, p, j: (i, 0, C_SV // LANES + p))],
        out_specs=pl.BlockSpec((1, tq, LANES), lambda i, p, j: (i, j, p)),
        out_shape=jax.ShapeDtypeStruct((b, s, D_MODEL), BF16),
        scratch_shapes=[pltpu.VMEM((nch, SB_TS, LANES), BF16),
                        pltpu.VMEM((nch, SB_TS, SB_KC), F32),
                        pltpu.VMEM((nch, SB_TS, SB_KC), BF16),
                        pltpu.VMEM((nch, SB_TS, SB_KC), F32),
                        pltpu.VMEM((nch, SB_TS, SB_KC), BF16),
                        pltpu.VMEM((nch, SB_TS, LANES), F32),
                        pltpu.VMEM((nch, SB_TS, LANES), F32),
                        pltpu.VMEM((nch, SB_TS, LANES), F32)],
        compiler_params=_cparams(3),
        name="sb",
    )(y3d, y3d, y3d)


LRU_TC = 256
LRU_HALO = 8


def _lru_kernel(x_ref, g_ref, cw_ref, cb_ref, wa_ref, ba_ref, wx_ref, bx_ref, lam_ref,
                o_ref, xbuf_ref, h_ref, *, ts):
    @pl.when(pl.program_id(2) == 0)
    def _():
        xbuf_ref[0:LRU_HALO, :] = jnp.zeros((LRU_HALO, LRU_TC), F32)
        h_ref[...] = jnp.zeros_like(h_ref)

    xbuf_ref[LRU_HALO:LRU_HALO + ts, :] = x_ref[0].astype(F32)
    cw = cw_ref[...]
    xc = cb_ref[...] + xbuf_ref[LRU_HALO:LRU_HALO + ts, :] * cw[3:4, :]
    for d in range(1, 4):
        xc = xc + xbuf_ref[LRU_HALO - d:LRU_HALO - d + ts, :] * cw[3 - d:4 - d, :]
    xbuf_ref[0:LRU_HALO, :] = xbuf_ref[ts:ts + LRU_HALO, :]

    xcb = xc.astype(BF16)
    r = jax.nn.sigmoid(_dot(xcb, wa_ref[0]) + ba_ref[...])
    gi = jax.nn.sigmoid(_dot(xcb, wx_ref[0]) + bx_ref[...])
    lam = lam_ref[...]
    softplus_neg_lam = jnp.maximum(-lam, 0.0) + _log1p_exp_neg_abs(lam)
    log_a = -LRU_C * r * softplus_neg_lam
    a = jnp.exp(log_a)
    one_m_a2 = -jnp.tanh(log_a) * (a * a + 1.0)
    u = one_m_a2 * lax.rsqrt(jnp.maximum(one_m_a2, F32_TINY)) * (gi * xc)

    row = lax.broadcasted_iota(jnp.int32, (ts, LRU_TC), 0) % SUBLANES
    d = 1
    while d < SUBLANES:
        keep = row >= d
        a_sh = jnp.where(keep, pltpu.roll(a, d, axis=0), 1.0)
        u_sh = jnp.where(keep, pltpu.roll(u, d, axis=0), 0.0)
        u = a * u_sh + u
        a = a * a_sh
        d *= 2
    state = h_ref[...]
    groups = []
    for r0 in range(0, ts, SUBLANES):
        h = u[r0:r0 + SUBLANES, :] + a[r0:r0 + SUBLANES, :] * state
        groups.append(h)
        state = h[SUBLANES - 1:SUBLANES, :]
    h_ref[...] = state
    h = jnp.concatenate(groups, axis=0)
    o_ref[0] = (h * jax.nn.gelu(g_ref[0].astype(F32))).astype(BF16)


def _lru(y3d, cw, cb, wa_bd, ba, wx_bd, bx, lam, ts=512):
    b, s, _ = y3d.shape
    nct = D_MODEL // LRU_TC
    vec = lambda rows: pl.BlockSpec((rows, LRU_TC), lambda i, c, j: (0, c))
    mat = pl.BlockSpec((1, LRU_TC, LRU_TC), lambda i, c, j: (c, 0, 0))
    return pl.pallas_call(
        functools.partial(_lru_kernel, ts=ts),
        grid=(b, nct, s // ts),
        in_specs=[pl.BlockSpec((1, ts, LRU_TC), lambda i, c, j: (i, j, C_LX // LRU_TC + c)),
                  pl.BlockSpec((1, ts, LRU_TC), lambda i, c, j: (i, j, C_LG // LRU_TC + c)),
                  vec(4), vec(1), mat, vec(1), mat, vec(1), vec(1)],
        out_specs=pl.BlockSpec((1, ts, LRU_TC), lambda i, c, j: (i, j, c)),
        out_shape=jax.ShapeDtypeStruct((b, s, D_MODEL), BF16),
        scratch_shapes=[pltpu.VMEM((ts + LRU_HALO, LRU_TC), F32),
                        pltpu.VMEM((1, LRU_TC), F32)],
        compiler_params=_cparams(3),
        name="lru",
    )(y3d, y3d, cw, cb, wa_bd, ba, wx_bd, bx, lam)


def _memkv_kernel(mem_ref, g_ref, w_ref, gk_ref, k_ref, v_ref):
    x = mem_ref[0]
    h = x * lax.rsqrt(jnp.mean(x * x, axis=-1, keepdims=True) + EPS) * g_ref[...]
    kv = _dot(h.astype(BF16), w_ref[...])
    for hd in range(MEM_HEADS):
        kh = kv[:, hd * MEM_HD:(hd + 1) * MEM_HD]
        kh = kh * lax.rsqrt(jnp.mean(kh * kh, axis=-1, keepdims=True) + EPS) * gk_ref[...]
        k_ref[0, :, hd * MEM_HD:(hd + 1) * MEM_HD] = kh.astype(BF16)
    v_ref[0] = kv[:, D_MODEL:].astype(BF16)


def _memkv(mem, g, w_kv, gk):
    b = mem.shape[0]
    out = jax.ShapeDtypeStruct((b, N_MEM, D_MODEL), BF16)
    return pl.pallas_call(
        _memkv_kernel,
        grid=(b,),
        in_specs=[pl.BlockSpec((1, N_MEM, D_MODEL), lambda i: (i, 0, 0)),
                  pl.BlockSpec((1, D_MODEL), lambda i: (0, 0)),
                  pl.BlockSpec((D_MODEL, 2 * D_MODEL), lambda i: (0, 0)),
                  pl.BlockSpec((1, MEM_HD), lambda i: (0, 0))],
        out_specs=[pl.BlockSpec((1, N_MEM, D_MODEL), lambda i: (i, 0, 0)),
                   pl.BlockSpec((1, N_MEM, D_MODEL), lambda i: (i, 0, 0))],
        out_shape=[out, out],
        compiler_params=_cparams(1),
        name="memkv",
    )(mem, g, w_kv, gk)


def _mem_kernel(q_ref, k_ref, v_ref, gq_ref, o_ref):
    for hd in range(MEM_HEADS):
        sl = slice(hd * MEM_HD, (hd + 1) * MEM_HD)
        q = q_ref[0, :, sl].astype(F32)
        q = q * lax.rsqrt(jnp.mean(q * q, axis=-1, keepdims=True) + EPS) * gq_ref[...]
        q = (q * (MEM_HD ** -0.5)).astype(BF16)
        s = _dot_nt(q, k_ref[0, :, sl])
        p = jnp.exp(s - jnp.max(s, axis=-1, keepdims=True))
        o = _dot(p.astype(BF16), v_ref[0, :, sl]) / jnp.sum(p, axis=-1, keepdims=True)
        o_ref[0, :, sl] = o.astype(BF16)


def _mem(y3d, mk, mv, gq, ts=512):
    b, s, _ = y3d.shape
    return pl.pallas_call(
        _mem_kernel,
        grid=(b, s // ts),
        in_specs=[pl.BlockSpec((1, ts, D_MODEL), lambda i, j: (i, j, C_MQ // D_MODEL)),
                  pl.BlockSpec((1, N_MEM, D_MODEL), lambda i, j: (i, 0, 0)),
                  pl.BlockSpec((1, N_MEM, D_MODEL), lambda i, j: (i, 0, 0)),
                  pl.BlockSpec((1, MEM_HD), lambda i, j: (0, 0))],
        out_specs=pl.BlockSpec((1, ts, D_MODEL), lambda i, j: (i, j, 0)),
        out_shape=jax.ShapeDtypeStruct((b, s, D_MODEL), BF16),
        compiler_params=_cparams(2),
        name="mem",
    )(y3d, mk, mv, gq)


def _merge_kernel(x_ref, b0_ref, b1_ref, b2_ref, b3_ref, gt_ref, bg_ref, wb_ref, wo_ref, o_ref):
    mixed = None
    for n, br in enumerate((b0_ref, b1_ref, b2_ref, b3_ref)):
        gate = jax.nn.sigmoid(
            gt_ref[:, n * D_MODEL:(n + 1) * D_MODEL].astype(F32) + bg_ref[n:n + 1, :])
        term = gate * _dot(br[...], wb_ref[n])
        mixed = term if mixed is None else mixed + term
    o_ref[...] = x_ref[...] + _dot(mixed.astype(BF16), wo_ref[...])


def _merge(x2d, branches, y2d, bg, wb, wo, tm=256):
    m = x2d.shape[0]
    row = lambda w: pl.BlockSpec((tm, w), lambda i: (i, 0))
    return pl.pallas_call(
        _merge_kernel,
        grid=(m // tm,),
        in_specs=[row(D_MODEL), row(D_MODEL), row(D_MODEL), row(D_MODEL), row(D_MODEL),
                  pl.BlockSpec((tm, 4 * D_MODEL), lambda i: (i, C_GT // (4 * D_MODEL))),
                  pl.BlockSpec((4, D_MODEL), lambda i: (0, 0)),
                  pl.BlockSpec((4, D_MODEL, D_MODEL), lambda i: (0, 0, 0)),
                  pl.BlockSpec((D_MODEL, D_MODEL), lambda i: (0, 0))],
        out_specs=row(D_MODEL),
        out_shape=jax.ShapeDtypeStruct((m, D_MODEL), F32),
        compiler_params=_cparams(1),
        name="merge",
    )(x2d, *branches, y2d, bg, wb, wo)


FFN_TC = 256
FFN_HALO = 8
FFN_RB = 64


def _ffn_kernel(x_ref, g_ref, wg_ref, wv_ref, cw_ref, cb_ref, wd_ref, o_ref,
                h_ref, gbuf_ref, vbuf_ref, act_ref, *, tm, tiles_per_seq):
    i = pl.program_id(0)
    x = x_ref[...]
    h_ref[...] = (x * lax.rsqrt(jnp.mean(x * x, axis=-1, keepdims=True) + EPS)
                  * g_ref[...]).astype(BF16)

    @pl.when((i % tiles_per_seq) == 0)
    def _():
        gbuf_ref[0:FFN_HALO, :] = jnp.zeros((FFN_HALO, D_FF), F32)

    def up(c):
        cols = slice(c * FFN_TC, (c + 1) * FFN_TC)
        gbuf_ref[FFN_HALO:FFN_HALO + tm, cols] = _dot(h_ref[...], wg_ref[:, cols])
        vbuf_ref[:, cols] = _dot(h_ref[...], wv_ref[:, cols])

    def activate(c):
        cols = slice(c * FFN_TC, (c + 1) * FFN_TC)
        cw = cw_ref[:, cols]
        cb = cb_ref[:, cols]
        for r0 in range(0, tm, FFN_RB):
            gate = cb
            for d in range(3):
                lo = FFN_HALO + r0 - d
                gate = gate + gbuf_ref[lo:lo + FFN_RB, cols] * cw[2 - d:3 - d, :]
            act = gate * jax.nn.sigmoid(gate) * vbuf_ref[r0:r0 + FFN_RB, cols]
            act_ref[r0:r0 + FFN_RB, cols] = act.astype(BF16)

    nc = D_FF // FFN_TC
    up(0)
    for c in range(1, nc):
        up(c)
        activate(c - 1)
    activate(nc - 1)
    o_ref[...] = x_ref[...] + _dot(act_ref[...], wd_ref[...])
    gbuf_ref[0:FFN_HALO, :] = gbuf_ref[tm:tm + FFN_HALO, :]


def _ffn(x2d, g, w_up, cw, cb, w_down, seq, tm=512):
    m = x2d.shape[0]
    resident = pl.Buffered(1)
    return pl.pallas_call(
        functools.partial(_ffn_kernel, tm=tm, tiles_per_seq=seq // tm),
        grid=(m // tm,),
        in_specs=[pl.BlockSpec((tm, D_MODEL), lambda i: (i, 0)),
                  pl.BlockSpec((1, D_MODEL), lambda i: (0, 0)),
                  pl.BlockSpec((D_MODEL, D_FF), lambda i: (0, 0), pipeline_mode=resident),
                  pl.BlockSpec((D_MODEL, D_FF), lambda i: (0, 1), pipeline_mode=resident),
                  pl.BlockSpec((3, D_FF), lambda i: (0, 0)),
                  pl.BlockSpec((1, D_FF), lambda i: (0, 0)),
                  pl.BlockSpec((D_FF, D_MODEL), lambda i: (0, 0), pipeline_mode=resident)],
        out_specs=pl.BlockSpec((tm, D_MODEL), lambda i: (i, 0)),
        out_shape=jax.ShapeDtypeStruct((m, D_MODEL), F32),
        scratch_shapes=[pltpu.VMEM((tm, D_MODEL), BF16),
                        pltpu.VMEM((tm + FFN_HALO, D_FF), F32),
                        pltpu.VMEM((tm, D_FF), F32),
                        pltpu.VMEM((tm, D_FF), BF16)],
        compiler_params=_cparams(1),
        name="ffn",
    )(x2d, g, w_up, w_up, cw, cb, w_down)


def _block_diag(w):
    per = LRU_TC // HEAD_DIM
    w4 = w.reshape(D_MODEL // LRU_TC, per, HEAD_DIM, HEAD_DIM)
    eye = jnp.eye(per, dtype=w.dtype)
    bd = jnp.einsum('cpde,pq->cpdqe', w4, eye)
    return bd.reshape(D_MODEL // LRU_TC, LRU_TC, LRU_TC)


def kernel(x, mem, attn_norm_g, mem_norm_g, w_in, b_forget, fox_q_norm_g, fox_k_norm_g,
           lru_conv_w, lru_conv_b, lru_w_a, lru_b_a, lru_w_x, lru_b_x, lru_lambda,
           w_mem_kv, mem_q_norm_g, mem_k_norm_g, b_gate, w_branch, w_out,
           ffn_norm_g, w_up, ffn_conv_w, ffn_conv_b, w_down):
    b, s, d = x.shape
    depth = w_in.shape[0]
    m = b * s
    x2d = x.reshape(m, d)
    for l in range(depth):
        wy = jnp.concatenate([w_in[l, :, W_GATES:], w_in[l, :, :W_FOX_END],
                              w_in[l, :, W_F_END:W_GATES]], axis=1)
        sb_q_scale = jnp.where((jnp.arange(N_Y) >= C_SQ) & (jnp.arange(N_Y) < C_SK),
                               HEAD_DIM ** -0.5 * LOG2E, 1.0).astype(F32)
        wy = (wy * sb_q_scale).astype(BF16)
        wf = jnp.pad(w_in[l, :, W_FOX_END:W_F_END], ((0, 0), (0, LANES - N_HEADS)))
        bf = jnp.pad(b_forget[l], (0, LANES - N_HEADS)).reshape(1, LANES)

        y2d, f2d = _proj(x2d, attn_norm_g[l].reshape(1, d), wy, wf)
        y3d = y2d.reshape(b, s, N_Y)

        fcum = _fcum(f2d.reshape(b, s, LANES), bf)
        qa, ka, vt = _foxprep(y3d, fcum,
                              jnp.tile(fox_q_norm_g[l], 2).reshape(1, LANES),
                              jnp.tile(fox_k_norm_g[l], 2).reshape(1, LANES))
        y_fox = _fox(qa, ka, vt)
        y_sb = _sb(y3d)
        y_lru = _lru(y3d, lru_conv_w[l], lru_conv_b[l].reshape(1, d),
                     _block_diag(lru_w_a[l]).astype(BF16), lru_b_a[l].reshape(1, d),
                     _block_diag(lru_w_x[l]).astype(BF16), lru_b_x[l].reshape(1, d),
                     lru_lambda[l].reshape(1, d))
        mk, mv = _memkv(mem, mem_norm_g[l].reshape(1, d), w_mem_kv[l].astype(BF16),
                        mem_k_norm_g[l].reshape(1, MEM_HD))
        y_mem = _mem(y3d, mk, mv, mem_q_norm_g[l].reshape(1, MEM_HD))

        branches = [t.reshape(m, d) for t in (y_fox, y_lru, y_sb, y_mem)]
        x2d = _merge(x2d, branches, y2d, b_gate[l], w_branch[l].astype(BF16),
                     w_out[l].astype(BF16))
        x2d = _ffn(x2d, ffn_norm_g[l].reshape(1, d), w_up[l].astype(BF16), ffn_conv_w[l],
                   ffn_conv_b[l].reshape(1, D_FF), w_down[l].astype(BF16), s)
    return x2d.reshape(b, s, d)
```

```python
import functools

import jax
import jax.numpy as jnp
from jax import lax
from jax.experimental import pallas as pl
from jax.experimental.pallas import tpu as pltpu

F32 = jnp.float32
BF16 = jnp.bfloat16
HIGHEST = lax.Precision.HIGHEST

D_MODEL = 1024
HEAD_DIM = 64
N_HEADS = 16
N_PAIRS = N_HEADS // 2
LANES = 128
SUBLANES = 8
N_MEM = 256
MEM_HEADS = 4
MEM_HD = 256
D_FF = 2816
LRU_C = 8.0
EPS = 1e-6
NEG = -1e30
LOG2E = 1.4426950408889634
F32_TINY = 1.1754944e-38

C_GT, C_FQ, C_FK, C_FV, C_LX, C_LG, C_SQ, C_SK, C_SV, C_MQ = (
    0, 4096, 5120, 6144, 7168, 8192, 9216, 10240, 11264, 12288)
N_Y = 13312
W_FOX_END, W_F_END, W_GATES = 3072, 3088, 9232

VMEM_LIMIT = 56 * 1024 * 1024


def _cparams(n_axes):
    return pltpu.CompilerParams(dimension_semantics=("arbitrary",) * n_axes,
                                vmem_limit_bytes=VMEM_LIMIT)


def _dot(a, b):
    return jnp.dot(a, b, preferred_element_type=F32)


def _dot_nt(a, b):
    return lax.dot_general(a, b, (((1,), (1,)), ((), ())), preferred_element_type=F32)


def _log1p_exp_neg_abs(z):
    return jnp.log(1.0 + jnp.exp(-jnp.abs(z)))


def _proj_kernel(x_ref, g_ref, w_ref, wf_ref, y_ref, f_ref, h_ref):
    @pl.when(pl.program_id(1) == 0)
    def _():
        x = x_ref[...]
        h = x * lax.rsqrt(jnp.mean(x * x, axis=-1, keepdims=True) + EPS) * g_ref[...]
        h_ref[...] = h.astype(BF16)
        f_ref[...] = jnp.dot(h, wf_ref[...], precision=HIGHEST, preferred_element_type=F32)

    y_ref[...] = _dot(h_ref[...], w_ref[...]).astype(BF16)


def _proj(x2d, g, wy, wf, tm=1024, tn=1024):
    m = x2d.shape[0]
    return pl.pallas_call(
        _proj_kernel,
        grid=(m // tm, N_Y // tn),
        in_specs=[pl.BlockSpec((tm, D_MODEL), lambda i, j: (i, 0)),
                  pl.BlockSpec((1, D_MODEL), lambda i, j: (0, 0)),
                  pl.BlockSpec((D_MODEL, tn), lambda i, j: (0, j)),
                  pl.BlockSpec((D_MODEL, LANES), lambda i, j: (0, 0))],
        out_specs=[pl.BlockSpec((tm, tn), lambda i, j: (i, j)),
                   pl.BlockSpec((tm, LANES), lambda i, j: (i, 0))],
        out_shape=[jax.ShapeDtypeStruct((m, N_Y), BF16),
                   jax.ShapeDtypeStruct((m, LANES), F32)],
        scratch_shapes=[pltpu.VMEM((tm, D_MODEL), BF16)],
        compiler_params=_cparams(2),
        name="proj",
    )(x2d, g, wy, wf)


def _fcum_kernel(f_ref, b_ref, o_ref, carry_ref):
    @pl.when(pl.program_id(1) == 0)
    def _():
        carry_ref[...] = jnp.zeros_like(carry_ref)

    z = f_ref[0] + b_ref[...]
    log_f = jnp.minimum(z, 0.0) - _log1p_exp_neg_abs(z)
    ts = z.shape[0]
    row = lax.broadcasted_iota(jnp.int32, (ts, ts), 0)
    col = lax.broadcasted_iota(jnp.int32, (ts, ts), 1)
    tri = (row >= col).astype(F32)
    c = jnp.dot(tri, log_f, precision=HIGHEST, preferred_element_type=F32) + carry_ref[...]
    carry_ref[...] = c[ts - 1:ts, :]

    lane = lax.broadcasted_iota(jnp.int32, (ts, LANES), 1)
    f = jnp.where(lane < N_HEADS, c * LOG2E, 0.0)
    hi = f.astype(BF16).astype(F32)
    r1 = f - hi
    mid = r1.astype(BF16).astype(F32)
    lo = (r1 - mid).astype(BF16).astype(F32)
    packed = hi + pltpu.roll(mid, N_HEADS, axis=1) + pltpu.roll(lo, 2 * N_HEADS, axis=1)
    o_ref[0] = packed.astype(BF16)


def _fcum(f3d, b_pad, ts=512):
    b, s, _ = f3d.shape
    return pl.pallas_call(
        _fcum_kernel,
        grid=(b, s // ts),
        in_specs=[pl.BlockSpec((1, ts, LANES), lambda i, j: (i, j, 0)),
                  pl.BlockSpec((1, LANES), lambda i, j: (0, 0))],
        out_specs=pl.BlockSpec((1, ts, LANES), lambda i, j: (i, j, 0)),
        out_shape=jax.ShapeDtypeStruct((b, s, LANES), BF16),
        scratch_shapes=[pltpu.VMEM((1, LANES), F32)],
        compiler_params=_cparams(2),
        name="fcum",
    )(f3d, b_pad)


def _foxprep_kernel(yq_ref, yk_ref, yv_ref, f_ref, gq_ref, gk_ref, qa_ref, ka_ref, vt_ref):
    hp = pl.program_id(1)
    for c in range(vt_ref.shape[2]):
        v = yv_ref[0, c * FOX_TS:(c + 1) * FOX_TS, :]
        vt_ref[0, 0, c] = v.astype(F32).T.astype(BF16)
    ts = yq_ref.shape[1]
    lane = lax.broadcasted_iota(jnp.int32, (ts, LANES), 1)
    lo_half = lane < HEAD_DIM

    hr = lax.broadcasted_iota(jnp.int32, (LANES, LANES), 0)
    hc = lax.broadcasted_iota(jnp.int32, (LANES, LANES), 1)
    same_head = ((hr < HEAD_DIM) == (hc < HEAD_DIM)).astype(BF16)

    def headnorm(y, g):
        ms = _dot((y * y).astype(BF16), same_head) * (1.0 / HEAD_DIM)
        return y * lax.rsqrt(ms + EPS) * g

    qn = headnorm(yq_ref[0].astype(F32), gq_ref[...]) * (HEAD_DIM ** -0.5 * LOG2E)
    kn = headnorm(yk_ref[0].astype(F32), gk_ref[...])

    pieces = f_ref[0]
    sel_r = lax.broadcasted_iota(jnp.int32, (LANES, 2 * LANES), 0)
    sel_c = lax.broadcasted_iota(jnp.int32, (LANES, 2 * LANES), 1)
    ones_q = ((lane >= HEAD_DIM + 3) & (lane < HEAD_DIM + 6)).astype(F32)
    ones_k = ((lane >= HEAD_DIM) & (lane < HEAD_DIM + 3)).astype(F32)
    for e in range(2):
        head = 2 * hp + e
        piece_of_row = sel_r - head
        to_q = (piece_of_row == (sel_c - HEAD_DIM) * N_HEADS) & (sel_c < HEAD_DIM + 3)
        to_k = (piece_of_row == (sel_c - (LANES + HEAD_DIM + 3)) * N_HEADS) & (
            sel_c >= LANES + HEAD_DIM + 3)
        valid = (piece_of_row >= 0) & (piece_of_row <= 2 * N_HEADS)
        sel = (jnp.where(valid & to_q, 1.0, 0.0) - jnp.where(valid & to_k, 1.0, 0.0)).astype(BF16)
        placed = _dot(pieces, sel)
        q_part = qn if e == 0 else pltpu.roll(qn, HEAD_DIM, axis=1)
        k_part = kn if e == 0 else pltpu.roll(kn, HEAD_DIM, axis=1)
        q_aug = jnp.where(lo_half, q_part, placed[:, :LANES] + ones_q)
        k_aug = jnp.where(lo_half, k_part, placed[:, LANES:] + ones_k)
        qa_ref[0, e] = q_aug.astype(BF16)
        ka_ref[0, e] = k_aug.astype(BF16)


def _foxprep(y3d, fcum, gq2, gk2, ts=2048):
    b, s, _ = y3d.shape
    out = jax.ShapeDtypeStruct((b, N_HEADS, s, LANES), BF16)
    vt = jax.ShapeDtypeStruct((b, N_PAIRS, s // FOX_TS, LANES, FOX_TS), BF16)
    return pl.pallas_call(
        _foxprep_kernel,
        grid=(b, N_PAIRS, s // ts),
        in_specs=[pl.BlockSpec((1, ts, LANES), lambda i, p, j: (i, j, C_FQ // LANES + p)),
                  pl.BlockSpec((1, ts, LANES), lambda i, p, j: (i, j, C_FK // LANES + p)),
                  pl.BlockSpec((1, ts, LANES), lambda i, p, j: (i, j, C_FV // LANES + p)),
                  pl.BlockSpec((1, ts, LANES), lambda i, p, j: (i, j, 0)),
                  pl.BlockSpec((1, LANES), lambda i, p, j: (0, 0)),
                  pl.BlockSpec((1, LANES), lambda i, p, j: (0, 0))],
        out_specs=[pl.BlockSpec((1, 2, ts, LANES), lambda i, p, j: (i, p, j, 0)),
                   pl.BlockSpec((1, 2, ts, LANES), lambda i, p, j: (i, p, j, 0)),
                   pl.BlockSpec((1, 1, ts // FOX_TS, LANES, FOX_TS),
                                lambda i, p, j: (i, p, j, 0, 0))],
        out_shape=[out, out, vt],
        compiler_params=_cparams(3),
        name="foxprep",
    )(y3d, y3d, y3d, fcum, gq2, gk2)


FOX_TS = 512
FOX_KB = 64


def _fox_kernel(qa_ref, ka_ref, vt_ref, o_ref, s_ref, p_ref, m_ref, l_ref, a_ref, acc_ref,
                *, tq):
    i = pl.program_id(2)
    ts = FOX_TS
    nslab = tq // ts
    chains = [(h, e) for h in range(nslab) for e in range(2)]
    key = lax.broadcasted_iota(jnp.int32, (FOX_KB, LANES), 0)
    query = lax.broadcasted_iota(jnp.int32, (FOX_KB, LANES), 1)

    m_ref[...] = jnp.full(m_ref.shape, NEG, F32)
    l_ref[...] = jnp.zeros(l_ref.shape, F32)
    acc_ref[...] = jnp.zeros(acc_ref.shape, F32)

    def logits(n, block):
        h, e = chains[n]
        start = pl.multiple_of(block * ts, ts)
        s = _dot_nt(ka_ref[0, e, pl.ds(start, ts), :], qa_ref[0, e, h * ts:(h + 1) * ts, :])
        for c in range(ts // LANES):
            s_ref[n, c] = s[:, c * LANES:(c + 1) * LANES]

    def softmax(n, diagonal):
        kb = FOX_KB
        for c0 in range(0, ts, LANES):
            cols = slice(c0, c0 + LANES)

            def piece(k0):
                s = s_ref[n, c0 // LANES, k0:k0 + kb, :]
                if diagonal:
                    s = jnp.where(key + k0 <= query + c0, s, NEG)
                return s

            top = piece(0)
            for k0 in range(kb, ts, kb):
                top = jnp.maximum(top, piece(k0))
            m_old = m_ref[n, :, cols]
            m_new = jnp.maximum(m_old, jnp.max(top, axis=0, keepdims=True))
            alpha = jnp.exp2(m_old - m_new)
            total = None
            for k0 in range(0, ts, kb):
                p = jnp.exp2(piece(k0) - m_new[0:1, :])
                p_ref[n, c0 // LANES, k0:k0 + kb, :] = p.astype(BF16)
                total = p if total is None else total + p
            l_ref[n, :, cols] = alpha * l_ref[n, :, cols] + jnp.sum(total, axis=0, keepdims=True)
            m_ref[n, :, cols] = m_new
            a_ref[n, :, cols] = alpha

    def accumulate(n, block):
        e = chains[n][1]
        vt = vt_ref[0, 0, block, e * HEAD_DIM:(e + 1) * HEAD_DIM, :]
        p = jnp.concatenate([p_ref[n, c] for c in range(ts // LANES)], axis=1)
        acc_ref[n] = a_ref[n, 0:1, :] * acc_ref[n] + _dot(vt, p)

    def step(block, live, diagonal_slab, live_next):
        prev = None
        for n in live:
            softmax(n, chains[n][0] == diagonal_slab)
            if n in live_next:
                logits(n, block + 1)
            if prev is not None:
                accumulate(prev, block)
            prev = n
        accumulate(prev, block)

    everyone = list(range(len(chains)))
    for n in everyone:
        logits(n, 0)

    def body(j, _):
        step(j, everyone, None, everyone)
        return 0

    lax.fori_loop(0, i * nslab, body, 0)
    for c in range(nslab):
        live = [n for n in everyone if chains[n][0] >= c]
        live_next = [n for n in everyone if chains[n][0] >= c + 1] if c + 1 < nslab else []
        step(i * nslab + c, live, c, live_next)

    for h in range(nslab):
        o0 = acc_ref[2 * h] / l_ref[2 * h, 0:1, :]
        o1 = acc_ref[2 * h + 1] / l_ref[2 * h + 1, 0:1, :]
        o_ref[0, h * ts:(h + 1) * ts, :] = jnp.concatenate([o0, o1], axis=0).T.astype(BF16)


def _fox(qa, ka, vt, tq=1024):
    b, _, s, _ = qa.shape
    nch = 2 * (tq // FOX_TS)
    return pl.pallas_call(
        functools.partial(_fox_kernel, tq=tq),
        grid=(b, N_PAIRS, s // tq),
        in_specs=[pl.BlockSpec((1, 2, tq, LANES), lambda i, p, j: (i, p, j, 0)),
                  pl.BlockSpec((1, 2, s, LANES), lambda i, p, j: (i, p, 0, 0)),
                  pl.BlockSpec((1, 1, s // FOX_TS, LANES, FOX_TS),
                               lambda i, p, j: (i, p, 0, 0, 0))],
        out_specs=pl.BlockSpec((1, tq, LANES), lambda i, p, j: (i, j, p)),
        out_shape=jax.ShapeDtypeStruct((b, s, D_MODEL), BF16),
        scratch_shapes=[pltpu.VMEM((nch, FOX_TS // LANES, FOX_TS, LANES), F32),
                        pltpu.VMEM((nch, FOX_TS // LANES, FOX_TS, LANES), BF16),
                        pltpu.VMEM((nch, SUBLANES, FOX_TS), F32),
                        pltpu.VMEM((nch, SUBLANES, FOX_TS), F32),
                        pltpu.VMEM((nch, SUBLANES, FOX_TS), F32),
                        pltpu.VMEM((nch, HEAD_DIM, FOX_TS), F32)],
        compiler_params=_cparams(3),
        name="fox",
    )(qa, ka, vt)


SB_KC = 256
SB_TS = 512
SB_MAX_LOG2 = 126.0
SB_DEAD_LOG2 = -(SB_MAX_LOG2 + 150.0)


SB_RB = 64
SB_UNROLL = 2


def _sb_kernel(q_ref, k_ref, v_ref, o_ref, qm_ref, z_ref, lb_ref, ic_ref, w_ref,
               rs_ref, r_ref, acc_ref, *, tq):
    i = pl.program_id(2)
    kc, ts, rb = SB_KC, SB_TS, SB_RB
    nsub = tq // kc
    nslab = tq // ts
    chains = [(h, e) for h in range(nslab) for e in range(2)]
    row = lax.broadcasted_iota(jnp.int32, (rb, kc), 0)
    col = lax.broadcasted_iota(jnp.int32, (rb, kc), 1)
    kr = lax.broadcasted_iota(jnp.int32, (kc, kc), 0)
    kcol = lax.broadcasted_iota(jnp.int32, (kc, kc), 1)
    at_or_after = (kr >= kcol).astype(BF16)

    lane = lax.broadcasted_iota(jnp.int32, (ts, LANES), 1)
    for n, (h, e) in enumerate(chains):
        q2 = q_ref[0, h * ts:(h + 1) * ts, :]
        mine = (lane < HEAD_DIM) if e == 0 else (lane >= HEAD_DIM)
        qm_ref[n] = jnp.where(mine, q2, jnp.zeros_like(q2))
    r_ref[...] = jnp.zeros(r_ref.shape, F32)
    acc_ref[...] = jnp.zeros(acc_ref.shape, F32)

    def logits(n, start):
        z_ref[n] = jnp.minimum(_dot_nt(qm_ref[n], k_ref[0, pl.ds(start, kc), :]), SB_MAX_LOG2)

    def log_terms(n, offset):
        for r0 in range(0, ts, rb):
            rows = slice(r0, r0 + rb)
            log_1m_beta = jnp.log(1.0 + jnp.exp2(z_ref[n, rows, :])) * (-LOG2E)
            if offset is not None:
                log_1m_beta = jnp.where(col + (offset - r0) < row, log_1m_beta, 0.0)
            lb_ref[n, rows, :] = log_1m_beta.astype(BF16)
            rs_ref[n, rows, :] = jnp.broadcast_to(
                jnp.sum(log_1m_beta, axis=-1, keepdims=True), (rb, LANES))

    def cumulate(n):
        ic_ref[n] = _dot(lb_ref[n], at_or_after)

    def weights(n, offset):
        for r0 in range(0, ts, rb):
            rows = slice(r0, r0 + rb)
            r = r_ref[n, rows, :]
            log_w = (z_ref[n, rows, :] + ic_ref[n, rows, :]
                     + jnp.concatenate([r] * (kc // LANES), axis=1))
            if offset is not None:
                log_w = jnp.where(col + (offset - r0) < row, log_w, NEG)
            w_ref[n, rows, :] = jnp.exp2(log_w.astype(BF16))
            r_ref[n, rows, :] = r + rs_ref[n, rows, :]

    def accumulate(n, start):
        acc_ref[n] += _dot(w_ref[n], v_ref[0, pl.ds(start, kc), :])

    def chunk(start, next_start, live, offsets, live_next):
        prev = None
        for n in live:
            log_terms(n, offsets[chains[n][0]])
            cumulate(n)
            if prev is not None:
                weights(prev, offsets[chains[prev][0]])
                if prev in live_next:
                    logits(prev, next_start)
                accumulate(prev, start)
            prev = n
        weights(prev, offsets[chains[prev][0]])
        if prev in live_next:
            logits(prev, next_start)
        accumulate(prev, start)
        for n in live_next:
            if n not in live:
                logits(n, next_start)

    def visibility(c):
        live, offsets = [], {}
        for n, (h, _) in enumerate(chains):
            if c * kc >= (h + 1) * ts:
                continue
            live.append(n)
            offsets[h] = None if (c + 1) * kc <= h * ts else c * kc - h * ts
        return live, offsets

    everyone = list(range(len(chains)))
    no_offsets = {h: None for h in range(nslab)}
    live, offsets = visibility(nsub - 1)
    for n in live:
        logits(n, pl.multiple_of(i * tq + (nsub - 1) * kc, kc))
    for c in reversed(range(nsub)):
        start = pl.multiple_of(i * tq + c * kc, kc)
        live_next = visibility(c - 1)[0] if c > 0 else everyone
        chunk(start, pl.multiple_of(jnp.maximum(start - kc, 0), kc), live, offsets, live_next)
        if c > 0:
            live, offsets = visibility(c - 1)

    def alive(first_chain):
        return (jnp.max(r_ref[first_chain:]) > SB_DEAD_LOG2).astype(jnp.int32)

    trips = i * (nsub // SB_UNROLL)

    def walk(t0, live, first_chain):
        def body(state):
            t, _ = state
            for u in range(SB_UNROLL):
                start = pl.multiple_of(i * tq - (t * SB_UNROLL + u + 1) * kc, kc)
                chunk(start, pl.multiple_of(jnp.maximum(start - kc, 0), kc), live, no_offsets,
                      live)
            return t + 1, alive(first_chain)

        t_end, _ = lax.while_loop(lambda state: (state[0] < trips) & (state[1] > 0), body,
                                  (t0, alive(first_chain)))
        return t_end

    t_mid = walk(jnp.int32(0), everyone, len(chains) - 2)
    walk(t_mid, everyone[:2], 0)
    for h in range(nslab):
        o_ref[0, h * ts:(h + 1) * ts, :] = jnp.where(
            lane < HEAD_DIM, acc_ref[2 * h], acc_ref[2 * h + 1]).astype(BF16)


def _sb(y3d, tq=1024):
    b, s, _ = y3d.shape
    nch = 2 * (tq // SB_TS)
    return pl.pallas_call(
        functools.partial(_sb_kernel, tq=tq),
        grid=(b, N_PAIRS, s // tq),
        in_specs=[pl.BlockSpec((1, tq, LANES), lambda i, p, j: (i, j, C_SQ // LANES + p)),
                  pl.BlockSpec((1, s, LANES), lambda i, p, j: (i, 0, C_SK // LANES + p)),
                  pl.BlockSpec((1, s, LANES), lambda i, p, j: (i, 0, C_SV // LANES + p))],
        out_specs=pl.BlockSpec((1, tq, LANES), lambda i, p, j: (i, j, p)),
        out_shape=jax.ShapeDtypeStruct((b, s, D_MODEL), BF16),
        scratch_shapes=[pltpu.VMEM((nch, SB_TS, LANES), BF16),
                        pltpu.VMEM((nch, SB_TS, SB_KC), F32),
                        pltpu.VMEM((nch, SB_TS, SB_KC), BF16),
                        pltpu.VMEM((nch, SB_TS, SB_KC), F32),
                        pltpu.VMEM((nch, SB_TS, SB_KC), BF16),
                        pltpu.VMEM((nch, SB_TS, LANES), F32),
                        pltpu.VMEM((nch, SB_TS, LANES), F32),
                        pltpu.VMEM((nch, SB_TS, LANES), F32)],
        compiler_params=_cparams(3),
        name="sb",
    )(y3d, y3d, y3d)


LRU_TC = 256
LRU_HALO = 8


def _lru_kernel(x_ref, g_ref, cw_ref, cb_ref, wa_ref, ba_ref, wx_ref, bx_ref, lam_ref,
                o_ref, xbuf_ref, h_ref, *, ts):
    @pl.when(pl.program_id(2) == 0)
    def _():
        xbuf_ref[0:LRU_HALO, :] = jnp.zeros((LRU_HALO, LRU_TC), F32)
        h_ref[...] = jnp.zeros_like(h_ref)

    xbuf_ref[LRU_HALO:LRU_HALO + ts, :] = x_ref[0].astype(F32)
    cw = cw_ref[...]
    xc = cb_ref[...] + xbuf_ref[LRU_HALO:LRU_HALO + ts, :] * cw[3:4, :]
    for d in range(1, 4):
        xc = xc + xbuf_ref[LRU_HALO - d:LRU_HALO - d + ts, :] * cw[3 - d:4 - d, :]
    xbuf_ref[0:LRU_HALO, :] = xbuf_ref[ts:ts + LRU_HALO, :]

    xcb = xc.astype(BF16)
    r = jax.nn.sigmoid(_dot(xcb, wa_ref[0]) + ba_ref[...])
    gi = jax.nn.sigmoid(_dot(xcb, wx_ref[0]) + bx_ref[...])
    lam = lam_ref[...]
    softplus_neg_lam = jnp.maximum(-lam, 0.0) + _log1p_exp_neg_abs(lam)
    log_a = -LRU_C * r * softplus_neg_lam
    a = jnp.exp(log_a)
    one_m_a2 = -jnp.tanh(log_a) * (a * a + 1.0)
    u = one_m_a2 * lax.rsqrt(jnp.maximum(one_m_a2, F32_TINY)) * (gi * xc)

    row = lax.broadcasted_iota(jnp.int32, (ts, LRU_TC), 0) % SUBLANES
    d = 1
    while d < SUBLANES:
        keep = row >= d
        a_sh = jnp.where(keep, pltpu.roll(a, d, axis=0), 1.0)
        u_sh = jnp.where(keep, pltpu.roll(u, d, axis=0), 0.0)
        u = a * u_sh + u
        a = a * a_sh
        d *= 2
    state = h_ref[...]
    groups = []
    for r0 in range(0, ts, SUBLANES):
        h = u[r0:r0 + SUBLANES, :] + a[r0:r0 + SUBLANES, :] * state
        groups.append(h)
        state = h[SUBLANES - 1:SUBLANES, :]
    h_ref[...] = state
    h = jnp.concatenate(groups, axis=0)
    o_ref[0] = (h * jax.nn.gelu(g_ref[0].astype(F32))).astype(BF16)


def _lru(y3d, cw, cb, wa_bd, ba, wx_bd, bx, lam, ts=512):
    b, s, _ = y3d.shape
    nct = D_MODEL // LRU_TC
    vec = lambda rows: pl.BlockSpec((rows, LRU_TC), lambda i, c, j: (0, c))
    mat = pl.BlockSpec((1, LRU_TC, LRU_TC), lambda i, c, j: (c, 0, 0))
    return pl.pallas_call(
        functools.partial(_lru_kernel, ts=ts),
        grid=(b, nct, s // ts),
        in_specs=[pl.BlockSpec((1, ts, LRU_TC), lambda i, c, j: (i, j, C_LX // LRU_TC + c)),
                  pl.BlockSpec((1, ts, LRU_TC), lambda i, c, j: (i, j, C_LG // LRU_TC + c)),
                  vec(4), vec(1), mat, vec(1), mat, vec(1), vec(1)],
        out_specs=pl.BlockSpec((1, ts, LRU_TC), lambda i, c, j: (i, j, c)),
        out_shape=jax.ShapeDtypeStruct((b, s, D_MODEL), BF16),
        scratch_shapes=[pltpu.VMEM((ts + LRU_HALO, LRU_TC), F32),
                        pltpu.VMEM((1, LRU_TC), F32)],
        compiler_params=_cparams(3),
        name="lru",
    )(y3d, y3d, cw, cb, wa_bd, ba, wx_bd, bx, lam)


def _memkv_kernel(mem_ref, g_ref, w_ref, gk_ref, k_ref, v_ref):
    x = mem_ref[0]
    h = x * lax.rsqrt(jnp.mean(x * x, axis=-1, keepdims=True) + EPS) * g_ref[...]
    kv = _dot(h.astype(BF16), w_ref[...])
    for hd in range(MEM_HEADS):
        kh = kv[:, hd * MEM_HD:(hd + 1) * MEM_HD]
        kh = kh * lax.rsqrt(jnp.mean(kh * kh, axis=-1, keepdims=True) + EPS) * gk_ref[...]
        k_ref[0, :, hd * MEM_HD:(hd + 1) * MEM_HD] = kh.astype(BF16)
    v_ref[0] = kv[:, D_MODEL:].astype(BF16)


def _memkv(mem, g, w_kv, gk):
    b = mem.shape[0]
    out = jax.ShapeDtypeStruct((b, N_MEM, D_MODEL), BF16)
    return pl.pallas_call(
        _memkv_kernel,
        grid=(b,),
        in_specs=[pl.BlockSpec((1, N_MEM, D_MODEL), lambda i: (i, 0, 0)),
                  pl.BlockSpec((1, D_MODEL), lambda i: (0, 0)),
                  pl.BlockSpec((D_MODEL, 2 * D_MODEL), lambda i: (0, 0)),
                  pl.BlockSpec((1, MEM_HD), lambda i: (0, 0))],
        out_specs=[pl.BlockSpec((1, N_MEM, D_MODEL), lambda i: (i, 0, 0)),
                   pl.BlockSpec((1, N_MEM, D_MODEL), lambda i: (i, 0, 0))],
        out_shape=[out, out],
        compiler_params=_cparams(1),
        name="memkv",
    )(mem, g, w_kv, gk)


def _mem_kernel(q_ref, k_ref, v_ref, gq_ref, o_ref):
    for hd in range(MEM_HEADS):
        sl = slice(hd * MEM_HD, (hd + 1) * MEM_HD)
        q = q_ref[0, :, sl].astype(F32)
        q = q * lax.rsqrt(jnp.mean(q * q, axis=-1, keepdims=True) + EPS) * gq_ref[...]
        q = (q * (MEM_HD ** -0.5)).astype(BF16)
        s = _dot_nt(q, k_ref[0, :, sl])
        p = jnp.exp(s - jnp.max(s, axis=-1, keepdims=True))
        o = _dot(p.astype(BF16), v_ref[0, :, sl]) / jnp.sum(p, axis=-1, keepdims=True)
        o_ref[0, :, sl] = o.astype(BF16)


def _mem(y3d, mk, mv, gq, ts=512):
    b, s, _ = y3d.shape
    return pl.pallas_call(
        _mem_kernel,
        grid=(b, s // ts),
        in_specs=[pl.BlockSpec((1, ts, D_MODEL), lambda i, j: (i, j, C_MQ // D_MODEL)),
                  pl.BlockSpec((1, N_MEM, D_MODEL), lambda i, j: (i, 0, 0)),
                  pl.BlockSpec((1, N_MEM, D_MODEL), lambda i, j: (i, 0, 0)),
                  pl.BlockSpec((1, MEM_HD), lambda i, j: (0, 0))],
        out_specs=pl.BlockSpec((1, ts, D_MODEL), lambda i, j: (i, j, 0)),
        out_shape=jax.ShapeDtypeStruct((b, s, D_MODEL), BF16),
        compiler_params=_cparams(2),
        name="mem",
    )(y3d, mk, mv, gq)


def _merge_kernel(x_ref, b0_ref, b1_ref, b2_ref, b3_ref, gt_ref, bg_ref, wb_ref, wo_ref, o_ref):
    mixed = None
    for n, br in enumerate((b0_ref, b1_ref, b2_ref, b3_ref)):
        gate = jax.nn.sigmoid(
            gt_ref[:, n * D_MODEL:(n + 1) * D_MODEL].astype(F32) + bg_ref[n:n + 1, :])
        term = gate * _dot(br[...], wb_ref[n])
        mixed = term if mixed is None else mixed + term
    o_ref[...] = x_ref[...] + _dot(mixed.astype(BF16), wo_ref[...])


def _merge(x2d, branches, y2d, bg, wb, wo, tm=256):
    m = x2d.shape[0]
    row = lambda w: pl.BlockSpec((tm, w), lambda i: (i, 0))
    return pl.pallas_call(
        _merge_kernel,
        grid=(m // tm,),
        in_specs=[row(D_MODEL), row(D_MODEL), row(D_MODEL), row(D_MODEL), row(D_MODEL),
                  pl.BlockSpec((tm, 4 * D_MODEL), lambda i: (i, C_GT // (4 * D_MODEL))),
                  pl.BlockSpec((4, D_MODEL), lambda i: (0, 0)),
                  pl.BlockSpec((4, D_MODEL, D_MODEL), lambda i: (0, 0, 0)),
                  pl.BlockSpec((D_MODEL, D_MODEL), lambda i: (0, 0))],
        out_specs=row(D_MODEL),
        out_shape=jax.ShapeDtypeStruct((m, D_MODEL), F32),
        compiler_params=_cparams(1),
        name="merge",
    )(x2d, *branches, y2d, bg, wb, wo)


FFN_TC = 256
FFN_HALO = 8
FFN_RB = 64


def _ffn_kernel(x_ref, g_ref, wg_ref, wv_ref, cw_ref, cb_ref, wd_ref, o_ref,
                h_ref, gbuf_ref, vbuf_ref, act_ref, *, tm, tiles_per_seq):
    i = pl.program_id(0)
    x = x_ref[...]
    h_ref[...] = (x * lax.rsqrt(jnp.mean(x * x, axis=-1, keepdims=True) + EPS)
                  * g_ref[...]).astype(BF16)

    @pl.when((i % tiles_per_seq) == 0)
    def _():
        gbuf_ref[0:FFN_HALO, :] = jnp.zeros((FFN_HALO, D_FF), F32)

    def up(c):
        cols = slice(c * FFN_TC, (c + 1) * FFN_TC)
        gbuf_ref[FFN_HALO:FFN_HALO + tm, cols] = _dot(h_ref[...], wg_ref[:, cols])
        vbuf_ref[:, cols] = _dot(h_ref[...], wv_ref[:, cols])

    def activate(c):
        cols = slice(c * FFN_TC, (c + 1) * FFN_TC)
        cw = cw_ref[:, cols]
        cb = cb_ref[:, cols]
        for r0 in range(0, tm, FFN_RB):
            gate = cb
            for d in range(3):
                lo = FFN_HALO + r0 - d
                gate = gate + gbuf_ref[lo:lo + FFN_RB, cols] * cw[2 - d:3 - d, :]
            act = gate * jax.nn.sigmoid(gate) * vbuf_ref[r0:r0 + FFN_RB, cols]
            act_ref[r0:r0 + FFN_RB, cols] = act.astype(BF16)

    nc = D_FF // FFN_TC
    up(0)
    for c in range(1, nc):
        up(c)
        activate(c - 1)
    activate(nc - 1)
    o_ref[...] = x_ref[...] + _dot(act_ref[...], wd_ref[...])
    gbuf_ref[0:FFN_HALO, :] = gbuf_ref[tm:tm + FFN_HALO, :]


def _ffn(x2d, g, w_up, cw, cb, w_down, seq, tm=512):
    m = x2d.shape[0]
    resident = pl.Buffered(1)
    return pl.pallas_call(
        functools.partial(_ffn_kernel, tm=tm, tiles_per_seq=seq // tm),
        grid=(m // tm,),
        in_specs=[pl.BlockSpec((tm, D_MODEL), lambda i: (i, 0)),
                  pl.BlockSpec((1, D_MODEL), lambda i: (0, 0)),
                  pl.BlockSpec((D_MODEL, D_FF), lambda i: (0, 0), pipeline_mode=resident),
                  pl.BlockSpec((D_MODEL, D_FF), lambda i: (0, 1), pipeline_mode=resident),
                  pl.BlockSpec((3, D_FF), lambda i: (0, 0)),
                  pl.BlockSpec((1, D_FF), lambda i: (0, 0)),
                  pl.BlockSpec((D_FF, D_MODEL), lambda i: (0, 0), pipeline_mode=resident)],
        out_specs=pl.BlockSpec((tm, D_MODEL), lambda i: (i, 0)),
        out_shape=jax.ShapeDtypeStruct((m, D_MODEL), F32),
        scratch_shapes=[pltpu.VMEM((tm, D_MODEL), BF16),
                        pltpu.VMEM((tm + FFN_HALO, D_FF), F32),
                        pltpu.VMEM((tm, D_FF), F32),
                        pltpu.VMEM((tm, D_FF), BF16)],
        compiler_params=_cparams(1),
        name="ffn",
    )(x2d, g, w_up, w_up, cw, cb, w_down)


def _block_diag(w):
    per = LRU_TC // HEAD_DIM
    w4 = w.reshape(D_MODEL // LRU_TC, per, HEAD_DIM, HEAD_DIM)
    eye = jnp.eye(per, dtype=w.dtype)
    bd = jnp.einsum('cpde,pq->cpdqe', w4, eye)
    return bd.reshape(D_MODEL // LRU_TC, LRU_TC, LRU_TC)


def kernel(x, mem, attn_norm_g, mem_norm_g, w_in, b_forget, fox_q_norm_g, fox_k_norm_g,
           lru_conv_w, lru_conv_b, lru_w_a, lru_b_a, lru_w_x, lru_b_x, lru_lambda,
           w_mem_kv, mem_q_norm_g, mem_k_norm_g, b_gate, w_branch, w_out,
           ffn_norm_g, w_up, ffn_conv_w, ffn_conv_b, w_down):
    b, s, d = x.shape
    depth = w_in.shape[0]
    m = b * s
    x2d = x.reshape(m, d)
    for l in range(depth):
        wy = jnp.concatenate([w_in[l, :, W_GATES:], w_in[l, :, :W_FOX_END],
                              w_in[l, :, W_F_END:W_GATES]], axis=1)
        sb_q_scale = jnp.where((jnp.arange(N_Y) >= C_SQ) & (jnp.arange(N_Y) < C_SK),
                               HEAD_DIM ** -0.5 * LOG2E, 1.0).astype(F32)
        wy = (wy * sb_q_scale).astype(BF16)
        wf = jnp.pad(w_in[l, :, W_FOX_END:W_F_END], ((0, 0), (0, LANES - N_HEADS)))
        bf = jnp.pad(b_forget[l], (0, LANES - N_HEADS)).reshape(1, LANES)

        y2d, f2d = _proj(x2d, attn_norm_g[l].reshape(1, d), wy, wf)
        y3d = y2d.reshape(b, s, N_Y)

        fcum = _fcum(f2d.reshape(b, s, LANES), bf)
        qa, ka, vt = _foxprep(y3d, fcum,
                              jnp.tile(fox_q_norm_g[l], 2).reshape(1, LANES),
                              jnp.tile(fox_k_norm_g[l], 2).reshape(1, LANES))
        y_fox = _fox(qa, ka, vt)
        y_sb = _sb(y3d)
        y_lru = _lru(y3d, lru_conv_w[l], lru_conv_b[l].reshape(1, d),
                     _block_diag(lru_w_a[l]).astype(BF16), lru_b_a[l].reshape(1, d),
                     _block_diag(lru_w_x[l]).astype(BF16), lru_b_x[l].reshape(1, d),
                     lru_lambda[l].reshape(1, d))
        mk, mv = _memkv(mem, mem_norm_g[l].reshape(1, d), w_mem_kv[l].astype(BF16),
                        mem_k_norm_g[l].reshape(1, MEM_HD))
        y_mem = _mem(y3d, mk, mv, mem_q_norm_g[l].reshape(1, MEM_HD))

        branches = [t.reshape(m, d) for t in (y_fox, y_lru, y_sb, y_mem)]
        x2d = _merge(x2d, branches, y2d, b_gate[l], w_branch[l].astype(BF16),
                     w_out[l].astype(BF16))
        x2d = _ffn(x2d, ffn_norm_g[l].reshape(1, d), w_up[l].astype(BF16), ffn_conv_w[l],
                   ffn_conv_b[l].reshape(1, D_FF), w_down[l].astype(BF16), s)
    return x2d.reshape(b, s, d)
```

```python
import functools

import jax
import jax.numpy as jnp
from jax import lax
from jax.experimental import pallas as pl
from jax.experimental.pallas import tpu as pltpu

F32 = jnp.float32
BF16 = jnp.bfloat16
HIGHEST = lax.Precision.HIGHEST

D_MODEL = 1024
HEAD_DIM = 64
N_HEADS = 16
N_PAIRS = N_HEADS // 2
LANES = 128
SUBLANES = 8
N_MEM = 256
MEM_HEADS = 4
MEM_HD = 256
D_FF = 2816
LRU_C = 8.0
EPS = 1e-6
NEG = -1e30
LOG2E = 1.4426950408889634
F32_TINY = 1.1754944e-38

C_GT, C_FQ, C_FK, C_FV, C_LX, C_LG, C_SQ, C_SK, C_SV, C_MQ = (
    0, 4096, 5120, 6144, 7168, 8192, 9216, 10240, 11264, 12288)
N_Y = 13312
W_FOX_END, W_F_END, W_GATES = 3072, 3088, 9232

VMEM_LIMIT = 56 * 1024 * 1024


def _cparams(n_axes):
    return pltpu.CompilerParams(dimension_semantics=("arbitrary",) * n_axes,
                                vmem_limit_bytes=VMEM_LIMIT)


def _dot(a, b):
    return jnp.dot(a, b, preferred_element_type=F32)


def _dot_nt(a, b):
    return lax.dot_general(a, b, (((1,), (1,)), ((), ())), preferred_element_type=F32)


def _log1p_exp_neg_abs(z):
    return jnp.log(1.0 + jnp.exp(-jnp.abs(z)))


def _proj_kernel(x_ref, g_ref, w_ref, wf_ref, y_ref, f_ref, h_ref):
    @pl.when(pl.program_id(1) == 0)
    def _():
        x = x_ref[...]
        h = x * lax.rsqrt(jnp.mean(x * x, axis=-1, keepdims=True) + EPS) * g_ref[...]
        h_ref[...] = h.astype(BF16)
        f_ref[...] = jnp.dot(h, wf_ref[...], precision=HIGHEST, preferred_element_type=F32)

    y_ref[...] = _dot(h_ref[...], w_ref[...]).astype(BF16)


def _proj(x2d, g, wy, wf, tm=1024, tn=1024):
    m = x2d.shape[0]
    return pl.pallas_call(
        _proj_kernel,
        grid=(m // tm, N_Y // tn),
        in_specs=[pl.BlockSpec((tm, D_MODEL), lambda i, j: (i, 0)),
                  pl.BlockSpec((1, D_MODEL), lambda i, j: (0, 0)),
                  pl.BlockSpec((D_MODEL, tn), lambda i, j: (0, j)),
                  pl.BlockSpec((D_MODEL, LANES), lambda i, j: (0, 0))],
        out_specs=[pl.BlockSpec((tm, tn), lambda i, j: (i, j)),
                   pl.BlockSpec((tm, LANES), lambda i, j: (i, 0))],
        out_shape=[jax.ShapeDtypeStruct((m, N_Y), BF16),
                   jax.ShapeDtypeStruct((m, LANES), F32)],
        scratch_shapes=[pltpu.VMEM((tm, D_MODEL), BF16)],
        compiler_params=_cparams(2),
        name="proj",
    )(x2d, g, wy, wf)


def _fcum_kernel(f_ref, b_ref, o_ref, carry_ref):
    @pl.when(pl.program_id(1) == 0)
    def _():
        carry_ref[...] = jnp.zeros_like(carry_ref)

    z = f_ref[0] + b_ref[...]
    log_f = jnp.minimum(z, 0.0) - _log1p_exp_neg_abs(z)
    ts = z.shape[0]
    row = lax.broadcasted_iota(jnp.int32, (ts, ts), 0)
    col = lax.broadcasted_iota(jnp.int32, (ts, ts), 1)
    tri = (row >= col).astype(F32)
    c = jnp.dot(tri, log_f, precision=HIGHEST, preferred_element_type=F32) + carry_ref[...]
    carry_ref[...] = c[ts - 1:ts, :]

    lane = lax.broadcasted_iota(jnp.int32, (ts, LANES), 1)
    f = jnp.where(lane < N_HEADS, c * LOG2E, 0.0)
    hi = f.astype(BF16).astype(F32)
    r1 = f - hi
    mid = r1.astype(BF16).astype(F32)
    lo = (r1 - mid).astype(BF16).astype(F32)
    packed = hi + pltpu.roll(mid, N_HEADS, axis=1) + pltpu.roll(lo, 2 * N_HEADS, axis=1)
    o_ref[0] = packed.astype(BF16)


def _fcum(f3d, b_pad, ts=512):
    b, s, _ = f3d.shape
    return pl.pallas_call(
        _fcum_kernel,
        grid=(b, s // ts),
        in_specs=[pl.BlockSpec((1, ts, LANES), lambda i, j: (i, j, 0)),
                  pl.BlockSpec((1, LANES), lambda i, j: (0, 0))],
        out_specs=pl.BlockSpec((1, ts, LANES), lambda i, j: (i, j, 0)),
        out_shape=jax.ShapeDtypeStruct((b, s, LANES), BF16),
        scratch_shapes=[pltpu.VMEM((1, LANES), F32)],
        compiler_params=_cparams(2),
        name="fcum",
    )(f3d, b_pad)


def _foxprep_kernel(yq_ref, yk_ref, yv_ref, f_ref, gq_ref, gk_ref, qa_ref, ka_ref, vt_ref):
    hp = pl.program_id(1)
    ones = jnp.ones((FOX_ONES, FOX_TS), BF16)
    for c in range(vt_ref.shape[2]):
        vt = yv_ref[0, c * FOX_TS:(c + 1) * FOX_TS, :].astype(F32).T.astype(BF16)
        vt_ref[0, 0, c] = jnp.concatenate(
            [vt[:HEAD_DIM], ones, vt[HEAD_DIM:], ones], axis=0)
    ts = yq_ref.shape[1]
    lane = lax.broadcasted_iota(jnp.int32, (ts, LANES), 1)
    lo_half = lane < HEAD_DIM

    hr = lax.broadcasted_iota(jnp.int32, (LANES, LANES), 0)
    hc = lax.broadcasted_iota(jnp.int32, (LANES, LANES), 1)
    same_head = ((hr < HEAD_DIM) == (hc < HEAD_DIM)).astype(BF16)

    def headnorm(y, g):
        ms = _dot((y * y).astype(BF16), same_head) * (1.0 / HEAD_DIM)
        return y * lax.rsqrt(ms + EPS) * g

    qn = headnorm(yq_ref[0].astype(F32), gq_ref[...]) * (HEAD_DIM ** -0.5 * LOG2E)
    kn = headnorm(yk_ref[0].astype(F32), gk_ref[...])

    pieces = f_ref[0]
    sel_r = lax.broadcasted_iota(jnp.int32, (LANES, 2 * LANES), 0)
    sel_c = lax.broadcasted_iota(jnp.int32, (LANES, 2 * LANES), 1)
    ones_q = ((lane >= HEAD_DIM + 3) & (lane < HEAD_DIM + 6)).astype(F32)
    ones_k = ((lane >= HEAD_DIM) & (lane < HEAD_DIM + 3)).astype(F32)
    for e in range(2):
        head = 2 * hp + e
        piece_of_row = sel_r - head
        to_q = (piece_of_row == (sel_c - HEAD_DIM) * N_HEADS) & (sel_c < HEAD_DIM + 3)
        to_k = (piece_of_row == (sel_c - (LANES + HEAD_DIM + 3)) * N_HEADS) & (
            sel_c >= LANES + HEAD_DIM + 3)
        valid = (piece_of_row >= 0) & (piece_of_row <= 2 * N_HEADS)
        sel = (jnp.where(valid & to_q, 1.0, 0.0) - jnp.where(valid & to_k, 1.0, 0.0)).astype(BF16)
        placed = _dot(pieces, sel)
        q_part = qn if e == 0 else pltpu.roll(qn, HEAD_DIM, axis=1)
        k_part = kn if e == 0 else pltpu.roll(kn, HEAD_DIM, axis=1)
        q_aug = jnp.where(lo_half, q_part, placed[:, :LANES] + ones_q)
        k_aug = jnp.where(lo_half, k_part, placed[:, LANES:] + ones_k)
        qa_ref[0, e] = q_aug.astype(BF16)
        ka_ref[0, e] = k_aug.astype(BF16)


def _foxprep(y3d, fcum, gq2, gk2, ts=2048):
    b, s, _ = y3d.shape
    out = jax.ShapeDtypeStruct((b, N_HEADS, s, LANES), BF16)
    vt = jax.ShapeDtypeStruct((b, N_PAIRS, s // FOX_TS, 2 * FOX_VROWS, FOX_TS), BF16)
    return pl.pallas_call(
        _foxprep_kernel,
        grid=(b, N_PAIRS, s // ts),
        in_specs=[pl.BlockSpec((1, ts, LANES), lambda i, p, j: (i, j, C_FQ // LANES + p)),
                  pl.BlockSpec((1, ts, LANES), lambda i, p, j: (i, j, C_FK // LANES + p)),
                  pl.BlockSpec((1, ts, LANES), lambda i, p, j: (i, j, C_FV // LANES + p)),
                  pl.BlockSpec((1, ts, LANES), lambda i, p, j: (i, j, 0)),
                  pl.BlockSpec((1, LANES), lambda i, p, j: (0, 0)),
                  pl.BlockSpec((1, LANES), lambda i, p, j: (0, 0))],
        out_specs=[pl.BlockSpec((1, 2, ts, LANES), lambda i, p, j: (i, p, j, 0)),
                   pl.BlockSpec((1, 2, ts, LANES), lambda i, p, j: (i, p, j, 0)),
                   pl.BlockSpec((1, 1, ts // FOX_TS, 2 * FOX_VROWS, FOX_TS),
                                lambda i, p, j: (i, p, j, 0, 0))],
        out_shape=[out, out, vt],
        compiler_params=_cparams(3),
        name="foxprep",
    )(y3d, y3d, y3d, fcum, gq2, gk2)


FOX_TS = 512
FOX_KB = 64
FOX_ONES = 16
FOX_VROWS = HEAD_DIM + FOX_ONES


def _fox_kernel(qa_ref, ka_ref, vt_ref, o_ref, s_ref, p_ref, m_ref, a_ref, acc_ref, *, tq):
    i = pl.program_id(2)
    ts = FOX_TS
    nslab = tq // ts
    chains = [(h, e) for h in range(nslab) for e in range(2)]
    key = lax.broadcasted_iota(jnp.int32, (FOX_KB, LANES), 0)
    query = lax.broadcasted_iota(jnp.int32, (FOX_KB, LANES), 1)

    m_ref[...] = jnp.full(m_ref.shape, NEG, F32)
    acc_ref[...] = jnp.zeros(acc_ref.shape, F32)

    def logits(n, block):
        h, e = chains[n]
        start = pl.multiple_of(block * ts, ts)
        s = _dot_nt(ka_ref[0, e, pl.ds(start, ts), :], qa_ref[0, e, h * ts:(h + 1) * ts, :])
        for c in range(ts // LANES):
            s_ref[n, c, 0:ts, :] = s[:, c * LANES:(c + 1) * LANES]

    def softmax(n, diagonal):
        kb = FOX_KB
        for c0 in range(0, ts, LANES):
            cols = slice(c0, c0 + LANES)

            def piece(k0):
                s = s_ref[n, c0 // LANES, k0:k0 + kb, :]
                if diagonal:
                    s = jnp.where(key + k0 <= query + c0, s, NEG)
                return s

            top = piece(0)
            for k0 in range(kb, ts, kb):
                top = jnp.maximum(top, piece(k0))
            m_old = m_ref[n, :, cols]
            m_new = jnp.maximum(m_old, jnp.max(top, axis=0, keepdims=True))
            for k0 in range(0, ts, kb):
                p_ref[n, c0 // LANES, k0:k0 + kb, :] = jnp.exp2(
                    (piece(k0) - m_new[0:1, :]).astype(BF16))
            m_ref[n, :, cols] = m_new
            a_ref[n, :, cols] = jnp.exp2(m_old - m_new)

    def accumulate(n, block):
        e = chains[n][1]
        vt = vt_ref[0, 0, block, e * FOX_VROWS:(e + 1) * FOX_VROWS, :]
        p = jnp.concatenate([p_ref[n, c, 0:ts, :] for c in range(ts // LANES)], axis=1)
        acc_ref[n] = a_ref[n, 0:1, :] * acc_ref[n] + _dot(vt, p)

    def step(block, live, diagonal_slab, live_next):
        prev = None
        for n in live:
            softmax(n, chains[n][0] == diagonal_slab)
            if n in live_next:
                logits(n, block + 1)
            if prev is not None:
                accumulate(prev, block)
            prev = n
        accumulate(prev, block)

    everyone = list(range(len(chains)))
    for n in everyone:
        logits(n, 0)

    def body(j, _):
        step(j, everyone, None, everyone)
        return 0

    lax.fori_loop(0, i * nslab, body, 0)
    for c in range(nslab):
        live = [n for n in everyone if chains[n][0] >= c]
        live_next = [n for n in everyone if chains[n][0] >= c + 1] if c + 1 < nslab else []
        step(i * nslab + c, live, c, live_next)

    for h in range(nslab):
        outs = [acc_ref[2 * h + e, 0:HEAD_DIM, :] / acc_ref[2 * h + e, HEAD_DIM:HEAD_DIM + 1, :]
                for e in range(2)]
        o_ref[0, h * ts:(h + 1) * ts, :] = jnp.concatenate(outs, axis=0).T.astype(BF16)


def _fox(qa, ka, vt, tq=1024):
    b, _, s, _ = qa.shape
    nch = 2 * (tq // FOX_TS)
    return pl.pallas_call(
        functools.partial(_fox_kernel, tq=tq),
        grid=(b, N_PAIRS, s // tq),
        in_specs=[pl.BlockSpec((1, 2, tq, LANES), lambda i, p, j: (i, p, j, 0)),
                  pl.BlockSpec((1, 2, s, LANES), lambda i, p, j: (i, p, 0, 0)),
                  pl.BlockSpec((1, 1, s // FOX_TS, 2 * FOX_VROWS, FOX_TS),
                               lambda i, p, j: (i, p, 0, 0, 0))],
        out_specs=pl.BlockSpec((1, tq, LANES), lambda i, p, j: (i, j, p)),
        out_shape=jax.ShapeDtypeStruct((b, s, D_MODEL), BF16),
        scratch_shapes=[pltpu.VMEM((nch, FOX_TS // LANES, FOX_TS + SUBLANES, LANES), F32),
                        pltpu.VMEM((nch, FOX_TS // LANES, FOX_TS + 2 * SUBLANES, LANES), BF16),
                        pltpu.VMEM((nch, SUBLANES, FOX_TS), F32),
                        pltpu.VMEM((nch, SUBLANES, FOX_TS), F32),
                        pltpu.VMEM((nch, FOX_VROWS, FOX_TS), F32)],
        compiler_params=_cparams(3),
        name="fox",
    )(qa, ka, vt)


SB_KC = 256
SB_TS = 512
SB_MAX_LOG2 = 126.0
SB_DEAD_LOG2 = -(SB_MAX_LOG2 + 150.0)


SB_RB = 64
SB_UNROLL = 2


def _sb_kernel(q_ref, k_ref, v_ref, o_ref, qm_ref, z_ref, lb_ref, ic_ref, w_ref,
               rs_ref, r_ref, acc_ref, *, tq):
    i = pl.program_id(2)
    kc, ts, rb = SB_KC, SB_TS, SB_RB
    nsub = tq // kc
    nslab = tq // ts
    chains = [(h, e) for h in range(nslab) for e in range(2)]
    row = lax.broadcasted_iota(jnp.int32, (rb, kc), 0)
    col = lax.broadcasted_iota(jnp.int32, (rb, kc), 1)
    kr = lax.broadcasted_iota(jnp.int32, (kc, kc), 0)
    kcol = lax.broadcasted_iota(jnp.int32, (kc, kc), 1)
    at_or_after = (kr >= kcol).astype(BF16)

    lane = lax.broadcasted_iota(jnp.int32, (ts, LANES), 1)
    for n, (h, e) in enumerate(chains):
        q2 = q_ref[0, h * ts:(h + 1) * ts, :]
        mine = (lane < HEAD_DIM) if e == 0 else (lane >= HEAD_DIM)
        qm_ref[n] = jnp.where(mine, q2, jnp.zeros_like(q2))
    r_ref[...] = jnp.zeros(r_ref.shape, F32)
    acc_ref[...] = jnp.zeros(acc_ref.shape, F32)

    def logits(n, start):
        z_ref[n] = jnp.minimum(_dot_nt(qm_ref[n], k_ref[0, pl.ds(start, kc), :]), SB_MAX_LOG2)

    def log_terms(n, offset):
        for r0 in range(0, ts, rb):
            rows = slice(r0, r0 + rb)
            log_1m_beta = jnp.log(1.0 + jnp.exp2(z_ref[n, rows, :])) * (-LOG2E)
            if offset is not None:
                log_1m_beta = jnp.where(col + (offset - r0) < row, log_1m_beta, 0.0)
            lb_ref[n, rows, :] = log_1m_beta.astype(BF16)
            rs_ref[n, rows, :] = jnp.broadcast_to(
                jnp.sum(log_1m_beta, axis=-1, keepdims=True), (rb, LANES))

    def cumulate(n):
        ic_ref[n] = _dot(lb_ref[n], at_or_after)

    def weights(n, offset):
        for r0 in range(0, ts, rb):
            rows = slice(r0, r0 + rb)
            r = r_ref[n, rows, :]
            log_w = (z_ref[n, rows, :] + ic_ref[n, rows, :]
                     + jnp.concatenate([r] * (kc // LANES), axis=1))
            if offset is not None:
                log_w = jnp.where(col + (offset - r0) < row, log_w, NEG)
            w_ref[n, rows, :] = jnp.exp2(log_w.astype(BF16))
            r_ref[n, rows, :] = r + rs_ref[n, rows, :]

    def accumulate(n, start):
        acc_ref[n] += _dot(w_ref[n], v_ref[0, pl.ds(start, kc), :])

    def chunk(start, next_start, live, offsets, live_next):
        prev = None
        for n in live:
            log_terms(n, offsets[chains[n][0]])
            cumulate(n)
            if prev is not None:
                weights(prev, offsets[chains[prev][0]])
                if prev in live_next:
                    logits(prev, next_start)
                accumulate(prev, start)
            prev = n
        weights(prev, offsets[chains[prev][0]])
        if prev in live_next:
            logits(prev, next_start)
        accumulate(prev, start)
        for n in live_next:
            if n not in live:
                logits(n, next_start)

    def visibility(c):
        live, offsets = [], {}
        for n, (h, _) in enumerate(chains):
            if c * kc >= (h + 1) * ts:
                continue
            live.append(n)
            offsets[h] = None if (c + 1) * kc <= h * ts else c * kc - h * ts
        return live, offsets

    everyone = list(range(len(chains)))
    no_offsets = {h: None for h in range(nslab)}
    live, offsets = visibility(nsub - 1)
    for n in live:
        logits(n, pl.multiple_of(i * tq + (nsub - 1) * kc, kc))
    for c in reversed(range(nsub)):
        start = pl.multiple_of(i * tq + c * kc, kc)
        live_next = visibility(c - 1)[0] if c > 0 else everyone
        chunk(start, pl.multiple_of(jnp.maximum(start - kc, 0), kc), live, offsets, live_next)
        if c > 0:
            live, offsets = visibility(c - 1)

    def alive(first_chain):
        return (jnp.max(r_ref[first_chain:]) > SB_DEAD_LOG2).astype(jnp.int32)

    trips = i * (nsub // SB_UNROLL)

    def walk(t0, live, first_chain):
        def body(state):
            t, _ = state
            for u in range(SB_UNROLL):
                start = pl.multiple_of(i * tq - (t * SB_UNROLL + u + 1) * kc, kc)
                chunk(start, pl.multiple_of(jnp.maximum(start - kc, 0), kc), live, no_offsets,
                      live)
            return t + 1, alive(first_chain)

        t_end, _ = lax.while_loop(lambda state: (state[0] < trips) & (state[1] > 0), body,
                                  (t0, alive(first_chain)))
        return t_end

    t_mid = walk(jnp.int32(0), everyone, len(chains) - 2)
    walk(t_mid, everyone[:2], 0)
    for h in range(nslab):
        o_ref[0, h * ts:(h + 1) * ts, :] = jnp.where(
            lane < HEAD_DIM, acc_ref[2 * h], acc_ref[2 * h + 1]).astype(BF16)


def _sb(y3d, tq=1024):
    b, s, _ = y3d.shape
    nch = 2 * (tq // SB_TS)
    return pl.pallas_call(
        functools.partial(_sb_kernel, tq=tq),
        grid=(b, N_PAIRS, s // tq),
        in_specs=[pl.BlockSpec((1, tq, LANES), lambda i, p, j: (i, j, C_SQ // LANES + p)),
                  pl.BlockSpec((1, s, LANES), lambda i, p, j: (i, 0, C_SK // LANES + p)),
                  pl.BlockSpec((1, s, LANES), lambda i, p, j: (i, 0, C_SV // LANES + p))],
        out_specs=pl.BlockSpec((1, tq, LANES), lambda i, p, j: (i, j, p)),
        out_shape=jax.ShapeDtypeStruct((b, s, D_MODEL), BF16),
        scratch_shapes=[pltpu.VMEM((nch, SB_TS, LANES), BF16),
                        pltpu.VMEM((nch, SB_TS, SB_KC), F32),
                        pltpu.VMEM((nch, SB_TS, SB_KC), BF16),
                        pltpu.VMEM((nch, SB_TS, SB_KC), F32),
                        pltpu.VMEM((nch, SB_TS, SB_KC), BF16),
                        pltpu.VMEM((nch, SB_TS, LANES), F32),
                        pltpu.VMEM((nch, SB_TS, LANES), F32),
                        pltpu.VMEM((nch, SB_TS, LANES), F32)],
        compiler_params=_cparams(3),
        name="sb",
    )(y3d, y3d, y3d)


LRU_TC = 256
LRU_HALO = 8


def _lru_kernel(x_ref, g_ref, cw_ref, cb_ref, wa_ref, ba_ref, wx_ref, bx_ref, lam_ref,
                o_ref, xbuf_ref, h_ref, *, ts):
    @pl.when(pl.program_id(2) == 0)
    def _():
        xbuf_ref[0:LRU_HALO, :] = jnp.zeros((LRU_HALO, LRU_TC), F32)
        h_ref[...] = jnp.zeros_like(h_ref)

    xbuf_ref[LRU_HALO:LRU_HALO + ts, :] = x_ref[0].astype(F32)
    cw = cw_ref[...]
    xc = cb_ref[...] + xbuf_ref[LRU_HALO:LRU_HALO + ts, :] * cw[3:4, :]
    for d in range(1, 4):
        xc = xc + xbuf_ref[LRU_HALO - d:LRU_HALO - d + ts, :] * cw[3 - d:4 - d, :]
    xbuf_ref[0:LRU_HALO, :] = xbuf_ref[ts:ts + LRU_HALO, :]

    xcb = xc.astype(BF16)
    r = jax.nn.sigmoid(_dot(xcb, wa_ref[0]) + ba_ref[...])
    gi = jax.nn.sigmoid(_dot(xcb, wx_ref[0]) + bx_ref[...])
    lam = lam_ref[...]
    softplus_neg_lam = jnp.maximum(-lam, 0.0) + _log1p_exp_neg_abs(lam)
    log_a = -LRU_C * r * softplus_neg_lam
    a = jnp.exp(log_a)
    one_m_a2 = -jnp.tanh(log_a) * (a * a + 1.0)
    u = one_m_a2 * lax.rsqrt(jnp.maximum(one_m_a2, F32_TINY)) * (gi * xc)

    row = lax.broadcasted_iota(jnp.int32, (ts, LRU_TC), 0) % SUBLANES
    d = 1
    while d < SUBLANES:
        keep = row >= d
        a_sh = jnp.where(keep, pltpu.roll(a, d, axis=0), 1.0)
        u_sh = jnp.where(keep, pltpu.roll(u, d, axis=0), 0.0)
        u = a * u_sh + u
        a = a * a_sh
        d *= 2
    state = h_ref[...]
    groups = []
    for r0 in range(0, ts, SUBLANES):
        h = u[r0:r0 + SUBLANES, :] + a[r0:r0 + SUBLANES, :] * state
        groups.append(h)
        state = h[SUBLANES - 1:SUBLANES, :]
    h_ref[...] = state
    h = jnp.concatenate(groups, axis=0)
    o_ref[0] = (h * jax.nn.gelu(g_ref[0].astype(F32))).astype(BF16)


def _lru(y3d, cw, cb, wa_bd, ba, wx_bd, bx, lam, ts=512):
    b, s, _ = y3d.shape
    nct = D_MODEL // LRU_TC
    vec = lambda rows: pl.BlockSpec((rows, LRU_TC), lambda i, c, j: (0, c))
    mat = pl.BlockSpec((1, LRU_TC, LRU_TC), lambda i, c, j: (c, 0, 0))
    return pl.pallas_call(
        functools.partial(_lru_kernel, ts=ts),
        grid=(b, nct, s // ts),
        in_specs=[pl.BlockSpec((1, ts, LRU_TC), lambda i, c, j: (i, j, C_LX // LRU_TC + c)),
                  pl.BlockSpec((1, ts, LRU_TC), lambda i, c, j: (i, j, C_LG // LRU_TC + c)),
                  vec(4), vec(1), mat, vec(1), mat, vec(1), vec(1)],
        out_specs=pl.BlockSpec((1, ts, LRU_TC), lambda i, c, j: (i, j, c)),
        out_shape=jax.ShapeDtypeStruct((b, s, D_MODEL), BF16),
        scratch_shapes=[pltpu.VMEM((ts + LRU_HALO, LRU_TC), F32),
                        pltpu.VMEM((1, LRU_TC), F32)],
        compiler_params=_cparams(3),
        name="lru",
    )(y3d, y3d, cw, cb, wa_bd, ba, wx_bd, bx, lam)


def _memkv_kernel(mem_ref, g_ref, w_ref, gk_ref, k_ref, v_ref):
    x = mem_ref[0]
    h = x * lax.rsqrt(jnp.mean(x * x, axis=-1, keepdims=True) + EPS) * g_ref[...]
    kv = _dot(h.astype(BF16), w_ref[...])
    for hd in range(MEM_HEADS):
        kh = kv[:, hd * MEM_HD:(hd + 1) * MEM_HD]
        kh = kh * lax.rsqrt(jnp.mean(kh * kh, axis=-1, keepdims=True) + EPS) * gk_ref[...]
        k_ref[0, :, hd * MEM_HD:(hd + 1) * MEM_HD] = kh.astype(BF16)
    v_ref[0] = kv[:, D_MODEL:].astype(BF16)


def _memkv(mem, g, w_kv, gk):
    b = mem.shape[0]
    out = jax.ShapeDtypeStruct((b, N_MEM, D_MODEL), BF16)
    return pl.pallas_call(
        _memkv_kernel,
        grid=(b,),
        in_specs=[pl.BlockSpec((1, N_MEM, D_MODEL), lambda i: (i, 0, 0)),
                  pl.BlockSpec((1, D_MODEL), lambda i: (0, 0)),
                  pl.BlockSpec((D_MODEL, 2 * D_MODEL), lambda i: (0, 0)),
                  pl.BlockSpec((1, MEM_HD), lambda i: (0, 0))],
        out_specs=[pl.BlockSpec((1, N_MEM, D_MODEL), lambda i: (i, 0, 0)),
                   pl.BlockSpec((1, N_MEM, D_MODEL), lambda i: (i, 0, 0))],
        out_shape=[out, out],
        compiler_params=_cparams(1),
        name="memkv",
    )(mem, g, w_kv, gk)


def _mem_kernel(q_ref, k_ref, v_ref, gq_ref, o_ref):
    for hd in range(MEM_HEADS):
        sl = slice(hd * MEM_HD, (hd + 1) * MEM_HD)
        q = q_ref[0, :, sl].astype(F32)
        q = q * lax.rsqrt(jnp.mean(q * q, axis=-1, keepdims=True) + EPS) * gq_ref[...]
        q = (q * (MEM_HD ** -0.5)).astype(BF16)
        s = _dot_nt(q, k_ref[0, :, sl])
        p = jnp.exp(s - jnp.max(s, axis=-1, keepdims=True))
        o = _dot(p.astype(BF16), v_ref[0, :, sl]) / jnp.sum(p, axis=-1, keepdims=True)
        o_ref[0, :, sl] = o.astype(BF16)


def _mem(y3d, mk, mv, gq, ts=512):
    b, s, _ = y3d.shape
    return pl.pallas_call(
        _mem_kernel,
        grid=(b, s // ts),
        in_specs=[pl.BlockSpec((1, ts, D_MODEL), lambda i, j: (i, j, C_MQ // D_MODEL)),
                  pl.BlockSpec((1, N_MEM, D_MODEL), lambda i, j: (i, 0, 0)),
                  pl.BlockSpec((1, N_MEM, D_MODEL), lambda i, j: (i, 0, 0)),
                  pl.BlockSpec((1, MEM_HD), lambda i, j: (0, 0))],
        out_specs=pl.BlockSpec((1, ts, D_MODEL), lambda i, j: (i, j, 0)),
        out_shape=jax.ShapeDtypeStruct((b, s, D_MODEL), BF16),
        compiler_params=_cparams(2),
        name="mem",
    )(y3d, mk, mv, gq)


def _merge_kernel(x_ref, b0_ref, b1_ref, b2_ref, b3_ref, gt_ref, bg_ref, wb_ref, wo_ref, o_ref):
    mixed = None
    for n, br in enumerate((b0_ref, b1_ref, b2_ref, b3_ref)):
        gate = jax.nn.sigmoid(
            gt_ref[:, n * D_MODEL:(n + 1) * D_MODEL].astype(F32) + bg_ref[n:n + 1, :])
        term = gate * _dot(br[...], wb_ref[n])
        mixed = term if mixed is None else mixed + term
    o_ref[...] = x_ref[...] + _dot(mixed.astype(BF16), wo_ref[...])


def _merge(x2d, branches, y2d, bg, wb, wo, tm=256):
    m = x2d.shape[0]
    row = lambda w: pl.BlockSpec((tm, w), lambda i: (i, 0))
    return pl.pallas_call(
        _merge_kernel,
        grid=(m // tm,),
        in_specs=[row(D_MODEL), row(D_MODEL), row(D_MODEL), row(D_MODEL), row(D_MODEL),
                  pl.BlockSpec((tm, 4 * D_MODEL), lambda i: (i, C_GT // (4 * D_MODEL))),
                  pl.BlockSpec((4, D_MODEL), lambda i: (0, 0)),
                  pl.BlockSpec((4, D_MODEL, D_MODEL), lambda i: (0, 0, 0)),
                  pl.BlockSpec((D_MODEL, D_MODEL), lambda i: (0, 0))],
        out_specs=row(D_MODEL),
        out_shape=jax.ShapeDtypeStruct((m, D_MODEL), F32),
        compiler_params=_cparams(1),
        name="merge",
    )(x2d, *branches, y2d, bg, wb, wo)


FFN_TC = 256
FFN_HALO = 8
FFN_RB = 64


def _ffn_kernel(x_ref, g_ref, wg_ref, wv_ref, cw_ref, cb_ref, wd_ref, o_ref,
                h_ref, gbuf_ref, vbuf_ref, act_ref, *, tm, tiles_per_seq):
    i = pl.program_id(0)
    x = x_ref[...]
    h_ref[...] = (x * lax.rsqrt(jnp.mean(x * x, axis=-1, keepdims=True) + EPS)
                  * g_ref[...]).astype(BF16)

    @pl.when((i % tiles_per_seq) == 0)
    def _():
        gbuf_ref[0:FFN_HALO, :] = jnp.zeros((FFN_HALO, D_FF), F32)

    def up(c):
        cols = slice(c * FFN_TC, (c + 1) * FFN_TC)
        gbuf_ref[FFN_HALO:FFN_HALO + tm, cols] = _dot(h_ref[...], wg_ref[:, cols])
        vbuf_ref[:, cols] = _dot(h_ref[...], wv_ref[:, cols])

    def activate(c):
        cols = slice(c * FFN_TC, (c + 1) * FFN_TC)
        cw = cw_ref[:, cols]
        cb = cb_ref[:, cols]
        for r0 in range(0, tm, FFN_RB):
            gate = cb
            for d in range(3):
                lo = FFN_HALO + r0 - d
                gate = gate + gbuf_ref[lo:lo + FFN_RB, cols] * cw[2 - d:3 - d, :]
            act = gate * jax.nn.sigmoid(gate) * vbuf_ref[r0:r0 + FFN_RB, cols]
            act_ref[r0:r0 + FFN_RB, cols] = act.astype(BF16)

    nc = D_FF // FFN_TC
    up(0)
    for c in range(1, nc):
        up(c)
        activate(c - 1)
    activate(nc - 1)
    o_ref[...] = x_ref[...] + _dot(act_ref[...], wd_ref[...])
    gbuf_ref[0:FFN_HALO, :] = gbuf_ref[tm:tm + FFN_HALO, :]


def _ffn(x2d, g, w_up, cw, cb, w_down, seq, tm=512):
    m = x2d.shape[0]
    resident = pl.Buffered(1)
    return pl.pallas_call(
        functools.partial(_ffn_kernel, tm=tm, tiles_per_seq=seq // tm),
        grid=(m // tm,),
        in_specs=[pl.BlockSpec((tm, D_MODEL), lambda i: (i, 0)),
                  pl.BlockSpec((1, D_MODEL), lambda i: (0, 0)),
                  pl.BlockSpec((D_MODEL, D_FF), lambda i: (0, 0), pipeline_mode=resident),
                  pl.BlockSpec((D_MODEL, D_FF), lambda i: (0, 1), pipeline_mode=resident),
                  pl.BlockSpec((3, D_FF), lambda i: (0, 0)),
                  pl.BlockSpec((1, D_FF), lambda i: (0, 0)),
                  pl.BlockSpec((D_FF, D_MODEL), lambda i: (0, 0), pipeline_mode=resident)],
        out_specs=pl.BlockSpec((tm, D_MODEL), lambda i: (i, 0)),
        out_shape=jax.ShapeDtypeStruct((m, D_MODEL), F32),
        scratch_shapes=[pltpu.VMEM((tm, D_MODEL), BF16),
                        pltpu.VMEM((tm + FFN_HALO, D_FF), F32),
                        pltpu.VMEM((tm, D_FF), F32),
                        pltpu.VMEM((tm, D_FF), BF16)],
        compiler_params=_cparams(1),
        name="ffn",
    )(x2d, g, w_up, w_up, cw, cb, w_down)


def _block_diag(w):
    per = LRU_TC // HEAD_DIM
    w4 = w.reshape(D_MODEL // LRU_TC, per, HEAD_DIM, HEAD_DIM)
    eye = jnp.eye(per, dtype=w.dtype)
    bd = jnp.einsum('cpde,pq->cpdqe', w4, eye)
    return bd.reshape(D_MODEL // LRU_TC, LRU_TC, LRU_TC)


def kernel(x, mem, attn_norm_g, mem_norm_g, w_in, b_forget, fox_q_norm_g, fox_k_norm_g,
           lru_conv_w, lru_conv_b, lru_w_a, lru_b_a, lru_w_x, lru_b_x, lru_lambda,
           w_mem_kv, mem_q_norm_g, mem_k_norm_g, b_gate, w_branch, w_out,
           ffn_norm_g, w_up, ffn_conv_w, ffn_conv_b, w_down):
    b, s, d = x.shape
    depth = w_in.shape[0]
    m = b * s
    x2d = x.reshape(m, d)
    for l in range(depth):
        wy = jnp.concatenate([w_in[l, :, W_GATES:], w_in[l, :, :W_FOX_END],
                              w_in[l, :, W_F_END:W_GATES]], axis=1)
        sb_q_scale = jnp.where((jnp.arange(N_Y) >= C_SQ) & (jnp.arange(N_Y) < C_SK),
                               HEAD_DIM ** -0.5 * LOG2E, 1.0).astype(F32)
        wy = (wy * sb_q_scale).astype(BF16)
        wf = jnp.pad(w_in[l, :, W_FOX_END:W_F_END], ((0, 0), (0, LANES - N_HEADS)))
        bf = jnp.pad(b_forget[l], (0, LANES - N_HEADS)).reshape(1, LANES)

        y2d, f2d = _proj(x2d, attn_norm_g[l].reshape(1, d), wy, wf)
        y3d = y2d.reshape(b, s, N_Y)

        fcum = _fcum(f2d.reshape(b, s, LANES), bf)
        qa, ka, vt = _foxprep(y3d, fcum,
                              jnp.tile(fox_q_norm_g[l], 2).reshape(1, LANES),
                              jnp.tile(fox_k_norm_g[l], 2).reshape(1, LANES))
        y_fox = _fox(qa, ka, vt)
        y_sb = _sb(y3d)
        y_lru = _lru(y3d, lru_conv_w[l], lru_conv_b[l].reshape(1, d),
                     _block_diag(lru_w_a[l]).astype(BF16), lru_b_a[l].reshape(1, d),
                     _block_diag(lru_w_x[l]).astype(BF16), lru_b_x[l].reshape(1, d),
                     lru_lambda[l].reshape(1, d))
        mk, mv = _memkv(mem, mem_norm_g[l].reshape(1, d), w_mem_kv[l].astype(BF16),
                        mem_k_norm_g[l].reshape(1, MEM_HD))
        y_mem = _mem(y3d, mk, mv, mem_q_norm_g[l].reshape(1, MEM_HD))

        branches = [t.reshape(m, d) for t in (y_fox, y_lru, y_sb, y_mem)]
        x2d = _merge(x2d, branches, y2d, b_gate[l], w_branch[l].astype(BF16),
                     w_out[l].astype(BF16))
        x2d = _ffn(x2d, ffn_norm_g[l].reshape(1, d), w_up[l].astype(BF16), ffn_conv_w[l],
                   ffn_conv_b[l].reshape(1, D_FF), w_down[l].astype(BF16), s)
    return x2d.reshape(b, s, d)
```

```python
import functools

import jax
import jax.numpy as jnp
from jax import lax
from jax.experimental import pallas as pl
from jax.experimental.pallas import tpu as pltpu

F32 = jnp.float32
BF16 = jnp.bfloat16
HIGHEST = lax.Precision.HIGHEST

D_MODEL = 1024
HEAD_DIM = 64
N_HEADS = 16
N_PAIRS = N_HEADS // 2
LANES = 128
SUBLANES = 8
N_MEM = 256
MEM_HEADS = 4
MEM_HD = 256
D_FF = 2816
LRU_C = 8.0
EPS = 1e-6
NEG = -1e30
LOG2E = 1.4426950408889634
F32_TINY = 1.1754944e-38

C_GT, C_FQ, C_FK, C_FV, C_LX, C_LG, C_SQ, C_SK, C_SV, C_MQ = (
    0, 4096, 5120, 6144, 7168, 8192, 9216, 10240, 11264, 12288)
N_Y = 13312
W_FOX_END, W_F_END, W_GATES = 3072, 3088, 9232

VMEM_LIMIT = 56 * 1024 * 1024


def _cparams(n_axes):
    return pltpu.CompilerParams(dimension_semantics=("arbitrary",) * n_axes,
                                vmem_limit_bytes=VMEM_LIMIT)


def _dot(a, b):
    return jnp.dot(a, b, preferred_element_type=F32)


def _dot_nt(a, b):
    return lax.dot_general(a, b, (((1,), (1,)), ((), ())), preferred_element_type=F32)


def _log1p_exp_neg_abs(z):
    return jnp.log(1.0 + jnp.exp(-jnp.abs(z)))


def _proj_kernel(x_ref, g_ref, w_ref, wf_ref, y_ref, f_ref, h_ref):
    @pl.when(pl.program_id(1) == 0)
    def _():
        x = x_ref[...]
        h = x * lax.rsqrt(jnp.mean(x * x, axis=-1, keepdims=True) + EPS) * g_ref[...]
        h_ref[...] = h.astype(BF16)
        f_ref[...] = jnp.dot(h, wf_ref[...], precision=HIGHEST, preferred_element_type=F32)

    y_ref[...] = _dot(h_ref[...], w_ref[...]).astype(BF16)


def _proj(x2d, g, wy, wf, tm=1024, tn=1024):
    m = x2d.shape[0]
    return pl.pallas_call(
        _proj_kernel,
        grid=(m // tm, N_Y // tn),
        in_specs=[pl.BlockSpec((tm, D_MODEL), lambda i, j: (i, 0)),
                  pl.BlockSpec((1, D_MODEL), lambda i, j: (0, 0)),
                  pl.BlockSpec((D_MODEL, tn), lambda i, j: (0, j)),
                  pl.BlockSpec((D_MODEL, LANES), lambda i, j: (0, 0))],
        out_specs=[pl.BlockSpec((tm, tn), lambda i, j: (i, j)),
                   pl.BlockSpec((tm, LANES), lambda i, j: (i, 0))],
        out_shape=[jax.ShapeDtypeStruct((m, N_Y), BF16),
                   jax.ShapeDtypeStruct((m, LANES), F32)],
        scratch_shapes=[pltpu.VMEM((tm, D_MODEL), BF16)],
        compiler_params=_cparams(2),
        name="proj",
    )(x2d, g, wy, wf)


def _fcum_kernel(f_ref, b_ref, o_ref, carry_ref):
    @pl.when(pl.program_id(1) == 0)
    def _():
        carry_ref[...] = jnp.zeros_like(carry_ref)

    z = f_ref[0] + b_ref[...]
    log_f = jnp.minimum(z, 0.0) - _log1p_exp_neg_abs(z)
    ts = z.shape[0]
    row = lax.broadcasted_iota(jnp.int32, (ts, ts), 0)
    col = lax.broadcasted_iota(jnp.int32, (ts, ts), 1)
    tri = (row >= col).astype(F32)
    c = jnp.dot(tri, log_f, precision=HIGHEST, preferred_element_type=F32) + carry_ref[...]
    carry_ref[...] = c[ts - 1:ts, :]

    lane = lax.broadcasted_iota(jnp.int32, (ts, LANES), 1)
    f = jnp.where(lane < N_HEADS, c * LOG2E, 0.0)
    hi = f.astype(BF16).astype(F32)
    r1 = f - hi
    mid = r1.astype(BF16).astype(F32)
    lo = (r1 - mid).astype(BF16).astype(F32)
    packed = hi + pltpu.roll(mid, N_HEADS, axis=1) + pltpu.roll(lo, 2 * N_HEADS, axis=1)
    o_ref[0] = packed.astype(BF16)


def _fcum(f3d, b_pad, ts=512):
    b, s, _ = f3d.shape
    return pl.pallas_call(
        _fcum_kernel,
        grid=(b, s // ts),
        in_specs=[pl.BlockSpec((1, ts, LANES), lambda i, j: (i, j, 0)),
                  pl.BlockSpec((1, LANES), lambda i, j: (0, 0))],
        out_specs=pl.BlockSpec((1, ts, LANES), lambda i, j: (i, j, 0)),
        out_shape=jax.ShapeDtypeStruct((b, s, LANES), BF16),
        scratch_shapes=[pltpu.VMEM((1, LANES), F32)],
        compiler_params=_cparams(2),
        name="fcum",
    )(f3d, b_pad)


def _foxprep_kernel(yq_ref, yk_ref, yv_ref, f_ref, gq_ref, gk_ref, qa_ref, ka_ref, vt_ref):
    hp = pl.program_id(1)
    ones = jnp.ones((FOX_ONES, FOX_TS), BF16)
    for c in range(vt_ref.shape[2]):
        vt = yv_ref[0, c * FOX_TS:(c + 1) * FOX_TS, :].astype(F32).T.astype(BF16)
        vt_ref[0, 0, c] = jnp.concatenate(
            [vt[:HEAD_DIM], ones, vt[HEAD_DIM:], ones], axis=0)
    ts = yq_ref.shape[1]
    lane = lax.broadcasted_iota(jnp.int32, (ts, LANES), 1)
    lo_half = lane < HEAD_DIM

    hr = lax.broadcasted_iota(jnp.int32, (LANES, LANES), 0)
    hc = lax.broadcasted_iota(jnp.int32, (LANES, LANES), 1)
    same_head = ((hr < HEAD_DIM) == (hc < HEAD_DIM)).astype(BF16)

    def headnorm(y, g):
        ms = _dot((y * y).astype(BF16), same_head) * (1.0 / HEAD_DIM)
        return y * lax.rsqrt(ms + EPS) * g

    qn = headnorm(yq_ref[0].astype(F32), gq_ref[...]) * (HEAD_DIM ** -0.5 * LOG2E)
    kn = headnorm(yk_ref[0].astype(F32), gk_ref[...])

    pieces = f_ref[0]
    sel_r = lax.broadcasted_iota(jnp.int32, (LANES, 2 * LANES), 0)
    sel_c = lax.broadcasted_iota(jnp.int32, (LANES, 2 * LANES), 1)
    ones_q = ((lane >= HEAD_DIM + 3) & (lane < HEAD_DIM + 6)).astype(F32)
    ones_k = ((lane >= HEAD_DIM) & (lane < HEAD_DIM + 3)).astype(F32)
    for e in range(2):
        head = 2 * hp + e
        piece_of_row = sel_r - head
        to_q = (piece_of_row == (sel_c - HEAD_DIM) * N_HEADS) & (sel_c < HEAD_DIM + 3)
        to_k = (piece_of_row == (sel_c - (LANES + HEAD_DIM + 3)) * N_HEADS) & (
            sel_c >= LANES + HEAD_DIM + 3)
        valid = (piece_of_row >= 0) & (piece_of_row <= 2 * N_HEADS)
        sel = (jnp.where(valid & to_q, 1.0, 0.0) - jnp.where(valid & to_k, 1.0, 0.0)).astype(BF16)
        placed = _dot(pieces, sel)
        q_part = qn if e == 0 else pltpu.roll(qn, HEAD_DIM, axis=1)
        k_part = kn if e == 0 else pltpu.roll(kn, HEAD_DIM, axis=1)
        q_aug = jnp.where(lo_half, q_part, placed[:, :LANES] + ones_q)
        k_aug = jnp.where(lo_half, k_part, placed[:, LANES:] + ones_k)
        qa_ref[0, e] = q_aug.astype(BF16)
        ka_ref[0, e] = k_aug.astype(BF16)


def _foxprep(y3d, fcum, gq2, gk2, ts=2048):
    b, s, _ = y3d.shape
    out = jax.ShapeDtypeStruct((b, N_HEADS, s, LANES), BF16)
    vt = jax.ShapeDtypeStruct((b, N_PAIRS, s // FOX_TS, 2 * FOX_VROWS, FOX_TS), BF16)
    return pl.pallas_call(
        _foxprep_kernel,
        grid=(b, N_PAIRS, s // ts),
        in_specs=[pl.BlockSpec((1, ts, LANES), lambda i, p, j: (i, j, C_FQ // LANES + p)),
                  pl.BlockSpec((1, ts, LANES), lambda i, p, j: (i, j, C_FK // LANES + p)),
                  pl.BlockSpec((1, ts, LANES), lambda i, p, j: (i, j, C_FV // LANES + p)),
                  pl.BlockSpec((1, ts, LANES), lambda i, p, j: (i, j, 0)),
                  pl.BlockSpec((1, LANES), lambda i, p, j: (0, 0)),
                  pl.BlockSpec((1, LANES), lambda i, p, j: (0, 0))],
        out_specs=[pl.BlockSpec((1, 2, ts, LANES), lambda i, p, j: (i, p, j, 0)),
                   pl.BlockSpec((1, 2, ts, LANES), lambda i, p, j: (i, p, j, 0)),
                   pl.BlockSpec((1, 1, ts // FOX_TS, 2 * FOX_VROWS, FOX_TS),
                                lambda i, p, j: (i, p, j, 0, 0))],
        out_shape=[out, out, vt],
        compiler_params=_cparams(3),
        name="foxprep",
    )(y3d, y3d, y3d, fcum, gq2, gk2)


FOX_TS = 512
FOX_KB = 64
FOX_ONES = 16
FOX_VROWS = HEAD_DIM + FOX_ONES


def _fox_kernel(qa_ref, ka_ref, vt_ref, o_ref, s_ref, p_ref, m_ref, a_ref, acc_ref, *, tq):
    i = pl.program_id(2)
    ts = FOX_TS
    nslab = tq // ts
    chains = [(h, e) for h in range(nslab) for e in range(2)]
    key = lax.broadcasted_iota(jnp.int32, (FOX_KB, LANES), 0)
    query = lax.broadcasted_iota(jnp.int32, (FOX_KB, LANES), 1)

    m_ref[...] = jnp.full(m_ref.shape, NEG, F32)
    acc_ref[...] = jnp.zeros(acc_ref.shape, F32)

    def logits(n, block):
        h, e = chains[n]
        start = pl.multiple_of(block * ts, ts)
        s = _dot_nt(ka_ref[0, e, pl.ds(start, ts), :], qa_ref[0, e, h * ts:(h + 1) * ts, :])
        for c in range(ts // LANES):
            s_ref[n, c, 0:ts, :] = s[:, c * LANES:(c + 1) * LANES]

    def softmax(n, diagonal):
        kb = FOX_KB
        strips = range(0, ts, LANES)

        def piece(c0, k0):
            s = s_ref[n, c0 // LANES, k0:k0 + kb, :]
            if diagonal and k0 + kb > c0:
                s = jnp.where(key + k0 <= query + c0, s, NEG)
            return s

        def last_key(c0):
            return c0 + LANES if diagonal else ts

        m_new = {}
        for c0 in strips:
            cols = slice(c0, c0 + LANES)
            top = piece(c0, 0)
            for k0 in range(kb, last_key(c0), kb):
                top = jnp.maximum(top, piece(c0, k0))
            m_old = m_ref[n, :, cols]
            m_new[c0] = jnp.maximum(m_old, jnp.max(top, axis=0, keepdims=True))
            m_ref[n, :, cols] = m_new[c0]
            a_ref[n, :, cols] = jnp.exp2(m_old - m_new[c0])
        for c0 in strips:
            for k0 in range(0, last_key(c0), kb):
                p_ref[n, c0 // LANES, k0:k0 + kb, :] = jnp.exp2(
                    (piece(c0, k0) - m_new[c0][0:1, :]).astype(BF16))
            if last_key(c0) < ts:
                p_ref[n, c0 // LANES, last_key(c0):ts, :] = jnp.zeros(
                    (ts - last_key(c0), LANES), BF16)

    def accumulate(n, block):
        e = chains[n][1]
        vt = vt_ref[0, 0, block, e * FOX_VROWS:(e + 1) * FOX_VROWS, :]
        p = jnp.concatenate([p_ref[n, c, 0:ts, :] for c in range(ts // LANES)], axis=1)
        acc_ref[n] = a_ref[n, 0:1, :] * acc_ref[n] + _dot(vt, p)

    def step(block, live, diagonal_slab, live_next):
        prev = None
        for n in live:
            softmax(n, chains[n][0] == diagonal_slab)
            if n in live_next:
                logits(n, block + 1)
            if prev is not None:
                accumulate(prev, block)
            prev = n
        accumulate(prev, block)

    everyone = list(range(len(chains)))
    for n in everyone:
        logits(n, 0)

    def body(j, _):
        step(j, everyone, None, everyone)
        return 0

    lax.fori_loop(0, i * nslab, body, 0)
    for c in range(nslab):
        live = [n for n in everyone if chains[n][0] >= c]
        live_next = [n for n in everyone if chains[n][0] >= c + 1] if c + 1 < nslab else []
        step(i * nslab + c, live, c, live_next)

    for h in range(nslab):
        outs = [acc_ref[2 * h + e, 0:HEAD_DIM, :] / acc_ref[2 * h + e, HEAD_DIM:HEAD_DIM + 1, :]
                for e in range(2)]
        o_ref[0, h * ts:(h + 1) * ts, :] = jnp.concatenate(outs, axis=0).T.astype(BF16)


def _fox(qa, ka, vt, tq=2048):
    b, _, s, _ = qa.shape
    nch = 2 * (tq // FOX_TS)
    return pl.pallas_call(
        functools.partial(_fox_kernel, tq=tq),
        grid=(b, N_PAIRS, s // tq),
        in_specs=[pl.BlockSpec((1, 2, tq, LANES), lambda i, p, j: (i, p, j, 0)),
                  pl.BlockSpec((1, 2, s, LANES), lambda i, p, j: (i, p, 0, 0)),
                  pl.BlockSpec((1, 1, s // FOX_TS, 2 * FOX_VROWS, FOX_TS),
                               lambda i, p, j: (i, p, 0, 0, 0))],
        out_specs=pl.BlockSpec((1, tq, LANES), lambda i, p, j: (i, j, p)),
        out_shape=jax.ShapeDtypeStruct((b, s, D_MODEL), BF16),
        scratch_shapes=[pltpu.VMEM((nch, FOX_TS // LANES, FOX_TS + SUBLANES, LANES), F32),
                        pltpu.VMEM((nch, FOX_TS // LANES, FOX_TS + 2 * SUBLANES, LANES), BF16),
                        pltpu.VMEM((nch, SUBLANES, FOX_TS), F32),
                        pltpu.VMEM((nch, SUBLANES, FOX_TS), F32),
                        pltpu.VMEM((nch, FOX_VROWS, FOX_TS), F32)],
        compiler_params=_cparams(3),
        name="fox",
    )(qa, ka, vt)


SB_KC = 256
SB_TS = 512
SB_MAX_LOG2 = 126.0
SB_DEAD_LOG2 = -(SB_MAX_LOG2 + 150.0)


SB_RB = 64
SB_UNROLL = 2


def _sb_kernel(q_ref, k_ref, v_ref, o_ref, qm_ref, z_ref, lb_ref, ic_ref, w_ref,
               rs_ref, r_ref, acc_ref, *, tq):
    i = pl.program_id(2)
    kc, ts, rb = SB_KC, SB_TS, SB_RB
    nsub = tq // kc
    nslab = tq // ts
    chains = [(h, e) for h in range(nslab) for e in range(2)]
    row = lax.broadcasted_iota(jnp.int32, (rb, kc), 0)
    col = lax.broadcasted_iota(jnp.int32, (rb, kc), 1)
    kr = lax.broadcasted_iota(jnp.int32, (kc, kc), 0)
    kcol = lax.broadcasted_iota(jnp.int32, (kc, kc), 1)
    at_or_after = (kr >= kcol).astype(BF16)

    lane = lax.broadcasted_iota(jnp.int32, (ts, LANES), 1)
    for n, (h, e) in enumerate(chains):
        q2 = q_ref[0, h * ts:(h + 1) * ts, :]
        mine = (lane < HEAD_DIM) if e == 0 else (lane >= HEAD_DIM)
        qm_ref[n] = jnp.where(mine, q2, jnp.zeros_like(q2))
    r_ref[...] = jnp.zeros(r_ref.shape, F32)
    acc_ref[...] = jnp.zeros(acc_ref.shape, F32)

    def logits(n, start):
        z_ref[n] = jnp.minimum(_dot_nt(qm_ref[n], k_ref[0, pl.ds(start, kc), :]), SB_MAX_LOG2)

    def log_terms(n, offset):
        for r0 in range(0, ts, rb):
            rows = slice(r0, r0 + rb)
            log_1m_beta = jnp.log(1.0 + jnp.exp2(z_ref[n, rows, :])) * (-LOG2E)
            if offset is not None:
                log_1m_beta = jnp.where(col + (offset - r0) < row, log_1m_beta, 0.0)
            lb_ref[n, rows, :] = log_1m_beta.astype(BF16)
            rs_ref[n, rows, :] = jnp.broadcast_to(
                jnp.sum(log_1m_beta, axis=-1, keepdims=True), (rb, LANES))

    def cumulate(n):
        ic_ref[n] = _dot(lb_ref[n], at_or_after)

    def weights(n, offset):
        for r0 in range(0, ts, rb):
            rows = slice(r0, r0 + rb)
            r = r_ref[n, rows, :]
            log_w = (z_ref[n, rows, :] + ic_ref[n, rows, :]
                     + jnp.concatenate([r] * (kc // LANES), axis=1))
            if offset is not None:
                log_w = jnp.where(col + (offset - r0) < row, log_w, NEG)
            w_ref[n, rows, :] = jnp.exp2(log_w.astype(BF16))
            r_ref[n, rows, :] = r + rs_ref[n, rows, :]

    def accumulate(n, start):
        acc_ref[n] += _dot(w_ref[n], v_ref[0, pl.ds(start, kc), :])

    def chunk(start, next_start, live, offsets, live_next):
        prev = None
        for n in live:
            log_terms(n, offsets[chains[n][0]])
            cumulate(n)
            if prev is not None:
                weights(prev, offsets[chains[prev][0]])
                if prev in live_next:
                    logits(prev, next_start)
                accumulate(prev, start)
            prev = n
        weights(prev, offsets[chains[prev][0]])
        if prev in live_next:
            logits(prev, next_start)
        accumulate(prev, start)
        for n in live_next:
            if n not in live:
                logits(n, next_start)

    def visibility(c):
        live, offsets = [], {}
        for n, (h, _) in enumerate(chains):
            if c * kc >= (h + 1) * ts:
                continue
            live.append(n)
            offsets[h] = None if (c + 1) * kc <= h * ts else c * kc - h * ts
        return live, offsets

    everyone = list(range(len(chains)))
    no_offsets = {h: None for h in range(nslab)}
    live, offsets = visibility(nsub - 1)
    for n in live:
        logits(n, pl.multiple_of(i * tq + (nsub - 1) * kc, kc))
    for c in reversed(range(nsub)):
        start = pl.multiple_of(i * tq + c * kc, kc)
        live_next = visibility(c - 1)[0] if c > 0 else everyone
        chunk(start, pl.multiple_of(jnp.maximum(start - kc, 0), kc), live, offsets, live_next)
        if c > 0:
            live, offsets = visibility(c - 1)

    def alive(first_chain):
        return (jnp.max(r_ref[first_chain:]) > SB_DEAD_LOG2).astype(jnp.int32)

    trips = i * (nsub // SB_UNROLL)

    def walk(t0, live, first_chain):
        def body(state):
            t, _ = state
            for u in range(SB_UNROLL):
                start = pl.multiple_of(i * tq - (t * SB_UNROLL + u + 1) * kc, kc)
                chunk(start, pl.multiple_of(jnp.maximum(start - kc, 0), kc), live, no_offsets,
                      live)
            return t + 1, alive(first_chain)

        t_end, _ = lax.while_loop(lambda state: (state[0] < trips) & (state[1] > 0), body,
                                  (t0, alive(first_chain)))
        return t_end

    t_mid = walk(jnp.int32(0), everyone, len(chains) - 2)
    walk(t_mid, everyone[:2], 0)
    for h in range(nslab):
        o_ref[0, h * ts:(h + 1) * ts, :] = jnp.where(
            lane < HEAD_DIM, acc_ref[2 * h], acc_ref[2 * h + 1]).astype(BF16)


def _sb(y3d, tq=1024):
    b, s, _ = y3d.shape
    nch = 2 * (tq // SB_TS)
    return pl.pallas_call(
        functools.partial(_sb_kernel, tq=tq),
        grid=(b, N_PAIRS, s // tq),
        in_specs=[pl.BlockSpec((1, tq, LANES), lambda i, p, j: (i, j, C_SQ // LANES + p)),
                  pl.BlockSpec((1, s, LANES), lambda i, p, j: (i, 0, C_SK // LANES + p)),
                  pl.BlockSpec((1, s, LANES), lambda i, p, j: (i, 0, C_SV // LANES + p))],
        out_specs=pl.BlockSpec((1, tq, LANES), lambda i, p, j: (i, j, p)),
        out_shape=jax.ShapeDtypeStruct((b, s, D_MODEL), BF16),
        scratch_shapes=[pltpu.VMEM((nch, SB_TS, LANES), BF16),
                        pltpu.VMEM((nch, SB_TS, SB_KC), F32),
                        pltpu.VMEM((nch, SB_TS, SB_KC), BF16),
                        pltpu.VMEM((nch, SB_TS, SB_KC), F32),
                        pltpu.VMEM((nch, SB_TS, SB_KC), BF16),
                        pltpu.VMEM((nch, SB_TS, LANES), F32),
                        pltpu.VMEM((nch, SB_TS, LANES), F32),
                        pltpu.VMEM((nch, SB_TS, LANES), F32)],
        compiler_params=_cparams(3),
        name="sb",
    )(y3d, y3d, y3d)


LRU_TC = 256
LRU_HALO = 8


def _lru_kernel(x_ref, g_ref, cw_ref, cb_ref, wa_ref, ba_ref, wx_ref, bx_ref, lam_ref,
                o_ref, xbuf_ref, h_ref, *, ts):
    @pl.when(pl.program_id(2) == 0)
    def _():
        xbuf_ref[0:LRU_HALO, :] = jnp.zeros((LRU_HALO, LRU_TC), F32)
        h_ref[...] = jnp.zeros_like(h_ref)

    xbuf_ref[LRU_HALO:LRU_HALO + ts, :] = x_ref[0].astype(F32)
    cw = cw_ref[...]
    xc = cb_ref[...] + xbuf_ref[LRU_HALO:LRU_HALO + ts, :] * cw[3:4, :]
    for d in range(1, 4):
        xc = xc + xbuf_ref[LRU_HALO - d:LRU_HALO - d + ts, :] * cw[3 - d:4 - d, :]
    xbuf_ref[0:LRU_HALO, :] = xbuf_ref[ts:ts + LRU_HALO, :]

    xcb = xc.astype(BF16)
    r = jax.nn.sigmoid(_dot(xcb, wa_ref[0]) + ba_ref[...])
    gi = jax.nn.sigmoid(_dot(xcb, wx_ref[0]) + bx_ref[...])
    lam = lam_ref[...]
    softplus_neg_lam = jnp.maximum(-lam, 0.0) + _log1p_exp_neg_abs(lam)
    log_a = -LRU_C * r * softplus_neg_lam
    a = jnp.exp(log_a)
    one_m_a2 = -jnp.tanh(log_a) * (a * a + 1.0)
    u = one_m_a2 * lax.rsqrt(jnp.maximum(one_m_a2, F32_TINY)) * (gi * xc)

    row = lax.broadcasted_iota(jnp.int32, (ts, LRU_TC), 0) % SUBLANES
    d = 1
    while d < SUBLANES:
        keep = row >= d
        a_sh = jnp.where(keep, pltpu.roll(a, d, axis=0), 1.0)
        u_sh = jnp.where(keep, pltpu.roll(u, d, axis=0), 0.0)
        u = a * u_sh + u
        a = a * a_sh
        d *= 2
    state = h_ref[...]
    groups = []
    for r0 in range(0, ts, SUBLANES):
        h = u[r0:r0 + SUBLANES, :] + a[r0:r0 + SUBLANES, :] * state
        groups.append(h)
        state = h[SUBLANES - 1:SUBLANES, :]
    h_ref[...] = state
    h = jnp.concatenate(groups, axis=0)
    o_ref[0] = (h * jax.nn.gelu(g_ref[0].astype(F32))).astype(BF16)


def _lru(y3d, cw, cb, wa_bd, ba, wx_bd, bx, lam, ts=512):
    b, s, _ = y3d.shape
    nct = D_MODEL // LRU_TC
    vec = lambda rows: pl.BlockSpec((rows, LRU_TC), lambda i, c, j: (0, c))
    mat = pl.BlockSpec((1, LRU_TC, LRU_TC), lambda i, c, j: (c, 0, 0))
    return pl.pallas_call(
        functools.partial(_lru_kernel, ts=ts),
        grid=(b, nct, s // ts),
        in_specs=[pl.BlockSpec((1, ts, LRU_TC), lambda i, c, j: (i, j, C_LX // LRU_TC + c)),
                  pl.BlockSpec((1, ts, LRU_TC), lambda i, c, j: (i, j, C_LG // LRU_TC + c)),
                  vec(4), vec(1), mat, vec(1), mat, vec(1), vec(1)],
        out_specs=pl.BlockSpec((1, ts, LRU_TC), lambda i, c, j: (i, j, c)),
        out_shape=jax.ShapeDtypeStruct((b, s, D_MODEL), BF16),
        scratch_shapes=[pltpu.VMEM((ts + LRU_HALO, LRU_TC), F32),
                        pltpu.VMEM((1, LRU_TC), F32)],
        compiler_params=_cparams(3),
        name="lru",
    )(y3d, y3d, cw, cb, wa_bd, ba, wx_bd, bx, lam)


def _memkv_kernel(mem_ref, g_ref, w_ref, gk_ref, k_ref, v_ref):
    x = mem_ref[0]
    h = x * lax.rsqrt(jnp.mean(x * x, axis=-1, keepdims=True) + EPS) * g_ref[...]
    kv = _dot(h.astype(BF16), w_ref[...])
    for hd in range(MEM_HEADS):
        kh = kv[:, hd * MEM_HD:(hd + 1) * MEM_HD]
        kh = kh * lax.rsqrt(jnp.mean(kh * kh, axis=-1, keepdims=True) + EPS) * gk_ref[...]
        k_ref[0, :, hd * MEM_HD:(hd + 1) * MEM_HD] = kh.astype(BF16)
    v_ref[0] = kv[:, D_MODEL:].astype(BF16)


def _memkv(mem, g, w_kv, gk):
    b = mem.shape[0]
    out = jax.ShapeDtypeStruct((b, N_MEM, D_MODEL), BF16)
    return pl.pallas_call(
        _memkv_kernel,
        grid=(b,),
        in_specs=[pl.BlockSpec((1, N_MEM, D_MODEL), lambda i: (i, 0, 0)),
                  pl.BlockSpec((1, D_MODEL), lambda i: (0, 0)),
                  pl.BlockSpec((D_MODEL, 2 * D_MODEL), lambda i: (0, 0)),
                  pl.BlockSpec((1, MEM_HD), lambda i: (0, 0))],
        out_specs=[pl.BlockSpec((1, N_MEM, D_MODEL), lambda i: (i, 0, 0)),
                   pl.BlockSpec((1, N_MEM, D_MODEL), lambda i: (i, 0, 0))],
        out_shape=[out, out],
        compiler_params=_cparams(1),
        name="memkv",
    )(mem, g, w_kv, gk)


def _mem_kernel(q_ref, k_ref, v_ref, gq_ref, o_ref):
    for hd in range(MEM_HEADS):
        sl = slice(hd * MEM_HD, (hd + 1) * MEM_HD)
        q = q_ref[0, :, sl].astype(F32)
        q = q * lax.rsqrt(jnp.mean(q * q, axis=-1, keepdims=True) + EPS) * gq_ref[...]
        q = (q * (MEM_HD ** -0.5)).astype(BF16)
        s = _dot_nt(q, k_ref[0, :, sl])
        p = jnp.exp(s - jnp.max(s, axis=-1, keepdims=True))
        o = _dot(p.astype(BF16), v_ref[0, :, sl]) / jnp.sum(p, axis=-1, keepdims=True)
        o_ref[0, :, sl] = o.astype(BF16)


def _mem(y3d, mk, mv, gq, ts=512):
    b, s, _ = y3d.shape
    return pl.pallas_call(
        _mem_kernel,
        grid=(b, s // ts),
        in_specs=[pl.BlockSpec((1, ts, D_MODEL), lambda i, j: (i, j, C_MQ // D_MODEL)),
                  pl.BlockSpec((1, N_MEM, D_MODEL), lambda i, j: (i, 0, 0)),
                  pl.BlockSpec((1, N_MEM, D_MODEL), lambda i, j: (i, 0, 0)),
                  pl.BlockSpec((1, MEM_HD), lambda i, j: (0, 0))],
        out_specs=pl.BlockSpec((1, ts, D_MODEL), lambda i, j: (i, j, 0)),
        out_shape=jax.ShapeDtypeStruct((b, s, D_MODEL), BF16),
        compiler_params=_cparams(2),
        name="mem",
    )(y3d, mk, mv, gq)


def _merge_kernel(x_ref, b0_ref, b1_ref, b2_ref, b3_ref, gt_ref, bg_ref, wb_ref, wo_ref, o_ref):
    mixed = None
    for n, br in enumerate((b0_ref, b1_ref, b2_ref, b3_ref)):
        gate = jax.nn.sigmoid(
            gt_ref[:, n * D_MODEL:(n + 1) * D_MODEL].astype(F32) + bg_ref[n:n + 1, :])
        term = gate * _dot(br[...], wb_ref[n])
        mixed = term if mixed is None else mixed + term
    o_ref[...] = x_ref[...] + _dot(mixed.astype(BF16), wo_ref[...])


def _merge(x2d, branches, y2d, bg, wb, wo, tm=256):
    m = x2d.shape[0]
    row = lambda w: pl.BlockSpec((tm, w), lambda i: (i, 0))
    return pl.pallas_call(
        _merge_kernel,
        grid=(m // tm,),
        in_specs=[row(D_MODEL), row(D_MODEL), row(D_MODEL), row(D_MODEL), row(D_MODEL),
                  pl.BlockSpec((tm, 4 * D_MODEL), lambda i: (i, C_GT // (4 * D_MODEL))),
                  pl.BlockSpec((4, D_MODEL), lambda i: (0, 0)),
                  pl.BlockSpec((4, D_MODEL, D_MODEL), lambda i: (0, 0, 0)),
                  pl.BlockSpec((D_MODEL, D_MODEL), lambda i: (0, 0))],
        out_specs=row(D_MODEL),
        out_shape=jax.ShapeDtypeStruct((m, D_MODEL), F32),
        compiler_params=_cparams(1),
        name="merge",
    )(x2d, *branches, y2d, bg, wb, wo)


FFN_TC = 256
FFN_HALO = 8
FFN_RB = 64


def _ffn_kernel(x_ref, g_ref, wg_ref, wv_ref, cw_ref, cb_ref, wd_ref, o_ref,
                h_ref, gbuf_ref, vbuf_ref, act_ref, *, tm, tiles_per_seq):
    i = pl.program_id(0)
    x = x_ref[...]
    h_ref[...] = (x * lax.rsqrt(jnp.mean(x * x, axis=-1, keepdims=True) + EPS)
                  * g_ref[...]).astype(BF16)

    @pl.when((i % tiles_per_seq) == 0)
    def _():
        gbuf_ref[0:FFN_HALO, :] = jnp.zeros((FFN_HALO, D_FF), F32)

    def up(c):
        cols = slice(c * FFN_TC, (c + 1) * FFN_TC)
        gbuf_ref[FFN_HALO:FFN_HALO + tm, cols] = _dot(h_ref[...], wg_ref[:, cols])
        vbuf_ref[:, cols] = _dot(h_ref[...], wv_ref[:, cols])

    def activate(c):
        cols = slice(c * FFN_TC, (c + 1) * FFN_TC)
        cw = cw_ref[:, cols]
        cb = cb_ref[:, cols]
        for r0 in range(0, tm, FFN_RB):
            gate = cb
            for d in range(3):
                lo = FFN_HALO + r0 - d
                gate = gate + gbuf_ref[lo:lo + FFN_RB, cols] * cw[2 - d:3 - d, :]
            act = gate * jax.nn.sigmoid(gate) * vbuf_ref[r0:r0 + FFN_RB, cols]
            act_ref[r0:r0 + FFN_RB, cols] = act.astype(BF16)

    nc = D_FF // FFN_TC
    up(0)
    for c in range(1, nc):
        up(c)
        activate(c - 1)
    activate(nc - 1)
    o_ref[...] = x_ref[...] + _dot(act_ref[...], wd_ref[...])
    gbuf_ref[0:FFN_HALO, :] = gbuf_ref[tm:tm + FFN_HALO, :]


def _ffn(x2d, g, w_up, cw, cb, w_down, seq, tm=512):
    m = x2d.shape[0]
    resident = pl.Buffered(1)
    return pl.pallas_call(
        functools.partial(_ffn_kernel, tm=tm, tiles_per_seq=seq // tm),
        grid=(m // tm,),
        in_specs=[pl.BlockSpec((tm, D_MODEL), lambda i: (i, 0)),
                  pl.BlockSpec((1, D_MODEL), lambda i: (0, 0)),
                  pl.BlockSpec((D_MODEL, D_FF), lambda i: (0, 0), pipeline_mode=resident),
                  pl.BlockSpec((D_MODEL, D_FF), lambda i: (0, 1), pipeline_mode=resident),
                  pl.BlockSpec((3, D_FF), lambda i: (0, 0)),
                  pl.BlockSpec((1, D_FF), lambda i: (0, 0)),
                  pl.BlockSpec((D_FF, D_MODEL), lambda i: (0, 0), pipeline_mode=resident)],
        out_specs=pl.BlockSpec((tm, D_MODEL), lambda i: (i, 0)),
        out_shape=jax.ShapeDtypeStruct((m, D_MODEL), F32),
        scratch_shapes=[pltpu.VMEM((tm, D_MODEL), BF16),
                        pltpu.VMEM((tm + FFN_HALO, D_FF), F32),
                        pltpu.VMEM((tm, D_FF), F32),
                        pltpu.VMEM((tm, D_FF), BF16)],
        compiler_params=_cparams(1),
        name="ffn",
    )(x2d, g, w_up, w_up, cw, cb, w_down)


def _block_diag(w):
    per = LRU_TC // HEAD_DIM
    w4 = w.reshape(D_MODEL // LRU_TC, per, HEAD_DIM, HEAD_DIM)
    eye = jnp.eye(per, dtype=w.dtype)
    bd = jnp.einsum('cpde,pq->cpdqe', w4, eye)
    return bd.reshape(D_MODEL // LRU_TC, LRU_TC, LRU_TC)


def kernel(x, mem, attn_norm_g, mem_norm_g, w_in, b_forget, fox_q_norm_g, fox_k_norm_g,
           lru_conv_w, lru_conv_b, lru_w_a, lru_b_a, lru_w_x, lru_b_x, lru_lambda,
           w_mem_kv, mem_q_norm_g, mem_k_norm_g, b_gate, w_branch, w_out,
           ffn_norm_g, w_up, ffn_conv_w, ffn_conv_b, w_down):
    b, s, d = x.shape
    depth = w_in.shape[0]
    m = b * s
    x2d = x.reshape(m, d)
    for l in range(depth):
        wy = jnp.concatenate([w_in[l, :, W_GATES:], w_in[l, :, :W_FOX_END],
                              w_in[l, :, W_F_END:W_GATES]], axis=1)
        sb_q_scale = jnp.where((jnp.arange(N_Y) >= C_SQ) & (jnp.arange(N_Y) < C_SK),
                               HEAD_DIM ** -0.5 * LOG2E, 1.0).astype(F32)
        wy = (wy * sb_q_scale).astype(BF16)
        wf = jnp.pad(w_in[l, :, W_FOX_END:W_F_END], ((0, 0), (0, LANES - N_HEADS)))
        bf = jnp.pad(b_forget[l], (0, LANES - N_HEADS)).reshape(1, LANES)

        y2d, f2d = _proj(x2d, attn_norm_g[l].reshape(1, d), wy, wf)
        y3d = y2d.reshape(b, s, N_Y)

        fcum = _fcum(f2d.reshape(b, s, LANES), bf)
        qa, ka, vt = _foxprep(y3d, fcum,
                              jnp.tile(fox_q_norm_g[l], 2).reshape(1, LANES),
                              jnp.tile(fox_k_norm_g[l], 2).reshape(1, LANES))
        y_fox = _fox(qa, ka, vt)
        y_sb = _sb(y3d)
        y_lru = _lru(y3d, lru_conv_w[l], lru_conv_b[l].reshape(1, d),
                     _block_diag(lru_w_a[l]).astype(BF16), lru_b_a[l].reshape(1, d),
                     _block_diag(lru_w_x[l]).astype(BF16), lru_b_x[l].reshape(1, d),
                     lru_lambda[l].reshape(1, d))
        mk, mv = _memkv(mem, mem_norm_g[l].reshape(1, d), w_mem_kv[l].astype(BF16),
                        mem_k_norm_g[l].reshape(1, MEM_HD))
        y_mem = _mem(y3d, mk, mv, mem_q_norm_g[l].reshape(1, MEM_HD))

        branches = [t.reshape(m, d) for t in (y_fox, y_lru, y_sb, y_mem)]
        x2d = _merge(x2d, branches, y2d, b_gate[l], w_branch[l].astype(BF16),
                     w_out[l].astype(BF16))
        x2d = _ffn(x2d, ffn_norm_g[l].reshape(1, d), w_up[l].astype(BF16), ffn_conv_w[l],
                   ffn_conv_b[l].reshape(1, D_FF), w_down[l].astype(BF16), s)
    return x2d.reshape(b, s, d)
```

```python
import functools

import jax
import jax.numpy as jnp
from jax import lax
from jax.experimental import pallas as pl
from jax.experimental.pallas import tpu as pltpu

F32 = jnp.float32
BF16 = jnp.bfloat16
HIGHEST = lax.Precision.HIGHEST

D_MODEL = 1024
HEAD_DIM = 64
N_HEADS = 16
N_PAIRS = N_HEADS // 2
LANES = 128
SUBLANES = 8
N_MEM = 256
MEM_HEADS = 4
MEM_HD = 256
D_FF = 2816
LRU_C = 8.0
EPS = 1e-6
NEG = -1e30
LOG2E = 1.4426950408889634
F32_TINY = 1.1754944e-38

C_GT, C_FQ, C_FK, C_FV, C_LX, C_LG, C_SQ, C_SK, C_SV, C_MQ = (
    0, 4096, 5120, 6144, 7168, 8192, 9216, 10240, 11264, 12288)
N_Y = 13312
W_FOX_END, W_F_END, W_GATES = 3072, 3088, 9232

VMEM_LIMIT = 56 * 1024 * 1024


def _cparams(n_axes):
    return pltpu.CompilerParams(dimension_semantics=("arbitrary",) * n_axes,
                                vmem_limit_bytes=VMEM_LIMIT)


def _dot(a, b):
    return jnp.dot(a, b, preferred_element_type=F32)


def _dot_nt(a, b):
    return lax.dot_general(a, b, (((1,), (1,)), ((), ())), preferred_element_type=F32)


def _log1p_exp_neg_abs(z):
    return jnp.log(1.0 + jnp.exp(-jnp.abs(z)))


def _proj_kernel(x_ref, g_ref, w_ref, wf_ref, y_ref, f_ref, h_ref):
    @pl.when(pl.program_id(1) == 0)
    def _():
        x = x_ref[...]
        h = x * lax.rsqrt(jnp.mean(x * x, axis=-1, keepdims=True) + EPS) * g_ref[...]
        h_ref[...] = h.astype(BF16)
        f_ref[...] = jnp.dot(h, wf_ref[...], precision=HIGHEST, preferred_element_type=F32)

    y_ref[...] = _dot(h_ref[...], w_ref[...]).astype(BF16)


def _proj(x2d, g, wy, wf, tm=1024, tn=1024):
    m = x2d.shape[0]
    return pl.pallas_call(
        _proj_kernel,
        grid=(m // tm, N_Y // tn),
        in_specs=[pl.BlockSpec((tm, D_MODEL), lambda i, j: (i, 0)),
                  pl.BlockSpec((1, D_MODEL), lambda i, j: (0, 0)),
                  pl.BlockSpec((D_MODEL, tn), lambda i, j: (0, j)),
                  pl.BlockSpec((D_MODEL, LANES), lambda i, j: (0, 0))],
        out_specs=[pl.BlockSpec((tm, tn), lambda i, j: (i, j)),
                   pl.BlockSpec((tm, LANES), lambda i, j: (i, 0))],
        out_shape=[jax.ShapeDtypeStruct((m, N_Y), BF16),
                   jax.ShapeDtypeStruct((m, LANES), F32)],
        scratch_shapes=[pltpu.VMEM((tm, D_MODEL), BF16)],
        compiler_params=_cparams(2),
        name="proj",
    )(x2d, g, wy, wf)


def _fcum_kernel(f_ref, b_ref, o_ref, carry_ref):
    @pl.when(pl.program_id(1) == 0)
    def _():
        carry_ref[...] = jnp.zeros_like(carry_ref)

    z = f_ref[0] + b_ref[...]
    log_f = jnp.minimum(z, 0.0) - _log1p_exp_neg_abs(z)
    ts = z.shape[0]
    row = lax.broadcasted_iota(jnp.int32, (ts, ts), 0)
    col = lax.broadcasted_iota(jnp.int32, (ts, ts), 1)
    tri = (row >= col).astype(F32)
    c = jnp.dot(tri, log_f, precision=HIGHEST, preferred_element_type=F32) + carry_ref[...]
    carry_ref[...] = c[ts - 1:ts, :]

    lane = lax.broadcasted_iota(jnp.int32, (ts, LANES), 1)
    f = jnp.where(lane < N_HEADS, c * LOG2E, 0.0)
    hi = f.astype(BF16).astype(F32)
    r1 = f - hi
    mid = r1.astype(BF16).astype(F32)
    lo = (r1 - mid).astype(BF16).astype(F32)
    packed = hi + pltpu.roll(mid, N_HEADS, axis=1) + pltpu.roll(lo, 2 * N_HEADS, axis=1)
    o_ref[0] = packed.astype(BF16)


def _fcum(f3d, b_pad, ts=512):
    b, s, _ = f3d.shape
    return pl.pallas_call(
        _fcum_kernel,
        grid=(b, s // ts),
        in_specs=[pl.BlockSpec((1, ts, LANES), lambda i, j: (i, j, 0)),
                  pl.BlockSpec((1, LANES), lambda i, j: (0, 0))],
        out_specs=pl.BlockSpec((1, ts, LANES), lambda i, j: (i, j, 0)),
        out_shape=jax.ShapeDtypeStruct((b, s, LANES), BF16),
        scratch_shapes=[pltpu.VMEM((1, LANES), F32)],
        compiler_params=_cparams(2),
        name="fcum",
    )(f3d, b_pad)


def _foxprep_kernel(yq_ref, yk_ref, yv_ref, f_ref, gq_ref, gk_ref, qa_ref, ka_ref, vt_ref):
    hp = pl.program_id(1)
    ones = jnp.ones((FOX_ONES, FOX_TS), BF16)
    for c in range(vt_ref.shape[2]):
        vt = yv_ref[0, c * FOX_TS:(c + 1) * FOX_TS, :].astype(F32).T.astype(BF16)
        vt_ref[0, 0, c] = jnp.concatenate(
            [vt[:HEAD_DIM], ones, vt[HEAD_DIM:], ones], axis=0)
    ts = yq_ref.shape[1]
    lane = lax.broadcasted_iota(jnp.int32, (ts, LANES), 1)
    lo_half = lane < HEAD_DIM

    hr = lax.broadcasted_iota(jnp.int32, (LANES, LANES), 0)
    hc = lax.broadcasted_iota(jnp.int32, (LANES, LANES), 1)
    same_head = ((hr < HEAD_DIM) == (hc < HEAD_DIM)).astype(BF16)

    def headnorm(y, g):
        ms = _dot((y * y).astype(BF16), same_head) * (1.0 / HEAD_DIM)
        return y * lax.rsqrt(ms + EPS) * g

    qn = headnorm(yq_ref[0].astype(F32), gq_ref[...]) * (HEAD_DIM ** -0.5 * LOG2E)
    kn = headnorm(yk_ref[0].astype(F32), gk_ref[...])

    pieces = f_ref[0]
    sel_r = lax.broadcasted_iota(jnp.int32, (LANES, 2 * LANES), 0)
    sel_c = lax.broadcasted_iota(jnp.int32, (LANES, 2 * LANES), 1)
    ones_q = ((lane >= HEAD_DIM + 3) & (lane < HEAD_DIM + 6)).astype(F32)
    ones_k = ((lane >= HEAD_DIM) & (lane < HEAD_DIM + 3)).astype(F32)
    for e in range(2):
        head = 2 * hp + e
        piece_of_row = sel_r - head
        to_q = (piece_of_row == (sel_c - HEAD_DIM) * N_HEADS) & (sel_c < HEAD_DIM + 3)
        to_k = (piece_of_row == (sel_c - (LANES + HEAD_DIM + 3)) * N_HEADS) & (
            sel_c >= LANES + HEAD_DIM + 3)
        valid = (piece_of_row >= 0) & (piece_of_row <= 2 * N_HEADS)
        sel = (jnp.where(valid & to_q, 1.0, 0.0) - jnp.where(valid & to_k, 1.0, 0.0)).astype(BF16)
        placed = _dot(pieces, sel)
        q_part = qn if e == 0 else pltpu.roll(qn, HEAD_DIM, axis=1)
        k_part = kn if e == 0 else pltpu.roll(kn, HEAD_DIM, axis=1)
        q_aug = jnp.where(lo_half, q_part, placed[:, :LANES] + ones_q)
        k_aug = jnp.where(lo_half, k_part, placed[:, LANES:] + ones_k)
        qa_ref[0, e] = q_aug.astype(BF16)
        ka_ref[0, e] = k_aug.astype(BF16)


def _foxprep(y3d, fcum, gq2, gk2, ts=2048):
    b, s, _ = y3d.shape
    out = jax.ShapeDtypeStruct((b, N_HEADS, s, LANES), BF16)
    vt = jax.ShapeDtypeStruct((b, N_PAIRS, s // FOX_TS, 2 * FOX_VROWS, FOX_TS), BF16)
    return pl.pallas_call(
        _foxprep_kernel,
        grid=(b, N_PAIRS, s // ts),
        in_specs=[pl.BlockSpec((1, ts, LANES), lambda i, p, j: (i, j, C_FQ // LANES + p)),
                  pl.BlockSpec((1, ts, LANES), lambda i, p, j: (i, j, C_FK // LANES + p)),
                  pl.BlockSpec((1, ts, LANES), lambda i, p, j: (i, j, C_FV // LANES + p)),
                  pl.BlockSpec((1, ts, LANES), lambda i, p, j: (i, j, 0)),
                  pl.BlockSpec((1, LANES), lambda i, p, j: (0, 0)),
                  pl.BlockSpec((1, LANES), lambda i, p, j: (0, 0))],
        out_specs=[pl.BlockSpec((1, 2, ts, LANES), lambda i, p, j: (i, p, j, 0)),
                   pl.BlockSpec((1, 2, ts, LANES), lambda i, p, j: (i, p, j, 0)),
                   pl.BlockSpec((1, 1, ts // FOX_TS, 2 * FOX_VROWS, FOX_TS),
                                lambda i, p, j: (i, p, j, 0, 0))],
        out_shape=[out, out, vt],
        compiler_params=_cparams(3),
        name="foxprep",
    )(y3d, y3d, y3d, fcum, gq2, gk2)


FOX_TS = 512
FOX_KB = 64
FOX_ONES = 16
FOX_VROWS = HEAD_DIM + FOX_ONES


def _fox_kernel(qa_ref, ka_ref, vt_ref, o_ref, s_ref, p_ref, m_ref, a_ref, acc_ref, *, tq):
    i = pl.program_id(2)
    ts = FOX_TS
    nslab = tq // ts
    chains = [(h, e) for h in range(nslab) for e in range(2)]
    key = lax.broadcasted_iota(jnp.int32, (FOX_KB, LANES), 0)
    query = lax.broadcasted_iota(jnp.int32, (FOX_KB, LANES), 1)

    m_ref[...] = jnp.full(m_ref.shape, NEG, F32)
    acc_ref[...] = jnp.zeros(acc_ref.shape, F32)

    def logits(n, block):
        h, e = chains[n]
        start = pl.multiple_of(block * ts, ts)
        s = _dot_nt(ka_ref[0, e, pl.ds(start, ts), :], qa_ref[0, e, h * ts:(h + 1) * ts, :])
        for c in range(ts // LANES):
            s_ref[n, c, 0:ts, :] = s[:, c * LANES:(c + 1) * LANES]

    def softmax(n, diagonal):
        kb = FOX_KB
        strips = range(0, ts, LANES)

        def piece(c0, k0):
            s = s_ref[n, c0 // LANES, k0:k0 + kb, :]
            if diagonal and k0 + kb > c0:
                s = jnp.where(key + k0 <= query + c0, s, NEG)
            return s

        def last_key(c0):
            return c0 + LANES if diagonal else ts

        m_new = {}
        for c0 in strips:
            cols = slice(c0, c0 + LANES)
            top = piece(c0, 0)
            for k0 in range(kb, last_key(c0), kb):
                top = jnp.maximum(top, piece(c0, k0))
            m_old = m_ref[n, :, cols]
            m_new[c0] = jnp.maximum(m_old, jnp.max(top, axis=0, keepdims=True))
            m_ref[n, :, cols] = m_new[c0]
            a_ref[n, :, cols] = jnp.exp2(m_old - m_new[c0])
        for c0 in strips:
            for k0 in range(0, last_key(c0), kb):
                p_ref[n, c0 // LANES, k0:k0 + kb, :] = jnp.exp2(
                    (piece(c0, k0) - m_new[c0][0:1, :]).astype(BF16))
            if last_key(c0) < ts:
                p_ref[n, c0 // LANES, last_key(c0):ts, :] = jnp.zeros(
                    (ts - last_key(c0), LANES), BF16)

    def accumulate(n, block):
        e = chains[n][1]
        vt = vt_ref[0, 0, block, e * FOX_VROWS:(e + 1) * FOX_VROWS, :]
        p = jnp.concatenate([p_ref[n, c, 0:ts, :] for c in range(ts // LANES)], axis=1)
        acc_ref[n] = a_ref[n, 0:1, :] * acc_ref[n] + _dot(vt, p)

    def step(block, live, diagonal_slab, live_next):
        prev = None
        for n in live:
            softmax(n, chains[n][0] == diagonal_slab)
            if n in live_next:
                logits(n, block + 1)
            if prev is not None:
                accumulate(prev, block)
            prev = n
        accumulate(prev, block)

    everyone = list(range(len(chains)))
    for n in everyone:
        logits(n, 0)

    def body(j, _):
        step(j, everyone, None, everyone)
        return 0

    lax.fori_loop(0, i * nslab, body, 0)
    for c in range(nslab):
        live = [n for n in everyone if chains[n][0] >= c]
        live_next = [n for n in everyone if chains[n][0] >= c + 1] if c + 1 < nslab else []
        step(i * nslab + c, live, c, live_next)

    for h in range(nslab):
        outs = [acc_ref[2 * h + e, 0:HEAD_DIM, :] / acc_ref[2 * h + e, HEAD_DIM:HEAD_DIM + 1, :]
                for e in range(2)]
        o_ref[0, h * ts:(h + 1) * ts, :] = jnp.concatenate(outs, axis=0).T.astype(BF16)


def _fox(qa, ka, vt, tq=2048):
    b, _, s, _ = qa.shape
    nch = 2 * (tq // FOX_TS)
    return pl.pallas_call(
        functools.partial(_fox_kernel, tq=tq),
        grid=(b, N_PAIRS, s // tq),
        in_specs=[pl.BlockSpec((1, 2, tq, LANES), lambda i, p, j: (i, p, j, 0)),
                  pl.BlockSpec((1, 2, s, LANES), lambda i, p, j: (i, p, 0, 0)),
                  pl.BlockSpec((1, 1, s // FOX_TS, 2 * FOX_VROWS, FOX_TS),
                               lambda i, p, j: (i, p, 0, 0, 0))],
        out_specs=pl.BlockSpec((1, tq, LANES), lambda i, p, j: (i, j, p)),
        out_shape=jax.ShapeDtypeStruct((b, s, D_MODEL), BF16),
        scratch_shapes=[pltpu.VMEM((nch, FOX_TS // LANES, FOX_TS + SUBLANES, LANES), F32),
                        pltpu.VMEM((nch, FOX_TS // LANES, FOX_TS + 2 * SUBLANES, LANES), BF16),
                        pltpu.VMEM((nch, SUBLANES, FOX_TS), F32),
                        pltpu.VMEM((nch, SUBLANES, FOX_TS), F32),
                        pltpu.VMEM((nch, FOX_VROWS, FOX_TS), F32)],
        compiler_params=_cparams(3),
        name="fox",
    )(qa, ka, vt)


SB_KC = 256
SB_TS = 512
SB_MAX_LOG2 = 126.0
SB_DEAD_LOG2 = -(SB_MAX_LOG2 + 150.0)


SB_RB = 64
SB_UNROLL = 2


def _sb_kernel(q_ref, k_ref, v_ref, o_ref, qm_ref, z_ref, lb_ref, ic_ref, w_ref,
               rs_ref, r_ref, acc_ref, *, tq):
    i = pl.program_id(2)
    kc, ts, rb = SB_KC, SB_TS, SB_RB
    nslab = tq // ts
    chains = [(h, e) for h in range(nslab) for e in range(2)]
    row = lax.broadcasted_iota(jnp.int32, (rb, kc), 0)
    col = lax.broadcasted_iota(jnp.int32, (rb, kc), 1)
    kr = lax.broadcasted_iota(jnp.int32, (kc, kc), 0)
    kcol = lax.broadcasted_iota(jnp.int32, (kc, kc), 1)
    at_or_after = (kr >= kcol).astype(BF16)

    lane = lax.broadcasted_iota(jnp.int32, (ts, LANES), 1)
    for n, (h, e) in enumerate(chains):
        q2 = q_ref[0, h * ts:(h + 1) * ts, :]
        mine = (lane < HEAD_DIM) if e == 0 else (lane >= HEAD_DIM)
        qm_ref[n] = jnp.where(mine, q2, jnp.zeros_like(q2))
    r_ref[...] = jnp.zeros(r_ref.shape, F32)
    acc_ref[...] = jnp.zeros(acc_ref.shape, F32)

    def logits(n, start):
        z_ref[n] = jnp.minimum(_dot_nt(qm_ref[n], k_ref[0, pl.ds(start, kc), :]), SB_MAX_LOG2)

    def log_terms(n, offset):
        for r0 in range(0, ts, rb):
            rows = slice(r0, r0 + rb)
            log_1m_beta = jnp.log(1.0 + jnp.exp2(z_ref[n, rows, :])) * (-LOG2E)
            if offset is not None:
                log_1m_beta = jnp.where(col + (offset - r0) < row, log_1m_beta, 0.0)
            lb_ref[n, rows, :] = log_1m_beta.astype(BF16)
            rs_ref[n, rows, :] = jnp.broadcast_to(
                jnp.sum(log_1m_beta, axis=-1, keepdims=True), (rb, LANES))

    def cumulate(n):
        ic_ref[n] = _dot(lb_ref[n], at_or_after)

    def weights(n, offset):
        for r0 in range(0, ts, rb):
            rows = slice(r0, r0 + rb)
            r = r_ref[n, rows, :]
            log_w = (z_ref[n, rows, :] + ic_ref[n, rows, :]
                     + jnp.concatenate([r] * (kc // LANES), axis=1))
            if offset is not None:
                log_w = jnp.where(col + (offset - r0) < row, log_w, NEG)
            w_ref[n, rows, :] = jnp.exp2(log_w.astype(BF16))
            r_ref[n, rows, :] = r + rs_ref[n, rows, :]

    def accumulate(n, start):
        acc_ref[n] += _dot(w_ref[n], v_ref[0, pl.ds(start, kc), :])

    def chunk(starts, live, offset):
        def finish(n):
            start = starts[chains[n][0]]
            weights(n, offset)
            logits(n, pl.multiple_of(jnp.maximum(start - kc, 0), kc))
            accumulate(n, start)

        prev = None
        for n in live:
            log_terms(n, offset)
            cumulate(n)
            if prev is not None:
                finish(prev)
            prev = n
        finish(prev)

    everyone = list(range(len(chains)))
    nown = ts // kc
    base = [i * tq + h * ts for h in range(nslab)]
    for n in everyone:
        logits(n, pl.multiple_of(base[chains[n][0]] + (nown - 1) * kc, kc))
    for c in reversed(range(nown)):
        chunk({h: pl.multiple_of(base[h] + c * kc, kc) for h in range(nslab)}, everyone, c * kc)

    def alive(h):
        return (jnp.max(r_ref[2 * h:2 * h + 2]) > SB_DEAD_LOG2).astype(jnp.int32)

    def walk(t0, slabs):
        live = [n for n in everyone if chains[n][0] in slabs]
        per_trip = kc * SB_UNROLL
        trips = i * (tq // per_trip) + (slabs[0] * ts) // per_trip

        def all_alive():
            flag = alive(slabs[0])
            for h in slabs[1:]:
                flag = flag * alive(h)
            return flag

        def body(state):
            t, _ = state
            for u in range(SB_UNROLL):
                below = (t * SB_UNROLL + u + 1) * kc
                chunk({h: pl.multiple_of(base[h] - below, kc) for h in slabs}, live, None)
            return t + 1, all_alive()

        t_end, _ = lax.while_loop(lambda state: (state[0] < trips) & (state[1] > 0), body,
                                  (t0, all_alive()))
        return t_end

    t_all = walk(jnp.int32(0), list(range(nslab)))
    for h in range(nslab):
        walk(t_all, [h])
    for h in range(nslab):
        o_ref[0, h * ts:(h + 1) * ts, :] = jnp.where(
            lane < HEAD_DIM, acc_ref[2 * h], acc_ref[2 * h + 1]).astype(BF16)


def _sb(y3d, tq=1024):
    b, s, _ = y3d.shape
    nch = 2 * (tq // SB_TS)
    return pl.pallas_call(
        functools.partial(_sb_kernel, tq=tq),
        grid=(b, N_PAIRS, s // tq),
        in_specs=[pl.BlockSpec((1, tq, LANES), lambda i, p, j: (i, j, C_SQ // LANES + p)),
                  pl.BlockSpec((1, s, LANES), lambda i, p, j: (i, 0, C_SK // LANES + p)),
                  pl.BlockSpec((1, s, LANES), lambda i, p, j: (i, 0, C_SV // LANES + p))],
        out_specs=pl.BlockSpec((1, tq, LANES), lambda i, p, j: (i, j, p)),
        out_shape=jax.ShapeDtypeStruct((b, s, D_MODEL), BF16),
        scratch_shapes=[pltpu.VMEM((nch, SB_TS, LANES), BF16),
                        pltpu.VMEM((nch, SB_TS, SB_KC), F32),
                        pltpu.VMEM((nch, SB_TS, SB_KC), BF16),
                        pltpu.VMEM((nch, SB_TS, SB_KC), F32),
                        pltpu.VMEM((nch, SB_TS, SB_KC), BF16),
                        pltpu.VMEM((nch, SB_TS, LANES), F32),
                        pltpu.VMEM((nch, SB_TS, LANES), F32),
                        pltpu.VMEM((nch, SB_TS, LANES), F32)],
        compiler_params=_cparams(3),
        name="sb",
    )(y3d, y3d, y3d)


LRU_TC = 256
LRU_HALO = 8


def _lru_kernel(x_ref, g_ref, cw_ref, cb_ref, wa_ref, ba_ref, wx_ref, bx_ref, lam_ref,
                o_ref, xbuf_ref, h_ref, *, ts):
    @pl.when(pl.program_id(2) == 0)
    def _():
        xbuf_ref[0:LRU_HALO, :] = jnp.zeros((LRU_HALO, LRU_TC), F32)
        h_ref[...] = jnp.zeros_like(h_ref)

    xbuf_ref[LRU_HALO:LRU_HALO + ts, :] = x_ref[0].astype(F32)
    cw = cw_ref[...]
    xc = cb_ref[...] + xbuf_ref[LRU_HALO:LRU_HALO + ts, :] * cw[3:4, :]
    for d in range(1, 4):
        xc = xc + xbuf_ref[LRU_HALO - d:LRU_HALO - d + ts, :] * cw[3 - d:4 - d, :]
    xbuf_ref[0:LRU_HALO, :] = xbuf_ref[ts:ts + LRU_HALO, :]

    xcb = xc.astype(BF16)
    r = jax.nn.sigmoid(_dot(xcb, wa_ref[0]) + ba_ref[...])
    gi = jax.nn.sigmoid(_dot(xcb, wx_ref[0]) + bx_ref[...])
    lam = lam_ref[...]
    softplus_neg_lam = jnp.maximum(-lam, 0.0) + _log1p_exp_neg_abs(lam)
    log_a = -LRU_C * r * softplus_neg_lam
    a = jnp.exp(log_a)
    one_m_a2 = -jnp.tanh(log_a) * (a * a + 1.0)
    u = one_m_a2 * lax.rsqrt(jnp.maximum(one_m_a2, F32_TINY)) * (gi * xc)

    row = lax.broadcasted_iota(jnp.int32, (ts, LRU_TC), 0) % SUBLANES
    d = 1
    while d < SUBLANES:
        keep = row >= d
        a_sh = jnp.where(keep, pltpu.roll(a, d, axis=0), 1.0)
        u_sh = jnp.where(keep, pltpu.roll(u, d, axis=0), 0.0)
        u = a * u_sh + u
        a = a * a_sh
        d *= 2
    state = h_ref[...]
    groups = []
    for r0 in range(0, ts, SUBLANES):
        h = u[r0:r0 + SUBLANES, :] + a[r0:r0 + SUBLANES, :] * state
        groups.append(h)
        state = h[SUBLANES - 1:SUBLANES, :]
    h_ref[...] = state
    h = jnp.concatenate(groups, axis=0)
    o_ref[0] = (h * jax.nn.gelu(g_ref[0].astype(F32))).astype(BF16)


def _lru(y3d, cw, cb, wa_bd, ba, wx_bd, bx, lam, ts=512):
    b, s, _ = y3d.shape
    nct = D_MODEL // LRU_TC
    vec = lambda rows: pl.BlockSpec((rows, LRU_TC), lambda i, c, j: (0, c))
    mat = pl.BlockSpec((1, LRU_TC, LRU_TC), lambda i, c, j: (c, 0, 0))
    return pl.pallas_call(
        functools.partial(_lru_kernel, ts=ts),
        grid=(b, nct, s // ts),
        in_specs=[pl.BlockSpec((1, ts, LRU_TC), lambda i, c, j: (i, j, C_LX // LRU_TC + c)),
                  pl.BlockSpec((1, ts, LRU_TC), lambda i, c, j: (i, j, C_LG // LRU_TC + c)),
                  vec(4), vec(1), mat, vec(1), mat, vec(1), vec(1)],
        out_specs=pl.BlockSpec((1, ts, LRU_TC), lambda i, c, j: (i, j, c)),
        out_shape=jax.ShapeDtypeStruct((b, s, D_MODEL), BF16),
        scratch_shapes=[pltpu.VMEM((ts + LRU_HALO, LRU_TC), F32),
                        pltpu.VMEM((1, LRU_TC), F32)],
        compiler_params=_cparams(3),
        name="lru",
    )(y3d, y3d, cw, cb, wa_bd, ba, wx_bd, bx, lam)


def _memkv_kernel(mem_ref, g_ref, w_ref, gk_ref, k_ref, v_ref):
    x = mem_ref[0]
    h = x * lax.rsqrt(jnp.mean(x * x, axis=-1, keepdims=True) + EPS) * g_ref[...]
    kv = _dot(h.astype(BF16), w_ref[...])
    for hd in range(MEM_HEADS):
        kh = kv[:, hd * MEM_HD:(hd + 1) * MEM_HD]
        kh = kh * lax.rsqrt(jnp.mean(kh * kh, axis=-1, keepdims=True) + EPS) * gk_ref[...]
        k_ref[0, :, hd * MEM_HD:(hd + 1) * MEM_HD] = kh.astype(BF16)
    v_ref[0] = kv[:, D_MODEL:].astype(BF16)


def _memkv(mem, g, w_kv, gk):
    b = mem.shape[0]
    out = jax.ShapeDtypeStruct((b, N_MEM, D_MODEL), BF16)
    return pl.pallas_call(
        _memkv_kernel,
        grid=(b,),
        in_specs=[pl.BlockSpec((1, N_MEM, D_MODEL), lambda i: (i, 0, 0)),
                  pl.BlockSpec((1, D_MODEL), lambda i: (0, 0)),
                  pl.BlockSpec((D_MODEL, 2 * D_MODEL), lambda i: (0, 0)),
                  pl.BlockSpec((1, MEM_HD), lambda i: (0, 0))],
        out_specs=[pl.BlockSpec((1, N_MEM, D_MODEL), lambda i: (i, 0, 0)),
                   pl.BlockSpec((1, N_MEM, D_MODEL), lambda i: (i, 0, 0))],
        out_shape=[out, out],
        compiler_params=_cparams(1),
        name="memkv",
    )(mem, g, w_kv, gk)


def _mem_kernel(q_ref, k_ref, v_ref, gq_ref, o_ref):
    for hd in range(MEM_HEADS):
        sl = slice(hd * MEM_HD, (hd + 1) * MEM_HD)
        q = q_ref[0, :, sl].astype(F32)
        q = q * lax.rsqrt(jnp.mean(q * q, axis=-1, keepdims=True) + EPS) * gq_ref[...]
        q = (q * (MEM_HD ** -0.5)).astype(BF16)
        s = _dot_nt(q, k_ref[0, :, sl])
        p = jnp.exp(s - jnp.max(s, axis=-1, keepdims=True))
        o = _dot(p.astype(BF16), v_ref[0, :, sl]) / jnp.sum(p, axis=-1, keepdims=True)
        o_ref[0, :, sl] = o.astype(BF16)


def _mem(y3d, mk, mv, gq, ts=512):
    b, s, _ = y3d.shape
    return pl.pallas_call(
        _mem_kernel,
        grid=(b, s // ts),
        in_specs=[pl.BlockSpec((1, ts, D_MODEL), lambda i, j: (i, j, C_MQ // D_MODEL)),
                  pl.BlockSpec((1, N_MEM, D_MODEL), lambda i, j: (i, 0, 0)),
                  pl.BlockSpec((1, N_MEM, D_MODEL), lambda i, j: (i, 0, 0)),
                  pl.BlockSpec((1, MEM_HD), lambda i, j: (0, 0))],
        out_specs=pl.BlockSpec((1, ts, D_MODEL), lambda i, j: (i, j, 0)),
        out_shape=jax.ShapeDtypeStruct((b, s, D_MODEL), BF16),
        compiler_params=_cparams(2),
        name="mem",
    )(y3d, mk, mv, gq)


def _merge_kernel(x_ref, b0_ref, b1_ref, b2_ref, b3_ref, gt_ref, bg_ref, wb_ref, wo_ref, o_ref):
    mixed = None
    for n, br in enumerate((b0_ref, b1_ref, b2_ref, b3_ref)):
        gate = jax.nn.sigmoid(
            gt_ref[:, n * D_MODEL:(n + 1) * D_MODEL].astype(F32) + bg_ref[n:n + 1, :])
        term = gate * _dot(br[...], wb_ref[n])
        mixed = term if mixed is None else mixed + term
    o_ref[...] = x_ref[...] + _dot(mixed.astype(BF16), wo_ref[...])


def _merge(x2d, branches, y2d, bg, wb, wo, tm=256):
    m = x2d.shape[0]
    row = lambda w: pl.BlockSpec((tm, w), lambda i: (i, 0))
    return pl.pallas_call(
        _merge_kernel,
        grid=(m // tm,),
        in_specs=[row(D_MODEL), row(D_MODEL), row(D_MODEL), row(D_MODEL), row(D_MODEL),
                  pl.BlockSpec((tm, 4 * D_MODEL), lambda i: (i, C_GT // (4 * D_MODEL))),
                  pl.BlockSpec((4, D_MODEL), lambda i: (0, 0)),
                  pl.BlockSpec((4, D_MODEL, D_MODEL), lambda i: (0, 0, 0)),
                  pl.BlockSpec((D_MODEL, D_MODEL), lambda i: (0, 0))],
        out_specs=row(D_MODEL),
        out_shape=jax.ShapeDtypeStruct((m, D_MODEL), F32),
        compiler_params=_cparams(1),
        name="merge",
    )(x2d, *branches, y2d, bg, wb, wo)


FFN_TC = 256
FFN_HALO = 8
FFN_RB = 64


def _ffn_kernel(x_ref, g_ref, wg_ref, wv_ref, cw_ref, cb_ref, wd_ref, o_ref,
                h_ref, gbuf_ref, vbuf_ref, act_ref, *, tm, tiles_per_seq):
    i = pl.program_id(0)
    x = x_ref[...]
    h_ref[...] = (x * lax.rsqrt(jnp.mean(x * x, axis=-1, keepdims=True) + EPS)
                  * g_ref[...]).astype(BF16)

    @pl.when((i % tiles_per_seq) == 0)
    def _():
        gbuf_ref[0:FFN_HALO, :] = jnp.zeros((FFN_HALO, D_FF), F32)

    def up(c):
        cols = slice(c * FFN_TC, (c + 1) * FFN_TC)
        gbuf_ref[FFN_HALO:FFN_HALO + tm, cols] = _dot(h_ref[...], wg_ref[:, cols])
        vbuf_ref[:, cols] = _dot(h_ref[...], wv_ref[:, cols])

    def activate(c):
        cols = slice(c * FFN_TC, (c + 1) * FFN_TC)
        cw = cw_ref[:, cols]
        cb = cb_ref[:, cols]
        for r0 in range(0, tm, FFN_RB):
            gate = cb
            for d in range(3):
                lo = FFN_HALO + r0 - d
                gate = gate + gbuf_ref[lo:lo + FFN_RB, cols] * cw[2 - d:3 - d, :]
            act = gate * jax.nn.sigmoid(gate) * vbuf_ref[r0:r0 + FFN_RB, cols]
            act_ref[r0:r0 + FFN_RB, cols] = act.astype(BF16)

    nc = D_FF // FFN_TC
    up(0)
    for c in range(1, nc):
        up(c)
        activate(c - 1)
    activate(nc - 1)
    o_ref[...] = x_ref[...] + _dot(act_ref[...], wd_ref[...])
    gbuf_ref[0:FFN_HALO, :] = gbuf_ref[tm:tm + FFN_HALO, :]


def _ffn(x2d, g, w_up, cw, cb, w_down, seq, tm=512):
    m = x2d.shape[0]
    resident = pl.Buffered(1)
    return pl.pallas_call(
        functools.partial(_ffn_kernel, tm=tm, tiles_per_seq=seq // tm),
        grid=(m // tm,),
        in_specs=[pl.BlockSpec((tm, D_MODEL), lambda i: (i, 0)),
                  pl.BlockSpec((1, D_MODEL), lambda i: (0, 0)),
                  pl.BlockSpec((D_MODEL, D_FF), lambda i: (0, 0), pipeline_mode=resident),
                  pl.BlockSpec((D_MODEL, D_FF), lambda i: (0, 1), pipeline_mode=resident),
                  pl.BlockSpec((3, D_FF), lambda i: (0, 0)),
                  pl.BlockSpec((1, D_FF), lambda i: (0, 0)),
                  pl.BlockSpec((D_FF, D_MODEL), lambda i: (0, 0), pipeline_mode=resident)],
        out_specs=pl.BlockSpec((tm, D_MODEL), lambda i: (i, 0)),
        out_shape=jax.ShapeDtypeStruct((m, D_MODEL), F32),
        scratch_shapes=[pltpu.VMEM((tm, D_MODEL), BF16),
                        pltpu.VMEM((tm + FFN_HALO, D_FF), F32),
                        pltpu.VMEM((tm, D_FF), F32),
                        pltpu.VMEM((tm, D_FF), BF16)],
        compiler_params=_cparams(1),
        name="ffn",
    )(x2d, g, w_up, w_up, cw, cb, w_down)


def _block_diag(w):
    per = LRU_TC // HEAD_DIM
    w4 = w.reshape(D_MODEL // LRU_TC, per, HEAD_DIM, HEAD_DIM)
    eye = jnp.eye(per, dtype=w.dtype)
    bd = jnp.einsum('cpde,pq->cpdqe', w4, eye)
    return bd.reshape(D_MODEL // LRU_TC, LRU_TC, LRU_TC)


def kernel(x, mem, attn_norm_g, mem_norm_g, w_in, b_forget, fox_q_norm_g, fox_k_norm_g,
           lru_conv_w, lru_conv_b, lru_w_a, lru_b_a, lru_w_x, lru_b_x, lru_lambda,
           w_mem_kv, mem_q_norm_g, mem_k_norm_g, b_gate, w_branch, w_out,
           ffn_norm_g, w_up, ffn_conv_w, ffn_conv_b, w_down):
    b, s, d = x.shape
    depth = w_in.shape[0]
    m = b * s
    x2d = x.reshape(m, d)
    for l in range(depth):
        wy = jnp.concatenate([w_in[l, :, W_GATES:], w_in[l, :, :W_FOX_END],
                              w_in[l, :, W_F_END:W_GATES]], axis=1)
        sb_q_scale = jnp.where((jnp.arange(N_Y) >= C_SQ) & (jnp.arange(N_Y) < C_SK),
                               HEAD_DIM ** -0.5 * LOG2E, 1.0).astype(F32)
        wy = (wy * sb_q_scale).astype(BF16)
        wf = jnp.pad(w_in[l, :, W_FOX_END:W_F_END], ((0, 0), (0, LANES - N_HEADS)))
        bf = jnp.pad(b_forget[l], (0, LANES - N_HEADS)).reshape(1, LANES)

        y2d, f2d = _proj(x2d, attn_norm_g[l].reshape(1, d), wy, wf)
        y3d = y2d.reshape(b, s, N_Y)

        fcum = _fcum(f2d.reshape(b, s, LANES), bf)
        qa, ka, vt = _foxprep(y3d, fcum,
                              jnp.tile(fox_q_norm_g[l], 2).reshape(1, LANES),
                              jnp.tile(fox_k_norm_g[l], 2).reshape(1, LANES))
        y_fox = _fox(qa, ka, vt)
        y_sb = _sb(y3d)
        y_lru = _lru(y3d, lru_conv_w[l], lru_conv_b[l].reshape(1, d),
                     _block_diag(lru_w_a[l]).astype(BF16), lru_b_a[l].reshape(1, d),
                     _block_diag(lru_w_x[l]).astype(BF16), lru_b_x[l].reshape(1, d),
                     lru_lambda[l].reshape(1, d))
        mk, mv = _memkv(mem, mem_norm_g[l].reshape(1, d), w_mem_kv[l].astype(BF16),
                        mem_k_norm_g[l].reshape(1, MEM_HD))
        y_mem = _mem(y3d, mk, mv, mem_q_norm_g[l].reshape(1, MEM_HD))

        branches = [t.reshape(m, d) for t in (y_fox, y_lru, y_sb, y_mem)]
        x2d = _merge(x2d, branches, y2d, b_gate[l], w_branch[l].astype(BF16),
                     w_out[l].astype(BF16))
        x2d = _ffn(x2d, ffn_norm_g[l].reshape(1, d), w_up[l].astype(BF16), ffn_conv_w[l],
                   ffn_conv_b[l].reshape(1, D_FF), w_down[l].astype(BF16), s)
    return x2d.reshape(b, s, d)
```

```python
import functools

import jax
import jax.numpy as jnp
from jax import lax
from jax.experimental import pallas as pl
from jax.experimental.pallas import tpu as pltpu

F32 = jnp.float32
BF16 = jnp.bfloat16

D_MODEL = 1024
HEAD_DIM = 64
N_HEADS = 16
N_PAIRS = N_HEADS // 2
LANES = 128
SUBLANES = 8
N_MEM = 256
MEM_HEADS = 4
MEM_HD = 256
D_FF = 2816
LRU_C = 8.0
EPS = 1e-6
NEG = -1e30
LOG2E = 1.4426950408889634
F32_TINY = 1.1754944e-38

C_GT, C_FQ, C_FK, C_FV, C_LX, C_LG, C_SQ, C_SK, C_SV, C_MQ = (
    0, 4096, 5120, 6144, 7168, 8192, 9216, 10240, 11264, 12288)
N_Y = 13312
W_FOX_END, W_F_END, W_GATES = 3072, 3088, 9232

VMEM_LIMIT = 56 * 1024 * 1024


def _cparams(n_axes):
    return pltpu.CompilerParams(dimension_semantics=("arbitrary",) * n_axes,
                                vmem_limit_bytes=VMEM_LIMIT)


def _dot(a, b):
    return jnp.dot(a, b, preferred_element_type=F32)


def _dot_nt(a, b):
    return lax.dot_general(a, b, (((1,), (1,)), ((), ())), preferred_element_type=F32)


def _log1p_exp_neg_abs(z):
    return jnp.log(1.0 + jnp.exp(-jnp.abs(z)))


def _proj_kernel(x_ref, g_ref, w_ref, wf_ref, y_ref, f_ref, h_ref):
    @pl.when(pl.program_id(1) == 0)
    def _():
        x = x_ref[...]
        h = x * lax.rsqrt(jnp.mean(x * x, axis=-1, keepdims=True) + EPS) * g_ref[...]
        h_hi = h.astype(BF16)
        h_ref[...] = h_hi
        h_lo = (h - h_hi.astype(F32)).astype(BF16)
        f_ref[...] = (_dot(h_hi, wf_ref[0]) + _dot(h_lo, wf_ref[0])) + _dot(h_hi, wf_ref[1])

    y_ref[...] = _dot(h_ref[...], w_ref[...]).astype(BF16)


def _proj(x2d, g, wy, wf, tm=1024, tn=1024):
    m = x2d.shape[0]
    return pl.pallas_call(
        _proj_kernel,
        grid=(m // tm, N_Y // tn),
        in_specs=[pl.BlockSpec((tm, D_MODEL), lambda i, j: (i, 0)),
                  pl.BlockSpec((1, D_MODEL), lambda i, j: (0, 0)),
                  pl.BlockSpec((D_MODEL, tn), lambda i, j: (0, j)),
                  pl.BlockSpec((2, D_MODEL, LANES), lambda i, j: (0, 0, 0))],
        out_specs=[pl.BlockSpec((tm, tn), lambda i, j: (i, j)),
                   pl.BlockSpec((tm, LANES), lambda i, j: (i, 0))],
        out_shape=[jax.ShapeDtypeStruct((m, N_Y), BF16),
                   jax.ShapeDtypeStruct((m, LANES), F32)],
        scratch_shapes=[pltpu.VMEM((tm, D_MODEL), BF16)],
        compiler_params=_cparams(2),
        name="proj",
    )(x2d, g, wy, wf)


def _fcum_kernel(f_ref, b_ref, o_ref, carry_ref):
    @pl.when(pl.program_id(1) == 0)
    def _():
        carry_ref[...] = jnp.zeros_like(carry_ref)

    z = f_ref[0] + b_ref[...]
    log_f = jnp.minimum(z, 0.0) - _log1p_exp_neg_abs(z)
    ts = z.shape[0]
    row = lax.broadcasted_iota(jnp.int32, (ts, ts), 0)
    col = lax.broadcasted_iota(jnp.int32, (ts, ts), 1)
    tri = (row >= col).astype(BF16)
    hi = log_f.astype(BF16)
    r1 = log_f - hi.astype(F32)
    mid = r1.astype(BF16)
    lo = (r1 - mid.astype(F32)).astype(BF16)
    c = (_dot(tri, hi) + _dot(tri, mid)) + _dot(tri, lo) + carry_ref[...]
    carry_ref[...] = c[ts - 1:ts, :]

    lane = lax.broadcasted_iota(jnp.int32, (ts, LANES), 1)
    f = jnp.where(lane < N_HEADS, c * LOG2E, 0.0)
    hi = f.astype(BF16).astype(F32)
    r1 = f - hi
    mid = r1.astype(BF16).astype(F32)
    lo = (r1 - mid).astype(BF16).astype(F32)
    packed = hi + pltpu.roll(mid, N_HEADS, axis=1) + pltpu.roll(lo, 2 * N_HEADS, axis=1)
    o_ref[0] = packed.astype(BF16)


def _fcum(f3d, b_pad, ts=512):
    b, s, _ = f3d.shape
    return pl.pallas_call(
        _fcum_kernel,
        grid=(b, s // ts),
        in_specs=[pl.BlockSpec((1, ts, LANES), lambda i, j: (i, j, 0)),
                  pl.BlockSpec((1, LANES), lambda i, j: (0, 0))],
        out_specs=pl.BlockSpec((1, ts, LANES), lambda i, j: (i, j, 0)),
        out_shape=jax.ShapeDtypeStruct((b, s, LANES), BF16),
        scratch_shapes=[pltpu.VMEM((1, LANES), F32)],
        compiler_params=_cparams(2),
        name="fcum",
    )(f3d, b_pad)


def _foxprep_kernel(yq_ref, yk_ref, yv_ref, f_ref, gq_ref, gk_ref, qa_ref, ka_ref, vt_ref):
    hp = pl.program_id(1)
    ones = jnp.ones((FOX_ONES, FOX_TS), BF16)
    for c in range(vt_ref.shape[2]):
        vt = yv_ref[0, c * FOX_TS:(c + 1) * FOX_TS, :].astype(F32).T.astype(BF16)
        vt_ref[0, 0, c] = jnp.concatenate(
            [vt[:HEAD_DIM], ones, vt[HEAD_DIM:], ones], axis=0)
    ts = yq_ref.shape[1]
    lane = lax.broadcasted_iota(jnp.int32, (ts, LANES), 1)
    lo_half = lane < HEAD_DIM

    hr = lax.broadcasted_iota(jnp.int32, (LANES, LANES), 0)
    hc = lax.broadcasted_iota(jnp.int32, (LANES, LANES), 1)
    same_head = ((hr < HEAD_DIM) == (hc < HEAD_DIM)).astype(BF16)

    def headnorm(y, g):
        ms = _dot((y * y).astype(BF16), same_head) * (1.0 / HEAD_DIM)
        return y * lax.rsqrt(ms + EPS) * g

    qn = headnorm(yq_ref[0].astype(F32), gq_ref[...]) * (HEAD_DIM ** -0.5 * LOG2E)
    kn = headnorm(yk_ref[0].astype(F32), gk_ref[...])

    pieces = f_ref[0]
    sel_r = lax.broadcasted_iota(jnp.int32, (LANES, 2 * LANES), 0)
    sel_c = lax.broadcasted_iota(jnp.int32, (LANES, 2 * LANES), 1)
    ones_q = ((lane >= HEAD_DIM + 3) & (lane < HEAD_DIM + 6)).astype(F32)
    ones_k = ((lane >= HEAD_DIM) & (lane < HEAD_DIM + 3)).astype(F32)
    for e in range(2):
        head = 2 * hp + e
        piece_of_row = sel_r - head
        to_q = (piece_of_row == (sel_c - HEAD_DIM) * N_HEADS) & (sel_c < HEAD_DIM + 3)
        to_k = (piece_of_row == (sel_c - (LANES + HEAD_DIM + 3)) * N_HEADS) & (
            sel_c >= LANES + HEAD_DIM + 3)
        valid = (piece_of_row >= 0) & (piece_of_row <= 2 * N_HEADS)
        sel = (jnp.where(valid & to_q, 1.0, 0.0) - jnp.where(valid & to_k, 1.0, 0.0)).astype(BF16)
        placed = _dot(pieces, sel)
        q_part = qn if e == 0 else pltpu.roll(qn, HEAD_DIM, axis=1)
        k_part = kn if e == 0 else pltpu.roll(kn, HEAD_DIM, axis=1)
        q_aug = jnp.where(lo_half, q_part, placed[:, :LANES] + ones_q)
        k_aug = jnp.where(lo_half, k_part, placed[:, LANES:] + ones_k)
        qa_ref[0, e] = q_aug.astype(BF16)
        ka_ref[0, e] = k_aug.astype(BF16)


def _foxprep(y3d, fcum, gq2, gk2, ts=2048):
    b, s, _ = y3d.shape
    out = jax.ShapeDtypeStruct((b, N_HEADS, s, LANES), BF16)
    vt = jax.ShapeDtypeStruct((b, N_PAIRS, s // FOX_TS, 2 * FOX_VROWS, FOX_TS), BF16)
    return pl.pallas_call(
        _foxprep_kernel,
        grid=(b, N_PAIRS, s // ts),
        in_specs=[pl.BlockSpec((1, ts, LANES), lambda i, p, j: (i, j, C_FQ // LANES + p)),
                  pl.BlockSpec((1, ts, LANES), lambda i, p, j: (i, j, C_FK // LANES + p)),
                  pl.BlockSpec((1, ts, LANES), lambda i, p, j: (i, j, C_FV // LANES + p)),
                  pl.BlockSpec((1, ts, LANES), lambda i, p, j: (i, j, 0)),
                  pl.BlockSpec((1, LANES), lambda i, p, j: (0, 0)),
                  pl.BlockSpec((1, LANES), lambda i, p, j: (0, 0))],
        out_specs=[pl.BlockSpec((1, 2, ts, LANES), lambda i, p, j: (i, p, j, 0)),
                   pl.BlockSpec((1, 2, ts, LANES), lambda i, p, j: (i, p, j, 0)),
                   pl.BlockSpec((1, 1, ts // FOX_TS, 2 * FOX_VROWS, FOX_TS),
                                lambda i, p, j: (i, p, j, 0, 0))],
        out_shape=[out, out, vt],
        compiler_params=_cparams(3),
        name="foxprep",
    )(y3d, y3d, y3d, fcum, gq2, gk2)


FOX_TS = 512
FOX_KB = 64
FOX_ONES = 16
FOX_VROWS = HEAD_DIM + FOX_ONES


def _fox_kernel(qa_ref, ka_ref, vt_ref, o_ref, s_ref, p_ref, m_ref, a_ref, acc_ref, *, tq):
    i = pl.program_id(2)
    ts = FOX_TS
    nslab = tq // ts
    chains = [(h, e) for h in range(nslab) for e in range(2)]
    key = lax.broadcasted_iota(jnp.int32, (FOX_KB, LANES), 0)
    query = lax.broadcasted_iota(jnp.int32, (FOX_KB, LANES), 1)

    m_ref[...] = jnp.full(m_ref.shape, NEG, F32)
    acc_ref[...] = jnp.zeros(acc_ref.shape, F32)

    def logits(n, block):
        h, e = chains[n]
        start = pl.multiple_of(block * ts, ts)
        s = _dot_nt(ka_ref[0, e, pl.ds(start, ts), :], qa_ref[0, e, h * ts:(h + 1) * ts, :])
        for c in range(ts // LANES):
            s_ref[n, c, 0:ts, :] = s[:, c * LANES:(c + 1) * LANES]

    def softmax(n, diagonal):
        kb = FOX_KB
        strips = range(0, ts, LANES)

        def piece(c0, k0):
            s = s_ref[n, c0 // LANES, k0:k0 + kb, :]
            if diagonal and k0 + kb > c0:
                s = jnp.where(key + k0 <= query + c0, s, NEG)
            return s

        def last_key(c0):
            return c0 + LANES if diagonal else ts

        m_new = {}
        for c0 in strips:
            cols = slice(c0, c0 + LANES)
            top = piece(c0, 0)
            for k0 in range(kb, last_key(c0), kb):
                top = jnp.maximum(top, piece(c0, k0))
            m_old = m_ref[n, :, cols]
            m_new[c0] = jnp.maximum(m_old, jnp.max(top, axis=0, keepdims=True))
            m_ref[n, :, cols] = m_new[c0]
            a_ref[n, :, cols] = jnp.exp2(m_old - m_new[c0])
        for c0 in strips:
            for k0 in range(0, last_key(c0), kb):
                p_ref[n, c0 // LANES, k0:k0 + kb, :] = jnp.exp2(
                    (piece(c0, k0) - m_new[c0][0:1, :]).astype(BF16))
            if last_key(c0) < ts:
                p_ref[n, c0 // LANES, last_key(c0):ts, :] = jnp.zeros(
                    (ts - last_key(c0), LANES), BF16)

    def accumulate(n, block):
        e = chains[n][1]
        vt = vt_ref[0, 0, block, e * FOX_VROWS:(e + 1) * FOX_VROWS, :]
        p = jnp.concatenate([p_ref[n, c, 0:ts, :] for c in range(ts // LANES)], axis=1)
        acc_ref[n] = a_ref[n, 0:1, :] * acc_ref[n] + _dot(vt, p)

    def step(block, live, diagonal_slab, live_next):
        prev = None
        for n in live:
            softmax(n, chains[n][0] == diagonal_slab)
            if n in live_next:
                logits(n, block + 1)
            if prev is not None:
                accumulate(prev, block)
            prev = n
        accumulate(prev, block)

    everyone = list(range(len(chains)))
    for n in everyone:
        logits(n, 0)

    def body(j, _):
        step(j, everyone, None, everyone)
        return 0

    lax.fori_loop(0, i * nslab, body, 0)
    for c in range(nslab):
        live = [n for n in everyone if chains[n][0] >= c]
        live_next = [n for n in everyone if chains[n][0] >= c + 1] if c + 1 < nslab else []
        step(i * nslab + c, live, c, live_next)

    for h in range(nslab):
        outs = [acc_ref[2 * h + e, 0:HEAD_DIM, :] / acc_ref[2 * h + e, HEAD_DIM:HEAD_DIM + 1, :]
                for e in range(2)]
        o_ref[0, h * ts:(h + 1) * ts, :] = jnp.concatenate(outs, axis=0).T.astype(BF16)


def _fox(qa, ka, vt, tq=2048):
    b, _, s, _ = qa.shape
    nch = 2 * (tq // FOX_TS)
    return pl.pallas_call(
        functools.partial(_fox_kernel, tq=tq),
        grid=(b, N_PAIRS, s // tq),
        in_specs=[pl.BlockSpec((1, 2, tq, LANES), lambda i, p, j: (i, p, j, 0)),
                  pl.BlockSpec((1, 2, s, LANES), lambda i, p, j: (i, p, 0, 0)),
                  pl.BlockSpec((1, 1, s // FOX_TS, 2 * FOX_VROWS, FOX_TS),
                               lambda i, p, j: (i, p, 0, 0, 0))],
        out_specs=pl.BlockSpec((1, tq, LANES), lambda i, p, j: (i, j, p)),
        out_shape=jax.ShapeDtypeStruct((b, s, D_MODEL), BF16),
        scratch_shapes=[pltpu.VMEM((nch, FOX_TS // LANES, FOX_TS + SUBLANES, LANES), F32),
                        pltpu.VMEM((nch, FOX_TS // LANES, FOX_TS + 2 * SUBLANES, LANES), BF16),
                        pltpu.VMEM((nch, SUBLANES, FOX_TS), F32),
                        pltpu.VMEM((nch, SUBLANES, FOX_TS), F32),
                        pltpu.VMEM((nch, FOX_VROWS, FOX_TS), F32)],
        compiler_params=_cparams(3),
        name="fox",
    )(qa, ka, vt)


SB_KC = 256
SB_TS = 512
SB_MAX_LOG2 = 126.0
SB_DEAD_LOG2 = -(SB_MAX_LOG2 + 150.0)


SB_RB = 64
SB_UNROLL = 2


def _sb_kernel(q_ref, k_ref, v_ref, o_ref, qm_ref, z_ref, lb_ref, ic_ref, w_ref,
               rs_ref, r_ref, acc_ref, *, tq):
    i = pl.program_id(2)
    kc, ts, rb = SB_KC, SB_TS, SB_RB
    nsub = tq // kc
    nslab = tq // ts
    chains = [(h, e) for h in range(nslab) for e in range(2)]
    row = lax.broadcasted_iota(jnp.int32, (rb, kc), 0)
    col = lax.broadcasted_iota(jnp.int32, (rb, kc), 1)
    kr = lax.broadcasted_iota(jnp.int32, (kc, kc), 0)
    kcol = lax.broadcasted_iota(jnp.int32, (kc, kc), 1)
    at_or_after = (kr >= kcol).astype(BF16)

    lane = lax.broadcasted_iota(jnp.int32, (ts, LANES), 1)
    for n, (h, e) in enumerate(chains):
        q2 = q_ref[0, h * ts:(h + 1) * ts, :]
        mine = (lane < HEAD_DIM) if e == 0 else (lane >= HEAD_DIM)
        qm_ref[n] = jnp.where(mine, q2, jnp.zeros_like(q2))
    r_ref[...] = jnp.zeros(r_ref.shape, F32)
    acc_ref[...] = jnp.zeros(acc_ref.shape, F32)

    def logits(n, start):
        z_ref[n] = jnp.minimum(_dot_nt(qm_ref[n], k_ref[0, pl.ds(start, kc), :]), SB_MAX_LOG2)

    def log_terms(n, offset):
        for r0 in range(0, ts, rb):
            rows = slice(r0, r0 + rb)
            log_1m_beta = jnp.log(1.0 + jnp.exp2(z_ref[n, rows, :])) * (-LOG2E)
            if offset is not None:
                log_1m_beta = jnp.where(col + (offset - r0) < row, log_1m_beta, 0.0)
            lb_ref[n, rows, :] = log_1m_beta.astype(BF16)
            rs_ref[n, rows, :] = jnp.broadcast_to(
                jnp.sum(log_1m_beta, axis=-1, keepdims=True), (rb, LANES))

    def cumulate(n):
        ic_ref[n] = _dot(lb_ref[n], at_or_after)

    def weights(n, offset):
        for r0 in range(0, ts, rb):
            rows = slice(r0, r0 + rb)
            r = r_ref[n, rows, :]
            log_w = (z_ref[n, rows, :] + ic_ref[n, rows, :]
                     + jnp.concatenate([r] * (kc // LANES), axis=1))
            if offset is not None:
                log_w = jnp.where(col + (offset - r0) < row, log_w, NEG)
            w_ref[n, rows, :] = jnp.exp2(log_w.astype(BF16))
            r_ref[n, rows, :] = r + rs_ref[n, rows, :]

    def accumulate(n, start):
        acc_ref[n] += _dot(w_ref[n], v_ref[0, pl.ds(start, kc), :])

    def chunk(start, next_start, live, offsets, live_next):
        prev = None
        for n in live:
            log_terms(n, offsets[chains[n][0]])
            cumulate(n)
            if prev is not None:
                weights(prev, offsets[chains[prev][0]])
                if prev in live_next:
                    logits(prev, next_start)
                accumulate(prev, start)
            prev = n
        weights(prev, offsets[chains[prev][0]])
        if prev in live_next:
            logits(prev, next_start)
        accumulate(prev, start)
        for n in live_next:
            if n not in live:
                logits(n, next_start)

    def visibility(c):
        live, offsets = [], {}
        for n, (h, _) in enumerate(chains):
            if c * kc >= (h + 1) * ts:
                continue
            live.append(n)
            offsets[h] = None if (c + 1) * kc <= h * ts else c * kc - h * ts
        return live, offsets

    everyone = list(range(len(chains)))
    no_offsets = {h: None for h in range(nslab)}
    live, offsets = visibility(nsub - 1)
    for n in live:
        logits(n, pl.multiple_of(i * tq + (nsub - 1) * kc, kc))
    for c in reversed(range(nsub)):
        start = pl.multiple_of(i * tq + c * kc, kc)
        live_next = visibility(c - 1)[0] if c > 0 else everyone
        chunk(start, pl.multiple_of(jnp.maximum(start - kc, 0), kc), live, offsets, live_next)
        if c > 0:
            live, offsets = visibility(c - 1)

    def alive(first_chain):
        return (jnp.max(r_ref[first_chain:]) > SB_DEAD_LOG2).astype(jnp.int32)

    trips = i * (nsub // SB_UNROLL)

    def walk(t0, live, first_chain):
        def body(state):
            t, _ = state
            for u in range(SB_UNROLL):
                start = pl.multiple_of(i * tq - (t * SB_UNROLL + u + 1) * kc, kc)
                chunk(start, pl.multiple_of(jnp.maximum(start - kc, 0), kc), live, no_offsets,
                      live)
            return t + 1, alive(first_chain)

        t_end, _ = lax.while_loop(lambda state: (state[0] < trips) & (state[1] > 0), body,
                                  (t0, alive(first_chain)))
        return t_end

    t_mid = walk(jnp.int32(0), everyone, len(chains) - 2)
    walk(t_mid, everyone[:2], 0)
    for h in range(nslab):
        o_ref[0, h * ts:(h + 1) * ts, :] = jnp.where(
            lane < HEAD_DIM, acc_ref[2 * h], acc_ref[2 * h + 1]).astype(BF16)


def _sb(y3d, tq=1024):
    b, s, _ = y3d.shape
    nch = 2 * (tq // SB_TS)
    return pl.pallas_call(
        functools.partial(_sb_kernel, tq=tq),
        grid=(b, N_PAIRS, s // tq),
        in_specs=[pl.BlockSpec((1, tq, LANES), lambda i, p, j: (i, j, C_SQ // LANES + p)),
                  pl.BlockSpec((1, s, LANES), lambda i, p, j: (i, 0, C_SK // LANES + p)),
                  pl.BlockSpec((1, s, LANES), lambda i, p, j: (i, 0, C_SV // LANES + p))],
        out_specs=pl.BlockSpec((1, tq, LANES), lambda i, p, j: (i, j, p)),
        out_shape=jax.ShapeDtypeStruct((b, s, D_MODEL), BF16),
        scratch_shapes=[pltpu.VMEM((nch, SB_TS, LANES), BF16),
                        pltpu.VMEM((nch, SB_TS, SB_KC), F32),
                        pltpu.VMEM((nch, SB_TS, SB_KC), BF16),
                        pltpu.VMEM((nch, SB_TS, SB_KC), F32),
                        pltpu.VMEM((nch, SB_TS, SB_KC), BF16),
                        pltpu.VMEM((nch, SB_TS, LANES), F32),
                        pltpu.VMEM((nch, SB_TS, LANES), F32),
                        pltpu.VMEM((nch, SB_TS, LANES), F32)],
        compiler_params=_cparams(3),
        name="sb",
    )(y3d, y3d, y3d)


LRU_TC = 256
LRU_HALO = 8


def _lru_kernel(x_ref, g_ref, cw_ref, cb_ref, wa_ref, ba_ref, wx_ref, bx_ref, lam_ref,
                o_ref, xbuf_ref, h_ref, *, ts):
    @pl.when(pl.program_id(2) == 0)
    def _():
        xbuf_ref[0:LRU_HALO, :] = jnp.zeros((LRU_HALO, LRU_TC), F32)
        h_ref[...] = jnp.zeros_like(h_ref)

    xbuf_ref[LRU_HALO:LRU_HALO + ts, :] = x_ref[0].astype(F32)
    cw = cw_ref[...]
    xc = cb_ref[...] + xbuf_ref[LRU_HALO:LRU_HALO + ts, :] * cw[3:4, :]
    for d in range(1, 4):
        xc = xc + xbuf_ref[LRU_HALO - d:LRU_HALO - d + ts, :] * cw[3 - d:4 - d, :]
    xbuf_ref[0:LRU_HALO, :] = xbuf_ref[ts:ts + LRU_HALO, :]

    xcb = xc.astype(BF16)
    r = jax.nn.sigmoid(_dot(xcb, wa_ref[0]) + ba_ref[...])
    gi = jax.nn.sigmoid(_dot(xcb, wx_ref[0]) + bx_ref[...])
    lam = lam_ref[...]
    softplus_neg_lam = jnp.maximum(-lam, 0.0) + _log1p_exp_neg_abs(lam)
    log_a = -LRU_C * r * softplus_neg_lam
    a = jnp.exp(log_a)
    one_m_a2 = -jnp.tanh(log_a) * (a * a + 1.0)
    u = one_m_a2 * lax.rsqrt(jnp.maximum(one_m_a2, F32_TINY)) * (gi * xc)

    row = lax.broadcasted_iota(jnp.int32, (ts, LRU_TC), 0) % SUBLANES
    d = 1
    while d < SUBLANES:
        keep = row >= d
        a_sh = jnp.where(keep, pltpu.roll(a, d, axis=0), 1.0)
        u_sh = jnp.where(keep, pltpu.roll(u, d, axis=0), 0.0)
        u = a * u_sh + u
        a = a * a_sh
        d *= 2
    state = h_ref[...]
    groups = []
    for r0 in range(0, ts, SUBLANES):
        h = u[r0:r0 + SUBLANES, :] + a[r0:r0 + SUBLANES, :] * state
        groups.append(h)
        state = h[SUBLANES - 1:SUBLANES, :]
    h_ref[...] = state
    h = jnp.concatenate(groups, axis=0)
    o_ref[0] = (h * jax.nn.gelu(g_ref[0].astype(F32))).astype(BF16)


def _lru(y3d, cw, cb, wa_bd, ba, wx_bd, bx, lam, ts=512):
    b, s, _ = y3d.shape
    nct = D_MODEL // LRU_TC
    vec = lambda rows: pl.BlockSpec((rows, LRU_TC), lambda i, c, j: (0, c))
    mat = pl.BlockSpec((1, LRU_TC, LRU_TC), lambda i, c, j: (c, 0, 0))
    return pl.pallas_call(
        functools.partial(_lru_kernel, ts=ts),
        grid=(b, nct, s // ts),
        in_specs=[pl.BlockSpec((1, ts, LRU_TC), lambda i, c, j: (i, j, C_LX // LRU_TC + c)),
                  pl.BlockSpec((1, ts, LRU_TC), lambda i, c, j: (i, j, C_LG // LRU_TC + c)),
                  vec(4), vec(1), mat, vec(1), mat, vec(1), vec(1)],
        out_specs=pl.BlockSpec((1, ts, LRU_TC), lambda i, c, j: (i, j, c)),
        out_shape=jax.ShapeDtypeStruct((b, s, D_MODEL), BF16),
        scratch_shapes=[pltpu.VMEM((ts + LRU_HALO, LRU_TC), F32),
                        pltpu.VMEM((1, LRU_TC), F32)],
        compiler_params=_cparams(3),
        name="lru",
    )(y3d, y3d, cw, cb, wa_bd, ba, wx_bd, bx, lam)


def _memkv_kernel(mem_ref, g_ref, w_ref, gk_ref, k_ref, v_ref):
    x = mem_ref[0]
    h = x * lax.rsqrt(jnp.mean(x * x, axis=-1, keepdims=True) + EPS) * g_ref[...]
    kv = _dot(h.astype(BF16), w_ref[...])
    for hd in range(MEM_HEADS):
        kh = kv[:, hd * MEM_HD:(hd + 1) * MEM_HD]
        kh = kh * lax.rsqrt(jnp.mean(kh * kh, axis=-1, keepdims=True) + EPS) * gk_ref[...]
        k_ref[0, :, hd * MEM_HD:(hd + 1) * MEM_HD] = kh.astype(BF16)
    v_ref[0] = kv[:, D_MODEL:].astype(BF16)


def _memkv(mem, g, w_kv, gk):
    b = mem.shape[0]
    out = jax.ShapeDtypeStruct((b, N_MEM, D_MODEL), BF16)
    return pl.pallas_call(
        _memkv_kernel,
        grid=(b,),
        in_specs=[pl.BlockSpec((1, N_MEM, D_MODEL), lambda i: (i, 0, 0)),
                  pl.BlockSpec((1, D_MODEL), lambda i: (0, 0)),
                  pl.BlockSpec((D_MODEL, 2 * D_MODEL), lambda i: (0, 0)),
                  pl.BlockSpec((1, MEM_HD), lambda i: (0, 0))],
        out_specs=[pl.BlockSpec((1, N_MEM, D_MODEL), lambda i: (i, 0, 0)),
                   pl.BlockSpec((1, N_MEM, D_MODEL), lambda i: (i, 0, 0))],
        out_shape=[out, out],
        compiler_params=_cparams(1),
        name="memkv",
    )(mem, g, w_kv, gk)


def _mem_kernel(q_ref, k_ref, v_ref, gq_ref, o_ref):
    for hd in range(MEM_HEADS):
        sl = slice(hd * MEM_HD, (hd + 1) * MEM_HD)
        q = q_ref[0, :, sl].astype(F32)
        q = q * lax.rsqrt(jnp.mean(q * q, axis=-1, keepdims=True) + EPS) * gq_ref[...]
        q = (q * (MEM_HD ** -0.5)).astype(BF16)
        s = _dot_nt(q, k_ref[0, :, sl])
        p = jnp.exp(s - jnp.max(s, axis=-1, keepdims=True))
        o = _dot(p.astype(BF16), v_ref[0, :, sl]) / jnp.sum(p, axis=-1, keepdims=True)
        o_ref[0, :, sl] = o.astype(BF16)


def _mem(y3d, mk, mv, gq, ts=512):
    b, s, _ = y3d.shape
    return pl.pallas_call(
        _mem_kernel,
        grid=(b, s // ts),
        in_specs=[pl.BlockSpec((1, ts, D_MODEL), lambda i, j: (i, j, C_MQ // D_MODEL)),
                  pl.BlockSpec((1, N_MEM, D_MODEL), lambda i, j: (i, 0, 0)),
                  pl.BlockSpec((1, N_MEM, D_MODEL), lambda i, j: (i, 0, 0)),
                  pl.BlockSpec((1, MEM_HD), lambda i, j: (0, 0))],
        out_specs=pl.BlockSpec((1, ts, D_MODEL), lambda i, j: (i, j, 0)),
        out_shape=jax.ShapeDtypeStruct((b, s, D_MODEL), BF16),
        compiler_params=_cparams(2),
        name="mem",
    )(y3d, mk, mv, gq)


def _merge_kernel(x_ref, b0_ref, b1_ref, b2_ref, b3_ref, gt_ref, bg_ref, wb_ref, wo_ref, o_ref):
    mixed = None
    for n, br in enumerate((b0_ref, b1_ref, b2_ref, b3_ref)):
        gate = jax.nn.sigmoid(
            gt_ref[:, n * D_MODEL:(n + 1) * D_MODEL].astype(F32) + bg_ref[n:n + 1, :])
        term = gate * _dot(br[...], wb_ref[n])
        mixed = term if mixed is None else mixed + term
    o_ref[...] = x_ref[...] + _dot(mixed.astype(BF16), wo_ref[...])


def _merge(x2d, branches, y2d, bg, wb, wo, layer, tm=256):
    m = x2d.shape[0]
    row = lambda w: pl.BlockSpec((tm, w), lambda i: (i, 0))
    return pl.pallas_call(
        _merge_kernel,
        grid=(m // tm,),
        in_specs=[row(D_MODEL), row(D_MODEL), row(D_MODEL), row(D_MODEL), row(D_MODEL),
                  pl.BlockSpec((tm, 4 * D_MODEL), lambda i: (i, C_GT // (4 * D_MODEL))),
                  pl.BlockSpec((4, D_MODEL), lambda i: (0, 0)),
                  pl.BlockSpec((None, 4, D_MODEL, D_MODEL), lambda i: (layer, 0, 0, 0)),
                  pl.BlockSpec((None, D_MODEL, D_MODEL), lambda i: (layer, 0, 0))],
        out_specs=row(D_MODEL),
        out_shape=jax.ShapeDtypeStruct((m, D_MODEL), F32),
        compiler_params=_cparams(1),
        name="merge",
    )(x2d, *branches, y2d, bg, wb, wo)


FFN_TC = 256
FFN_HALO = 8
FFN_RB = 64


def _ffn_kernel(x_ref, g_ref, wg_ref, wv_ref, cw_ref, cb_ref, wd_ref, o_ref,
                h_ref, gbuf_ref, vbuf_ref, act_ref, *, tm, tiles_per_seq):
    i = pl.program_id(0)
    x = x_ref[...]
    h_ref[...] = (x * lax.rsqrt(jnp.mean(x * x, axis=-1, keepdims=True) + EPS)
                  * g_ref[...]).astype(BF16)

    @pl.when((i % tiles_per_seq) == 0)
    def _():
        gbuf_ref[0:FFN_HALO, :] = jnp.zeros((FFN_HALO, D_FF), F32)

    def up(c):
        cols = slice(c * FFN_TC, (c + 1) * FFN_TC)
        gbuf_ref[FFN_HALO:FFN_HALO + tm, cols] = _dot(h_ref[...], wg_ref[:, cols])
        vbuf_ref[:, cols] = _dot(h_ref[...], wv_ref[:, cols])

    def activate(c):
        cols = slice(c * FFN_TC, (c + 1) * FFN_TC)
        cw = cw_ref[:, cols]
        cb = cb_ref[:, cols]
        for r0 in range(0, tm, FFN_RB):
            gate = cb
            for d in range(3):
                lo = FFN_HALO + r0 - d
                gate = gate + gbuf_ref[lo:lo + FFN_RB, cols] * cw[2 - d:3 - d, :]
            act = gate * jax.nn.sigmoid(gate) * vbuf_ref[r0:r0 + FFN_RB, cols]
            act_ref[r0:r0 + FFN_RB, cols] = act.astype(BF16)

    nc = D_FF // FFN_TC
    up(0)
    for c in range(1, nc):
        up(c)
        activate(c - 1)
    activate(nc - 1)
    o_ref[...] = x_ref[...] + _dot(act_ref[...], wd_ref[...])
    gbuf_ref[0:FFN_HALO, :] = gbuf_ref[tm:tm + FFN_HALO, :]


def _ffn(x2d, g, w_up, cw, cb, w_down, layer, seq, tm=512):
    m = x2d.shape[0]
    resident = pl.Buffered(1)
    return pl.pallas_call(
        functools.partial(_ffn_kernel, tm=tm, tiles_per_seq=seq // tm),
        grid=(m // tm,),
        in_specs=[pl.BlockSpec((tm, D_MODEL), lambda i: (i, 0)),
                  pl.BlockSpec((1, D_MODEL), lambda i: (0, 0)),
                  pl.BlockSpec((None, D_MODEL, D_FF), lambda i: (layer, 0, 0),
                               pipeline_mode=resident),
                  pl.BlockSpec((None, D_MODEL, D_FF), lambda i: (layer, 0, 1),
                               pipeline_mode=resident),
                  pl.BlockSpec((3, D_FF), lambda i: (0, 0)),
                  pl.BlockSpec((1, D_FF), lambda i: (0, 0)),
                  pl.BlockSpec((None, D_FF, D_MODEL), lambda i: (layer, 0, 0),
                               pipeline_mode=resident)],
        out_specs=pl.BlockSpec((tm, D_MODEL), lambda i: (i, 0)),
        out_shape=jax.ShapeDtypeStruct((m, D_MODEL), F32),
        scratch_shapes=[pltpu.VMEM((tm, D_MODEL), BF16),
                        pltpu.VMEM((tm + FFN_HALO, D_FF), F32),
                        pltpu.VMEM((tm, D_FF), F32),
                        pltpu.VMEM((tm, D_FF), BF16)],
        compiler_params=_cparams(1),
        name="ffn",
    )(x2d, g, w_up, w_up, cw, cb, w_down)


def _block_diag(w):
    per = LRU_TC // HEAD_DIM
    w4 = w.reshape(D_MODEL // LRU_TC, per, HEAD_DIM, HEAD_DIM)
    eye = jnp.eye(per, dtype=w.dtype)
    bd = jnp.einsum('cpde,pq->cpdqe', w4, eye)
    return bd.reshape(D_MODEL // LRU_TC, LRU_TC, LRU_TC)


def kernel(x, mem, attn_norm_g, mem_norm_g, w_in, b_forget, fox_q_norm_g, fox_k_norm_g,
           lru_conv_w, lru_conv_b, lru_w_a, lru_b_a, lru_w_x, lru_b_x, lru_lambda,
           w_mem_kv, mem_q_norm_g, mem_k_norm_g, b_gate, w_branch, w_out,
           ffn_norm_g, w_up, ffn_conv_w, ffn_conv_b, w_down):
    b, s, d = x.shape
    depth = w_in.shape[0]
    m = b * s
    x2d = x.reshape(m, d)
    w_branch_bf, w_out_bf = w_branch.astype(BF16), w_out.astype(BF16)
    w_up_bf, w_down_bf = w_up.astype(BF16), w_down.astype(BF16)
    for l in range(depth):
        wy = jnp.concatenate([w_in[l, :, W_GATES:], w_in[l, :, :W_FOX_END],
                              w_in[l, :, W_F_END:W_GATES]], axis=1)
        sb_q_scale = jnp.where((jnp.arange(N_Y) >= C_SQ) & (jnp.arange(N_Y) < C_SK),
                               HEAD_DIM ** -0.5 * LOG2E, 1.0).astype(F32)
        wy = (wy * sb_q_scale).astype(BF16)
        wf = jnp.pad(w_in[l, :, W_FOX_END:W_F_END], ((0, 0), (0, LANES - N_HEADS)))
        wf_hi = wf.astype(BF16)
        wf = jnp.stack([wf_hi, (wf - wf_hi.astype(F32)).astype(BF16)])
        bf = jnp.pad(b_forget[l], (0, LANES - N_HEADS)).reshape(1, LANES)

        y2d, f2d = _proj(x2d, attn_norm_g[l].reshape(1, d), wy, wf)
        y3d = y2d.reshape(b, s, N_Y)

        fcum = _fcum(f2d.reshape(b, s, LANES), bf)
        qa, ka, vt = _foxprep(y3d, fcum,
                              jnp.tile(fox_q_norm_g[l], 2).reshape(1, LANES),
                              jnp.tile(fox_k_norm_g[l], 2).reshape(1, LANES))
        y_fox = _fox(qa, ka, vt)
        y_sb = _sb(y3d)
        y_lru = _lru(y3d, lru_conv_w[l], lru_conv_b[l].reshape(1, d),
                     _block_diag(lru_w_a[l]).astype(BF16), lru_b_a[l].reshape(1, d),
                     _block_diag(lru_w_x[l]).astype(BF16), lru_b_x[l].reshape(1, d),
                     lru_lambda[l].reshape(1, d))
        mk, mv = _memkv(mem, mem_norm_g[l].reshape(1, d), w_mem_kv[l].astype(BF16),
                        mem_k_norm_g[l].reshape(1, MEM_HD))
        y_mem = _mem(y3d, mk, mv, mem_q_norm_g[l].reshape(1, MEM_HD))

        branches = [t.reshape(m, d) for t in (y_fox, y_lru, y_sb, y_mem)]
        x2d = _merge(x2d, branches, y2d, b_gate[l], w_branch_bf, w_out_bf, l)
        x2d = _ffn(x2d, ffn_norm_g[l].reshape(1, d), w_up_bf, ffn_conv_w[l],
                   ffn_conv_b[l].reshape(1, D_FF), w_down_bf, l, s)
    return x2d.reshape(b, s, d)
```

```python
import functools

import jax
import jax.numpy as jnp
from jax import lax
from jax.experimental import pallas as pl
from jax.experimental.pallas import tpu as pltpu

F32 = jnp.float32
BF16 = jnp.bfloat16

D_MODEL = 1024
HEAD_DIM = 64
N_HEADS = 16
N_PAIRS = N_HEADS // 2
LANES = 128
SUBLANES = 8
N_MEM = 256
MEM_HEADS = 4
MEM_HD = 256
D_FF = 2816
LRU_C = 8.0
EPS = 1e-6
NEG = -1e30
LOG2E = 1.4426950408889634
F32_TINY = 1.1754944e-38

C_GT, C_FQ, C_FK, C_FV, C_LX, C_LG, C_SQ, C_SK, C_SV, C_MQ = (
    0, 4096, 5120, 6144, 7168, 8192, 9216, 10240, 11264, 12288)
N_Y = 13312
W_FOX_END, W_F_END, W_GATES = 3072, 3088, 9232

VMEM_LIMIT = 56 * 1024 * 1024


def _cparams(n_axes):
    return pltpu.CompilerParams(dimension_semantics=("arbitrary",) * n_axes,
                                vmem_limit_bytes=VMEM_LIMIT)


def _dot(a, b):
    return jnp.dot(a, b, preferred_element_type=F32)


def _dot_nt(a, b):
    return lax.dot_general(a, b, (((1,), (1,)), ((), ())), preferred_element_type=F32)


def _log1p_exp_neg_abs(z):
    return jnp.log(1.0 + jnp.exp(-jnp.abs(z)))


def _proj_kernel(x_ref, g_ref, w_ref, wf_ref, y_ref, f_ref, h_ref):
    @pl.when(pl.program_id(1) == 0)
    def _():
        x = x_ref[...]
        h = x * lax.rsqrt(jnp.mean(x * x, axis=-1, keepdims=True) + EPS) * g_ref[...]
        h_hi = h.astype(BF16)
        h_ref[...] = h_hi
        h_lo = (h - h_hi.astype(F32)).astype(BF16)
        f_ref[...] = (_dot(h_hi, wf_ref[0]) + _dot(h_lo, wf_ref[0])) + _dot(h_hi, wf_ref[1])

    y_ref[...] = _dot(h_ref[...], w_ref[...]).astype(BF16)


def _proj(x2d, g, wy, wf, tm=1024, tn=1024):
    m = x2d.shape[0]
    return pl.pallas_call(
        _proj_kernel,
        grid=(m // tm, N_Y // tn),
        in_specs=[pl.BlockSpec((tm, D_MODEL), lambda i, j: (i, 0)),
                  pl.BlockSpec((1, D_MODEL), lambda i, j: (0, 0)),
                  pl.BlockSpec((D_MODEL, tn), lambda i, j: (0, j)),
                  pl.BlockSpec((2, D_MODEL, LANES), lambda i, j: (0, 0, 0))],
        out_specs=[pl.BlockSpec((tm, tn), lambda i, j: (i, j)),
                   pl.BlockSpec((tm, LANES), lambda i, j: (i, 0))],
        out_shape=[jax.ShapeDtypeStruct((m, N_Y), BF16),
                   jax.ShapeDtypeStruct((m, LANES), F32)],
        scratch_shapes=[pltpu.VMEM((tm, D_MODEL), BF16)],
        compiler_params=_cparams(2),
        name="proj",
    )(x2d, g, wy, wf)


F_PIECES = 3
FOX_TS = 512
FOX_KB = 64
FOX_ONES = 16
FOX_VROWS = HEAD_DIM + FOX_ONES


def _fcum_kernel(f_ref, b_ref, o_ref, carry_ref):
    @pl.when(pl.program_id(1) == 0)
    def _():
        carry_ref[...] = jnp.zeros_like(carry_ref)

    z = f_ref[0] + b_ref[...]
    log_f = jnp.minimum(z, 0.0) - _log1p_exp_neg_abs(z)
    ts = z.shape[0]
    row = lax.broadcasted_iota(jnp.int32, (ts, ts), 0)
    col = lax.broadcasted_iota(jnp.int32, (ts, ts), 1)
    tri = (row >= col).astype(BF16)
    hi = log_f.astype(BF16)
    r1 = log_f - hi.astype(F32)
    mid = r1.astype(BF16)
    lo = (r1 - mid.astype(F32)).astype(BF16)
    c = (_dot(tri, hi) + _dot(tri, mid)) + _dot(tri, lo) + carry_ref[...]
    carry_ref[...] = c[ts - 1:ts, :]

    lane = lax.broadcasted_iota(jnp.int32, (ts, LANES), 1)
    f = jnp.where(lane < N_HEADS, c * LOG2E, 0.0)
    hi = f.astype(BF16).astype(F32)
    r1 = f - hi
    mid = r1.astype(BF16).astype(F32)
    lo = (r1 - mid).astype(BF16).astype(F32)
    packed = hi + pltpu.roll(mid, N_HEADS, axis=1) + pltpu.roll(lo, 2 * N_HEADS, axis=1)
    o_ref[0] = packed.astype(BF16)


def _fcum(f3d, b_pad, ts=512):
    b, s, _ = f3d.shape
    return pl.pallas_call(
        _fcum_kernel,
        grid=(b, s // ts),
        in_specs=[pl.BlockSpec((1, ts, LANES), lambda i, j: (i, j, 0)),
                  pl.BlockSpec((1, LANES), lambda i, j: (0, 0))],
        out_specs=pl.BlockSpec((1, ts, LANES), lambda i, j: (i, j, 0)),
        out_shape=jax.ShapeDtypeStruct((b, s, LANES), BF16),
        scratch_shapes=[pltpu.VMEM((1, LANES), F32)],
        compiler_params=_cparams(2),
        name="fcum",
    )(f3d, b_pad)


def _foxprep_kernel(yq_ref, yk_ref, yv_ref, f_ref, gq_ref, gk_ref, qa_ref, ka_ref, vt_ref):
    hp = pl.program_id(1)
    ones = jnp.ones((FOX_ONES, FOX_TS), BF16)
    for c in range(vt_ref.shape[2]):
        vt = yv_ref[0, c * FOX_TS:(c + 1) * FOX_TS, :].astype(F32).T.astype(BF16)
        vt_ref[0, 0, c] = jnp.concatenate(
            [vt[:HEAD_DIM], ones, vt[HEAD_DIM:], ones], axis=0)
    ts = yq_ref.shape[1]
    lane = lax.broadcasted_iota(jnp.int32, (ts, LANES), 1)
    lo_half = lane < HEAD_DIM

    hr = lax.broadcasted_iota(jnp.int32, (LANES, LANES), 0)
    hc = lax.broadcasted_iota(jnp.int32, (LANES, LANES), 1)
    same_head = ((hr < HEAD_DIM) == (hc < HEAD_DIM)).astype(BF16)

    def headnorm(y, g):
        ms = _dot((y * y).astype(BF16), same_head) * (1.0 / HEAD_DIM)
        return y * lax.rsqrt(ms + EPS) * g

    qn = headnorm(yq_ref[0].astype(F32), gq_ref[...]) * (HEAD_DIM ** -0.5 * LOG2E)
    kn = headnorm(yk_ref[0].astype(F32), gk_ref[...])

    pieces = f_ref[0]
    sel_r = lax.broadcasted_iota(jnp.int32, (LANES, 2 * LANES), 0)
    sel_c = lax.broadcasted_iota(jnp.int32, (LANES, 2 * LANES), 1)
    ones_q = ((lane >= HEAD_DIM + F_PIECES) & (lane < HEAD_DIM + 2 * F_PIECES)).astype(F32)
    ones_k = ((lane >= HEAD_DIM) & (lane < HEAD_DIM + F_PIECES)).astype(F32)
    for e in range(2):
        head = 2 * hp + e
        piece_of_row = sel_r - head
        to_q = (piece_of_row == (sel_c - HEAD_DIM) * N_HEADS) & (sel_c < HEAD_DIM + F_PIECES)
        k_col0 = LANES + HEAD_DIM + F_PIECES
        to_k = (piece_of_row == (sel_c - k_col0) * N_HEADS) & (sel_c >= k_col0)
        valid = (piece_of_row >= 0) & (piece_of_row < F_PIECES * N_HEADS)
        sel = (jnp.where(valid & to_q, 1.0, 0.0) - jnp.where(valid & to_k, 1.0, 0.0)).astype(BF16)
        placed = _dot(pieces, sel)
        q_part = qn if e == 0 else pltpu.roll(qn, HEAD_DIM, axis=1)
        k_part = kn if e == 0 else pltpu.roll(kn, HEAD_DIM, axis=1)
        q_aug = jnp.where(lo_half, q_part, placed[:, :LANES] + ones_q)
        k_aug = jnp.where(lo_half, k_part, placed[:, LANES:] + ones_k)
        qa_ref[0, e] = q_aug.astype(BF16)
        ka_ref[0, e] = k_aug.astype(BF16)


def _foxprep(y3d, fcum, gq2, gk2, ts=2048):
    b, s, _ = y3d.shape
    out = jax.ShapeDtypeStruct((b, N_HEADS, s, LANES), BF16)
    vt = jax.ShapeDtypeStruct((b, N_PAIRS, s // FOX_TS, 2 * FOX_VROWS, FOX_TS), BF16)
    return pl.pallas_call(
        _foxprep_kernel,
        grid=(b, N_PAIRS, s // ts),
        in_specs=[pl.BlockSpec((1, ts, LANES), lambda i, p, j: (i, j, C_FQ // LANES + p)),
                  pl.BlockSpec((1, ts, LANES), lambda i, p, j: (i, j, C_FK // LANES + p)),
                  pl.BlockSpec((1, ts, LANES), lambda i, p, j: (i, j, C_FV // LANES + p)),
                  pl.BlockSpec((1, ts, LANES), lambda i, p, j: (i, j, 0)),
                  pl.BlockSpec((1, LANES), lambda i, p, j: (0, 0)),
                  pl.BlockSpec((1, LANES), lambda i, p, j: (0, 0))],
        out_specs=[pl.BlockSpec((1, 2, ts, LANES), lambda i, p, j: (i, p, j, 0)),
                   pl.BlockSpec((1, 2, ts, LANES), lambda i, p, j: (i, p, j, 0)),
                   pl.BlockSpec((1, 1, ts // FOX_TS, 2 * FOX_VROWS, FOX_TS),
                                lambda i, p, j: (i, p, j, 0, 0))],
        out_shape=[out, out, vt],
        compiler_params=_cparams(3),
        name="foxprep",
    )(y3d, y3d, y3d, fcum, gq2, gk2)


def _fox_kernel(qa_ref, ka_ref, vt_ref, o_ref, s_ref, p_ref, m_ref, a_ref, acc_ref, *, tq):
    i = pl.program_id(2)
    ts = FOX_TS
    nslab = tq // ts
    chains = [(h, e) for h in range(nslab) for e in range(2)]
    key = lax.broadcasted_iota(jnp.int32, (FOX_KB, LANES), 0)
    query = lax.broadcasted_iota(jnp.int32, (FOX_KB, LANES), 1)

    m_ref[...] = jnp.full(m_ref.shape, NEG, F32)
    acc_ref[...] = jnp.zeros(acc_ref.shape, F32)

    def logits(n, block):
        h, e = chains[n]
        start = pl.multiple_of(block * ts, ts)
        s = _dot_nt(ka_ref[0, e, pl.ds(start, ts), :], qa_ref[0, e, h * ts:(h + 1) * ts, :])
        for c in range(ts // LANES):
            s_ref[n, c, 0:ts, :] = s[:, c * LANES:(c + 1) * LANES]

    def softmax(n, diagonal):
        kb = FOX_KB
        strips = range(0, ts, LANES)

        def piece(c0, k0):
            s = s_ref[n, c0 // LANES, k0:k0 + kb, :]
            if diagonal and k0 + kb > c0:
                s = jnp.where(key + k0 <= query + c0, s, NEG)
            return s

        def last_key(c0):
            return c0 + LANES if diagonal else ts

        m_new = {}
        for c0 in strips:
            cols = slice(c0, c0 + LANES)
            top = piece(c0, 0)
            for k0 in range(kb, last_key(c0), kb):
                top = jnp.maximum(top, piece(c0, k0))
            m_old = m_ref[n, :, cols]
            m_new[c0] = jnp.maximum(m_old, jnp.max(top, axis=0, keepdims=True))
            m_ref[n, :, cols] = m_new[c0]
            a_ref[n, :, cols] = jnp.exp2(m_old - m_new[c0])
        for c0 in strips:
            for k0 in range(0, last_key(c0), kb):
                p_ref[n, c0 // LANES, k0:k0 + kb, :] = jnp.exp2(
                    (piece(c0, k0) - m_new[c0][0:1, :]).astype(BF16))
            if last_key(c0) < ts:
                p_ref[n, c0 // LANES, last_key(c0):ts, :] = jnp.zeros(
                    (ts - last_key(c0), LANES), BF16)

    def accumulate(n, block):
        e = chains[n][1]
        vt = vt_ref[0, 0, block, e * FOX_VROWS:(e + 1) * FOX_VROWS, :]
        p = jnp.concatenate([p_ref[n, c, 0:ts, :] for c in range(ts // LANES)], axis=1)
        acc_ref[n] = a_ref[n, 0:1, :] * acc_ref[n] + _dot(vt, p)

    def step(block, live, diagonal_slab, live_next):
        prev = None
        for n in live:
            softmax(n, chains[n][0] == diagonal_slab)
            if n in live_next:
                logits(n, block + 1)
            if prev is not None:
                accumulate(prev, block)
            prev = n
        accumulate(prev, block)

    everyone = list(range(len(chains)))
    for n in everyone:
        logits(n, 0)

    def body(j, _):
        step(j, everyone, None, everyone)
        return 0

    lax.fori_loop(0, i * nslab, body, 0)
    for c in range(nslab):
        live = [n for n in everyone if chains[n][0] >= c]
        live_next = [n for n in everyone if chains[n][0] >= c + 1] if c + 1 < nslab else []
        step(i * nslab + c, live, c, live_next)

    for h in range(nslab):
        outs = [acc_ref[2 * h + e, 0:HEAD_DIM, :] / acc_ref[2 * h + e, HEAD_DIM:HEAD_DIM + 1, :]
                for e in range(2)]
        o_ref[0, h * ts:(h + 1) * ts, :] = jnp.concatenate(outs, axis=0).T.astype(BF16)


def _fox(qa, ka, vt, tq=2048):
    b, _, s, _ = qa.shape
    nch = 2 * (tq // FOX_TS)
    return pl.pallas_call(
        functools.partial(_fox_kernel, tq=tq),
        grid=(b, N_PAIRS, s // tq),
        in_specs=[pl.BlockSpec((1, 2, tq, LANES), lambda i, p, j: (i, p, j, 0)),
                  pl.BlockSpec((1, 2, s, LANES), lambda i, p, j: (i, p, 0, 0)),
                  pl.BlockSpec((1, 1, s // FOX_TS, 2 * FOX_VROWS, FOX_TS),
                               lambda i, p, j: (i, p, 0, 0, 0))],
        out_specs=pl.BlockSpec((1, tq, LANES), lambda i, p, j: (i, j, p)),
        out_shape=jax.ShapeDtypeStruct((b, s, D_MODEL), BF16),
        scratch_shapes=[pltpu.VMEM((nch, FOX_TS // LANES, FOX_TS + SUBLANES, LANES), F32),
                        pltpu.VMEM((nch, FOX_TS // LANES, FOX_TS + 2 * SUBLANES, LANES), BF16),
                        pltpu.VMEM((nch, SUBLANES, FOX_TS), F32),
                        pltpu.VMEM((nch, SUBLANES, FOX_TS), F32),
                        pltpu.VMEM((nch, FOX_VROWS, FOX_TS), F32)],
        compiler_params=_cparams(3),
        name="fox",
    )(qa, ka, vt)


SB_KC = 256
SB_TS = 512
SB_MAX_LOG2 = 126.0
SB_DEAD_LOG2 = -(SB_MAX_LOG2 + 150.0)


SB_RB = 64
SB_UNROLL = 2


def _sb_kernel(q_ref, k_ref, v_ref, o_ref, qm_ref, z_ref, lb_ref, ic_ref, w_ref,
               rs_ref, r_ref, acc_ref, *, tq):
    i = pl.program_id(2)
    kc, ts, rb = SB_KC, SB_TS, SB_RB
    nsub = tq // kc
    nslab = tq // ts
    chains = [(h, e) for h in range(nslab) for e in range(2)]
    row = lax.broadcasted_iota(jnp.int32, (rb, kc), 0)
    col = lax.broadcasted_iota(jnp.int32, (rb, kc), 1)
    kr = lax.broadcasted_iota(jnp.int32, (kc, kc), 0)
    kcol = lax.broadcasted_iota(jnp.int32, (kc, kc), 1)
    at_or_after = (kr >= kcol).astype(BF16)

    lane = lax.broadcasted_iota(jnp.int32, (ts, LANES), 1)
    for n, (h, e) in enumerate(chains):
        q2 = q_ref[0, h * ts:(h + 1) * ts, :]
        mine = (lane < HEAD_DIM) if e == 0 else (lane >= HEAD_DIM)
        qm_ref[n] = jnp.where(mine, q2, jnp.zeros_like(q2))
    r_ref[...] = jnp.zeros(r_ref.shape, F32)
    acc_ref[...] = jnp.zeros(acc_ref.shape, F32)

    def logits(n, start):
        z_ref[n] = jnp.minimum(_dot_nt(qm_ref[n], k_ref[0, pl.ds(start, kc), :]), SB_MAX_LOG2)

    def log_terms(n, offset):
        for r0 in range(0, ts, rb):
            rows = slice(r0, r0 + rb)
            log_1m_beta = jnp.log(1.0 + jnp.exp2(z_ref[n, rows, :])) * (-LOG2E)
            if offset is not None:
                log_1m_beta = jnp.where(col + (offset - r0) < row, log_1m_beta, 0.0)
            lb_ref[n, rows, :] = log_1m_beta.astype(BF16)
            rs_ref[n, rows, :] = jnp.broadcast_to(
                jnp.sum(log_1m_beta, axis=-1, keepdims=True), (rb, LANES))

    def cumulate(n):
        ic_ref[n] = _dot(lb_ref[n], at_or_after)

    def weights(n, offset):
        for r0 in range(0, ts, rb):
            rows = slice(r0, r0 + rb)
            r = r_ref[n, rows, :]
            log_w = (z_ref[n, rows, :] + ic_ref[n, rows, :]
                     + jnp.concatenate([r] * (kc // LANES), axis=1))
            if offset is not None:
                log_w = jnp.where(col + (offset - r0) < row, log_w, NEG)
            w_ref[n, rows, :] = jnp.exp2(log_w.astype(BF16))
            r_ref[n, rows, :] = r + rs_ref[n, rows, :]

    def accumulate(n, start):
        acc_ref[n] += _dot(w_ref[n], v_ref[0, pl.ds(start, kc), :])

    def chunk(start, next_start, live, offsets, live_next):
        prev = None
        for n in live:
            log_terms(n, offsets[chains[n][0]])
            cumulate(n)
            if prev is not None:
                weights(prev, offsets[chains[prev][0]])
                if prev in live_next:
                    logits(prev, next_start)
                accumulate(prev, start)
            prev = n
        weights(prev, offsets[chains[prev][0]])
        if prev in live_next:
            logits(prev, next_start)
        accumulate(prev, start)
        for n in live_next:
            if n not in live:
                logits(n, next_start)

    def visibility(c):
        live, offsets = [], {}
        for n, (h, _) in enumerate(chains):
            if c * kc >= (h + 1) * ts:
                continue
            live.append(n)
            offsets[h] = None if (c + 1) * kc <= h * ts else c * kc - h * ts
        return live, offsets

    everyone = list(range(len(chains)))
    no_offsets = {h: None for h in range(nslab)}
    live, offsets = visibility(nsub - 1)
    for n in live:
        logits(n, pl.multiple_of(i * tq + (nsub - 1) * kc, kc))
    for c in reversed(range(nsub)):
        start = pl.multiple_of(i * tq + c * kc, kc)
        live_next = visibility(c - 1)[0] if c > 0 else everyone
        chunk(start, pl.multiple_of(jnp.maximum(start - kc, 0), kc), live, offsets, live_next)
        if c > 0:
            live, offsets = visibility(c - 1)

    def alive(first_chain):
        return (jnp.max(r_ref[first_chain:]) > SB_DEAD_LOG2).astype(jnp.int32)

    trips = i * (nsub // SB_UNROLL)

    def walk(t0, live, first_chain):
        def body(state):
            t, _ = state
            for u in range(SB_UNROLL):
                start = pl.multiple_of(i * tq - (t * SB_UNROLL + u + 1) * kc, kc)
                chunk(start, pl.multiple_of(jnp.maximum(start - kc, 0), kc), live, no_offsets,
                      live)
            return t + 1, alive(first_chain)

        t_end, _ = lax.while_loop(lambda state: (state[0] < trips) & (state[1] > 0), body,
                                  (t0, alive(first_chain)))
        return t_end

    t_mid = walk(jnp.int32(0), everyone, len(chains) - 2)
    walk(t_mid, everyone[:2], 0)
    for h in range(nslab):
        o_ref[0, h * ts:(h + 1) * ts, :] = jnp.where(
            lane < HEAD_DIM, acc_ref[2 * h], acc_ref[2 * h + 1]).astype(BF16)


def _sb(y3d, tq=1024):
    b, s, _ = y3d.shape
    nch = 2 * (tq // SB_TS)
    return pl.pallas_call(
        functools.partial(_sb_kernel, tq=tq),
        grid=(b, N_PAIRS, s // tq),
        in_specs=[pl.BlockSpec((1, tq, LANES), lambda i, p, j: (i, j, C_SQ // LANES + p)),
                  pl.BlockSpec((1, s, LANES), lambda i, p, j: (i, 0, C_SK // LANES + p)),
                  pl.BlockSpec((1, s, LANES), lambda i, p, j: (i, 0, C_SV // LANES + p))],
        out_specs=pl.BlockSpec((1, tq, LANES), lambda i, p, j: (i, j, p)),
        out_shape=jax.ShapeDtypeStruct((b, s, D_MODEL), BF16),
        scratch_shapes=[pltpu.VMEM((nch, SB_TS, LANES), BF16),
                        pltpu.VMEM((nch, SB_TS, SB_KC), F32),
                        pltpu.VMEM((nch, SB_TS, SB_KC), BF16),
                        pltpu.VMEM((nch, SB_TS, SB_KC), F32),
                        pltpu.VMEM((nch, SB_TS, SB_KC), BF16),
                        pltpu.VMEM((nch, SB_TS, LANES), F32),
                        pltpu.VMEM((nch, SB_TS, LANES), F32),
                        pltpu.VMEM((nch, SB_TS, LANES), F32)],
        compiler_params=_cparams(3),
        name="sb",
    )(y3d, y3d, y3d)


LRU_TC = 256
LRU_HALO = 8


def _lru_kernel(x_ref, g_ref, cw_ref, cb_ref, wa_ref, ba_ref, wx_ref, bx_ref, lam_ref,
                o_ref, xbuf_ref, h_ref, *, ts):
    @pl.when(pl.program_id(2) == 0)
    def _():
        xbuf_ref[0:LRU_HALO, :] = jnp.zeros((LRU_HALO, LRU_TC), F32)
        h_ref[...] = jnp.zeros_like(h_ref)

    xbuf_ref[LRU_HALO:LRU_HALO + ts, :] = x_ref[0].astype(F32)
    cw = cw_ref[...]
    xc = cb_ref[...] + xbuf_ref[LRU_HALO:LRU_HALO + ts, :] * cw[3:4, :]
    for d in range(1, 4):
        xc = xc + xbuf_ref[LRU_HALO - d:LRU_HALO - d + ts, :] * cw[3 - d:4 - d, :]
    xbuf_ref[0:LRU_HALO, :] = xbuf_ref[ts:ts + LRU_HALO, :]

    xcb = xc.astype(BF16)
    r = jax.nn.sigmoid(_dot(xcb, wa_ref[0]) + ba_ref[...])
    gi = jax.nn.sigmoid(_dot(xcb, wx_ref[0]) + bx_ref[...])
    lam = lam_ref[...]
    softplus_neg_lam = jnp.maximum(-lam, 0.0) + _log1p_exp_neg_abs(lam)
    log_a = -LRU_C * r * softplus_neg_lam
    a = jnp.exp(log_a)
    one_m_a2 = -jnp.tanh(log_a) * (a * a + 1.0)
    u = one_m_a2 * lax.rsqrt(jnp.maximum(one_m_a2, F32_TINY)) * (gi * xc)

    row = lax.broadcasted_iota(jnp.int32, (ts, LRU_TC), 0) % SUBLANES
    d = 1
    while d < SUBLANES:
        keep = row >= d
        a_sh = jnp.where(keep, pltpu.roll(a, d, axis=0), 1.0)
        u_sh = jnp.where(keep, pltpu.roll(u, d, axis=0), 0.0)
        u = a * u_sh + u
        a = a * a_sh
        d *= 2
    state = h_ref[...]
    groups = []
    for r0 in range(0, ts, SUBLANES):
        h = u[r0:r0 + SUBLANES, :] + a[r0:r0 + SUBLANES, :] * state
        groups.append(h)
        state = h[SUBLANES - 1:SUBLANES, :]
    h_ref[...] = state
    h = jnp.concatenate(groups, axis=0)
    o_ref[0] = (h * jax.nn.gelu(g_ref[0].astype(F32))).astype(BF16)


def _lru(y3d, cw, cb, wa_bd, ba, wx_bd, bx, lam, ts=512):
    b, s, _ = y3d.shape
    nct = D_MODEL // LRU_TC
    vec = lambda rows: pl.BlockSpec((rows, LRU_TC), lambda i, c, j: (0, c))
    mat = pl.BlockSpec((1, LRU_TC, LRU_TC), lambda i, c, j: (c, 0, 0))
    return pl.pallas_call(
        functools.partial(_lru_kernel, ts=ts),
        grid=(b, nct, s // ts),
        in_specs=[pl.BlockSpec((1, ts, LRU_TC), lambda i, c, j: (i, j, C_LX // LRU_TC + c)),
                  pl.BlockSpec((1, ts, LRU_TC), lambda i, c, j: (i, j, C_LG // LRU_TC + c)),
                  vec(4), vec(1), mat, vec(1), mat, vec(1), vec(1)],
        out_specs=pl.BlockSpec((1, ts, LRU_TC), lambda i, c, j: (i, j, c)),
        out_shape=jax.ShapeDtypeStruct((b, s, D_MODEL), BF16),
        scratch_shapes=[pltpu.VMEM((ts + LRU_HALO, LRU_TC), F32),
                        pltpu.VMEM((1, LRU_TC), F32)],
        compiler_params=_cparams(3),
        name="lru",
    )(y3d, y3d, cw, cb, wa_bd, ba, wx_bd, bx, lam)


def _memkv_kernel(mem_ref, g_ref, w_ref, gk_ref, k_ref, v_ref):
    x = mem_ref[0]
    h = x * lax.rsqrt(jnp.mean(x * x, axis=-1, keepdims=True) + EPS) * g_ref[...]
    kv = _dot(h.astype(BF16), w_ref[...])
    for hd in range(MEM_HEADS):
        kh = kv[:, hd * MEM_HD:(hd + 1) * MEM_HD]
        kh = kh * lax.rsqrt(jnp.mean(kh * kh, axis=-1, keepdims=True) + EPS) * gk_ref[...]
        k_ref[0, :, hd * MEM_HD:(hd + 1) * MEM_HD] = kh.astype(BF16)
    v_ref[0] = kv[:, D_MODEL:].astype(BF16)


def _memkv(mem, g, w_kv, gk):
    b = mem.shape[0]
    out = jax.ShapeDtypeStruct((b, N_MEM, D_MODEL), BF16)
    return pl.pallas_call(
        _memkv_kernel,
        grid=(b,),
        in_specs=[pl.BlockSpec((1, N_MEM, D_MODEL), lambda i: (i, 0, 0)),
                  pl.BlockSpec((1, D_MODEL), lambda i: (0, 0)),
                  pl.BlockSpec((D_MODEL, 2 * D_MODEL), lambda i: (0, 0)),
                  pl.BlockSpec((1, MEM_HD), lambda i: (0, 0))],
        out_specs=[pl.BlockSpec((1, N_MEM, D_MODEL), lambda i: (i, 0, 0)),
                   pl.BlockSpec((1, N_MEM, D_MODEL), lambda i: (i, 0, 0))],
        out_shape=[out, out],
        compiler_params=_cparams(1),
        name="memkv",
    )(mem, g, w_kv, gk)


def _mem_kernel(q_ref, k_ref, v_ref, gq_ref, o_ref):
    for hd in range(MEM_HEADS):
        sl = slice(hd * MEM_HD, (hd + 1) * MEM_HD)
        q = q_ref[0, :, sl].astype(F32)
        q = q * lax.rsqrt(jnp.mean(q * q, axis=-1, keepdims=True) + EPS) * gq_ref[...]
        q = (q * (MEM_HD ** -0.5)).astype(BF16)
        s = _dot_nt(q, k_ref[0, :, sl])
        p = jnp.exp(s - jnp.max(s, axis=-1, keepdims=True))
        o = _dot(p.astype(BF16), v_ref[0, :, sl]) / jnp.sum(p, axis=-1, keepdims=True)
        o_ref[0, :, sl] = o.astype(BF16)


def _mem(y3d, mk, mv, gq, ts=512):
    b, s, _ = y3d.shape
    return pl.pallas_call(
        _mem_kernel,
        grid=(b, s // ts),
        in_specs=[pl.BlockSpec((1, ts, D_MODEL), lambda i, j: (i, j, C_MQ // D_MODEL)),
                  pl.BlockSpec((1, N_MEM, D_MODEL), lambda i, j: (i, 0, 0)),
                  pl.BlockSpec((1, N_MEM, D_MODEL), lambda i, j: (i, 0, 0)),
                  pl.BlockSpec((1, MEM_HD), lambda i, j: (0, 0))],
        out_specs=pl.BlockSpec((1, ts, D_MODEL), lambda i, j: (i, j, 0)),
        out_shape=jax.ShapeDtypeStruct((b, s, D_MODEL), BF16),
        compiler_params=_cparams(2),
        name="mem",
    )(y3d, mk, mv, gq)


def _merge_kernel(x_ref, b0_ref, b1_ref, b2_ref, b3_ref, gt_ref, bg_ref, wb_ref, wo_ref, o_ref):
    mixed = None
    for n, br in enumerate((b0_ref, b1_ref, b2_ref, b3_ref)):
        gate = jax.nn.sigmoid(
            gt_ref[:, n * D_MODEL:(n + 1) * D_MODEL].astype(F32) + bg_ref[n:n + 1, :])
        term = gate * _dot(br[...], wb_ref[n])
        mixed = term if mixed is None else mixed + term
    o_ref[...] = x_ref[...] + _dot(mixed.astype(BF16), wo_ref[...])


def _merge(x2d, branches, y2d, bg, wb, wo, layer, tm=256):
    m = x2d.shape[0]
    row = lambda w: pl.BlockSpec((tm, w), lambda i: (i, 0))
    return pl.pallas_call(
        _merge_kernel,
        grid=(m // tm,),
        in_specs=[row(D_MODEL), row(D_MODEL), row(D_MODEL), row(D_MODEL), row(D_MODEL),
                  pl.BlockSpec((tm, 4 * D_MODEL), lambda i: (i, C_GT // (4 * D_MODEL))),
                  pl.BlockSpec((4, D_MODEL), lambda i: (0, 0)),
                  pl.BlockSpec((None, 4, D_MODEL, D_MODEL), lambda i: (layer, 0, 0, 0)),
                  pl.BlockSpec((None, D_MODEL, D_MODEL), lambda i: (layer, 0, 0))],
        out_specs=row(D_MODEL),
        out_shape=jax.ShapeDtypeStruct((m, D_MODEL), F32),
        compiler_params=_cparams(1),
        name="merge",
    )(x2d, *branches, y2d, bg, wb, wo)


FFN_TC = 256
FFN_HALO = 8
FFN_RB = 64


def _ffn_kernel(x_ref, g_ref, wg_ref, wv_ref, cw_ref, cb_ref, wd_ref, o_ref,
                h_ref, gbuf_ref, vbuf_ref, act_ref, *, tm, tiles_per_seq):
    i = pl.program_id(0)
    x = x_ref[...]
    h_ref[...] = (x * lax.rsqrt(jnp.mean(x * x, axis=-1, keepdims=True) + EPS)
                  * g_ref[...]).astype(BF16)

    @pl.when((i % tiles_per_seq) == 0)
    def _():
        gbuf_ref[0:FFN_HALO, :] = jnp.zeros((FFN_HALO, D_FF), F32)

    def up(c):
        cols = slice(c * FFN_TC, (c + 1) * FFN_TC)
        gbuf_ref[FFN_HALO:FFN_HALO + tm, cols] = _dot(h_ref[...], wg_ref[:, cols])
        vbuf_ref[:, cols] = _dot(h_ref[...], wv_ref[:, cols])

    def activate(c):
        cols = slice(c * FFN_TC, (c + 1) * FFN_TC)
        cw = cw_ref[:, cols]
        cb = cb_ref[:, cols]
        for r0 in range(0, tm, FFN_RB):
            gate = cb
            for d in range(3):
                lo = FFN_HALO + r0 - d
                gate = gate + gbuf_ref[lo:lo + FFN_RB, cols] * cw[2 - d:3 - d, :]
            act = gate * jax.nn.sigmoid(gate) * vbuf_ref[r0:r0 + FFN_RB, cols]
            act_ref[r0:r0 + FFN_RB, cols] = act.astype(BF16)

    nc = D_FF // FFN_TC
    up(0)
    for c in range(1, nc):
        up(c)
        activate(c - 1)
    activate(nc - 1)
    o_ref[...] = x_ref[...] + _dot(act_ref[...], wd_ref[...])
    gbuf_ref[0:FFN_HALO, :] = gbuf_ref[tm:tm + FFN_HALO, :]


def _ffn(x2d, g, w_up, cw, cb, w_down, layer, seq, tm=512):
    m = x2d.shape[0]
    resident = pl.Buffered(1)
    return pl.pallas_call(
        functools.partial(_ffn_kernel, tm=tm, tiles_per_seq=seq // tm),
        grid=(m // tm,),
        in_specs=[pl.BlockSpec((tm, D_MODEL), lambda i: (i, 0)),
                  pl.BlockSpec((1, D_MODEL), lambda i: (0, 0)),
                  pl.BlockSpec((None, D_MODEL, D_FF), lambda i: (layer, 0, 0),
                               pipeline_mode=resident),
                  pl.BlockSpec((None, D_MODEL, D_FF), lambda i: (layer, 0, 1),
                               pipeline_mode=resident),
                  pl.BlockSpec((3, D_FF), lambda i: (0, 0)),
                  pl.BlockSpec((1, D_FF), lambda i: (0, 0)),
                  pl.BlockSpec((None, D_FF, D_MODEL), lambda i: (layer, 0, 0),
                               pipeline_mode=resident)],
        out_specs=pl.BlockSpec((tm, D_MODEL), lambda i: (i, 0)),
        out_shape=jax.ShapeDtypeStruct((m, D_MODEL), F32),
        scratch_shapes=[pltpu.VMEM((tm, D_MODEL), BF16),
                        pltpu.VMEM((tm + FFN_HALO, D_FF), F32),
                        pltpu.VMEM((tm, D_FF), F32),
                        pltpu.VMEM((tm, D_FF), BF16)],
        compiler_params=_cparams(1),
        name="ffn",
    )(x2d, g, w_up, w_up, cw, cb, w_down)


def _block_diag(w):
    per = LRU_TC // HEAD_DIM
    w4 = w.reshape(D_MODEL // LRU_TC, per, HEAD_DIM, HEAD_DIM)
    eye = jnp.eye(per, dtype=w.dtype)
    bd = jnp.einsum('cpde,pq->cpdqe', w4, eye)
    return bd.reshape(D_MODEL // LRU_TC, LRU_TC, LRU_TC)


def kernel(x, mem, attn_norm_g, mem_norm_g, w_in, b_forget, fox_q_norm_g, fox_k_norm_g,
           lru_conv_w, lru_conv_b, lru_w_a, lru_b_a, lru_w_x, lru_b_x, lru_lambda,
           w_mem_kv, mem_q_norm_g, mem_k_norm_g, b_gate, w_branch, w_out,
           ffn_norm_g, w_up, ffn_conv_w, ffn_conv_b, w_down):
    b, s, d = x.shape
    depth = w_in.shape[0]
    m = b * s
    x2d = x.reshape(m, d)
    w_branch_bf, w_out_bf = w_branch.astype(BF16), w_out.astype(BF16)
    w_up_bf, w_down_bf = w_up.astype(BF16), w_down.astype(BF16)
    for l in range(depth):
        wy = jnp.concatenate([w_in[l, :, W_GATES:], w_in[l, :, :W_FOX_END],
                              w_in[l, :, W_F_END:W_GATES]], axis=1)
        sb_q_scale = jnp.where((jnp.arange(N_Y) >= C_SQ) & (jnp.arange(N_Y) < C_SK),
                               HEAD_DIM ** -0.5 * LOG2E, 1.0).astype(F32)
        wy = (wy * sb_q_scale).astype(BF16)
        wf = jnp.pad(w_in[l, :, W_FOX_END:W_F_END], ((0, 0), (0, LANES - N_HEADS)))
        wf_hi = wf.astype(BF16)
        wf = jnp.stack([wf_hi, (wf - wf_hi.astype(F32)).astype(BF16)])
        bf = jnp.pad(b_forget[l], (0, LANES - N_HEADS)).reshape(1, LANES)

        y2d, f2d = _proj(x2d, attn_norm_g[l].reshape(1, d), wy, wf)
        y3d = y2d.reshape(b, s, N_Y)

        fcum = _fcum(f2d.reshape(b, s, LANES), bf)
        qa, ka, vt = _foxprep(y3d, fcum,
                              jnp.tile(fox_q_norm_g[l], 2).reshape(1, LANES),
                              jnp.tile(fox_k_norm_g[l], 2).reshape(1, LANES))
        y_fox = _fox(qa, ka, vt)
        y_sb = _sb(y3d)
        y_lru = _lru(y3d, lru_conv_w[l], lru_conv_b[l].reshape(1, d),
                     _block_diag(lru_w_a[l]).astype(BF16), lru_b_a[l].reshape(1, d),
                     _block_diag(lru_w_x[l]).astype(BF16), lru_b_x[l].reshape(1, d),
                     lru_lambda[l].reshape(1, d))
        mk, mv = _memkv(mem, mem_norm_g[l].reshape(1, d), w_mem_kv[l].astype(BF16),
                        mem_k_norm_g[l].reshape(1, MEM_HD))
        y_mem = _mem(y3d, mk, mv, mem_q_norm_g[l].reshape(1, MEM_HD))

        branches = [t.reshape(m, d) for t in (y_fox, y_lru, y_sb, y_mem)]
        x2d = _merge(x2d, branches, y2d, b_gate[l], w_branch_bf, w_out_bf, l)
        x2d = _ffn(x2d, ffn_norm_g[l].reshape(1, d), w_up_bf, ffn_conv_w[l],
                   ffn_conv_b[l].reshape(1, D_FF), w_down_bf, l, s)
    return x2d.reshape(b, s, d)
```

```python
import functools

import jax
import jax.numpy as jnp
from jax import lax
from jax.experimental import pallas as pl
from jax.experimental.pallas import tpu as pltpu

F32 = jnp.float32
BF16 = jnp.bfloat16

D_MODEL = 1024
HEAD_DIM = 64
N_HEADS = 16
N_PAIRS = N_HEADS // 2
LANES = 128
SUBLANES = 8
N_MEM = 256
MEM_HEADS = 4
MEM_HD = 256
D_FF = 2816
LRU_C = 8.0
EPS = 1e-6
NEG = -1e30
LOG2E = 1.4426950408889634
F32_TINY = 1.1754944e-38

C_GT, C_FQ, C_FK, C_FV, C_LX, C_LG, C_SQ, C_SK, C_SV, C_MQ = (
    0, 4096, 5120, 6144, 7168, 8192, 9216, 10240, 11264, 12288)
N_Y = 13312
W_FOX_END, W_F_END, W_GATES = 3072, 3088, 9232

VMEM_LIMIT = 56 * 1024 * 1024


def _cparams(n_axes):
    return pltpu.CompilerParams(dimension_semantics=("arbitrary",) * n_axes,
                                vmem_limit_bytes=VMEM_LIMIT)


def _dot(a, b):
    return jnp.dot(a, b, preferred_element_type=F32)


def _dot_nt(a, b):
    return lax.dot_general(a, b, (((1,), (1,)), ((), ())), preferred_element_type=F32)


def _log1p_exp_neg_abs(z):
    return jnp.log(1.0 + jnp.exp(-jnp.abs(z)))


def _proj_kernel(x_ref, g_ref, w_ref, wf_ref, y_ref, f_ref, h_ref):
    @pl.when(pl.program_id(1) == 0)
    def _():
        x = x_ref[...]
        h = x * lax.rsqrt(jnp.mean(x * x, axis=-1, keepdims=True) + EPS) * g_ref[...]
        h_hi = h.astype(BF16)
        h_ref[...] = h_hi
        h_lo = (h - h_hi.astype(F32)).astype(BF16)
        f_ref[...] = (_dot(h_hi, wf_ref[0]) + _dot(h_lo, wf_ref[0])) + _dot(h_hi, wf_ref[1])

    y_ref[...] = _dot(h_ref[...], w_ref[...]).astype(BF16)


def _proj(x2d, g, wy, wf, tm=2048, tn=1024):
    m = x2d.shape[0]
    return pl.pallas_call(
        _proj_kernel,
        grid=(m // tm, N_Y // tn),
        in_specs=[pl.BlockSpec((tm, D_MODEL), lambda i, j: (i, 0)),
                  pl.BlockSpec((1, D_MODEL), lambda i, j: (0, 0)),
                  pl.BlockSpec((D_MODEL, tn), lambda i, j: (0, j)),
                  pl.BlockSpec((2, D_MODEL, LANES), lambda i, j: (0, 0, 0))],
        out_specs=[pl.BlockSpec((tm, tn), lambda i, j: (i, j)),
                   pl.BlockSpec((tm, LANES), lambda i, j: (i, 0))],
        out_shape=[jax.ShapeDtypeStruct((m, N_Y), BF16),
                   jax.ShapeDtypeStruct((m, LANES), F32)],
        scratch_shapes=[pltpu.VMEM((tm, D_MODEL), BF16)],
        compiler_params=_cparams(2),
        name="proj",
    )(x2d, g, wy, wf)


F_PIECES = 3
FOX_TS = 512
FOX_KB = 64
FOX_ONES = 16
FOX_VROWS = HEAD_DIM + FOX_ONES


def _fcum_kernel(f_ref, b_ref, o_ref, carry_ref):
    @pl.when(pl.program_id(1) == 0)
    def _():
        carry_ref[...] = jnp.zeros_like(carry_ref)

    z = f_ref[0] + b_ref[...]
    log_f = jnp.minimum(z, 0.0) - _log1p_exp_neg_abs(z)
    ts = z.shape[0]
    row = lax.broadcasted_iota(jnp.int32, (ts, ts), 0)
    col = lax.broadcasted_iota(jnp.int32, (ts, ts), 1)
    tri = (row >= col).astype(BF16)
    hi = log_f.astype(BF16)
    r1 = log_f - hi.astype(F32)
    mid = r1.astype(BF16)
    lo = (r1 - mid.astype(F32)).astype(BF16)
    c = (_dot(tri, hi) + _dot(tri, mid)) + _dot(tri, lo) + carry_ref[...]
    carry_ref[...] = c[ts - 1:ts, :]

    lane = lax.broadcasted_iota(jnp.int32, (ts, LANES), 1)
    f = jnp.where(lane < N_HEADS, c * LOG2E, 0.0)
    hi = f.astype(BF16).astype(F32)
    r1 = f - hi
    mid = r1.astype(BF16).astype(F32)
    lo = (r1 - mid).astype(BF16).astype(F32)
    packed = hi + pltpu.roll(mid, N_HEADS, axis=1) + pltpu.roll(lo, 2 * N_HEADS, axis=1)
    o_ref[0] = packed.astype(BF16)


def _fcum(f3d, b_pad, ts=512):
    b, s, _ = f3d.shape
    return pl.pallas_call(
        _fcum_kernel,
        grid=(b, s // ts),
        in_specs=[pl.BlockSpec((1, ts, LANES), lambda i, j: (i, j, 0)),
                  pl.BlockSpec((1, LANES), lambda i, j: (0, 0))],
        out_specs=pl.BlockSpec((1, ts, LANES), lambda i, j: (i, j, 0)),
        out_shape=jax.ShapeDtypeStruct((b, s, LANES), BF16),
        scratch_shapes=[pltpu.VMEM((1, LANES), F32)],
        compiler_params=_cparams(2),
        name="fcum",
    )(f3d, b_pad)


def _foxprep_kernel(yq_ref, yk_ref, yv_ref, f_ref, gq_ref, gk_ref, qa_ref, ka_ref, vt_ref):
    hp = pl.program_id(1)
    ones = jnp.ones((FOX_ONES, FOX_TS), BF16)
    for c in range(vt_ref.shape[2]):
        vt = yv_ref[0, c * FOX_TS:(c + 1) * FOX_TS, :].astype(F32).T.astype(BF16)
        vt_ref[0, 0, c] = jnp.concatenate(
            [vt[:HEAD_DIM], ones, vt[HEAD_DIM:], ones], axis=0)
    ts = yq_ref.shape[1]
    lane = lax.broadcasted_iota(jnp.int32, (ts, LANES), 1)
    lo_half = lane < HEAD_DIM

    hr = lax.broadcasted_iota(jnp.int32, (LANES, LANES), 0)
    hc = lax.broadcasted_iota(jnp.int32, (LANES, LANES), 1)
    same_head = ((hr < HEAD_DIM) == (hc < HEAD_DIM)).astype(BF16)

    def headnorm(y, g):
        ms = _dot((y * y).astype(BF16), same_head) * (1.0 / HEAD_DIM)
        return y * lax.rsqrt(ms + EPS) * g

    qn = headnorm(yq_ref[0].astype(F32), gq_ref[...]) * (HEAD_DIM ** -0.5 * LOG2E)
    kn = headnorm(yk_ref[0].astype(F32), gk_ref[...])

    pieces = f_ref[0]
    sel_r = lax.broadcasted_iota(jnp.int32, (LANES, 2 * LANES), 0)
    sel_c = lax.broadcasted_iota(jnp.int32, (LANES, 2 * LANES), 1)
    ones_q = ((lane >= HEAD_DIM + F_PIECES) & (lane < HEAD_DIM + 2 * F_PIECES)).astype(F32)
    ones_k = ((lane >= HEAD_DIM) & (lane < HEAD_DIM + F_PIECES)).astype(F32)
    for e in range(2):
        head = 2 * hp + e
        piece_of_row = sel_r - head
        to_q = (piece_of_row == (sel_c - HEAD_DIM) * N_HEADS) & (sel_c < HEAD_DIM + F_PIECES)
        k_col0 = LANES + HEAD_DIM + F_PIECES
        to_k = (piece_of_row == (sel_c - k_col0) * N_HEADS) & (sel_c >= k_col0)
        valid = (piece_of_row >= 0) & (piece_of_row < F_PIECES * N_HEADS)
        sel = (jnp.where(valid & to_q, 1.0, 0.0) - jnp.where(valid & to_k, 1.0, 0.0)).astype(BF16)
        placed = _dot(pieces, sel)
        q_part = qn if e == 0 else pltpu.roll(qn, HEAD_DIM, axis=1)
        k_part = kn if e == 0 else pltpu.roll(kn, HEAD_DIM, axis=1)
        q_aug = jnp.where(lo_half, q_part, placed[:, :LANES] + ones_q)
        k_aug = jnp.where(lo_half, k_part, placed[:, LANES:] + ones_k)
        qa_ref[0, e] = q_aug.astype(BF16)
        ka_ref[0, e] = k_aug.astype(BF16)


def _foxprep(y3d, fcum, gq2, gk2, ts=2048):
    b, s, _ = y3d.shape
    out = jax.ShapeDtypeStruct((b, N_HEADS, s, LANES), BF16)
    vt = jax.ShapeDtypeStruct((b, N_PAIRS, s // FOX_TS, 2 * FOX_VROWS, FOX_TS), BF16)
    return pl.pallas_call(
        _foxprep_kernel,
        grid=(b, N_PAIRS, s // ts),
        in_specs=[pl.BlockSpec((1, ts, LANES), lambda i, p, j: (i, j, C_FQ // LANES + p)),
                  pl.BlockSpec((1, ts, LANES), lambda i, p, j: (i, j, C_FK // LANES + p)),
                  pl.BlockSpec((1, ts, LANES), lambda i, p, j: (i, j, C_FV // LANES + p)),
                  pl.BlockSpec((1, ts, LANES), lambda i, p, j: (i, j, 0)),
                  pl.BlockSpec((1, LANES), lambda i, p, j: (0, 0)),
                  pl.BlockSpec((1, LANES), lambda i, p, j: (0, 0))],
        out_specs=[pl.BlockSpec((1, 2, ts, LANES), lambda i, p, j: (i, p, j, 0)),
                   pl.BlockSpec((1, 2, ts, LANES), lambda i, p, j: (i, p, j, 0)),
                   pl.BlockSpec((1, 1, ts // FOX_TS, 2 * FOX_VROWS, FOX_TS),
                                lambda i, p, j: (i, p, j, 0, 0))],
        out_shape=[out, out, vt],
        compiler_params=_cparams(3),
        name="foxprep",
    )(y3d, y3d, y3d, fcum, gq2, gk2)


def _fox_kernel(qa_ref, ka_ref, vt_ref, o_ref, s_ref, p_ref, m_ref, a_ref, acc_ref, *, tq):
    i = pl.program_id(2)
    ts = FOX_TS
    nslab = tq // ts
    chains = [(h, e) for h in range(nslab) for e in range(2)]
    key = lax.broadcasted_iota(jnp.int32, (FOX_KB, LANES), 0)
    query = lax.broadcasted_iota(jnp.int32, (FOX_KB, LANES), 1)

    m_ref[...] = jnp.full(m_ref.shape, NEG, F32)
    acc_ref[...] = jnp.zeros(acc_ref.shape, F32)

    def logits(n, block):
        h, e = chains[n]
        start = pl.multiple_of(block * ts, ts)
        s = _dot_nt(ka_ref[0, e, pl.ds(start, ts), :], qa_ref[0, e, h * ts:(h + 1) * ts, :])
        for c in range(ts // LANES):
            s_ref[n, c, 0:ts, :] = s[:, c * LANES:(c + 1) * LANES]

    def softmax(n, diagonal):
        kb = FOX_KB
        strips = range(0, ts, LANES)

        def piece(c0, k0):
            s = s_ref[n, c0 // LANES, k0:k0 + kb, :]
            if diagonal and k0 + kb > c0:
                s = jnp.where(key + k0 <= query + c0, s, NEG)
            return s

        def last_key(c0):
            return c0 + LANES if diagonal else ts

        m_new = {}
        for c0 in strips:
            cols = slice(c0, c0 + LANES)
            top = piece(c0, 0)
            for k0 in range(kb, last_key(c0), kb):
                top = jnp.maximum(top, piece(c0, k0))
            m_old = m_ref[n, :, cols]
            m_new[c0] = jnp.maximum(m_old, jnp.max(top, axis=0, keepdims=True))
            m_ref[n, :, cols] = m_new[c0]
            a_ref[n, :, cols] = jnp.exp2(m_old - m_new[c0])
        for c0 in strips:
            for k0 in range(0, last_key(c0), kb):
                p_ref[n, c0 // LANES, k0:k0 + kb, :] = jnp.exp2(
                    (piece(c0, k0) - m_new[c0][0:1, :]).astype(BF16))
            if last_key(c0) < ts:
                p_ref[n, c0 // LANES, last_key(c0):ts, :] = jnp.zeros(
                    (ts - last_key(c0), LANES), BF16)

    def accumulate(n, block):
        e = chains[n][1]
        vt = vt_ref[0, 0, block, e * FOX_VROWS:(e + 1) * FOX_VROWS, :]
        p = jnp.concatenate([p_ref[n, c, 0:ts, :] for c in range(ts // LANES)], axis=1)
        acc_ref[n] = a_ref[n, 0:1, :] * acc_ref[n] + _dot(vt, p)

    def step(block, live, diagonal_slab, live_next):
        prev = None
        for n in live:
            softmax(n, chains[n][0] == diagonal_slab)
            if n in live_next:
                logits(n, block + 1)
            if prev is not None:
                accumulate(prev, block)
            prev = n
        accumulate(prev, block)

    everyone = list(range(len(chains)))
    for n in everyone:
        logits(n, 0)

    def body(j, _):
        step(j, everyone, None, everyone)
        return 0

    lax.fori_loop(0, i * nslab, body, 0)
    for c in range(nslab):
        live = [n for n in everyone if chains[n][0] >= c]
        live_next = [n for n in everyone if chains[n][0] >= c + 1] if c + 1 < nslab else []
        step(i * nslab + c, live, c, live_next)

    for h in range(nslab):
        outs = [acc_ref[2 * h + e, 0:HEAD_DIM, :] / acc_ref[2 * h + e, HEAD_DIM:HEAD_DIM + 1, :]
                for e in range(2)]
        o_ref[0, h * ts:(h + 1) * ts, :] = jnp.concatenate(outs, axis=0).T.astype(BF16)


def _fox(qa, ka, vt, tq=2048):
    b, _, s, _ = qa.shape
    nch = 2 * (tq // FOX_TS)
    return pl.pallas_call(
        functools.partial(_fox_kernel, tq=tq),
        grid=(b, N_PAIRS, s // tq),
        in_specs=[pl.BlockSpec((1, 2, tq, LANES), lambda i, p, j: (i, p, j, 0)),
                  pl.BlockSpec((1, 2, s, LANES), lambda i, p, j: (i, p, 0, 0)),
                  pl.BlockSpec((1, 1, s // FOX_TS, 2 * FOX_VROWS, FOX_TS),
                               lambda i, p, j: (i, p, 0, 0, 0))],
        out_specs=pl.BlockSpec((1, tq, LANES), lambda i, p, j: (i, j, p)),
        out_shape=jax.ShapeDtypeStruct((b, s, D_MODEL), BF16),
        scratch_shapes=[pltpu.VMEM((nch, FOX_TS // LANES, FOX_TS + SUBLANES, LANES), F32),
                        pltpu.VMEM((nch, FOX_TS // LANES, FOX_TS + 2 * SUBLANES, LANES), BF16),
                        pltpu.VMEM((nch, SUBLANES, FOX_TS), F32),
                        pltpu.VMEM((nch, SUBLANES, FOX_TS), F32),
                        pltpu.VMEM((nch, FOX_VROWS, FOX_TS), F32)],
        compiler_params=_cparams(3),
        name="fox",
    )(qa, ka, vt)


SB_KC = 256
SB_TS = 512
SB_MAX_LOG2 = 126.0
SB_DEAD_LOG2 = -(SB_MAX_LOG2 + 150.0)


SB_RB = 64
SB_UNROLL = 1


def _sb_kernel(q_ref, k_ref, v_ref, o_ref, qm_ref, z_ref, lb_ref, ic_ref, w_ref,
               rs_ref, r_ref, acc_ref, *, tq):
    i = pl.program_id(2)
    kc, ts, rb = SB_KC, SB_TS, SB_RB
    nsub = tq // kc
    nslab = tq // ts
    chains = [(h, e) for h in range(nslab) for e in range(2)]
    row = lax.broadcasted_iota(jnp.int32, (rb, kc), 0)
    col = lax.broadcasted_iota(jnp.int32, (rb, kc), 1)
    kr = lax.broadcasted_iota(jnp.int32, (kc, kc), 0)
    kcol = lax.broadcasted_iota(jnp.int32, (kc, kc), 1)
    at_or_after = (kr >= kcol).astype(BF16)

    lane = lax.broadcasted_iota(jnp.int32, (ts, LANES), 1)
    for n, (h, e) in enumerate(chains):
        q2 = q_ref[0, h * ts:(h + 1) * ts, :]
        mine = (lane < HEAD_DIM) if e == 0 else (lane >= HEAD_DIM)
        qm_ref[n] = jnp.where(mine, q2, jnp.zeros_like(q2))
    r_ref[...] = jnp.zeros(r_ref.shape, F32)
    acc_ref[...] = jnp.zeros(acc_ref.shape, F32)

    def logits(n, start):
        z_ref[n] = jnp.minimum(_dot_nt(qm_ref[n], k_ref[0, pl.ds(start, kc), :]), SB_MAX_LOG2)

    def log_terms(n, offset):
        for r0 in range(0, ts, rb):
            rows = slice(r0, r0 + rb)
            log_1m_beta = jnp.log(1.0 + jnp.exp2(z_ref[n, rows, :])) * (-LOG2E)
            if offset is not None:
                log_1m_beta = jnp.where(col + (offset - r0) < row, log_1m_beta, 0.0)
            lb_ref[n, rows, :] = log_1m_beta.astype(BF16)
            rs_ref[n, rows, :] = jnp.broadcast_to(
                jnp.sum(log_1m_beta, axis=-1, keepdims=True), (rb, LANES))

    def cumulate(n):
        ic_ref[n] = _dot(lb_ref[n], at_or_after)

    def weights(n, offset):
        for r0 in range(0, ts, rb):
            rows = slice(r0, r0 + rb)
            r = r_ref[n, rows, :]
            log_w = (z_ref[n, rows, :] + ic_ref[n, rows, :]
                     + jnp.concatenate([r] * (kc // LANES), axis=1))
            if offset is not None:
                log_w = jnp.where(col + (offset - r0) < row, log_w, NEG)
            w_ref[n, rows, :] = jnp.exp2(log_w.astype(BF16))
            r_ref[n, rows, :] = r + rs_ref[n, rows, :]

    def accumulate(n, start):
        acc_ref[n] += _dot(w_ref[n], v_ref[0, pl.ds(start, kc), :])

    def chunk(start, next_start, live, offsets, live_next):
        prev = None
        for n in live:
            log_terms(n, offsets[chains[n][0]])
            cumulate(n)
            if prev is not None:
                weights(prev, offsets[chains[prev][0]])
                if prev in live_next:
                    logits(prev, next_start)
                accumulate(prev, start)
            prev = n
        weights(prev, offsets[chains[prev][0]])
        if prev in live_next:
            logits(prev, next_start)
        accumulate(prev, start)
        for n in live_next:
            if n not in live:
                logits(n, next_start)

    def visibility(c):
        live, offsets = [], {}
        for n, (h, _) in enumerate(chains):
            if c * kc >= (h + 1) * ts:
                continue
            live.append(n)
            offsets[h] = None if (c + 1) * kc <= h * ts else c * kc - h * ts
        return live, offsets

    everyone = list(range(len(chains)))
    no_offsets = {h: None for h in range(nslab)}
    live, offsets = visibility(nsub - 1)
    for n in live:
        logits(n, pl.multiple_of(i * tq + (nsub - 1) * kc, kc))
    for c in reversed(range(nsub)):
        start = pl.multiple_of(i * tq + c * kc, kc)
        live_next = visibility(c - 1)[0] if c > 0 else everyone
        chunk(start, pl.multiple_of(jnp.maximum(start - kc, 0), kc), live, offsets, live_next)
        if c > 0:
            live, offsets = visibility(c - 1)

    def alive(first_chain):
        return (jnp.max(r_ref[first_chain:]) > SB_DEAD_LOG2).astype(jnp.int32)

    trips = i * (nsub // SB_UNROLL)

    def walk(t0, live, first_chain):
        def body(state):
            t, _ = state
            for u in range(SB_UNROLL):
                start = pl.multiple_of(i * tq - (t * SB_UNROLL + u + 1) * kc, kc)
                chunk(start, pl.multiple_of(jnp.maximum(start - kc, 0), kc), live, no_offsets,
                      live)
            return t + 1, alive(first_chain)

        t_end, _ = lax.while_loop(lambda state: (state[0] < trips) & (state[1] > 0), body,
                                  (t0, alive(first_chain)))
        return t_end

    t_mid = walk(jnp.int32(0), everyone, len(chains) - 2)
    walk(t_mid, everyone[:2], 0)
    for h in range(nslab):
        o_ref[0, h * ts:(h + 1) * ts, :] = jnp.where(
            lane < HEAD_DIM, acc_ref[2 * h], acc_ref[2 * h + 1]).astype(BF16)


def _sb(y3d, tq=1024):
    b, s, _ = y3d.shape
    nch = 2 * (tq // SB_TS)
    return pl.pallas_call(
        functools.partial(_sb_kernel, tq=tq),
        grid=(b, N_PAIRS, s // tq),
        in_specs=[pl.BlockSpec((1, tq, LANES), lambda i, p, j: (i, j, C_SQ // LANES + p)),
                  pl.BlockSpec((1, s, LANES), lambda i, p, j: (i, 0, C_SK // LANES + p)),
                  pl.BlockSpec((1, s, LANES), lambda i, p, j: (i, 0, C_SV // LANES + p))],
        out_specs=pl.BlockSpec((1, tq, LANES), lambda i, p, j: (i, j, p)),
        out_shape=jax.ShapeDtypeStruct((b, s, D_MODEL), BF16),
        scratch_shapes=[pltpu.VMEM((nch, SB_TS, LANES), BF16),
                        pltpu.VMEM((nch, SB_TS, SB_KC), F32),
                        pltpu.VMEM((nch, SB_TS, SB_KC), BF16),
                        pltpu.VMEM((nch, SB_TS, SB_KC), F32),
                        pltpu.VMEM((nch, SB_TS, SB_KC), BF16),
                        pltpu.VMEM((nch, SB_TS, LANES), F32),
                        pltpu.VMEM((nch, SB_TS, LANES), F32),
                        pltpu.VMEM((nch, SB_TS, LANES), F32)],
        compiler_params=_cparams(3),
        name="sb",
    )(y3d, y3d, y3d)


LRU_TC = 256
LRU_HALO = 8


def _lru_kernel(x_ref, g_ref, cw_ref, cb_ref, wa_ref, ba_ref, wx_ref, bx_ref, lam_ref,
                o_ref, xbuf_ref, h_ref, *, ts):
    @pl.when(pl.program_id(2) == 0)
    def _():
        xbuf_ref[0:LRU_HALO, :] = jnp.zeros((LRU_HALO, LRU_TC), F32)
        h_ref[...] = jnp.zeros_like(h_ref)

    xbuf_ref[LRU_HALO:LRU_HALO + ts, :] = x_ref[0].astype(F32)
    cw = cw_ref[...]
    xc = cb_ref[...] + xbuf_ref[LRU_HALO:LRU_HALO + ts, :] * cw[3:4, :]
    for d in range(1, 4):
        xc = xc + xbuf_ref[LRU_HALO - d:LRU_HALO - d + ts, :] * cw[3 - d:4 - d, :]
    xbuf_ref[0:LRU_HALO, :] = xbuf_ref[ts:ts + LRU_HALO, :]

    xcb = xc.astype(BF16)
    r = jax.nn.sigmoid(_dot(xcb, wa_ref[0]) + ba_ref[...])
    gi = jax.nn.sigmoid(_dot(xcb, wx_ref[0]) + bx_ref[...])
    lam = lam_ref[...]
    softplus_neg_lam = jnp.maximum(-lam, 0.0) + _log1p_exp_neg_abs(lam)
    log_a = -LRU_C * r * softplus_neg_lam
    a = jnp.exp(log_a)
    one_m_a2 = -jnp.tanh(log_a) * (a * a + 1.0)
    u = one_m_a2 * lax.rsqrt(jnp.maximum(one_m_a2, F32_TINY)) * (gi * xc)

    row = lax.broadcasted_iota(jnp.int32, (ts, LRU_TC), 0) % SUBLANES
    d = 1
    while d < SUBLANES:
        keep = row >= d
        a_sh = jnp.where(keep, pltpu.roll(a, d, axis=0), 1.0)
        u_sh = jnp.where(keep, pltpu.roll(u, d, axis=0), 0.0)
        u = a * u_sh + u
        a = a * a_sh
        d *= 2
    state = h_ref[...]
    groups = []
    for r0 in range(0, ts, SUBLANES):
        h = u[r0:r0 + SUBLANES, :] + a[r0:r0 + SUBLANES, :] * state
        groups.append(h)
        state = h[SUBLANES - 1:SUBLANES, :]
    h_ref[...] = state
    h = jnp.concatenate(groups, axis=0)
    o_ref[0] = (h * jax.nn.gelu(g_ref[0].astype(F32))).astype(BF16)


def _lru(y3d, cw, cb, wa_bd, ba, wx_bd, bx, lam, ts=512):
    b, s, _ = y3d.shape
    nct = D_MODEL // LRU_TC
    vec = lambda rows: pl.BlockSpec((rows, LRU_TC), lambda i, c, j: (0, c))
    mat = pl.BlockSpec((1, LRU_TC, LRU_TC), lambda i, c, j: (c, 0, 0))
    return pl.pallas_call(
        functools.partial(_lru_kernel, ts=ts),
        grid=(b, nct, s // ts),
        in_specs=[pl.BlockSpec((1, ts, LRU_TC), lambda i, c, j: (i, j, C_LX // LRU_TC + c)),
                  pl.BlockSpec((1, ts, LRU_TC), lambda i, c, j: (i, j, C_LG // LRU_TC + c)),
                  vec(4), vec(1), mat, vec(1), mat, vec(1), vec(1)],
        out_specs=pl.BlockSpec((1, ts, LRU_TC), lambda i, c, j: (i, j, c)),
        out_shape=jax.ShapeDtypeStruct((b, s, D_MODEL), BF16),
        scratch_shapes=[pltpu.VMEM((ts + LRU_HALO, LRU_TC), F32),
                        pltpu.VMEM((1, LRU_TC), F32)],
        compiler_params=_cparams(3),
        name="lru",
    )(y3d, y3d, cw, cb, wa_bd, ba, wx_bd, bx, lam)


def _memkv_kernel(mem_ref, g_ref, w_ref, gk_ref, k_ref, v_ref):
    x = mem_ref[0]
    h = x * lax.rsqrt(jnp.mean(x * x, axis=-1, keepdims=True) + EPS) * g_ref[...]
    kv = _dot(h.astype(BF16), w_ref[...])
    for hd in range(MEM_HEADS):
        kh = kv[:, hd * MEM_HD:(hd + 1) * MEM_HD]
        kh = kh * lax.rsqrt(jnp.mean(kh * kh, axis=-1, keepdims=True) + EPS) * gk_ref[...]
        k_ref[0, :, hd * MEM_HD:(hd + 1) * MEM_HD] = kh.astype(BF16)
    v_ref[0] = kv[:, D_MODEL:].astype(BF16)


def _memkv(mem, g, w_kv, gk):
    b = mem.shape[0]
    out = jax.ShapeDtypeStruct((b, N_MEM, D_MODEL), BF16)
    return pl.pallas_call(
        _memkv_kernel,
        grid=(b,),
        in_specs=[pl.BlockSpec((1, N_MEM, D_MODEL), lambda i: (i, 0, 0)),
                  pl.BlockSpec((1, D_MODEL), lambda i: (0, 0)),
                  pl.BlockSpec((D_MODEL, 2 * D_MODEL), lambda i: (0, 0)),
                  pl.BlockSpec((1, MEM_HD), lambda i: (0, 0))],
        out_specs=[pl.BlockSpec((1, N_MEM, D_MODEL), lambda i: (i, 0, 0)),
                   pl.BlockSpec((1, N_MEM, D_MODEL), lambda i: (i, 0, 0))],
        out_shape=[out, out],
        compiler_params=_cparams(1),
        name="memkv",
    )(mem, g, w_kv, gk)


def _mem_kernel(q_ref, k_ref, v_ref, gq_ref, o_ref):
    for hd in range(MEM_HEADS):
        sl = slice(hd * MEM_HD, (hd + 1) * MEM_HD)
        q = q_ref[0, :, sl].astype(F32)
        q = q * lax.rsqrt(jnp.mean(q * q, axis=-1, keepdims=True) + EPS) * gq_ref[...]
        q = (q * (MEM_HD ** -0.5)).astype(BF16)
        s = _dot_nt(q, k_ref[0, :, sl])
        p = jnp.exp(s - jnp.max(s, axis=-1, keepdims=True))
        o = _dot(p.astype(BF16), v_ref[0, :, sl]) / jnp.sum(p, axis=-1, keepdims=True)
        o_ref[0, :, sl] = o.astype(BF16)


def _mem(y3d, mk, mv, gq, ts=512):
    b, s, _ = y3d.shape
    return pl.pallas_call(
        _mem_kernel,
        grid=(b, s // ts),
        in_specs=[pl.BlockSpec((1, ts, D_MODEL), lambda i, j: (i, j, C_MQ // D_MODEL)),
                  pl.BlockSpec((1, N_MEM, D_MODEL), lambda i, j: (i, 0, 0)),
                  pl.BlockSpec((1, N_MEM, D_MODEL), lambda i, j: (i, 0, 0)),
                  pl.BlockSpec((1, MEM_HD), lambda i, j: (0, 0))],
        out_specs=pl.BlockSpec((1, ts, D_MODEL), lambda i, j: (i, j, 0)),
        out_shape=jax.ShapeDtypeStruct((b, s, D_MODEL), BF16),
        compiler_params=_cparams(2),
        name="mem",
    )(y3d, mk, mv, gq)


def _merge_kernel(x_ref, b0_ref, b1_ref, b2_ref, b3_ref, gt_ref, bg_ref, wb_ref, wo_ref, o_ref):
    mixed = None
    for n, br in enumerate((b0_ref, b1_ref, b2_ref, b3_ref)):
        gate = jax.nn.sigmoid(
            gt_ref[:, n * D_MODEL:(n + 1) * D_MODEL].astype(F32) + bg_ref[n:n + 1, :])
        term = gate * _dot(br[...], wb_ref[n])
        mixed = term if mixed is None else mixed + term
    o_ref[...] = x_ref[...] + _dot(mixed.astype(BF16), wo_ref[...])


def _merge(x2d, branches, y2d, bg, wb, wo, layer, tm=256):
    m = x2d.shape[0]
    row = lambda w: pl.BlockSpec((tm, w), lambda i: (i, 0))
    return pl.pallas_call(
        _merge_kernel,
        grid=(m // tm,),
        in_specs=[row(D_MODEL), row(D_MODEL), row(D_MODEL), row(D_MODEL), row(D_MODEL),
                  pl.BlockSpec((tm, 4 * D_MODEL), lambda i: (i, C_GT // (4 * D_MODEL))),
                  pl.BlockSpec((4, D_MODEL), lambda i: (0, 0)),
                  pl.BlockSpec((None, 4, D_MODEL, D_MODEL), lambda i: (layer, 0, 0, 0)),
                  pl.BlockSpec((None, D_MODEL, D_MODEL), lambda i: (layer, 0, 0))],
        out_specs=row(D_MODEL),
        out_shape=jax.ShapeDtypeStruct((m, D_MODEL), F32),
        compiler_params=_cparams(1),
        name="merge",
    )(x2d, *branches, y2d, bg, wb, wo)


FFN_TC = 256
FFN_HALO = 8
FFN_RB = 64


def _ffn_kernel(x_ref, g_ref, wg_ref, wv_ref, cw_ref, cb_ref, wd_ref, o_ref,
                h_ref, gbuf_ref, vbuf_ref, act_ref, *, tm, tiles_per_seq):
    i = pl.program_id(0)
    x = x_ref[...]
    h_ref[...] = (x * lax.rsqrt(jnp.mean(x * x, axis=-1, keepdims=True) + EPS)
                  * g_ref[...]).astype(BF16)

    @pl.when((i % tiles_per_seq) == 0)
    def _():
        gbuf_ref[0:FFN_HALO, :] = jnp.zeros((FFN_HALO, D_FF), F32)

    def up(c):
        cols = slice(c * FFN_TC, (c + 1) * FFN_TC)
        gbuf_ref[FFN_HALO:FFN_HALO + tm, cols] = _dot(h_ref[...], wg_ref[:, cols])
        vbuf_ref[:, cols] = _dot(h_ref[...], wv_ref[:, cols])

    def activate(c):
        cols = slice(c * FFN_TC, (c + 1) * FFN_TC)
        cw = cw_ref[:, cols]
        cb = cb_ref[:, cols]
        for r0 in range(0, tm, FFN_RB):
            gate = cb
            for d in range(3):
                lo = FFN_HALO + r0 - d
                gate = gate + gbuf_ref[lo:lo + FFN_RB, cols] * cw[2 - d:3 - d, :]
            act = gate * jax.nn.sigmoid(gate) * vbuf_ref[r0:r0 + FFN_RB, cols]
            act_ref[r0:r0 + FFN_RB, cols] = act.astype(BF16)

    nc = D_FF // FFN_TC
    up(0)
    for c in range(1, nc):
        up(c)
        activate(c - 1)
    activate(nc - 1)
    o_ref[...] = x_ref[...] + _dot(act_ref[...], wd_ref[...])
    gbuf_ref[0:FFN_HALO, :] = gbuf_ref[tm:tm + FFN_HALO, :]


def _ffn(x2d, g, w_up, cw, cb, w_down, layer, seq, tm=512):
    m = x2d.shape[0]
    resident = pl.Buffered(1)
    return pl.pallas_call(
        functools.partial(_ffn_kernel, tm=tm, tiles_per_seq=seq // tm),
        grid=(m // tm,),
        in_specs=[pl.BlockSpec((tm, D_MODEL), lambda i: (i, 0)),
                  pl.BlockSpec((1, D_MODEL), lambda i: (0, 0)),
                  pl.BlockSpec((None, D_MODEL, D_FF), lambda i: (layer, 0, 0),
                               pipeline_mode=resident),
                  pl.BlockSpec((None, D_MODEL, D_FF), lambda i: (layer, 0, 1),
                               pipeline_mode=resident),
                  pl.BlockSpec((3, D_FF), lambda i: (0, 0)),
                  pl.BlockSpec((1, D_FF), lambda i: (0, 0)),
                  pl.BlockSpec((None, D_FF, D_MODEL), lambda i: (layer, 0, 0),
                               pipeline_mode=resident)],
        out_specs=pl.BlockSpec((tm, D_MODEL), lambda i: (i, 0)),
        out_shape=jax.ShapeDtypeStruct((m, D_MODEL), F32),
        scratch_shapes=[pltpu.VMEM((tm, D_MODEL), BF16),
                        pltpu.VMEM((tm + FFN_HALO, D_FF), F32),
                        pltpu.VMEM((tm, D_FF), F32),
                        pltpu.VMEM((tm, D_FF), BF16)],
        compiler_params=_cparams(1),
        name="ffn",
    )(x2d, g, w_up, w_up, cw, cb, w_down)


def _block_diag(w):
    per = LRU_TC // HEAD_DIM
    w4 = w.reshape(D_MODEL // LRU_TC, per, HEAD_DIM, HEAD_DIM)
    eye = jnp.eye(per, dtype=w.dtype)
    bd = jnp.einsum('cpde,pq->cpdqe', w4, eye)
    return bd.reshape(D_MODEL // LRU_TC, LRU_TC, LRU_TC)


def kernel(x, mem, attn_norm_g, mem_norm_g, w_in, b_forget, fox_q_norm_g, fox_k_norm_g,
           lru_conv_w, lru_conv_b, lru_w_a, lru_b_a, lru_w_x, lru_b_x, lru_lambda,
           w_mem_kv, mem_q_norm_g, mem_k_norm_g, b_gate, w_branch, w_out,
           ffn_norm_g, w_up, ffn_conv_w, ffn_conv_b, w_down):
    b, s, d = x.shape
    depth = w_in.shape[0]
    m = b * s
    x2d = x.reshape(m, d)
    w_branch_bf, w_out_bf = w_branch.astype(BF16), w_out.astype(BF16)
    w_up_bf, w_down_bf = w_up.astype(BF16), w_down.astype(BF16)
    for l in range(depth):
        wy = jnp.concatenate([w_in[l, :, W_GATES:], w_in[l, :, :W_FOX_END],
                              w_in[l, :, W_F_END:W_GATES]], axis=1)
        sb_q_scale = jnp.where((jnp.arange(N_Y) >= C_SQ) & (jnp.arange(N_Y) < C_SK),
                               HEAD_DIM ** -0.5 * LOG2E, 1.0).astype(F32)
        wy = (wy * sb_q_scale).astype(BF16)
        wf = jnp.pad(w_in[l, :, W_FOX_END:W_F_END], ((0, 0), (0, LANES - N_HEADS)))
        wf_hi = wf.astype(BF16)
        wf = jnp.stack([wf_hi, (wf - wf_hi.astype(F32)).astype(BF16)])
        bf = jnp.pad(b_forget[l], (0, LANES - N_HEADS)).reshape(1, LANES)

        y2d, f2d = _proj(x2d, attn_norm_g[l].reshape(1, d), wy, wf)
        y3d = y2d.reshape(b, s, N_Y)

        fcum = _fcum(f2d.reshape(b, s, LANES), bf)
        qa, ka, vt = _foxprep(y3d, fcum,
                              jnp.tile(fox_q_norm_g[l], 2).reshape(1, LANES),
                              jnp.tile(fox_k_norm_g[l], 2).reshape(1, LANES))
        y_fox = _fox(qa, ka, vt)
        y_sb = _sb(y3d)
        y_lru = _lru(y3d, lru_conv_w[l], lru_conv_b[l].reshape(1, d),
                     _block_diag(lru_w_a[l]).astype(BF16), lru_b_a[l].reshape(1, d),
                     _block_diag(lru_w_x[l]).astype(BF16), lru_b_x[l].reshape(1, d),
                     lru_lambda[l].reshape(1, d))
        mk, mv = _memkv(mem, mem_norm_g[l].reshape(1, d), w_mem_kv[l].astype(BF16),
                        mem_k_norm_g[l].reshape(1, MEM_HD))
        y_mem = _mem(y3d, mk, mv, mem_q_norm_g[l].reshape(1, MEM_HD))

        branches = [t.reshape(m, d) for t in (y_fox, y_lru, y_sb, y_mem)]
        x2d = _merge(x2d, branches, y2d, b_gate[l], w_branch_bf, w_out_bf, l)
        x2d = _ffn(x2d, ffn_norm_g[l].reshape(1, d), w_up_bf, ffn_conv_w[l],
                   ffn_conv_b[l].reshape(1, D_FF), w_down_bf, l, s)
    return x2d.reshape(b, s, d)
```

```python
import functools

import jax
import jax.numpy as jnp
from jax import lax
from jax.experimental import pallas as pl
from jax.experimental.pallas import tpu as pltpu

F32 = jnp.float32
BF16 = jnp.bfloat16

D_MODEL = 1024
HEAD_DIM = 64
N_HEADS = 16
N_PAIRS = N_HEADS // 2
LANES = 128
SUBLANES = 8
N_MEM = 256
MEM_HEADS = 4
MEM_HD = 256
D_FF = 2816
LRU_C = 8.0
EPS = 1e-6
NEG = -1e30
LOG2E = 1.4426950408889634
F32_TINY = 1.1754944e-38

C_GT, C_FQ, C_FK, C_FV, C_LX, C_LG, C_SQ, C_SK, C_SV, C_MQ = (
    0, 4096, 5120, 6144, 7168, 8192, 9216, 10240, 11264, 12288)
N_Y = 13312
W_FOX_END, W_F_END, W_GATES = 3072, 3088, 9232

VMEM_LIMIT = 56 * 1024 * 1024


def _cparams(n_axes):
    return pltpu.CompilerParams(dimension_semantics=("arbitrary",) * n_axes,
                                vmem_limit_bytes=VMEM_LIMIT)


def _dot(a, b):
    return jnp.dot(a, b, preferred_element_type=F32)


def _dot_nt(a, b):
    return lax.dot_general(a, b, (((1,), (1,)), ((), ())), preferred_element_type=F32)


def _log1p_exp_neg_abs(z):
    return jnp.log(1.0 + jnp.exp(-jnp.abs(z)))


def _wprep_kernel(w_ref, o_ref):
    o_ref[:, C_GT:C_FQ] = w_ref[:, W_GATES:].astype(BF16)
    o_ref[:, C_FQ:C_LX] = w_ref[:, :W_FOX_END].astype(BF16)
    shift = W_F_END - C_LX
    o_ref[:, C_LX:C_SQ] = w_ref[:, C_LX + shift:C_SQ + shift].astype(BF16)
    o_ref[:, C_SQ:C_SK] = (w_ref[:, C_SQ + shift:C_SK + shift]
                           * (HEAD_DIM ** -0.5 * LOG2E)).astype(BF16)
    o_ref[:, C_SK:N_Y] = w_ref[:, C_SK + shift:N_Y + shift].astype(BF16)


def _wprep(w_in, layer, tr=128):
    _, rows, cols = w_in.shape
    return pl.pallas_call(
        _wprep_kernel,
        grid=(rows // tr,),
        in_specs=[pl.BlockSpec((None, tr, cols), lambda i: (layer, i, 0))],
        out_specs=pl.BlockSpec((tr, N_Y), lambda i: (i, 0)),
        out_shape=jax.ShapeDtypeStruct((rows, N_Y), BF16),
        compiler_params=_cparams(1),
        name="wprep",
    )(w_in)


def _proj_kernel(x_ref, g_ref, w_ref, wf_ref, y_ref, f_ref, h_ref):
    @pl.when(pl.program_id(1) == 0)
    def _():
        x = x_ref[...]
        h = x * lax.rsqrt(jnp.mean(x * x, axis=-1, keepdims=True) + EPS) * g_ref[...]
        h_hi = h.astype(BF16)
        h_ref[...] = h_hi
        h_lo = (h - h_hi.astype(F32)).astype(BF16)
        f_ref[...] = (_dot(h_hi, wf_ref[0]) + _dot(h_lo, wf_ref[0])) + _dot(h_hi, wf_ref[1])

    y_ref[...] = _dot(h_ref[...], w_ref[...]).astype(BF16)


def _proj(x2d, g, wy, wf, tm=2048, tn=1024):
    m = x2d.shape[0]
    return pl.pallas_call(
        _proj_kernel,
        grid=(m // tm, N_Y // tn),
        in_specs=[pl.BlockSpec((tm, D_MODEL), lambda i, j: (i, 0)),
                  pl.BlockSpec((1, D_MODEL), lambda i, j: (0, 0)),
                  pl.BlockSpec((D_MODEL, tn), lambda i, j: (0, j)),
                  pl.BlockSpec((2, D_MODEL, LANES), lambda i, j: (0, 0, 0))],
        out_specs=[pl.BlockSpec((tm, tn), lambda i, j: (i, j)),
                   pl.BlockSpec((tm, LANES), lambda i, j: (i, 0))],
        out_shape=[jax.ShapeDtypeStruct((m, N_Y), BF16),
                   jax.ShapeDtypeStruct((m, LANES), F32)],
        scratch_shapes=[pltpu.VMEM((tm, D_MODEL), BF16)],
        compiler_params=_cparams(2),
        name="proj",
    )(x2d, g, wy, wf)


F_PIECES = 3
FOX_TS = 512
FOX_KB = 64
FOX_ONES = 16
FOX_VROWS = HEAD_DIM + FOX_ONES


def _fcum_kernel(f_ref, b_ref, o_ref, carry_ref):
    @pl.when(pl.program_id(1) == 0)
    def _():
        carry_ref[...] = jnp.zeros_like(carry_ref)

    z = f_ref[0] + b_ref[...]
    log_f = jnp.minimum(z, 0.0) - _log1p_exp_neg_abs(z)
    ts = z.shape[0]
    row = lax.broadcasted_iota(jnp.int32, (ts, ts), 0)
    col = lax.broadcasted_iota(jnp.int32, (ts, ts), 1)
    tri = (row >= col).astype(BF16)
    hi = log_f.astype(BF16)
    r1 = log_f - hi.astype(F32)
    mid = r1.astype(BF16)
    lo = (r1 - mid.astype(F32)).astype(BF16)
    c = (_dot(tri, hi) + _dot(tri, mid)) + _dot(tri, lo) + carry_ref[...]
    carry_ref[...] = c[ts - 1:ts, :]

    lane = lax.broadcasted_iota(jnp.int32, (ts, LANES), 1)
    f = jnp.where(lane < N_HEADS, c * LOG2E, 0.0)
    hi = f.astype(BF16).astype(F32)
    r1 = f - hi
    mid = r1.astype(BF16).astype(F32)
    lo = (r1 - mid).astype(BF16).astype(F32)
    packed = hi + pltpu.roll(mid, N_HEADS, axis=1) + pltpu.roll(lo, 2 * N_HEADS, axis=1)
    o_ref[0] = packed.astype(BF16)


def _fcum(f3d, b_pad, ts=512):
    b, s, _ = f3d.shape
    return pl.pallas_call(
        _fcum_kernel,
        grid=(b, s // ts),
        in_specs=[pl.BlockSpec((1, ts, LANES), lambda i, j: (i, j, 0)),
                  pl.BlockSpec((1, LANES), lambda i, j: (0, 0))],
        out_specs=pl.BlockSpec((1, ts, LANES), lambda i, j: (i, j, 0)),
        out_shape=jax.ShapeDtypeStruct((b, s, LANES), BF16),
        scratch_shapes=[pltpu.VMEM((1, LANES), F32)],
        compiler_params=_cparams(2),
        name="fcum",
    )(f3d, b_pad)


def _foxprep_kernel(yq_ref, yk_ref, yv_ref, f_ref, gq_ref, gk_ref, qa_ref, ka_ref, vt_ref):
    hp = pl.program_id(1)
    ones = jnp.ones((FOX_ONES, FOX_TS), BF16)
    for c in range(vt_ref.shape[2]):
        vt = yv_ref[0, c * FOX_TS:(c + 1) * FOX_TS, :].astype(F32).T.astype(BF16)
        vt_ref[0, 0, c] = jnp.concatenate(
            [vt[:HEAD_DIM], ones, vt[HEAD_DIM:], ones], axis=0)
    ts = yq_ref.shape[1]
    lane = lax.broadcasted_iota(jnp.int32, (ts, LANES), 1)
    lo_half = lane < HEAD_DIM

    hr = lax.broadcasted_iota(jnp.int32, (LANES, LANES), 0)
    hc = lax.broadcasted_iota(jnp.int32, (LANES, LANES), 1)
    same_head = ((hr < HEAD_DIM) == (hc < HEAD_DIM)).astype(BF16)

    def headnorm(y, g):
        ms = _dot((y * y).astype(BF16), same_head) * (1.0 / HEAD_DIM)
        return y * lax.rsqrt(ms + EPS) * g

    qn = headnorm(yq_ref[0].astype(F32), gq_ref[...]) * (HEAD_DIM ** -0.5 * LOG2E)
    kn = headnorm(yk_ref[0].astype(F32), gk_ref[...])

    pieces = f_ref[0]
    sel_r = lax.broadcasted_iota(jnp.int32, (LANES, 2 * LANES), 0)
    sel_c = lax.broadcasted_iota(jnp.int32, (LANES, 2 * LANES), 1)
    ones_q = ((lane >= HEAD_DIM + F_PIECES) & (lane < HEAD_DIM + 2 * F_PIECES)).astype(F32)
    ones_k = ((lane >= HEAD_DIM) & (lane < HEAD_DIM + F_PIECES)).astype(F32)
    for e in range(2):
        head = 2 * hp + e
        piece_of_row = sel_r - head
        to_q = (piece_of_row == (sel_c - HEAD_DIM) * N_HEADS) & (sel_c < HEAD_DIM + F_PIECES)
        k_col0 = LANES + HEAD_DIM + F_PIECES
        to_k = (piece_of_row == (sel_c - k_col0) * N_HEADS) & (sel_c >= k_col0)
        valid = (piece_of_row >= 0) & (piece_of_row < F_PIECES * N_HEADS)
        sel = (jnp.where(valid & to_q, 1.0, 0.0) - jnp.where(valid & to_k, 1.0, 0.0)).astype(BF16)
        placed = _dot(pieces, sel)
        q_part = qn if e == 0 else pltpu.roll(qn, HEAD_DIM, axis=1)
        k_part = kn if e == 0 else pltpu.roll(kn, HEAD_DIM, axis=1)
        q_aug = jnp.where(lo_half, q_part, placed[:, :LANES] + ones_q)
        k_aug = jnp.where(lo_half, k_part, placed[:, LANES:] + ones_k)
        qa_ref[0, e] = q_aug.astype(BF16)
        ka_ref[0, e] = k_aug.astype(BF16)


def _foxprep(y3d, fcum, gq2, gk2, ts=2048):
    b, s, _ = y3d.shape
    out = jax.ShapeDtypeStruct((b, N_HEADS, s, LANES), BF16)
    vt = jax.ShapeDtypeStruct((b, N_PAIRS, s // FOX_TS, 2 * FOX_VROWS, FOX_TS), BF16)
    return pl.pallas_call(
        _foxprep_kernel,
        grid=(b, N_PAIRS, s // ts),
        in_specs=[pl.BlockSpec((1, ts, LANES), lambda i, p, j: (i, j, C_FQ // LANES + p)),
                  pl.BlockSpec((1, ts, LANES), lambda i, p, j: (i, j, C_FK // LANES + p)),
                  pl.BlockSpec((1, ts, LANES), lambda i, p, j: (i, j, C_FV // LANES + p)),
                  pl.BlockSpec((1, ts, LANES), lambda i, p, j: (i, j, 0)),
                  pl.BlockSpec((1, LANES), lambda i, p, j: (0, 0)),
                  pl.BlockSpec((1, LANES), lambda i, p, j: (0, 0))],
        out_specs=[pl.BlockSpec((1, 2, ts, LANES), lambda i, p, j: (i, p, j, 0)),
                   pl.BlockSpec((1, 2, ts, LANES), lambda i, p, j: (i, p, j, 0)),
                   pl.BlockSpec((1, 1, ts // FOX_TS, 2 * FOX_VROWS, FOX_TS),
                                lambda i, p, j: (i, p, j, 0, 0))],
        out_shape=[out, out, vt],
        compiler_params=_cparams(3),
        name="foxprep",
    )(y3d, y3d, y3d, fcum, gq2, gk2)


def _fox_kernel(qa_ref, ka_ref, vt_ref, o_ref, s_ref, p_ref, m_ref, a_ref, acc_ref, *, tq):
    i = pl.program_id(2)
    ts = FOX_TS
    nslab = tq // ts
    chains = [(h, e) for h in range(nslab) for e in range(2)]
    key = lax.broadcasted_iota(jnp.int32, (FOX_KB, LANES), 0)
    query = lax.broadcasted_iota(jnp.int32, (FOX_KB, LANES), 1)

    m_ref[...] = jnp.full(m_ref.shape, NEG, F32)
    acc_ref[...] = jnp.zeros(acc_ref.shape, F32)

    def logits(n, block):
        h, e = chains[n]
        start = pl.multiple_of(block * ts, ts)
        s = _dot_nt(ka_ref[0, e, pl.ds(start, ts), :], qa_ref[0, e, h * ts:(h + 1) * ts, :])
        for c in range(ts // LANES):
            s_ref[n, c, 0:ts, :] = s[:, c * LANES:(c + 1) * LANES]

    def softmax(n, diagonal):
        kb = FOX_KB
        strips = range(0, ts, LANES)

        def piece(c0, k0):
            s = s_ref[n, c0 // LANES, k0:k0 + kb, :]
            if diagonal and k0 + kb > c0:
                s = jnp.where(key + k0 <= query + c0, s, NEG)
            return s

        def last_key(c0):
            return c0 + LANES if diagonal else ts

        m_new = {}
        for c0 in strips:
            cols = slice(c0, c0 + LANES)
            top = piece(c0, 0)
            for k0 in range(kb, last_key(c0), kb):
                top = jnp.maximum(top, piece(c0, k0))
            m_old = m_ref[n, :, cols]
            m_new[c0] = jnp.maximum(m_old, jnp.max(top, axis=0, keepdims=True))
            m_ref[n, :, cols] = m_new[c0]
            a_ref[n, :, cols] = jnp.exp2(m_old - m_new[c0])
        for c0 in strips:
            for k0 in range(0, last_key(c0), kb):
                p_ref[n, c0 // LANES, k0:k0 + kb, :] = jnp.exp2(
                    (piece(c0, k0) - m_new[c0][0:1, :]).astype(BF16))
            if last_key(c0) < ts:
                p_ref[n, c0 // LANES, last_key(c0):ts, :] = jnp.zeros(
                    (ts - last_key(c0), LANES), BF16)

    def accumulate(n, block):
        e = chains[n][1]
        vt = vt_ref[0, 0, block, e * FOX_VROWS:(e + 1) * FOX_VROWS, :]
        p = jnp.concatenate([p_ref[n, c, 0:ts, :] for c in range(ts // LANES)], axis=1)
        acc_ref[n] = a_ref[n, 0:1, :] * acc_ref[n] + _dot(vt, p)

    def step(block, live, diagonal_slab, live_next):
        prev = None
        for n in live:
            softmax(n, chains[n][0] == diagonal_slab)
            if n in live_next:
                logits(n, block + 1)
            if prev is not None:
                accumulate(prev, block)
            prev = n
        accumulate(prev, block)

    everyone = list(range(len(chains)))
    for n in everyone:
        logits(n, 0)

    def body(j, _):
        step(j, everyone, None, everyone)
        return 0

    lax.fori_loop(0, i * nslab, body, 0)
    for c in range(nslab):
        live = [n for n in everyone if chains[n][0] >= c]
        live_next = [n for n in everyone if chains[n][0] >= c + 1] if c + 1 < nslab else []
        step(i * nslab + c, live, c, live_next)

    for h in range(nslab):
        outs = [acc_ref[2 * h + e, 0:HEAD_DIM, :] / acc_ref[2 * h + e, HEAD_DIM:HEAD_DIM + 1, :]
                for e in range(2)]
        o_ref[0, h * ts:(h + 1) * ts, :] = jnp.concatenate(outs, axis=0).T.astype(BF16)


def _fox(qa, ka, vt, tq=2048):
    b, _, s, _ = qa.shape
    nch = 2 * (tq // FOX_TS)
    return pl.pallas_call(
        functools.partial(_fox_kernel, tq=tq),
        grid=(b, N_PAIRS, s // tq),
        in_specs=[pl.BlockSpec((1, 2, tq, LANES), lambda i, p, j: (i, p, j, 0)),
                  pl.BlockSpec((1, 2, s, LANES), lambda i, p, j: (i, p, 0, 0)),
                  pl.BlockSpec((1, 1, s // FOX_TS, 2 * FOX_VROWS, FOX_TS),
                               lambda i, p, j: (i, p, 0, 0, 0))],
        out_specs=pl.BlockSpec((1, tq, LANES), lambda i, p, j: (i, j, p)),
        out_shape=jax.ShapeDtypeStruct((b, s, D_MODEL), BF16),
        scratch_shapes=[pltpu.VMEM((nch, FOX_TS // LANES, FOX_TS + SUBLANES, LANES), F32),
                        pltpu.VMEM((nch, FOX_TS // LANES, FOX_TS + 2 * SUBLANES, LANES), BF16),
                        pltpu.VMEM((nch, SUBLANES, FOX_TS), F32),
                        pltpu.VMEM((nch, SUBLANES, FOX_TS), F32),
                        pltpu.VMEM((nch, FOX_VROWS, FOX_TS), F32)],
        compiler_params=_cparams(3),
        name="fox",
    )(qa, ka, vt)


SB_KC = 256
SB_TS = 512
SB_MAX_LOG2 = 126.0
SB_DEAD_LOG2 = -(SB_MAX_LOG2 + 150.0)


SB_RB = 64
SB_UNROLL = 1


def _sb_kernel(q_ref, k_ref, v_ref, o_ref, qm_ref, z_ref, lb_ref, ic_ref, w_ref,
               rs_ref, r_ref, acc_ref, *, tq):
    i = pl.program_id(2)
    kc, ts, rb = SB_KC, SB_TS, SB_RB
    nsub = tq // kc
    nslab = tq // ts
    chains = [(h, e) for h in range(nslab) for e in range(2)]
    row = lax.broadcasted_iota(jnp.int32, (rb, kc), 0)
    col = lax.broadcasted_iota(jnp.int32, (rb, kc), 1)
    kr = lax.broadcasted_iota(jnp.int32, (kc, kc), 0)
    kcol = lax.broadcasted_iota(jnp.int32, (kc, kc), 1)
    at_or_after = (kr >= kcol).astype(BF16)

    lane = lax.broadcasted_iota(jnp.int32, (ts, LANES), 1)
    for n, (h, e) in enumerate(chains):
        q2 = q_ref[0, h * ts:(h + 1) * ts, :]
        mine = (lane < HEAD_DIM) if e == 0 else (lane >= HEAD_DIM)
        qm_ref[n] = jnp.where(mine, q2, jnp.zeros_like(q2))
    r_ref[...] = jnp.zeros(r_ref.shape, F32)
    acc_ref[...] = jnp.zeros(acc_ref.shape, F32)

    def logits(n, start):
        z_ref[n] = jnp.minimum(_dot_nt(qm_ref[n], k_ref[0, pl.ds(start, kc), :]), SB_MAX_LOG2)

    def log_terms(n, offset):
        for r0 in range(0, ts, rb):
            rows = slice(r0, r0 + rb)
            log_1m_beta = jnp.log(1.0 + jnp.exp2(z_ref[n, rows, :])) * (-LOG2E)
            if offset is not None:
                log_1m_beta = jnp.where(col + (offset - r0) < row, log_1m_beta, 0.0)
            lb_ref[n, rows, :] = log_1m_beta.astype(BF16)
            rs_ref[n, rows, :] = jnp.broadcast_to(
                jnp.sum(log_1m_beta, axis=-1, keepdims=True), (rb, LANES))

    def cumulate(n):
        ic_ref[n] = _dot(lb_ref[n], at_or_after)

    def weights(n, offset):
        for r0 in range(0, ts, rb):
            rows = slice(r0, r0 + rb)
            r = r_ref[n, rows, :]
            log_w = (z_ref[n, rows, :] + ic_ref[n, rows, :]
                     + jnp.concatenate([r] * (kc // LANES), axis=1))
            if offset is not None:
                log_w = jnp.where(col + (offset - r0) < row, log_w, NEG)
            w_ref[n, rows, :] = jnp.exp2(log_w.astype(BF16))
            r_ref[n, rows, :] = r + rs_ref[n, rows, :]

    def accumulate(n, start):
        acc_ref[n] += _dot(w_ref[n], v_ref[0, pl.ds(start, kc), :])

    def chunk(start, next_start, live, offsets, live_next):
        prev = None
        for n in live:
            log_terms(n, offsets[chains[n][0]])
            cumulate(n)
            if prev is not None:
                weights(prev, offsets[chains[prev][0]])
                if prev in live_next:
                    logits(prev, next_start)
                accumulate(prev, start)
            prev = n
        weights(prev, offsets[chains[prev][0]])
        if prev in live_next:
            logits(prev, next_start)
        accumulate(prev, start)
        for n in live_next:
            if n not in live:
                logits(n, next_start)

    def visibility(c):
        live, offsets = [], {}
        for n, (h, _) in enumerate(chains):
            if c * kc >= (h + 1) * ts:
                continue
            live.append(n)
            offsets[h] = None if (c + 1) * kc <= h * ts else c * kc - h * ts
        return live, offsets

    everyone = list(range(len(chains)))
    no_offsets = {h: None for h in range(nslab)}
    live, offsets = visibility(nsub - 1)
    for n in live:
        logits(n, pl.multiple_of(i * tq + (nsub - 1) * kc, kc))
    for c in reversed(range(nsub)):
        start = pl.multiple_of(i * tq + c * kc, kc)
        live_next = visibility(c - 1)[0] if c > 0 else everyone
        chunk(start, pl.multiple_of(jnp.maximum(start - kc, 0), kc), live, offsets, live_next)
        if c > 0:
            live, offsets = visibility(c - 1)

    def alive(first_chain):
        return (jnp.max(r_ref[first_chain:]) > SB_DEAD_LOG2).astype(jnp.int32)

    trips = i * (nsub // SB_UNROLL)

    def walk(t0, live, first_chain):
        def body(state):
            t, _ = state
            for u in range(SB_UNROLL):
                start = pl.multiple_of(i * tq - (t * SB_UNROLL + u + 1) * kc, kc)
                chunk(start, pl.multiple_of(jnp.maximum(start - kc, 0), kc), live, no_offsets,
                      live)
            return t + 1, alive(first_chain)

        t_end, _ = lax.while_loop(lambda state: (state[0] < trips) & (state[1] > 0), body,
                                  (t0, alive(first_chain)))
        return t_end

    t_mid = walk(jnp.int32(0), everyone, len(chains) - 2)
    walk(t_mid, everyone[:2], 0)
    for h in range(nslab):
        o_ref[0, h * ts:(h + 1) * ts, :] = jnp.where(
            lane < HEAD_DIM, acc_ref[2 * h], acc_ref[2 * h + 1]).astype(BF16)


def _sb(y3d, tq=1024):
    b, s, _ = y3d.shape
    nch = 2 * (tq // SB_TS)
    return pl.pallas_call(
        functools.partial(_sb_kernel, tq=tq),
        grid=(b, N_PAIRS, s // tq),
        in_specs=[pl.BlockSpec((1, tq, LANES), lambda i, p, j: (i, j, C_SQ // LANES + p)),
                  pl.BlockSpec((1, s, LANES), lambda i, p, j: (i, 0, C_SK // LANES + p)),
                  pl.BlockSpec((1, s, LANES), lambda i, p, j: (i, 0, C_SV // LANES + p))],
        out_specs=pl.BlockSpec((1, tq, LANES), lambda i, p, j: (i, j, p)),
        out_shape=jax.ShapeDtypeStruct((b, s, D_MODEL), BF16),
        scratch_shapes=[pltpu.VMEM((nch, SB_TS, LANES), BF16),
                        pltpu.VMEM((nch, SB_TS, SB_KC), F32),
                        pltpu.VMEM((nch, SB_TS, SB_KC), BF16),
                        pltpu.VMEM((nch, SB_TS, SB_KC), F32),
                        pltpu.VMEM((nch, SB_TS, SB_KC), BF16),
                        pltpu.VMEM((nch, SB_TS, LANES), F32),
                        pltpu.VMEM((nch, SB_TS, LANES), F32),
                        pltpu.VMEM((nch, SB_TS, LANES), F32)],
        compiler_params=_cparams(3),
        name="sb",
    )(y3d, y3d, y3d)


LRU_TC = 256
LRU_HALO = 8


def _lru_kernel(x_ref, g_ref, cw_ref, cb_ref, wa_ref, ba_ref, wx_ref, bx_ref, lam_ref,
                o_ref, xbuf_ref, h_ref, *, ts):
    @pl.when(pl.program_id(2) == 0)
    def _():
        xbuf_ref[0:LRU_HALO, :] = jnp.zeros((LRU_HALO, LRU_TC), F32)
        h_ref[...] = jnp.zeros_like(h_ref)

    xbuf_ref[LRU_HALO:LRU_HALO + ts, :] = x_ref[0].astype(F32)
    cw = cw_ref[...]
    xc = cb_ref[...] + xbuf_ref[LRU_HALO:LRU_HALO + ts, :] * cw[3:4, :]
    for d in range(1, 4):
        xc = xc + xbuf_ref[LRU_HALO - d:LRU_HALO - d + ts, :] * cw[3 - d:4 - d, :]
    xbuf_ref[0:LRU_HALO, :] = xbuf_ref[ts:ts + LRU_HALO, :]

    xcb = xc.astype(BF16)
    r = jax.nn.sigmoid(_dot(xcb, wa_ref[0]) + ba_ref[...])
    gi = jax.nn.sigmoid(_dot(xcb, wx_ref[0]) + bx_ref[...])
    lam = lam_ref[...]
    softplus_neg_lam = jnp.maximum(-lam, 0.0) + _log1p_exp_neg_abs(lam)
    log_a = -LRU_C * r * softplus_neg_lam
    a = jnp.exp(log_a)
    one_m_a2 = -jnp.tanh(log_a) * (a * a + 1.0)
    u = one_m_a2 * lax.rsqrt(jnp.maximum(one_m_a2, F32_TINY)) * (gi * xc)

    row = lax.broadcasted_iota(jnp.int32, (ts, LRU_TC), 0) % SUBLANES
    d = 1
    while d < SUBLANES:
        keep = row >= d
        a_sh = jnp.where(keep, pltpu.roll(a, d, axis=0), 1.0)
        u_sh = jnp.where(keep, pltpu.roll(u, d, axis=0), 0.0)
        u = a * u_sh + u
        a = a * a_sh
        d *= 2
    state = h_ref[...]
    groups = []
    for r0 in range(0, ts, SUBLANES):
        h = u[r0:r0 + SUBLANES, :] + a[r0:r0 + SUBLANES, :] * state
        groups.append(h)
        state = h[SUBLANES - 1:SUBLANES, :]
    h_ref[...] = state
    h = jnp.concatenate(groups, axis=0)
    o_ref[0] = (h * jax.nn.gelu(g_ref[0].astype(F32))).astype(BF16)


def _lru(y3d, cw, cb, wa_bd, ba, wx_bd, bx, lam, ts=512):
    b, s, _ = y3d.shape
    nct = D_MODEL // LRU_TC
    vec = lambda rows: pl.BlockSpec((rows, LRU_TC), lambda i, c, j: (0, c))
    mat = pl.BlockSpec((1, LRU_TC, LRU_TC), lambda i, c, j: (c, 0, 0))
    return pl.pallas_call(
        functools.partial(_lru_kernel, ts=ts),
        grid=(b, nct, s // ts),
        in_specs=[pl.BlockSpec((1, ts, LRU_TC), lambda i, c, j: (i, j, C_LX // LRU_TC + c)),
                  pl.BlockSpec((1, ts, LRU_TC), lambda i, c, j: (i, j, C_LG // LRU_TC + c)),
                  vec(4), vec(1), mat, vec(1), mat, vec(1), vec(1)],
        out_specs=pl.BlockSpec((1, ts, LRU_TC), lambda i, c, j: (i, j, c)),
        out_shape=jax.ShapeDtypeStruct((b, s, D_MODEL), BF16),
        scratch_shapes=[pltpu.VMEM((ts + LRU_HALO, LRU_TC), F32),
                        pltpu.VMEM((1, LRU_TC), F32)],
        compiler_params=_cparams(3),
        name="lru",
    )(y3d, y3d, cw, cb, wa_bd, ba, wx_bd, bx, lam)


def _memkv_kernel(mem_ref, g_ref, w_ref, gk_ref, k_ref, v_ref):
    x = mem_ref[0]
    h = x * lax.rsqrt(jnp.mean(x * x, axis=-1, keepdims=True) + EPS) * g_ref[...]
    kv = _dot(h.astype(BF16), w_ref[...])
    for hd in range(MEM_HEADS):
        kh = kv[:, hd * MEM_HD:(hd + 1) * MEM_HD]
        kh = kh * lax.rsqrt(jnp.mean(kh * kh, axis=-1, keepdims=True) + EPS) * gk_ref[...]
        k_ref[0, :, hd * MEM_HD:(hd + 1) * MEM_HD] = kh.astype(BF16)
    v_ref[0] = kv[:, D_MODEL:].astype(BF16)


def _memkv(mem, g, w_kv, gk):
    b = mem.shape[0]
    out = jax.ShapeDtypeStruct((b, N_MEM, D_MODEL), BF16)
    return pl.pallas_call(
        _memkv_kernel,
        grid=(b,),
        in_specs=[pl.BlockSpec((1, N_MEM, D_MODEL), lambda i: (i, 0, 0)),
                  pl.BlockSpec((1, D_MODEL), lambda i: (0, 0)),
                  pl.BlockSpec((D_MODEL, 2 * D_MODEL), lambda i: (0, 0)),
                  pl.BlockSpec((1, MEM_HD), lambda i: (0, 0))],
        out_specs=[pl.BlockSpec((1, N_MEM, D_MODEL), lambda i: (i, 0, 0)),
                   pl.BlockSpec((1, N_MEM, D_MODEL), lambda i: (i, 0, 0))],
        out_shape=[out, out],
        compiler_params=_cparams(1),
        name="memkv",
    )(mem, g, w_kv, gk)


def _mem_kernel(q_ref, k_ref, v_ref, gq_ref, o_ref):
    for hd in range(MEM_HEADS):
        sl = slice(hd * MEM_HD, (hd + 1) * MEM_HD)
        q = q_ref[0, :, sl].astype(F32)
        q = q * lax.rsqrt(jnp.mean(q * q, axis=-1, keepdims=True) + EPS) * gq_ref[...]
        q = (q * (MEM_HD ** -0.5)).astype(BF16)
        s = _dot_nt(q, k_ref[0, :, sl])
        p = jnp.exp(s - jnp.max(s, axis=-1, keepdims=True))
        o = _dot(p.astype(BF16), v_ref[0, :, sl]) / jnp.sum(p, axis=-1, keepdims=True)
        o_ref[0, :, sl] = o.astype(BF16)


def _mem(y3d, mk, mv, gq, ts=512):
    b, s, _ = y3d.shape
    return pl.pallas_call(
        _mem_kernel,
        grid=(b, s // ts),
        in_specs=[pl.BlockSpec((1, ts, D_MODEL), lambda i, j: (i, j, C_MQ // D_MODEL)),
                  pl.BlockSpec((1, N_MEM, D_MODEL), lambda i, j: (i, 0, 0)),
                  pl.BlockSpec((1, N_MEM, D_MODEL), lambda i, j: (i, 0, 0)),
                  pl.BlockSpec((1, MEM_HD), lambda i, j: (0, 0))],
        out_specs=pl.BlockSpec((1, ts, D_MODEL), lambda i, j: (i, j, 0)),
        out_shape=jax.ShapeDtypeStruct((b, s, D_MODEL), BF16),
        compiler_params=_cparams(2),
        name="mem",
    )(y3d, mk, mv, gq)


def _merge_kernel(x_ref, b0_ref, b1_ref, b2_ref, b3_ref, gt_ref, bg_ref, wb_ref, wo_ref, o_ref):
    mixed = None
    for n, br in enumerate((b0_ref, b1_ref, b2_ref, b3_ref)):
        gate = jax.nn.sigmoid(
            gt_ref[:, n * D_MODEL:(n + 1) * D_MODEL].astype(F32) + bg_ref[n:n + 1, :])
        term = gate * _dot(br[...], wb_ref[n])
        mixed = term if mixed is None else mixed + term
    o_ref[...] = x_ref[...] + _dot(mixed.astype(BF16), wo_ref[...])


def _merge(x2d, branches, y2d, bg, wb, wo, layer, tm=256):
    m = x2d.shape[0]
    row = lambda w: pl.BlockSpec((tm, w), lambda i: (i, 0))
    return pl.pallas_call(
        _merge_kernel,
        grid=(m // tm,),
        in_specs=[row(D_MODEL), row(D_MODEL), row(D_MODEL), row(D_MODEL), row(D_MODEL),
                  pl.BlockSpec((tm, 4 * D_MODEL), lambda i: (i, C_GT // (4 * D_MODEL))),
                  pl.BlockSpec((4, D_MODEL), lambda i: (0, 0)),
                  pl.BlockSpec((None, 4, D_MODEL, D_MODEL), lambda i: (layer, 0, 0, 0)),
                  pl.BlockSpec((None, D_MODEL, D_MODEL), lambda i: (layer, 0, 0))],
        out_specs=row(D_MODEL),
        out_shape=jax.ShapeDtypeStruct((m, D_MODEL), F32),
        compiler_params=_cparams(1),
        name="merge",
    )(x2d, *branches, y2d, bg, wb, wo)


FFN_TC = 256
FFN_HALO = 8
FFN_RB = 64


def _ffn_kernel(x_ref, g_ref, wg_ref, wv_ref, cw_ref, cb_ref, wd_ref, o_ref,
                h_ref, gbuf_ref, vbuf_ref, act_ref, *, tm, tiles_per_seq):
    i = pl.program_id(0)
    x = x_ref[...]
    h_ref[...] = (x * lax.rsqrt(jnp.mean(x * x, axis=-1, keepdims=True) + EPS)
                  * g_ref[...]).astype(BF16)

    @pl.when((i % tiles_per_seq) == 0)
    def _():
        gbuf_ref[0:FFN_HALO, :] = jnp.zeros((FFN_HALO, D_FF), F32)

    def up(c):
        cols = slice(c * FFN_TC, (c + 1) * FFN_TC)
        gbuf_ref[FFN_HALO:FFN_HALO + tm, cols] = _dot(h_ref[...], wg_ref[:, cols])
        vbuf_ref[:, cols] = _dot(h_ref[...], wv_ref[:, cols])

    def activate(c):
        cols = slice(c * FFN_TC, (c + 1) * FFN_TC)
        cw = cw_ref[:, cols]
        cb = cb_ref[:, cols]
        for r0 in range(0, tm, FFN_RB):
            gate = cb
            for d in range(3):
                lo = FFN_HALO + r0 - d
                gate = gate + gbuf_ref[lo:lo + FFN_RB, cols] * cw[2 - d:3 - d, :]
            act = gate * jax.nn.sigmoid(gate) * vbuf_ref[r0:r0 + FFN_RB, cols]
            act_ref[r0:r0 + FFN_RB, cols] = act.astype(BF16)

    nc = D_FF // FFN_TC
    up(0)
    for c in range(1, nc):
        up(c)
        activate(c - 1)
    activate(nc - 1)
    o_ref[...] = x_ref[...] + _dot(act_ref[...], wd_ref[...])
    gbuf_ref[0:FFN_HALO, :] = gbuf_ref[tm:tm + FFN_HALO, :]


def _ffn(x2d, g, w_up, cw, cb, w_down, layer, seq, tm=512):
    m = x2d.shape[0]
    resident = pl.Buffered(1)
    return pl.pallas_call(
        functools.partial(_ffn_kernel, tm=tm, tiles_per_seq=seq // tm),
        grid=(m // tm,),
        in_specs=[pl.BlockSpec((tm, D_MODEL), lambda i: (i, 0)),
                  pl.BlockSpec((1, D_MODEL), lambda i: (0, 0)),
                  pl.BlockSpec((None, D_MODEL, D_FF), lambda i: (layer, 0, 0),
                               pipeline_mode=resident),
                  pl.BlockSpec((None, D_MODEL, D_FF), lambda i: (layer, 0, 1),
                               pipeline_mode=resident),
                  pl.BlockSpec((3, D_FF), lambda i: (0, 0)),
                  pl.BlockSpec((1, D_FF), lambda i: (0, 0)),
                  pl.BlockSpec((None, D_FF, D_MODEL), lambda i: (layer, 0, 0),
                               pipeline_mode=resident)],
        out_specs=pl.BlockSpec((tm, D_MODEL), lambda i: (i, 0)),
        out_shape=jax.ShapeDtypeStruct((m, D_MODEL), F32),
        scratch_shapes=[pltpu.VMEM((tm, D_MODEL), BF16),
                        pltpu.VMEM((tm + FFN_HALO, D_FF), F32),
                        pltpu.VMEM((tm, D_FF), F32),
                        pltpu.VMEM((tm, D_FF), BF16)],
        compiler_params=_cparams(1),
        name="ffn",
    )(x2d, g, w_up, w_up, cw, cb, w_down)


def _block_diag(w):
    per = LRU_TC // HEAD_DIM
    w4 = w.reshape(D_MODEL // LRU_TC, per, HEAD_DIM, HEAD_DIM)
    eye = jnp.eye(per, dtype=w.dtype)
    bd = jnp.einsum('cpde,pq->cpdqe', w4, eye)
    return bd.reshape(D_MODEL // LRU_TC, LRU_TC, LRU_TC)


def kernel(x, mem, attn_norm_g, mem_norm_g, w_in, b_forget, fox_q_norm_g, fox_k_norm_g,
           lru_conv_w, lru_conv_b, lru_w_a, lru_b_a, lru_w_x, lru_b_x, lru_lambda,
           w_mem_kv, mem_q_norm_g, mem_k_norm_g, b_gate, w_branch, w_out,
           ffn_norm_g, w_up, ffn_conv_w, ffn_conv_b, w_down):
    b, s, d = x.shape
    depth = w_in.shape[0]
    m = b * s
    x2d = x.reshape(m, d)
    w_branch_bf, w_out_bf = w_branch.astype(BF16), w_out.astype(BF16)
    w_up_bf, w_down_bf = w_up.astype(BF16), w_down.astype(BF16)
    for l in range(depth):
        wy = _wprep(w_in, l)
        wf = jnp.pad(w_in[l, :, W_FOX_END:W_F_END], ((0, 0), (0, LANES - N_HEADS)))
        wf_hi = wf.astype(BF16)
        wf = jnp.stack([wf_hi, (wf - wf_hi.astype(F32)).astype(BF16)])
        bf = jnp.pad(b_forget[l], (0, LANES - N_HEADS)).reshape(1, LANES)

        y2d, f2d = _proj(x2d, attn_norm_g[l].reshape(1, d), wy, wf)
        y3d = y2d.reshape(b, s, N_Y)

        fcum = _fcum(f2d.reshape(b, s, LANES), bf)
        qa, ka, vt = _foxprep(y3d, fcum,
                              jnp.tile(fox_q_norm_g[l], 2).reshape(1, LANES),
                              jnp.tile(fox_k_norm_g[l], 2).reshape(1, LANES))
        y_fox = _fox(qa, ka, vt)
        y_sb = _sb(y3d)
        y_lru = _lru(y3d, lru_conv_w[l], lru_conv_b[l].reshape(1, d),
                     _block_diag(lru_w_a[l]).astype(BF16), lru_b_a[l].reshape(1, d),
                     _block_diag(lru_w_x[l]).astype(BF16), lru_b_x[l].reshape(1, d),
                     lru_lambda[l].reshape(1, d))
        mk, mv = _memkv(mem, mem_norm_g[l].reshape(1, d), w_mem_kv[l].astype(BF16),
                        mem_k_norm_g[l].reshape(1, MEM_HD))
        y_mem = _mem(y3d, mk, mv, mem_q_norm_g[l].reshape(1, MEM_HD))

        branches = [t.reshape(m, d) for t in (y_fox, y_lru, y_sb, y_mem)]
        x2d = _merge(x2d, branches, y2d, b_gate[l], w_branch_bf, w_out_bf, l)
        x2d = _ffn(x2d, ffn_norm_g[l].reshape(1, d), w_up_bf, ffn_conv_w[l],
                   ffn_conv_b[l].reshape(1, D_FF), w_down_bf, l, s)
    return x2d.reshape(b, s, d)
```

```python
import functools

import jax
import jax.numpy as jnp
from jax import lax
from jax.experimental import pallas as pl
from jax.experimental.pallas import tpu as pltpu

F32 = jnp.float32
BF16 = jnp.bfloat16

D_MODEL = 1024
HEAD_DIM = 64
N_HEADS = 16
N_PAIRS = N_HEADS // 2
LANES = 128
SUBLANES = 8
N_MEM = 256
MEM_HEADS = 4
MEM_HD = 256
D_FF = 2816
LRU_C = 8.0
EPS = 1e-6
NEG = -1e30
LOG2E = 1.4426950408889634
F32_TINY = 1.1754944e-38

C_GT, C_FQ, C_FK, C_FV, C_LX, C_LG, C_SQ, C_SK, C_SV, C_MQ = (
    0, 4096, 5120, 6144, 7168, 8192, 9216, 10240, 11264, 12288)
N_Y = 13312
W_FOX_END, W_F_END, W_GATES = 3072, 3088, 9232

VMEM_LIMIT = 56 * 1024 * 1024


def _cparams(n_axes):
    return pltpu.CompilerParams(dimension_semantics=("arbitrary",) * n_axes,
                                vmem_limit_bytes=VMEM_LIMIT)


def _dot(a, b):
    return jnp.dot(a, b, preferred_element_type=F32)


def _dot_nt(a, b):
    return lax.dot_general(a, b, (((1,), (1,)), ((), ())), preferred_element_type=F32)


def _log1p_exp_neg_abs(z):
    return jnp.log(1.0 + jnp.exp(-jnp.abs(z)))


def _wprep_kernel(w_ref, o_ref):
    o_ref[:, C_GT:C_FQ] = w_ref[:, W_GATES:]
    o_ref[:, C_FQ:C_LX] = w_ref[:, :W_FOX_END]
    o_ref[:, C_LX:N_Y] = w_ref[:, W_F_END:W_GATES]


def _wprep(w_in, layer, tr=128):
    _, rows, cols = w_in.shape
    return pl.pallas_call(
        _wprep_kernel,
        grid=(rows // tr,),
        in_specs=[pl.BlockSpec((None, tr, cols), lambda i: (layer, i, 0))],
        out_specs=pl.BlockSpec((tr, N_Y), lambda i: (i, 0)),
        out_shape=jax.ShapeDtypeStruct((rows, N_Y), BF16),
        compiler_params=_cparams(1),
        name="wprep",
    )(w_in)


def _proj_kernel(x_ref, g_ref, w_ref, wf_ref, y_ref, f_ref, h_ref):
    @pl.when(pl.program_id(1) == 0)
    def _():
        x = x_ref[...]
        h = x * lax.rsqrt(jnp.mean(x * x, axis=-1, keepdims=True) + EPS) * g_ref[...]
        h_hi = h.astype(BF16)
        h_ref[...] = h_hi
        h_lo = (h - h_hi.astype(F32)).astype(BF16)
        f_ref[...] = (_dot(h_hi, wf_ref[0]) + _dot(h_lo, wf_ref[0])) + _dot(h_hi, wf_ref[1])

    y_ref[...] = _dot(h_ref[...], w_ref[...]).astype(BF16)


def _proj(x2d, g, wy, wf, tm=2048, tn=1024):
    m = x2d.shape[0]
    return pl.pallas_call(
        _proj_kernel,
        grid=(m // tm, N_Y // tn),
        in_specs=[pl.BlockSpec((tm, D_MODEL), lambda i, j: (i, 0)),
                  pl.BlockSpec((1, D_MODEL), lambda i, j: (0, 0)),
                  pl.BlockSpec((D_MODEL, tn), lambda i, j: (0, j)),
                  pl.BlockSpec((2, D_MODEL, LANES), lambda i, j: (0, 0, 0))],
        out_specs=[pl.BlockSpec((tm, tn), lambda i, j: (i, j)),
                   pl.BlockSpec((tm, LANES), lambda i, j: (i, 0))],
        out_shape=[jax.ShapeDtypeStruct((m, N_Y), BF16),
                   jax.ShapeDtypeStruct((m, LANES), F32)],
        scratch_shapes=[pltpu.VMEM((tm, D_MODEL), BF16)],
        compiler_params=_cparams(2),
        name="proj",
    )(x2d, g, wy, wf)


F_PIECES = 3
FOX_TS = 512
FOX_KB = 64
FOX_ONES = 16
FOX_VROWS = HEAD_DIM + FOX_ONES


def _fcum_kernel(f_ref, b_ref, o_ref, carry_ref):
    @pl.when(pl.program_id(1) == 0)
    def _():
        carry_ref[...] = jnp.zeros_like(carry_ref)

    z = f_ref[0] + b_ref[...]
    log_f = jnp.minimum(z, 0.0) - _log1p_exp_neg_abs(z)
    ts = z.shape[0]
    row = lax.broadcasted_iota(jnp.int32, (ts, ts), 0)
    col = lax.broadcasted_iota(jnp.int32, (ts, ts), 1)
    tri = (row >= col).astype(BF16)
    hi = log_f.astype(BF16)
    r1 = log_f - hi.astype(F32)
    mid = r1.astype(BF16)
    lo = (r1 - mid.astype(F32)).astype(BF16)
    c = (_dot(tri, hi) + _dot(tri, mid)) + _dot(tri, lo) + carry_ref[...]
    carry_ref[...] = c[ts - 1:ts, :]

    lane = lax.broadcasted_iota(jnp.int32, (ts, LANES), 1)
    f = jnp.where(lane < N_HEADS, c * LOG2E, 0.0)
    hi = f.astype(BF16).astype(F32)
    r1 = f - hi
    mid = r1.astype(BF16).astype(F32)
    lo = (r1 - mid).astype(BF16).astype(F32)
    packed = hi + pltpu.roll(mid, N_HEADS, axis=1) + pltpu.roll(lo, 2 * N_HEADS, axis=1)
    o_ref[0] = packed.astype(BF16)


def _fcum(f3d, b_pad, ts=512):
    b, s, _ = f3d.shape
    return pl.pallas_call(
        _fcum_kernel,
        grid=(b, s // ts),
        in_specs=[pl.BlockSpec((1, ts, LANES), lambda i, j: (i, j, 0)),
                  pl.BlockSpec((1, LANES), lambda i, j: (0, 0))],
        out_specs=pl.BlockSpec((1, ts, LANES), lambda i, j: (i, j, 0)),
        out_shape=jax.ShapeDtypeStruct((b, s, LANES), BF16),
        scratch_shapes=[pltpu.VMEM((1, LANES), F32)],
        compiler_params=_cparams(2),
        name="fcum",
    )(f3d, b_pad)


def _foxprep_kernel(yq_ref, yk_ref, yv_ref, f_ref, gq_ref, gk_ref, qa_ref, ka_ref, vt_ref):
    hp = pl.program_id(1)
    ones = jnp.ones((FOX_ONES, FOX_TS), BF16)
    for c in range(vt_ref.shape[2]):
        vt = yv_ref[0, c * FOX_TS:(c + 1) * FOX_TS, :].astype(F32).T.astype(BF16)
        vt_ref[0, 0, c] = jnp.concatenate(
            [vt[:HEAD_DIM], ones, vt[HEAD_DIM:], ones], axis=0)
    ts = yq_ref.shape[1]
    lane = lax.broadcasted_iota(jnp.int32, (ts, LANES), 1)
    lo_half = lane < HEAD_DIM

    hr = lax.broadcasted_iota(jnp.int32, (LANES, LANES), 0)
    hc = lax.broadcasted_iota(jnp.int32, (LANES, LANES), 1)
    same_head = ((hr < HEAD_DIM) == (hc < HEAD_DIM)).astype(BF16)

    def headnorm(y, g):
        ms = _dot((y * y).astype(BF16), same_head) * (1.0 / HEAD_DIM)
        return y * lax.rsqrt(ms + EPS) * g

    qn = headnorm(yq_ref[0].astype(F32), gq_ref[...]) * (HEAD_DIM ** -0.5 * LOG2E)
    kn = headnorm(yk_ref[0].astype(F32), gk_ref[...])

    pieces = f_ref[0]
    sel_r = lax.broadcasted_iota(jnp.int32, (LANES, 2 * LANES), 0)
    sel_c = lax.broadcasted_iota(jnp.int32, (LANES, 2 * LANES), 1)
    ones_q = ((lane >= HEAD_DIM + F_PIECES) & (lane < HEAD_DIM + 2 * F_PIECES)).astype(F32)
    ones_k = ((lane >= HEAD_DIM) & (lane < HEAD_DIM + F_PIECES)).astype(F32)
    for e in range(2):
        head = 2 * hp + e
        piece_of_row = sel_r - head
        to_q = (piece_of_row == (sel_c - HEAD_DIM) * N_HEADS) & (sel_c < HEAD_DIM + F_PIECES)
        k_col0 = LANES + HEAD_DIM + F_PIECES
        to_k = (piece_of_row == (sel_c - k_col0) * N_HEADS) & (sel_c >= k_col0)
        valid = (piece_of_row >= 0) & (piece_of_row < F_PIECES * N_HEADS)
        sel = (jnp.where(valid & to_q, 1.0, 0.0) - jnp.where(valid & to_k, 1.0, 0.0)).astype(BF16)
        placed = _dot(pieces, sel)
        q_part = qn if e == 0 else pltpu.roll(qn, HEAD_DIM, axis=1)
        k_part = kn if e == 0 else pltpu.roll(kn, HEAD_DIM, axis=1)
        q_aug = jnp.where(lo_half, q_part, placed[:, :LANES] + ones_q)
        k_aug = jnp.where(lo_half, k_part, placed[:, LANES:] + ones_k)
        qa_ref[0, e] = q_aug.astype(BF16)
        ka_ref[0, e] = k_aug.astype(BF16)


def _foxprep(y3d, fcum, gq2, gk2, ts=2048):
    b, s, _ = y3d.shape
    out = jax.ShapeDtypeStruct((b, N_HEADS, s, LANES), BF16)
    vt = jax.ShapeDtypeStruct((b, N_PAIRS, s // FOX_TS, 2 * FOX_VROWS, FOX_TS), BF16)
    return pl.pallas_call(
        _foxprep_kernel,
        grid=(b, N_PAIRS, s // ts),
        in_specs=[pl.BlockSpec((1, ts, LANES), lambda i, p, j: (i, j, C_FQ // LANES + p)),
                  pl.BlockSpec((1, ts, LANES), lambda i, p, j: (i, j, C_FK // LANES + p)),
                  pl.BlockSpec((1, ts, LANES), lambda i, p, j: (i, j, C_FV // LANES + p)),
                  pl.BlockSpec((1, ts, LANES), lambda i, p, j: (i, j, 0)),
                  pl.BlockSpec((1, LANES), lambda i, p, j: (0, 0)),
                  pl.BlockSpec((1, LANES), lambda i, p, j: (0, 0))],
        out_specs=[pl.BlockSpec((1, 2, ts, LANES), lambda i, p, j: (i, p, j, 0)),
                   pl.BlockSpec((1, 2, ts, LANES), lambda i, p, j: (i, p, j, 0)),
                   pl.BlockSpec((1, 1, ts // FOX_TS, 2 * FOX_VROWS, FOX_TS),
                                lambda i, p, j: (i, p, j, 0, 0))],
        out_shape=[out, out, vt],
        compiler_params=_cparams(3),
        name="foxprep",
    )(y3d, y3d, y3d, fcum, gq2, gk2)


def _fox_kernel(qa_ref, ka_ref, vt_ref, o_ref, s_ref, p_ref, m_ref, a_ref, acc_ref, *, tq):
    i = pl.program_id(2)
    ts = FOX_TS
    nslab = tq // ts
    chains = [(h, e) for h in range(nslab) for e in range(2)]
    key = lax.broadcasted_iota(jnp.int32, (FOX_KB, LANES), 0)
    query = lax.broadcasted_iota(jnp.int32, (FOX_KB, LANES), 1)

    m_ref[...] = jnp.full(m_ref.shape, NEG, F32)
    acc_ref[...] = jnp.zeros(acc_ref.shape, F32)

    def logits(n, block):
        h, e = chains[n]
        start = pl.multiple_of(block * ts, ts)
        s = _dot_nt(ka_ref[0, e, pl.ds(start, ts), :], qa_ref[0, e, h * ts:(h + 1) * ts, :])
        for c in range(ts // LANES):
            s_ref[n, c, 0:ts, :] = s[:, c * LANES:(c + 1) * LANES]

    def softmax(n, diagonal):
        kb = FOX_KB
        strips = range(0, ts, LANES)

        def piece(c0, k0):
            s = s_ref[n, c0 // LANES, k0:k0 + kb, :]
            if diagonal and k0 + kb > c0:
                s = jnp.where(key + k0 <= query + c0, s, NEG)
            return s

        def last_key(c0):
            return c0 + LANES if diagonal else ts

        m_new = {}
        for c0 in strips:
            cols = slice(c0, c0 + LANES)
            top = piece(c0, 0)
            for k0 in range(kb, last_key(c0), kb):
                top = jnp.maximum(top, piece(c0, k0))
            m_old = m_ref[n, :, cols]
            m_new[c0] = jnp.maximum(m_old, jnp.max(top, axis=0, keepdims=True))
            m_ref[n, :, cols] = m_new[c0]
            a_ref[n, :, cols] = jnp.exp2(m_old - m_new[c0])
        for c0 in strips:
            for k0 in range(0, last_key(c0), kb):
                p_ref[n, c0 // LANES, k0:k0 + kb, :] = jnp.exp2(
                    (piece(c0, k0) - m_new[c0][0:1, :]).astype(BF16))
            if last_key(c0) < ts:
                p_ref[n, c0 // LANES, last_key(c0):ts, :] = jnp.zeros(
                    (ts - last_key(c0), LANES), BF16)

    def accumulate(n, block):
        e = chains[n][1]
        vt = vt_ref[0, 0, block, e * FOX_VROWS:(e + 1) * FOX_VROWS, :]
        p = jnp.concatenate([p_ref[n, c, 0:ts, :] for c in range(ts // LANES)], axis=1)
        acc_ref[n] = a_ref[n, 0:1, :] * acc_ref[n] + _dot(vt, p)

    def step(block, live, diagonal_slab, live_next):
        prev = None
        for n in live:
            softmax(n, chains[n][0] == diagonal_slab)
            if n in live_next:
                logits(n, block + 1)
            if prev is not None:
                accumulate(prev, block)
            prev = n
        accumulate(prev, block)

    everyone = list(range(len(chains)))
    for n in everyone:
        logits(n, 0)

    def body(j, _):
        step(j, everyone, None, everyone)
        return 0

    lax.fori_loop(0, i * nslab, body, 0)
    for c in range(nslab):
        live = [n for n in everyone if chains[n][0] >= c]
        live_next = [n for n in everyone if chains[n][0] >= c + 1] if c + 1 < nslab else []
        step(i * nslab + c, live, c, live_next)

    for h in range(nslab):
        outs = [acc_ref[2 * h + e, 0:HEAD_DIM, :] / acc_ref[2 * h + e, HEAD_DIM:HEAD_DIM + 1, :]
                for e in range(2)]
        o_ref[0, h * ts:(h + 1) * ts, :] = jnp.concatenate(outs, axis=0).T.astype(BF16)


def _fox(qa, ka, vt, tq=2048):
    b, _, s, _ = qa.shape
    nch = 2 * (tq // FOX_TS)
    return pl.pallas_call(
        functools.partial(_fox_kernel, tq=tq),
        grid=(b, N_PAIRS, s // tq),
        in_specs=[pl.BlockSpec((1, 2, tq, LANES), lambda i, p, j: (i, p, j, 0)),
                  pl.BlockSpec((1, 2, s, LANES), lambda i, p, j: (i, p, 0, 0)),
                  pl.BlockSpec((1, 1, s // FOX_TS, 2 * FOX_VROWS, FOX_TS),
                               lambda i, p, j: (i, p, 0, 0, 0))],
        out_specs=pl.BlockSpec((1, tq, LANES), lambda i, p, j: (i, j, p)),
        out_shape=jax.ShapeDtypeStruct((b, s, D_MODEL), BF16),
        scratch_shapes=[pltpu.VMEM((nch, FOX_TS // LANES, FOX_TS + SUBLANES, LANES), F32),
                        pltpu.VMEM((nch, FOX_TS // LANES, FOX_TS + 2 * SUBLANES, LANES), BF16),
                        pltpu.VMEM((nch, SUBLANES, FOX_TS), F32),
                        pltpu.VMEM((nch, SUBLANES, FOX_TS), F32),
                        pltpu.VMEM((nch, FOX_VROWS, FOX_TS), F32)],
        compiler_params=_cparams(3),
        name="fox",
    )(qa, ka, vt)


SB_KC = 256
SB_TS = 512
SB_MAX_LOG2 = 126.0
SB_DEAD_LOG2 = -(SB_MAX_LOG2 + 150.0)


SB_RB = 64
SB_UNROLL = 1


def _sb_kernel(q_ref, k_ref, v_ref, o_ref, qm_ref, z_ref, lb_ref, ic_ref, w_ref,
               rs_ref, r_ref, acc_ref, *, tq):
    i = pl.program_id(2)
    kc, ts, rb = SB_KC, SB_TS, SB_RB
    nsub = tq // kc
    nslab = tq // ts
    chains = [(h, e) for h in range(nslab) for e in range(2)]
    row = lax.broadcasted_iota(jnp.int32, (rb, kc), 0)
    col = lax.broadcasted_iota(jnp.int32, (rb, kc), 1)
    kr = lax.broadcasted_iota(jnp.int32, (kc, kc), 0)
    kcol = lax.broadcasted_iota(jnp.int32, (kc, kc), 1)
    at_or_after = (kr >= kcol).astype(BF16)

    lane = lax.broadcasted_iota(jnp.int32, (ts, LANES), 1)
    for n, (h, e) in enumerate(chains):
        q2 = q_ref[0, h * ts:(h + 1) * ts, :]
        mine = (lane < HEAD_DIM) if e == 0 else (lane >= HEAD_DIM)
        qm_ref[n] = jnp.where(mine, q2, jnp.zeros_like(q2))
    r_ref[...] = jnp.zeros(r_ref.shape, F32)
    acc_ref[...] = jnp.zeros(acc_ref.shape, F32)

    def logits(n, start):
        z_ref[n] = jnp.minimum(_dot_nt(qm_ref[n], k_ref[0, pl.ds(start, kc), :]), SB_MAX_LOG2)

    def log_terms(n, offset):
        for r0 in range(0, ts, rb):
            rows = slice(r0, r0 + rb)
            log_1m_beta = jnp.log(1.0 + jnp.exp2(z_ref[n, rows, :])) * (-LOG2E)
            if offset is not None:
                log_1m_beta = jnp.where(col + (offset - r0) < row, log_1m_beta, 0.0)
            lb_ref[n, rows, :] = log_1m_beta.astype(BF16)
            rs_ref[n, rows, :] = jnp.broadcast_to(
                jnp.sum(log_1m_beta, axis=-1, keepdims=True), (rb, LANES))

    def cumulate(n):
        ic_ref[n] = _dot(lb_ref[n], at_or_after)

    def weights(n, offset):
        for r0 in range(0, ts, rb):
            rows = slice(r0, r0 + rb)
            r = r_ref[n, rows, :]
            log_w = (z_ref[n, rows, :] + ic_ref[n, rows, :]
                     + jnp.concatenate([r] * (kc // LANES), axis=1))
            if offset is not None:
                log_w = jnp.where(col + (offset - r0) < row, log_w, NEG)
            w_ref[n, rows, :] = jnp.exp2(log_w.astype(BF16))
            r_ref[n, rows, :] = r + rs_ref[n, rows, :]

    def accumulate(n, start):
        acc_ref[n] += _dot(w_ref[n], v_ref[0, pl.ds(start, kc), :])

    def chunk(start, next_start, live, offsets, live_next):
        prev = None
        for n in live:
            log_terms(n, offsets[chains[n][0]])
            cumulate(n)
            if prev is not None:
                weights(prev, offsets[chains[prev][0]])
                if prev in live_next:
                    logits(prev, next_start)
                accumulate(prev, start)
            prev = n
        weights(prev, offsets[chains[prev][0]])
        if prev in live_next:
            logits(prev, next_start)
        accumulate(prev, start)
        for n in live_next:
            if n not in live:
                logits(n, next_start)

    def visibility(c):
        live, offsets = [], {}
        for n, (h, _) in enumerate(chains):
            if c * kc >= (h + 1) * ts:
                continue
            live.append(n)
            offsets[h] = None if (c + 1) * kc <= h * ts else c * kc - h * ts
        return live, offsets

    everyone = list(range(len(chains)))
    no_offsets = {h: None for h in range(nslab)}
    live, offsets = visibility(nsub - 1)
    for n in live:
        logits(n, pl.multiple_of(i * tq + (nsub - 1) * kc, kc))
    for c in reversed(range(nsub)):
        start = pl.multiple_of(i * tq + c * kc, kc)
        live_next = visibility(c - 1)[0] if c > 0 else everyone
        chunk(start, pl.multiple_of(jnp.maximum(start - kc, 0), kc), live, offsets, live_next)
        if c > 0:
            live, offsets = visibility(c - 1)

    def alive(first_chain):
        return (jnp.max(r_ref[first_chain:]) > SB_DEAD_LOG2).astype(jnp.int32)

    trips = i * (nsub // SB_UNROLL)

    def walk(t0, live, first_chain):
        def body(state):
            t, _ = state
            for u in range(SB_UNROLL):
                start = pl.multiple_of(i * tq - (t * SB_UNROLL + u + 1) * kc, kc)
                chunk(start, pl.multiple_of(jnp.maximum(start - kc, 0), kc), live, no_offsets,
                      live)
            return t + 1, alive(first_chain)

        t_end, _ = lax.while_loop(lambda state: (state[0] < trips) & (state[1] > 0), body,
                                  (t0, alive(first_chain)))
        return t_end

    t_mid = walk(jnp.int32(0), everyone, len(chains) - 2)
    walk(t_mid, everyone[:2], 0)
    for h in range(nslab):
        o_ref[0, h * ts:(h + 1) * ts, :] = jnp.where(
            lane < HEAD_DIM, acc_ref[2 * h], acc_ref[2 * h + 1]).astype(BF16)


def _sb(y3d, tq=1024):
    b, s, _ = y3d.shape
    nch = 2 * (tq // SB_TS)
    return pl.pallas_call(
        functools.partial(_sb_kernel, tq=tq),
        grid=(b, N_PAIRS, s // tq),
        in_specs=[pl.BlockSpec((1, tq, LANES), lambda i, p, j: (i, j, C_SQ // LANES + p)),
                  pl.BlockSpec((1, s, LANES), lambda i, p, j: (i, 0, C_SK // LANES + p)),
                  pl.BlockSpec((1, s, LANES), lambda i, p, j: (i, 0, C_SV // LANES + p))],
        out_specs=pl.BlockSpec((1, tq, LANES), lambda i, p, j: (i, j, p)),
        out_shape=jax.ShapeDtypeStruct((b, s, D_MODEL), BF16),
        scratch_shapes=[pltpu.VMEM((nch, SB_TS, LANES), BF16),
                        pltpu.VMEM((nch, SB_TS, SB_KC), F32),
                        pltpu.VMEM((nch, SB_TS, SB_KC), BF16),
                        pltpu.VMEM((nch, SB_TS, SB_KC), F32),
                        pltpu.VMEM((nch, SB_TS, SB_KC), BF16),
                        pltpu.VMEM((nch, SB_TS, LANES), F32),
                        pltpu.VMEM((nch, SB_TS, LANES), F32),
                        pltpu.VMEM((nch, SB_TS, LANES), F32)],
        compiler_params=_cparams(3),
        name="sb",
    )(y3d, y3d, y3d)


LRU_TC = 256
LRU_HALO = 8


def _lru_kernel(x_ref, g_ref, cw_ref, cb_ref, wa_ref, ba_ref, wx_ref, bx_ref, lam_ref,
                o_ref, xbuf_ref, h_ref, *, ts):
    @pl.when(pl.program_id(2) == 0)
    def _():
        xbuf_ref[0:LRU_HALO, :] = jnp.zeros((LRU_HALO, LRU_TC), F32)
        h_ref[...] = jnp.zeros_like(h_ref)

    xbuf_ref[LRU_HALO:LRU_HALO + ts, :] = x_ref[0].astype(F32)
    cw = cw_ref[...]
    xc = cb_ref[...] + xbuf_ref[LRU_HALO:LRU_HALO + ts, :] * cw[3:4, :]
    for d in range(1, 4):
        xc = xc + xbuf_ref[LRU_HALO - d:LRU_HALO - d + ts, :] * cw[3 - d:4 - d, :]
    xbuf_ref[0:LRU_HALO, :] = xbuf_ref[ts:ts + LRU_HALO, :]

    xcb = xc.astype(BF16)
    r = jax.nn.sigmoid(_dot(xcb, wa_ref[0]) + ba_ref[...])
    gi = jax.nn.sigmoid(_dot(xcb, wx_ref[0]) + bx_ref[...])
    lam = lam_ref[...]
    softplus_neg_lam = jnp.maximum(-lam, 0.0) + _log1p_exp_neg_abs(lam)
    log_a = -LRU_C * r * softplus_neg_lam
    a = jnp.exp(log_a)
    one_m_a2 = -jnp.tanh(log_a) * (a * a + 1.0)
    u = one_m_a2 * lax.rsqrt(jnp.maximum(one_m_a2, F32_TINY)) * (gi * xc)

    row = lax.broadcasted_iota(jnp.int32, (ts, LRU_TC), 0) % SUBLANES
    d = 1
    while d < SUBLANES:
        keep = row >= d
        a_sh = jnp.where(keep, pltpu.roll(a, d, axis=0), 1.0)
        u_sh = jnp.where(keep, pltpu.roll(u, d, axis=0), 0.0)
        u = a * u_sh + u
        a = a * a_sh
        d *= 2
    state = h_ref[...]
    groups = []
    for r0 in range(0, ts, SUBLANES):
        h = u[r0:r0 + SUBLANES, :] + a[r0:r0 + SUBLANES, :] * state
        groups.append(h)
        state = h[SUBLANES - 1:SUBLANES, :]
    h_ref[...] = state
    h = jnp.concatenate(groups, axis=0)
    o_ref[0] = (h * jax.nn.gelu(g_ref[0].astype(F32))).astype(BF16)


def _lru(y3d, cw, cb, wa_bd, ba, wx_bd, bx, lam, ts=512):
    b, s, _ = y3d.shape
    nct = D_MODEL // LRU_TC
    vec = lambda rows: pl.BlockSpec((rows, LRU_TC), lambda i, c, j: (0, c))
    mat = pl.BlockSpec((1, LRU_TC, LRU_TC), lambda i, c, j: (c, 0, 0))
    return pl.pallas_call(
        functools.partial(_lru_kernel, ts=ts),
        grid=(b, nct, s // ts),
        in_specs=[pl.BlockSpec((1, ts, LRU_TC), lambda i, c, j: (i, j, C_LX // LRU_TC + c)),
                  pl.BlockSpec((1, ts, LRU_TC), lambda i, c, j: (i, j, C_LG // LRU_TC + c)),
                  vec(4), vec(1), mat, vec(1), mat, vec(1), vec(1)],
        out_specs=pl.BlockSpec((1, ts, LRU_TC), lambda i, c, j: (i, j, c)),
        out_shape=jax.ShapeDtypeStruct((b, s, D_MODEL), BF16),
        scratch_shapes=[pltpu.VMEM((ts + LRU_HALO, LRU_TC), F32),
                        pltpu.VMEM((1, LRU_TC), F32)],
        compiler_params=_cparams(3),
        name="lru",
    )(y3d, y3d, cw, cb, wa_bd, ba, wx_bd, bx, lam)


def _memkv_kernel(mem_ref, g_ref, w_ref, gk_ref, k_ref, v_ref):
    x = mem_ref[0]
    h = x * lax.rsqrt(jnp.mean(x * x, axis=-1, keepdims=True) + EPS) * g_ref[...]
    kv = _dot(h.astype(BF16), w_ref[...])
    for hd in range(MEM_HEADS):
        kh = kv[:, hd * MEM_HD:(hd + 1) * MEM_HD]
        kh = kh * lax.rsqrt(jnp.mean(kh * kh, axis=-1, keepdims=True) + EPS) * gk_ref[...]
        k_ref[0, :, hd * MEM_HD:(hd + 1) * MEM_HD] = kh.astype(BF16)
    v_ref[0] = kv[:, D_MODEL:].astype(BF16)


def _memkv(mem, g, w_kv, gk):
    b = mem.shape[0]
    out = jax.ShapeDtypeStruct((b, N_MEM, D_MODEL), BF16)
    return pl.pallas_call(
        _memkv_kernel,
        grid=(b,),
        in_specs=[pl.BlockSpec((1, N_MEM, D_MODEL), lambda i: (i, 0, 0)),
                  pl.BlockSpec((1, D_MODEL), lambda i: (0, 0)),
                  pl.BlockSpec((D_MODEL, 2 * D_MODEL), lambda i: (0, 0)),
                  pl.BlockSpec((1, MEM_HD), lambda i: (0, 0))],
        out_specs=[pl.BlockSpec((1, N_MEM, D_MODEL), lambda i: (i, 0, 0)),
                   pl.BlockSpec((1, N_MEM, D_MODEL), lambda i: (i, 0, 0))],
        out_shape=[out, out],
        compiler_params=_cparams(1),
        name="memkv",
    )(mem, g, w_kv, gk)


def _mem_kernel(q_ref, k_ref, v_ref, gq_ref, o_ref):
    for hd in range(MEM_HEADS):
        sl = slice(hd * MEM_HD, (hd + 1) * MEM_HD)
        q = q_ref[0, :, sl].astype(F32)
        q = q * lax.rsqrt(jnp.mean(q * q, axis=-1, keepdims=True) + EPS) * gq_ref[...]
        q = (q * (MEM_HD ** -0.5)).astype(BF16)
        s = _dot_nt(q, k_ref[0, :, sl])
        p = jnp.exp(s - jnp.max(s, axis=-1, keepdims=True))
        o = _dot(p.astype(BF16), v_ref[0, :, sl]) / jnp.sum(p, axis=-1, keepdims=True)
        o_ref[0, :, sl] = o.astype(BF16)


def _mem(y3d, mk, mv, gq, ts=512):
    b, s, _ = y3d.shape
    return pl.pallas_call(
        _mem_kernel,
        grid=(b, s // ts),
        in_specs=[pl.BlockSpec((1, ts, D_MODEL), lambda i, j: (i, j, C_MQ // D_MODEL)),
                  pl.BlockSpec((1, N_MEM, D_MODEL), lambda i, j: (i, 0, 0)),
                  pl.BlockSpec((1, N_MEM, D_MODEL), lambda i, j: (i, 0, 0)),
                  pl.BlockSpec((1, MEM_HD), lambda i, j: (0, 0))],
        out_specs=pl.BlockSpec((1, ts, D_MODEL), lambda i, j: (i, j, 0)),
        out_shape=jax.ShapeDtypeStruct((b, s, D_MODEL), BF16),
        compiler_params=_cparams(2),
        name="mem",
    )(y3d, mk, mv, gq)


def _merge_kernel(x_ref, b0_ref, b1_ref, b2_ref, b3_ref, gt_ref, bg_ref, wb_ref, wo_ref, o_ref):
    mixed = None
    for n, br in enumerate((b0_ref, b1_ref, b2_ref, b3_ref)):
        gate = jax.nn.sigmoid(
            gt_ref[:, n * D_MODEL:(n + 1) * D_MODEL].astype(F32) + bg_ref[n:n + 1, :])
        term = gate * _dot(br[...], wb_ref[n])
        mixed = term if mixed is None else mixed + term
    o_ref[...] = x_ref[...] + _dot(mixed.astype(BF16), wo_ref[...])


def _merge(x2d, branches, y2d, bg, wb, wo, layer, tm=256):
    m = x2d.shape[0]
    row = lambda w: pl.BlockSpec((tm, w), lambda i: (i, 0))
    return pl.pallas_call(
        _merge_kernel,
        grid=(m // tm,),
        in_specs=[row(D_MODEL), row(D_MODEL), row(D_MODEL), row(D_MODEL), row(D_MODEL),
                  pl.BlockSpec((tm, 4 * D_MODEL), lambda i: (i, C_GT // (4 * D_MODEL))),
                  pl.BlockSpec((4, D_MODEL), lambda i: (0, 0)),
                  pl.BlockSpec((None, 4, D_MODEL, D_MODEL), lambda i: (layer, 0, 0, 0)),
                  pl.BlockSpec((None, D_MODEL, D_MODEL), lambda i: (layer, 0, 0))],
        out_specs=row(D_MODEL),
        out_shape=jax.ShapeDtypeStruct((m, D_MODEL), F32),
        compiler_params=_cparams(1),
        name="merge",
    )(x2d, *branches, y2d, bg, wb, wo)


FFN_TC = 256
FFN_HALO = 8
FFN_RB = 64


def _ffn_kernel(x_ref, g_ref, wg_ref, wv_ref, cw_ref, cb_ref, wd_ref, o_ref,
                h_ref, gbuf_ref, vbuf_ref, act_ref, *, tm, tiles_per_seq):
    i = pl.program_id(0)
    x = x_ref[...]
    h_ref[...] = (x * lax.rsqrt(jnp.mean(x * x, axis=-1, keepdims=True) + EPS)
                  * g_ref[...]).astype(BF16)

    @pl.when((i % tiles_per_seq) == 0)
    def _():
        gbuf_ref[0:FFN_HALO, :] = jnp.zeros((FFN_HALO, D_FF), F32)

    def up(c):
        cols = slice(c * FFN_TC, (c + 1) * FFN_TC)
        gbuf_ref[FFN_HALO:FFN_HALO + tm, cols] = _dot(h_ref[...], wg_ref[:, cols])
        vbuf_ref[:, cols] = _dot(h_ref[...], wv_ref[:, cols])

    def activate(c):
        cols = slice(c * FFN_TC, (c + 1) * FFN_TC)
        cw = cw_ref[:, cols]
        cb = cb_ref[:, cols]
        for r0 in range(0, tm, FFN_RB):
            gate = cb
            for d in range(3):
                lo = FFN_HALO + r0 - d
                gate = gate + gbuf_ref[lo:lo + FFN_RB, cols] * cw[2 - d:3 - d, :]
            act = gate * jax.nn.sigmoid(gate) * vbuf_ref[r0:r0 + FFN_RB, cols]
            act_ref[r0:r0 + FFN_RB, cols] = act.astype(BF16)

    nc = D_FF // FFN_TC
    up(0)
    for c in range(1, nc):
        up(c)
        activate(c - 1)
    activate(nc - 1)
    o_ref[...] = x_ref[...] + _dot(act_ref[...], wd_ref[...])
    gbuf_ref[0:FFN_HALO, :] = gbuf_ref[tm:tm + FFN_HALO, :]


def _ffn(x2d, g, w_up, cw, cb, w_down, layer, seq, tm=512):
    m = x2d.shape[0]
    resident = pl.Buffered(1)
    return pl.pallas_call(
        functools.partial(_ffn_kernel, tm=tm, tiles_per_seq=seq // tm),
        grid=(m // tm,),
        in_specs=[pl.BlockSpec((tm, D_MODEL), lambda i: (i, 0)),
                  pl.BlockSpec((1, D_MODEL), lambda i: (0, 0)),
                  pl.BlockSpec((None, D_MODEL, D_FF), lambda i: (layer, 0, 0),
                               pipeline_mode=resident),
                  pl.BlockSpec((None, D_MODEL, D_FF), lambda i: (layer, 0, 1),
                               pipeline_mode=resident),
                  pl.BlockSpec((3, D_FF), lambda i: (0, 0)),
                  pl.BlockSpec((1, D_FF), lambda i: (0, 0)),
                  pl.BlockSpec((None, D_FF, D_MODEL), lambda i: (layer, 0, 0),
                               pipeline_mode=resident)],
        out_specs=pl.BlockSpec((tm, D_MODEL), lambda i: (i, 0)),
        out_shape=jax.ShapeDtypeStruct((m, D_MODEL), F32),
        scratch_shapes=[pltpu.VMEM((tm, D_MODEL), BF16),
                        pltpu.VMEM((tm + FFN_HALO, D_FF), F32),
                        pltpu.VMEM((tm, D_FF), F32),
                        pltpu.VMEM((tm, D_FF), BF16)],
        compiler_params=_cparams(1),
        name="ffn",
    )(x2d, g, w_up, w_up, cw, cb, w_down)


def _block_diag(w):
    per = LRU_TC // HEAD_DIM
    w4 = w.reshape(D_MODEL // LRU_TC, per, HEAD_DIM, HEAD_DIM)
    eye = jnp.eye(per, dtype=w.dtype)
    bd = jnp.einsum('cpde,pq->cpdqe', w4, eye)
    return bd.reshape(D_MODEL // LRU_TC, LRU_TC, LRU_TC)


def kernel(x, mem, attn_norm_g, mem_norm_g, w_in, b_forget, fox_q_norm_g, fox_k_norm_g,
           lru_conv_w, lru_conv_b, lru_w_a, lru_b_a, lru_w_x, lru_b_x, lru_lambda,
           w_mem_kv, mem_q_norm_g, mem_k_norm_g, b_gate, w_branch, w_out,
           ffn_norm_g, w_up, ffn_conv_w, ffn_conv_b, w_down):
    b, s, d = x.shape
    depth = w_in.shape[0]
    m = b * s
    x2d = x.reshape(m, d)
    w_branch_bf, w_out_bf = w_branch.astype(BF16), w_out.astype(BF16)
    w_up_bf, w_down_bf = w_up.astype(BF16), w_down.astype(BF16)
    col = jnp.arange(w_in.shape[-1])
    sb_q0 = C_SQ + W_F_END - C_LX
    in_scale = jnp.where((col >= sb_q0) & (col < sb_q0 + D_MODEL), HEAD_DIM ** -0.5 * LOG2E, 1.0)
    w_in_bf = (w_in * in_scale.astype(F32)).astype(BF16)
    for l in range(depth):
        wy = _wprep(w_in_bf, l)
        wf = jnp.pad(w_in[l, :, W_FOX_END:W_F_END], ((0, 0), (0, LANES - N_HEADS)))
        wf_hi = wf.astype(BF16)
        wf = jnp.stack([wf_hi, (wf - wf_hi.astype(F32)).astype(BF16)])
        bf = jnp.pad(b_forget[l], (0, LANES - N_HEADS)).reshape(1, LANES)

        y2d, f2d = _proj(x2d, attn_norm_g[l].reshape(1, d), wy, wf)
        y3d = y2d.reshape(b, s, N_Y)

        fcum = _fcum(f2d.reshape(b, s, LANES), bf)
        qa, ka, vt = _foxprep(y3d, fcum,
                              jnp.tile(fox_q_norm_g[l], 2).reshape(1, LANES),
                              jnp.tile(fox_k_norm_g[l], 2).reshape(1, LANES))
        y_fox = _fox(qa, ka, vt)
        y_sb = _sb(y3d)
        y_lru = _lru(y3d, lru_conv_w[l], lru_conv_b[l].reshape(1, d),
                     _block_diag(lru_w_a[l]).astype(BF16), lru_b_a[l].reshape(1, d),
                     _block_diag(lru_w_x[l]).astype(BF16), lru_b_x[l].reshape(1, d),
                     lru_lambda[l].reshape(1, d))
        mk, mv = _memkv(mem, mem_norm_g[l].reshape(1, d), w_mem_kv[l].astype(BF16),
                        mem_k_norm_g[l].reshape(1, MEM_HD))
        y_mem = _mem(y3d, mk, mv, mem_q_norm_g[l].reshape(1, MEM_HD))

        branches = [t.reshape(m, d) for t in (y_fox, y_lru, y_sb, y_mem)]
        x2d = _merge(x2d, branches, y2d, b_gate[l], w_branch_bf, w_out_bf, l)
        x2d = _ffn(x2d, ffn_norm_g[l].reshape(1, d), w_up_bf, ffn_conv_w[l],
                   ffn_conv_b[l].reshape(1, D_FF), w_down_bf, l, s)
    return x2d.reshape(b, s, d)
```

```python
import functools

import jax
import jax.numpy as jnp
from jax import lax
from jax.experimental import pallas as pl
from jax.experimental.pallas import tpu as pltpu

F32 = jnp.float32
BF16 = jnp.bfloat16

D_MODEL = 1024
HEAD_DIM = 64
N_HEADS = 16
N_PAIRS = N_HEADS // 2
LANES = 128
SUBLANES = 8
N_MEM = 256
MEM_HEADS = 4
MEM_HD = 256
D_FF = 2816
LRU_C = 8.0
EPS = 1e-6
NEG = -1e30
LOG2E = 1.4426950408889634
F32_TINY = 1.1754944e-38

C_GT, C_FQ, C_FK, C_FV, C_LX, C_LG, C_SQ, C_SK, C_SV, C_MQ = (
    0, 4096, 5120, 6144, 7168, 8192, 9216, 10240, 11264, 12288)
N_Y = 13312
W_FOX_END, W_F_END, W_GATES = 3072, 3088, 9232

VMEM_LIMIT = 56 * 1024 * 1024


def _cparams(n_axes):
    return pltpu.CompilerParams(dimension_semantics=("arbitrary",) * n_axes,
                                vmem_limit_bytes=VMEM_LIMIT)


def _dot(a, b):
    return jnp.dot(a, b, preferred_element_type=F32)


def _dot_nt(a, b):
    return lax.dot_general(a, b, (((1,), (1,)), ((), ())), preferred_element_type=F32)


def _log1p_exp_neg_abs(z):
    return jnp.log(1.0 + jnp.exp(-jnp.abs(z)))


def _wprep_kernel(w_ref, o_ref):
    o_ref[:, C_GT:C_FQ] = w_ref[:, W_GATES:].astype(BF16)
    o_ref[:, C_FQ:C_LX] = w_ref[:, :W_FOX_END].astype(BF16)
    shift = W_F_END - C_LX
    o_ref[:, C_LX:C_SQ] = w_ref[:, C_LX + shift:C_SQ + shift].astype(BF16)
    o_ref[:, C_SQ:C_SK] = (w_ref[:, C_SQ + shift:C_SK + shift]
                           * (HEAD_DIM ** -0.5 * LOG2E)).astype(BF16)
    o_ref[:, C_SK:N_Y] = w_ref[:, C_SK + shift:N_Y + shift].astype(BF16)


def _wprep(w_in, layer, tr=128):
    _, rows, cols = w_in.shape
    return pl.pallas_call(
        _wprep_kernel,
        grid=(rows // tr,),
        in_specs=[pl.BlockSpec((None, tr, cols), lambda i: (layer, i, 0))],
        out_specs=pl.BlockSpec((tr, N_Y), lambda i: (i, 0)),
        out_shape=jax.ShapeDtypeStruct((rows, N_Y), BF16),
        compiler_params=_cparams(1),
        name="wprep",
    )(w_in)


def _proj_kernel(x_ref, g_ref, w_ref, wf_ref, y_ref, f_ref, h_ref):
    @pl.when(pl.program_id(1) == 0)
    def _():
        x = x_ref[...]
        h = x * lax.rsqrt(jnp.mean(x * x, axis=-1, keepdims=True) + EPS) * g_ref[...]
        h_hi = h.astype(BF16)
        h_ref[...] = h_hi
        h_lo = (h - h_hi.astype(F32)).astype(BF16)
        f_ref[...] = (_dot(h_hi, wf_ref[0]) + _dot(h_lo, wf_ref[0])) + _dot(h_hi, wf_ref[1])

    y_ref[...] = _dot(h_ref[...], w_ref[...]).astype(BF16)


def _proj(x2d, g, wy, wf, tm=2048, tn=1024):
    m = x2d.shape[0]
    return pl.pallas_call(
        _proj_kernel,
        grid=(m // tm, N_Y // tn),
        in_specs=[pl.BlockSpec((tm, D_MODEL), lambda i, j: (i, 0)),
                  pl.BlockSpec((1, D_MODEL), lambda i, j: (0, 0)),
                  pl.BlockSpec((D_MODEL, tn), lambda i, j: (0, j)),
                  pl.BlockSpec((2, D_MODEL, LANES), lambda i, j: (0, 0, 0))],
        out_specs=[pl.BlockSpec((tm, tn), lambda i, j: (i, j)),
                   pl.BlockSpec((tm, LANES), lambda i, j: (i, 0))],
        out_shape=[jax.ShapeDtypeStruct((m, N_Y), BF16),
                   jax.ShapeDtypeStruct((m, LANES), F32)],
        scratch_shapes=[pltpu.VMEM((tm, D_MODEL), BF16)],
        compiler_params=_cparams(2),
        name="proj",
    )(x2d, g, wy, wf)


F_PIECES = 3
FOX_TS = 512
FOX_KB = 64
FOX_ONES = 16
FOX_VROWS = HEAD_DIM + FOX_ONES


def _fcum_kernel(f_ref, b_ref, o_ref, carry_ref):
    @pl.when(pl.program_id(1) == 0)
    def _():
        carry_ref[...] = jnp.zeros_like(carry_ref)

    z = f_ref[0] + b_ref[...]
    log_f = jnp.minimum(z, 0.0) - _log1p_exp_neg_abs(z)
    ts = z.shape[0]
    row = lax.broadcasted_iota(jnp.int32, (ts, ts), 0)
    col = lax.broadcasted_iota(jnp.int32, (ts, ts), 1)
    tri = (row >= col).astype(BF16)
    hi = log_f.astype(BF16)
    r1 = log_f - hi.astype(F32)
    mid = r1.astype(BF16)
    lo = (r1 - mid.astype(F32)).astype(BF16)
    c = (_dot(tri, hi) + _dot(tri, mid)) + _dot(tri, lo) + carry_ref[...]
    carry_ref[...] = c[ts - 1:ts, :]

    lane = lax.broadcasted_iota(jnp.int32, (ts, LANES), 1)
    f = jnp.where(lane < N_HEADS, c * LOG2E, 0.0)
    hi = f.astype(BF16).astype(F32)
    r1 = f - hi
    mid = r1.astype(BF16).astype(F32)
    lo = (r1 - mid).astype(BF16).astype(F32)
    packed = hi + pltpu.roll(mid, N_HEADS, axis=1) + pltpu.roll(lo, 2 * N_HEADS, axis=1)
    o_ref[0] = packed.astype(BF16)


def _fcum(f3d, b_pad, ts=512):
    b, s, _ = f3d.shape
    return pl.pallas_call(
        _fcum_kernel,
        grid=(b, s // ts),
        in_specs=[pl.BlockSpec((1, ts, LANES), lambda i, j: (i, j, 0)),
                  pl.BlockSpec((1, LANES), lambda i, j: (0, 0))],
        out_specs=pl.BlockSpec((1, ts, LANES), lambda i, j: (i, j, 0)),
        out_shape=jax.ShapeDtypeStruct((b, s, LANES), BF16),
        scratch_shapes=[pltpu.VMEM((1, LANES), F32)],
        compiler_params=_cparams(2),
        name="fcum",
    )(f3d, b_pad)


def _foxprep_kernel(yq_ref, yk_ref, yv_ref, f_ref, gq_ref, gk_ref, qa_ref, ka_ref, vt_ref):
    hp = pl.program_id(1)
    ones = jnp.ones((FOX_ONES, FOX_TS), BF16)
    for c in range(vt_ref.shape[2]):
        vt = yv_ref[0, c * FOX_TS:(c + 1) * FOX_TS, :].astype(F32).T.astype(BF16)
        vt_ref[0, 0, c] = jnp.concatenate(
            [vt[:HEAD_DIM], ones, vt[HEAD_DIM:], ones], axis=0)
    ts = yq_ref.shape[1]
    lane = lax.broadcasted_iota(jnp.int32, (ts, LANES), 1)
    lo_half = lane < HEAD_DIM

    hr = lax.broadcasted_iota(jnp.int32, (LANES, LANES), 0)
    hc = lax.broadcasted_iota(jnp.int32, (LANES, LANES), 1)
    same_head = ((hr < HEAD_DIM) == (hc < HEAD_DIM)).astype(BF16)

    def headnorm(y, g):
        ms = _dot((y * y).astype(BF16), same_head) * (1.0 / HEAD_DIM)
        return y * lax.rsqrt(ms + EPS) * g

    qn = headnorm(yq_ref[0].astype(F32), gq_ref[...]) * (HEAD_DIM ** -0.5 * LOG2E)
    kn = headnorm(yk_ref[0].astype(F32), gk_ref[...])

    pieces = f_ref[0]
    sel_r = lax.broadcasted_iota(jnp.int32, (LANES, 2 * LANES), 0)
    sel_c = lax.broadcasted_iota(jnp.int32, (LANES, 2 * LANES), 1)
    ones_q = ((lane >= HEAD_DIM + F_PIECES) & (lane < HEAD_DIM + 2 * F_PIECES)).astype(F32)
    ones_k = ((lane >= HEAD_DIM) & (lane < HEAD_DIM + F_PIECES)).astype(F32)
    for e in range(2):
        head = 2 * hp + e
        piece_of_row = sel_r - head
        to_q = (piece_of_row == (sel_c - HEAD_DIM) * N_HEADS) & (sel_c < HEAD_DIM + F_PIECES)
        k_col0 = LANES + HEAD_DIM + F_PIECES
        to_k = (piece_of_row == (sel_c - k_col0) * N_HEADS) & (sel_c >= k_col0)
        valid = (piece_of_row >= 0) & (piece_of_row < F_PIECES * N_HEADS)
        sel = (jnp.where(valid & to_q, 1.0, 0.0) - jnp.where(valid & to_k, 1.0, 0.0)).astype(BF16)
        placed = _dot(pieces, sel)
        q_part = qn if e == 0 else pltpu.roll(qn, HEAD_DIM, axis=1)
        k_part = kn if e == 0 else pltpu.roll(kn, HEAD_DIM, axis=1)
        q_aug = jnp.where(lo_half, q_part, placed[:, :LANES] + ones_q)
        k_aug = jnp.where(lo_half, k_part, placed[:, LANES:] + ones_k)
        qa_ref[0, e] = q_aug.astype(BF16)
        ka_ref[0, e] = k_aug.astype(BF16)


def _foxprep(y3d, fcum, gq2, gk2, ts=2048):
    b, s, _ = y3d.shape
    out = jax.ShapeDtypeStruct((b, N_HEADS, s, LANES), BF16)
    vt = jax.ShapeDtypeStruct((b, N_PAIRS, s // FOX_TS, 2 * FOX_VROWS, FOX_TS), BF16)
    return pl.pallas_call(
        _foxprep_kernel,
        grid=(b, N_PAIRS, s // ts),
        in_specs=[pl.BlockSpec((1, ts, LANES), lambda i, p, j: (i, j, C_FQ // LANES + p)),
                  pl.BlockSpec((1, ts, LANES), lambda i, p, j: (i, j, C_FK // LANES + p)),
                  pl.BlockSpec((1, ts, LANES), lambda i, p, j: (i, j, C_FV // LANES + p)),
                  pl.BlockSpec((1, ts, LANES), lambda i, p, j: (i, j, 0)),
                  pl.BlockSpec((1, LANES), lambda i, p, j: (0, 0)),
                  pl.BlockSpec((1, LANES), lambda i, p, j: (0, 0))],
        out_specs=[pl.BlockSpec((1, 2, ts, LANES), lambda i, p, j: (i, p, j, 0)),
                   pl.BlockSpec((1, 2, ts, LANES), lambda i, p, j: (i, p, j, 0)),
                   pl.BlockSpec((1, 1, ts // FOX_TS, 2 * FOX_VROWS, FOX_TS),
                                lambda i, p, j: (i, p, j, 0, 0))],
        out_shape=[out, out, vt],
        compiler_params=_cparams(3),
        name="foxprep",
    )(y3d, y3d, y3d, fcum, gq2, gk2)


def _fox_kernel(qa_ref, ka_ref, vt_ref, o_ref, s_ref, p_ref, m_ref, a_ref, acc_ref, *, tq):
    i = pl.program_id(2)
    ts = FOX_TS
    nslab = tq // ts
    chains = [(h, e) for h in range(nslab) for e in range(2)]
    key = lax.broadcasted_iota(jnp.int32, (FOX_KB, LANES), 0)
    query = lax.broadcasted_iota(jnp.int32, (FOX_KB, LANES), 1)

    m_ref[...] = jnp.full(m_ref.shape, NEG, F32)
    acc_ref[...] = jnp.zeros(acc_ref.shape, F32)

    def logits(n, block):
        h, e = chains[n]
        start = pl.multiple_of(block * ts, ts)
        s = _dot_nt(ka_ref[0, e, pl.ds(start, ts), :], qa_ref[0, e, h * ts:(h + 1) * ts, :])
        for c in range(ts // LANES):
            s_ref[n, c, 0:ts, :] = s[:, c * LANES:(c + 1) * LANES]

    def softmax(n, diagonal):
        kb = FOX_KB
        strips = range(0, ts, LANES)

        def piece(c0, k0):
            s = s_ref[n, c0 // LANES, k0:k0 + kb, :]
            if diagonal and k0 + kb > c0:
                s = jnp.where(key + k0 <= query + c0, s, NEG)
            return s

        def last_key(c0):
            return c0 + LANES if diagonal else ts

        m_new = {}
        for c0 in strips:
            cols = slice(c0, c0 + LANES)
            top = piece(c0, 0)
            for k0 in range(kb, last_key(c0), kb):
                top = jnp.maximum(top, piece(c0, k0))
            m_old = m_ref[n, :, cols]
            m_new[c0] = jnp.maximum(m_old, jnp.max(top, axis=0, keepdims=True))
            m_ref[n, :, cols] = m_new[c0]
            a_ref[n, :, cols] = jnp.exp2(m_old - m_new[c0])
        for c0 in strips:
            for k0 in range(0, last_key(c0), kb):
                p_ref[n, c0 // LANES, k0:k0 + kb, :] = jnp.exp2(
                    (piece(c0, k0) - m_new[c0][0:1, :]).astype(BF16))
            if last_key(c0) < ts:
                p_ref[n, c0 // LANES, last_key(c0):ts, :] = jnp.zeros(
                    (ts - last_key(c0), LANES), BF16)

    def accumulate(n, block):
        e = chains[n][1]
        vt = vt_ref[0, 0, block, e * FOX_VROWS:(e + 1) * FOX_VROWS, :]
        p = jnp.concatenate([p_ref[n, c, 0:ts, :] for c in range(ts // LANES)], axis=1)
        acc_ref[n] = a_ref[n, 0:1, :] * acc_ref[n] + _dot(vt, p)

    def step(block, live, diagonal_slab, live_next):
        prev = None
        for n in live:
            softmax(n, chains[n][0] == diagonal_slab)
            if n in live_next:
                logits(n, block + 1)
            if prev is not None:
                accumulate(prev, block)
            prev = n
        accumulate(prev, block)

    everyone = list(range(len(chains)))
    for n in everyone:
        logits(n, 0)

    def body(j, _):
        step(j, everyone, None, everyone)
        return 0

    lax.fori_loop(0, i * nslab, body, 0)
    for c in range(nslab):
        live = [n for n in everyone if chains[n][0] >= c]
        live_next = [n for n in everyone if chains[n][0] >= c + 1] if c + 1 < nslab else []
        step(i * nslab + c, live, c, live_next)

    for h in range(nslab):
        outs = [acc_ref[2 * h + e, 0:HEAD_DIM, :] / acc_ref[2 * h + e, HEAD_DIM:HEAD_DIM + 1, :]
                for e in range(2)]
        o_ref[0, h * ts:(h + 1) * ts, :] = jnp.concatenate(outs, axis=0).T.astype(BF16)


def _fox(qa, ka, vt, tq=2048):
    b, _, s, _ = qa.shape
    nch = 2 * (tq // FOX_TS)
    return pl.pallas_call(
        functools.partial(_fox_kernel, tq=tq),
        grid=(b, N_PAIRS, s // tq),
        in_specs=[pl.BlockSpec((1, 2, tq, LANES), lambda i, p, j: (i, p, j, 0)),
                  pl.BlockSpec((1, 2, s, LANES), lambda i, p, j: (i, p, 0, 0)),
                  pl.BlockSpec((1, 1, s // FOX_TS, 2 * FOX_VROWS, FOX_TS),
                               lambda i, p, j: (i, p, 0, 0, 0))],
        out_specs=pl.BlockSpec((1, tq, LANES), lambda i, p, j: (i, j, p)),
        out_shape=jax.ShapeDtypeStruct((b, s, D_MODEL), BF16),
        scratch_shapes=[pltpu.VMEM((nch, FOX_TS // LANES, FOX_TS + SUBLANES, LANES), F32),
                        pltpu.VMEM((nch, FOX_TS // LANES, FOX_TS + 2 * SUBLANES, LANES), BF16),
                        pltpu.VMEM((nch, SUBLANES, FOX_TS), F32),
                        pltpu.VMEM((nch, SUBLANES, FOX_TS), F32),
                        pltpu.VMEM((nch, FOX_VROWS, FOX_TS), F32)],
        compiler_params=_cparams(3),
        name="fox",
    )(qa, ka, vt)


SB_KC = 256
SB_TS = 512
SB_MAX_LOG2 = 126.0
SB_DEAD_LOG2 = -(SB_MAX_LOG2 + 150.0)


SB_RB = 64
SB_UNROLL = 1


def _sb_kernel(q_ref, k_ref, v_ref, o_ref, qm_ref, z_ref, lb_ref, ic_ref, w_ref,
               rs_ref, r_ref, acc_ref, *, tq):
    i = pl.program_id(2)
    kc, ts, rb = SB_KC, SB_TS, SB_RB
    nsub = tq // kc
    nslab = tq // ts
    chains = [(h, e) for h in range(nslab) for e in range(2)]
    row = lax.broadcasted_iota(jnp.int32, (rb, kc), 0)
    col = lax.broadcasted_iota(jnp.int32, (rb, kc), 1)
    kr = lax.broadcasted_iota(jnp.int32, (kc, kc), 0)
    kcol = lax.broadcasted_iota(jnp.int32, (kc, kc), 1)
    at_or_after = (kr >= kcol).astype(BF16)

    lane = lax.broadcasted_iota(jnp.int32, (ts, LANES), 1)
    for n, (h, e) in enumerate(chains):
        q2 = q_ref[0, h * ts:(h + 1) * ts, :]
        mine = (lane < HEAD_DIM) if e == 0 else (lane >= HEAD_DIM)
        qm_ref[n] = jnp.where(mine, q2, jnp.zeros_like(q2))
    r_ref[...] = jnp.zeros(r_ref.shape, F32)
    acc_ref[...] = jnp.zeros(acc_ref.shape, F32)

    def logits(n, start):
        z_ref[n] = jnp.minimum(_dot_nt(qm_ref[n], k_ref[0, pl.ds(start, kc), :]), SB_MAX_LOG2)

    def log_terms(n, offset):
        for r0 in range(0, ts, rb):
            rows = slice(r0, r0 + rb)
            log_1m_beta = jnp.log(1.0 + jnp.exp2(z_ref[n, rows, :])) * (-LOG2E)
            if offset is not None:
                log_1m_beta = jnp.where(col + (offset - r0) < row, log_1m_beta, 0.0)
            lb_ref[n, rows, :] = log_1m_beta.astype(BF16)
            rs_ref[n, rows, :] = jnp.broadcast_to(
                jnp.sum(log_1m_beta, axis=-1, keepdims=True), (rb, LANES))

    def cumulate(n):
        ic_ref[n] = _dot(lb_ref[n], at_or_after)

    def weights(n, offset):
        for r0 in range(0, ts, rb):
            rows = slice(r0, r0 + rb)
            r = r_ref[n, rows, :]
            log_w = (z_ref[n, rows, :] + ic_ref[n, rows, :]
                     + jnp.concatenate([r] * (kc // LANES), axis=1))
            if offset is not None:
                log_w = jnp.where(col + (offset - r0) < row, log_w, NEG)
            w_ref[n, rows, :] = jnp.exp2(log_w.astype(BF16))
            r_ref[n, rows, :] = r + rs_ref[n, rows, :]

    def accumulate(n, start):
        acc_ref[n] += _dot(w_ref[n], v_ref[0, pl.ds(start, kc), :])

    def chunk(start, next_start, live, offsets, live_next):
        prev = None
        for n in live:
            log_terms(n, offsets[chains[n][0]])
            cumulate(n)
            if prev is not None:
                weights(prev, offsets[chains[prev][0]])
                if prev in live_next:
                    logits(prev, next_start)
                accumulate(prev, start)
            prev = n
        weights(prev, offsets[chains[prev][0]])
        if prev in live_next:
            logits(prev, next_start)
        accumulate(prev, start)
        for n in live_next:
            if n not in live:
                logits(n, next_start)

    def visibility(c):
        live, offsets = [], {}
        for n, (h, _) in enumerate(chains):
            if c * kc >= (h + 1) * ts:
                continue
            live.append(n)
            offsets[h] = None if (c + 1) * kc <= h * ts else c * kc - h * ts
        return live, offsets

    everyone = list(range(len(chains)))
    no_offsets = {h: None for h in range(nslab)}
    live, offsets = visibility(nsub - 1)
    for n in live:
        logits(n, pl.multiple_of(i * tq + (nsub - 1) * kc, kc))
    for c in reversed(range(nsub)):
        start = pl.multiple_of(i * tq + c * kc, kc)
        live_next = visibility(c - 1)[0] if c > 0 else everyone
        chunk(start, pl.multiple_of(jnp.maximum(start - kc, 0), kc), live, offsets, live_next)
        if c > 0:
            live, offsets = visibility(c - 1)

    def alive(first_chain):
        return (jnp.max(r_ref[first_chain:]) > SB_DEAD_LOG2).astype(jnp.int32)

    trips = i * (nsub // SB_UNROLL)

    def walk(t0, live, first_chain):
        def body(state):
            t, _ = state
            for u in range(SB_UNROLL):
                start = pl.multiple_of(i * tq - (t * SB_UNROLL + u + 1) * kc, kc)
                chunk(start, pl.multiple_of(jnp.maximum(start - kc, 0), kc), live, no_offsets,
                      live)
            return t + 1, alive(first_chain)

        t_end, _ = lax.while_loop(lambda state: (state[0] < trips) & (state[1] > 0), body,
                                  (t0, alive(first_chain)))
        return t_end

    t_mid = walk(jnp.int32(0), everyone, len(chains) - 2)
    walk(t_mid, everyone[:2], 0)
    for h in range(nslab):
        o_ref[0, h * ts:(h + 1) * ts, :] = jnp.where(
            lane < HEAD_DIM, acc_ref[2 * h], acc_ref[2 * h + 1]).astype(BF16)


def _sb(y3d, tq=1024):
    b, s, _ = y3d.shape
    nch = 2 * (tq // SB_TS)
    return pl.pallas_call(
        functools.partial(_sb_kernel, tq=tq),
        grid=(b, N_PAIRS, s // tq),
        in_specs=[pl.BlockSpec((1, tq, LANES), lambda i, p, j: (i, j, C_SQ // LANES + p)),
                  pl.BlockSpec((1, s, LANES), lambda i, p, j: (i, 0, C_SK // LANES + p)),
                  pl.BlockSpec((1, s, LANES), lambda i, p, j: (i, 0, C_SV // LANES + p))],
        out_specs=pl.BlockSpec((1, tq, LANES), lambda i, p, j: (i, j, p)),
        out_shape=jax.ShapeDtypeStruct((b, s, D_MODEL), BF16),
        scratch_shapes=[pltpu.VMEM((nch, SB_TS, LANES), BF16),
                        pltpu.VMEM((nch, SB_TS, SB_KC), F32),
                        pltpu.VMEM((nch, SB_TS, SB_KC), BF16),
                        pltpu.VMEM((nch, SB_TS, SB_KC), F32),
                        pltpu.VMEM((nch, SB_TS, SB_KC), BF16),
                        pltpu.VMEM((nch, SB_TS, LANES), F32),
                        pltpu.VMEM((nch, SB_TS, LANES), F32),
                        pltpu.VMEM((nch, SB_TS, LANES), F32)],
        compiler_params=_cparams(3),
        name="sb",
    )(y3d, y3d, y3d)


LRU_TC = 256
LRU_HALO = 8


def _lru_kernel(x_ref, g_ref, cw_ref, cb_ref, wa_ref, ba_ref, wx_ref, bx_ref, lam_ref,
                o_ref, xbuf_ref, h_ref, *, ts):
    @pl.when(pl.program_id(2) == 0)
    def _():
        xbuf_ref[0:LRU_HALO, :] = jnp.zeros((LRU_HALO, LRU_TC), F32)
        h_ref[...] = jnp.zeros_like(h_ref)

    xbuf_ref[LRU_HALO:LRU_HALO + ts, :] = x_ref[0].astype(F32)
    cw = cw_ref[...]
    xc = cb_ref[...] + xbuf_ref[LRU_HALO:LRU_HALO + ts, :] * cw[3:4, :]
    for d in range(1, 4):
        xc = xc + xbuf_ref[LRU_HALO - d:LRU_HALO - d + ts, :] * cw[3 - d:4 - d, :]
    xbuf_ref[0:LRU_HALO, :] = xbuf_ref[ts:ts + LRU_HALO, :]

    xcb = xc.astype(BF16)
    r = jax.nn.sigmoid(_dot(xcb, wa_ref[0]) + ba_ref[...])
    gi = jax.nn.sigmoid(_dot(xcb, wx_ref[0]) + bx_ref[...])
    lam = lam_ref[...]
    softplus_neg_lam = jnp.maximum(-lam, 0.0) + _log1p_exp_neg_abs(lam)
    log_a = -LRU_C * r * softplus_neg_lam
    a = jnp.exp(log_a)
    one_m_a2 = -jnp.tanh(log_a) * (a * a + 1.0)
    u = one_m_a2 * lax.rsqrt(jnp.maximum(one_m_a2, F32_TINY)) * (gi * xc)

    row = lax.broadcasted_iota(jnp.int32, (ts, LRU_TC), 0) % SUBLANES
    d = 1
    while d < SUBLANES:
        keep = row >= d
        a_sh = jnp.where(keep, pltpu.roll(a, d, axis=0), 1.0)
        u_sh = jnp.where(keep, pltpu.roll(u, d, axis=0), 0.0)
        u = a * u_sh + u
        a = a * a_sh
        d *= 2
    state = h_ref[...]
    groups = []
    for r0 in range(0, ts, SUBLANES):
        h = u[r0:r0 + SUBLANES, :] + a[r0:r0 + SUBLANES, :] * state
        groups.append(h)
        state = h[SUBLANES - 1:SUBLANES, :]
    h_ref[...] = state
    h = jnp.concatenate(groups, axis=0)
    o_ref[0] = (h * jax.nn.gelu(g_ref[0].astype(F32))).astype(BF16)


def _lru(y3d, cw, cb, wa_bd, ba, wx_bd, bx, lam, ts=512):
    b, s, _ = y3d.shape
    nct = D_MODEL // LRU_TC
    vec = lambda rows: pl.BlockSpec((rows, LRU_TC), lambda i, c, j: (0, c))
    mat = pl.BlockSpec((1, LRU_TC, LRU_TC), lambda i, c, j: (c, 0, 0))
    return pl.pallas_call(
        functools.partial(_lru_kernel, ts=ts),
        grid=(b, nct, s // ts),
        in_specs=[pl.BlockSpec((1, ts, LRU_TC), lambda i, c, j: (i, j, C_LX // LRU_TC + c)),
                  pl.BlockSpec((1, ts, LRU_TC), lambda i, c, j: (i, j, C_LG // LRU_TC + c)),
                  vec(4), vec(1), mat, vec(1), mat, vec(1), vec(1)],
        out_specs=pl.BlockSpec((1, ts, LRU_TC), lambda i, c, j: (i, j, c)),
        out_shape=jax.ShapeDtypeStruct((b, s, D_MODEL), BF16),
        scratch_shapes=[pltpu.VMEM((ts + LRU_HALO, LRU_TC), F32),
                        pltpu.VMEM((1, LRU_TC), F32)],
        compiler_params=_cparams(3),
        name="lru",
    )(y3d, y3d, cw, cb, wa_bd, ba, wx_bd, bx, lam)


def _memkv_kernel(mem_ref, g_ref, w_ref, gk_ref, k_ref, v_ref):
    x = mem_ref[0]
    h = x * lax.rsqrt(jnp.mean(x * x, axis=-1, keepdims=True) + EPS) * g_ref[...]
    kv = _dot(h.astype(BF16), w_ref[...])
    for hd in range(MEM_HEADS):
        kh = kv[:, hd * MEM_HD:(hd + 1) * MEM_HD]
        kh = kh * lax.rsqrt(jnp.mean(kh * kh, axis=-1, keepdims=True) + EPS) * gk_ref[...]
        k_ref[0, :, hd * MEM_HD:(hd + 1) * MEM_HD] = kh.astype(BF16)
    v_ref[0] = kv[:, D_MODEL:].astype(BF16)


def _memkv(mem, g, w_kv, gk):
    b = mem.shape[0]
    out = jax.ShapeDtypeStruct((b, N_MEM, D_MODEL), BF16)
    return pl.pallas_call(
        _memkv_kernel,
        grid=(b,),
        in_specs=[pl.BlockSpec((1, N_MEM, D_MODEL), lambda i: (i, 0, 0)),
                  pl.BlockSpec((1, D_MODEL), lambda i: (0, 0)),
                  pl.BlockSpec((D_MODEL, 2 * D_MODEL), lambda i: (0, 0)),
                  pl.BlockSpec((1, MEM_HD), lambda i: (0, 0))],
        out_specs=[pl.BlockSpec((1, N_MEM, D_MODEL), lambda i: (i, 0, 0)),
                   pl.BlockSpec((1, N_MEM, D_MODEL), lambda i: (i, 0, 0))],
        out_shape=[out, out],
        compiler_params=_cparams(1),
        name="memkv",
    )(mem, g, w_kv, gk)


def _mem_kernel(q_ref, k_ref, v_ref, gq_ref, o_ref):
    for hd in range(MEM_HEADS):
        sl = slice(hd * MEM_HD, (hd + 1) * MEM_HD)
        q = q_ref[0, :, sl].astype(F32)
        q = q * lax.rsqrt(jnp.mean(q * q, axis=-1, keepdims=True) + EPS) * gq_ref[...]
        q = (q * (MEM_HD ** -0.5)).astype(BF16)
        s = _dot_nt(q, k_ref[0, :, sl])
        p = jnp.exp(s - jnp.max(s, axis=-1, keepdims=True))
        o = _dot(p.astype(BF16), v_ref[0, :, sl]) / jnp.sum(p, axis=-1, keepdims=True)
        o_ref[0, :, sl] = o.astype(BF16)


def _mem(y3d, mk, mv, gq, ts=512):
    b, s, _ = y3d.shape
    return pl.pallas_call(
        _mem_kernel,
        grid=(b, s // ts),
        in_specs=[pl.BlockSpec((1, ts, D_MODEL), lambda i, j: (i, j, C_MQ // D_MODEL)),
                  pl.BlockSpec((1, N_MEM, D_MODEL), lambda i, j: (i, 0, 0)),
                  pl.BlockSpec((1, N_MEM, D_MODEL), lambda i, j: (i, 0, 0)),
                  pl.BlockSpec((1, MEM_HD), lambda i, j: (0, 0))],
        out_specs=pl.BlockSpec((1, ts, D_MODEL), lambda i, j: (i, j, 0)),
        out_shape=jax.ShapeDtypeStruct((b, s, D_MODEL), BF16),
        compiler_params=_cparams(2),
        name="mem",
    )(y3d, mk, mv, gq)


def _merge_kernel(x_ref, b0_ref, b1_ref, b2_ref, b3_ref, gt_ref, bg_ref, wb_ref, wo_ref, o_ref):
    mixed = None
    for n, br in enumerate((b0_ref, b1_ref, b2_ref, b3_ref)):
        gate = jax.nn.sigmoid(
            gt_ref[:, n * D_MODEL:(n + 1) * D_MODEL].astype(F32) + bg_ref[n:n + 1, :])
        term = gate * _dot(br[...], wb_ref[n])
        mixed = term if mixed is None else mixed + term
    o_ref[...] = x_ref[...] + _dot(mixed.astype(BF16), wo_ref[...])


def _merge(x2d, branches, y2d, bg, wb, wo, layer, tm=256):
    m = x2d.shape[0]
    row = lambda w: pl.BlockSpec((tm, w), lambda i: (i, 0))
    return pl.pallas_call(
        _merge_kernel,
        grid=(m // tm,),
        in_specs=[row(D_MODEL), row(D_MODEL), row(D_MODEL), row(D_MODEL), row(D_MODEL),
                  pl.BlockSpec((tm, 4 * D_MODEL), lambda i: (i, C_GT // (4 * D_MODEL))),
                  pl.BlockSpec((4, D_MODEL), lambda i: (0, 0)),
                  pl.BlockSpec((None, 4, D_MODEL, D_MODEL), lambda i: (layer, 0, 0, 0)),
                  pl.BlockSpec((None, D_MODEL, D_MODEL), lambda i: (layer, 0, 0))],
        out_specs=row(D_MODEL),
        out_shape=jax.ShapeDtypeStruct((m, D_MODEL), F32),
        compiler_params=_cparams(1),
        name="merge",
    )(x2d, *branches, y2d, bg, wb, wo)


FFN_TC = 256
FFN_HALO = 8
FFN_RB = 64


def _ffn_kernel(x_ref, g_ref, wg_ref, wv_ref, cw_ref, cb_ref, wd_ref, o_ref,
                h_ref, gbuf_ref, vbuf_ref, act_ref, *, tm, tiles_per_seq):
    i = pl.program_id(0)
    x = x_ref[...]
    h_ref[...] = (x * lax.rsqrt(jnp.mean(x * x, axis=-1, keepdims=True) + EPS)
                  * g_ref[...]).astype(BF16)

    @pl.when((i % tiles_per_seq) == 0)
    def _():
        gbuf_ref[0:FFN_HALO, :] = jnp.zeros((FFN_HALO, D_FF), F32)

    def up(c):
        cols = slice(c * FFN_TC, (c + 1) * FFN_TC)
        gbuf_ref[FFN_HALO:FFN_HALO + tm, cols] = _dot(h_ref[...], wg_ref[:, cols])
        vbuf_ref[:, cols] = _dot(h_ref[...], wv_ref[:, cols])

    def activate(c):
        cols = slice(c * FFN_TC, (c + 1) * FFN_TC)
        cw = cw_ref[:, cols]
        cb = cb_ref[:, cols]
        for r0 in range(0, tm, FFN_RB):
            gate = cb
            for d in range(3):
                lo = FFN_HALO + r0 - d
                gate = gate + gbuf_ref[lo:lo + FFN_RB, cols] * cw[2 - d:3 - d, :]
            act = gate * jax.nn.sigmoid(gate) * vbuf_ref[r0:r0 + FFN_RB, cols]
            act_ref[r0:r0 + FFN_RB, cols] = act.astype(BF16)

    nc = D_FF // FFN_TC
    up(0)
    for c in range(1, nc):
        up(c)
        activate(c - 1)
    activate(nc - 1)
    o_ref[...] = x_ref[...] + _dot(act_ref[...], wd_ref[...])
    gbuf_ref[0:FFN_HALO, :] = gbuf_ref[tm:tm + FFN_HALO, :]


def _ffn(x2d, g, w_up, cw, cb, w_down, layer, seq, tm=512):
    m = x2d.shape[0]
    resident = pl.Buffered(1)
    return pl.pallas_call(
        functools.partial(_ffn_kernel, tm=tm, tiles_per_seq=seq // tm),
        grid=(m // tm,),
        in_specs=[pl.BlockSpec((tm, D_MODEL), lambda i: (i, 0)),
                  pl.BlockSpec((1, D_MODEL), lambda i: (0, 0)),
                  pl.BlockSpec((None, D_MODEL, D_FF), lambda i: (layer, 0, 0),
                               pipeline_mode=resident),
                  pl.BlockSpec((None, D_MODEL, D_FF), lambda i: (layer, 0, 1),
                               pipeline_mode=resident),
                  pl.BlockSpec((3, D_FF), lambda i: (0, 0)),
                  pl.BlockSpec((1, D_FF), lambda i: (0, 0)),
                  pl.BlockSpec((None, D_FF, D_MODEL), lambda i: (layer, 0, 0),
                               pipeline_mode=resident)],
        out_specs=pl.BlockSpec((tm, D_MODEL), lambda i: (i, 0)),
        out_shape=jax.ShapeDtypeStruct((m, D_MODEL), F32),
        scratch_shapes=[pltpu.VMEM((tm, D_MODEL), BF16),
                        pltpu.VMEM((tm + FFN_HALO, D_FF), F32),
                        pltpu.VMEM((tm, D_FF), F32),
                        pltpu.VMEM((tm, D_FF), BF16)],
        compiler_params=_cparams(1),
        name="ffn",
    )(x2d, g, w_up, w_up, cw, cb, w_down)


def _block_diag(w):
    per = LRU_TC // HEAD_DIM
    w4 = w.reshape(D_MODEL // LRU_TC, per, HEAD_DIM, HEAD_DIM)
    eye = jnp.eye(per, dtype=w.dtype)
    bd = jnp.einsum('cpde,pq->cpdqe', w4, eye)
    return bd.reshape(D_MODEL // LRU_TC, LRU_TC, LRU_TC)


def kernel(x, mem, attn_norm_g, mem_norm_g, w_in, b_forget, fox_q_norm_g, fox_k_norm_g,
           lru_conv_w, lru_conv_b, lru_w_a, lru_b_a, lru_w_x, lru_b_x, lru_lambda,
           w_mem_kv, mem_q_norm_g, mem_k_norm_g, b_gate, w_branch, w_out,
           ffn_norm_g, w_up, ffn_conv_w, ffn_conv_b, w_down):
    b, s, d = x.shape
    depth = w_in.shape[0]
    m = b * s
    x2d = x.reshape(m, d)
    w_branch_bf, w_out_bf = w_branch.astype(BF16), w_out.astype(BF16)
    w_up_bf, w_down_bf = w_up.astype(BF16), w_down.astype(BF16)
    for l in range(depth):
        wy = _wprep(w_in, l)
        wf = jnp.pad(w_in[l, :, W_FOX_END:W_F_END], ((0, 0), (0, LANES - N_HEADS)))
        wf_hi = wf.astype(BF16)
        wf = jnp.stack([wf_hi, (wf - wf_hi.astype(F32)).astype(BF16)])
        bf = jnp.pad(b_forget[l], (0, LANES - N_HEADS)).reshape(1, LANES)

        y2d, f2d = _proj(x2d, attn_norm_g[l].reshape(1, d), wy, wf)
        y3d = y2d.reshape(b, s, N_Y)

        fcum = _fcum(f2d.reshape(b, s, LANES), bf)
        qa, ka, vt = _foxprep(y3d, fcum,
                              jnp.tile(fox_q_norm_g[l], 2).reshape(1, LANES),
                              jnp.tile(fox_k_norm_g[l], 2).reshape(1, LANES))
        y_fox = _fox(qa, ka, vt)
        y_sb = _sb(y3d)
        y_lru = _lru(y3d, lru_conv_w[l], lru_conv_b[l].reshape(1, d),
                     _block_diag(lru_w_a[l]).astype(BF16), lru_b_a[l].reshape(1, d),
                     _block_diag(lru_w_x[l]).astype(BF16), lru_b_x[l].reshape(1, d),
                     lru_lambda[l].reshape(1, d))
        mk, mv = _memkv(mem, mem_norm_g[l].reshape(1, d), w_mem_kv[l].astype(BF16),
                        mem_k_norm_g[l].reshape(1, MEM_HD))
        y_mem = _mem(y3d, mk, mv, mem_q_norm_g[l].reshape(1, MEM_HD))

        branches = [t.reshape(m, d) for t in (y_fox, y_lru, y_sb, y_mem)]
        x2d = _merge(x2d, branches, y2d, b_gate[l], w_branch_bf, w_out_bf, l)
        x2d = _ffn(x2d, ffn_norm_g[l].reshape(1, d), w_up_bf, ffn_conv_w[l],
                   ffn_conv_b[l].reshape(1, D_FF), w_down_bf, l, s)
    return x2d.reshape(b, s, d)
```
